```python
import math
import numpy as np
import jax
import jax.numpy as jnp
from jax import lax

D_MODEL = 2048
BATCH = 1
SEQ = 8192
DEPTH = 1

HEAD_DIM = 128
N_HEADS = D_MODEL // HEAD_DIM
N_HEADS_DIL = (3 * N_HEADS) // 8
N_HEADS_NSA = N_HEADS - N_HEADS_DIL
NSA_KV_GROUPS = 2
NSA_GROUP = N_HEADS_NSA // NSA_KV_GROUPS
N_NSA_BRANCHES = 3
DIL_PAIRS = ((128, 1), (512, 4), (2048, 16))
CMP_BLOCK = 32
CMP_STRIDE = 16
CMP_HIDDEN = 256
SEL_BLOCK = 64
N_SELECT = 16
NSA_WINDOW = 512
D_FF = 5632
ROPE_THETA = 10000.0
QBLK = 128
LN_EPS = 1e-5
NEG = -1e30
FORCE_BONUS = 1e4

D_DIL = N_HEADS_DIL * HEAD_DIM
D_NSA = N_HEADS_NSA * HEAD_DIM
KV_W = NSA_KV_GROUPS * HEAD_DIM
IN_COLS = 3 * D_DIL + D_NSA + 6 * KV_W + N_HEADS_NSA * N_NSA_BRANCHES

kernel_name = "hybrid_dilated_nsa_macaron_deepnorm"


def layer_norm(x, g, b):
    xf = x.astype(jnp.float32)
    mu = xf.mean(-1, keepdims=True)
    var = jnp.square(xf - mu).mean(-1, keepdims=True)
    y = (xf - mu) * lax.rsqrt(var + LN_EPS) * g.astype(jnp.float32) + b.astype(jnp.float32)
    return y.astype(x.dtype)


def swiglu(x, w1, w3, w2):
    return (jax.nn.silu(x @ w1) * (x @ w3)) @ w2


def rope_tables(pos_f):
    inv_freq = ROPE_THETA ** (-jnp.arange(0, HEAD_DIM, 2, dtype=jnp.float32) / HEAD_DIM)
    ang = pos_f.astype(jnp.float32)[..., None] * inv_freq
    return jnp.cos(ang), jnp.sin(ang)


def apply_rope(x, cos, sin):
    xf = x.astype(jnp.float32)
    x1, x2 = jnp.split(xf, 2, axis=-1)
    c, s = cos[:, :, None, :], sin[:, :, None, :]
    return jnp.concatenate([x1 * c - x2 * s, x2 * c + x1 * s], axis=-1).astype(x.dtype)


def masked_softmax(s, mask):
    s = jnp.where(mask, s.astype(jnp.float32), NEG)
    m = s.max(-1, keepdims=True)
    e = jnp.where(mask, jnp.exp(s - m), 0.0)
    return e / jnp.maximum(e.sum(-1, keepdims=True), 1e-30)


def banded_causal_attention(q, k, v, window):
    N, H, L, D = q.shape
    n_prev = -(-window // QBLK)
    nb = -(-L // QBLK)
    Lp = nb * QBLK
    qb = jnp.pad(q, ((0, 0), (0, 0), (0, Lp - L), (0, 0))).reshape(N, H, nb, QBLK, D)
    pad_kv = ((0, 0), (0, 0), (n_prev * QBLK, Lp - L), (0, 0))
    kb = jnp.pad(k, pad_kv).reshape(N, H, nb + n_prev, QBLK, D)
    vb = jnp.pad(v, pad_kv).reshape(N, H, nb + n_prev, QBLK, D)
    kband = jnp.concatenate([kb[:, :, j:j + nb] for j in range(n_prev + 1)], axis=3)
    vband = jnp.concatenate([vb[:, :, j:j + nb] for j in range(n_prev + 1)], axis=3)
    s = jnp.einsum('nhbqd,nhbkd->nhbqk', qb, kband).astype(jnp.float32) * (D ** -0.5)
    qpos = jnp.arange(nb)[:, None] * QBLK + jnp.arange(QBLK)[None, :]
    kpos = jnp.arange(nb)[:, None] * QBLK - n_prev * QBLK + jnp.arange((n_prev + 1) * QBLK)[None, :]
    dist = qpos[:, :, None] - kpos[:, None, :]
    mask = (dist >= 0) & (dist <= window) & (kpos[:, None, :] >= 0)
    s = jnp.where(mask, s, NEG)
    m = s.max(-1, keepdims=True)
    e = jnp.exp(s - m)
    l = e.sum(-1)
    o = jnp.einsum('nhbqk,nhbkd->nhbqd', e, vband) / l[..., None]
    lse = m[..., 0] + jnp.log(l)
    return o.reshape(N, H, Lp, D)[:, :, :L], lse.reshape(N, H, Lp)[:, :, :L]


def dilated_attention(q, k, v):
    B, S, H, D = q.shape
    outs, lses = [], []
    for window, dil in DIL_PAIRS:
        L = S // dil

        def to_sub(t):
            return t.reshape(B, L, dil, H, D).transpose(0, 2, 3, 1, 4).reshape(B * dil, H, L, D)

        o, lse = banded_causal_attention(to_sub(q), to_sub(k), to_sub(v), window // dil)
        outs.append(o.reshape(B, dil, H, L, D).transpose(0, 3, 1, 2, 4).reshape(B, S, H, D))
        lses.append(lse.reshape(B, dil, H, L).transpose(0, 3, 1, 2).reshape(B, S, H))
    w = jax.nn.softmax(jnp.stack(lses, 0), axis=0)
    return (w[..., None] * jnp.stack(outs, 0)).sum(0).astype(q.dtype)


def native_sparse_attention(q, kc_tok, vc_tok, ks, vs, kw, vw, gates, positions,
                            cmp_pe, cmp_w1, cmp_b1, cmp_w2, cmp_b2):
    B, S, HN, D = q.shape
    G, HG = NSA_KV_GROUPS, NSA_GROUP
    scale = D ** -0.5
    n_cmp = (S - CMP_BLOCK) // CMP_STRIDE + 1
    n_sel = S // SEL_BLOCK
    k_sel = min(N_SELECT, n_sel)
    nb = S // QBLK

    cidx = np.arange(n_cmp)[:, None] * CMP_STRIDE + np.arange(CMP_BLOCK)[None, :]

    def compress(tok, j):
        blk = tok[:, cidx] + cmp_pe[j][None, None, :, None, :]
        flat = blk.transpose(0, 1, 3, 2, 4).reshape(B, n_cmp, G, CMP_BLOCK * D)
        hdn = jax.nn.gelu(flat @ cmp_w1[j] + cmp_b1[j])
        return hdn @ cmp_w2[j] + cmp_b2[j]

    pos_c = (positions[:, cidx[:, 0]] + positions[:, cidx[:, -1]]).astype(jnp.float32) * 0.5
    cos_c, sin_c = rope_tables(pos_c)
    kc = apply_rope(compress(kc_tok, 0), cos_c, sin_c).transpose(0, 2, 1, 3)
    vc = compress(vc_tok, 1).transpose(0, 2, 1, 3)
    cmp_end = jnp.asarray(cidx[:, -1])

    c_lo, c_hi = cidx[:, 0][:, None], cidx[:, -1][:, None]
    s_lo = (np.arange(n_sel) * SEL_BLOCK)[None, :]
    overlap = jnp.asarray(((c_lo <= s_lo + SEL_BLOCK - 1) & (c_hi >= s_lo)).astype(np.float32))
    sel_start = jnp.arange(n_sel) * SEL_BLOCK

    ks_blocks = ks.transpose(0, 2, 1, 3).reshape(B, G, n_sel, SEL_BLOCK, D)
    vs_blocks = vs.transpose(0, 2, 1, 3).reshape(B, G, n_sel, SEL_BLOCK, D)
    pad_w = ((0, 0), (0, 0), (NSA_WINDOW, 0), (0, 0))
    kw_pad = jnp.pad(kw.transpose(0, 2, 1, 3), pad_w)
    vw_pad = jnp.pad(vw.transpose(0, 2, 1, 3), pad_w)
    qg = q.reshape(B, S, G, HG, D).transpose(0, 2, 3, 1, 4)
    gg = gates.reshape(B, S, G, HG, N_NSA_BRANCHES).transpose(0, 2, 3, 1, 4)
    gather_blocks = jax.vmap(jax.vmap(lambda kb, ix: kb[ix]))

    def block_fn(i):
        t0 = i * QBLK
        t = t0 + jnp.arange(QBLK)
        qb = lax.dynamic_slice_in_dim(qg, t0, QBLK, axis=3)
        gb = lax.dynamic_slice_in_dim(gg, t0, QBLK, axis=3)
        s = jnp.einsum('bghqd,bgnd->bghqn', qb, kc) * scale
        p_cmp = masked_softmax(s, cmp_end[None, :] <= t[:, None])
        o_cmp = jnp.einsum('bghqn,bgnd->bghqd', p_cmp, vc)
        imp = p_cmp.sum(2) @ overlap
        valid = sel_start[None, :] <= t[:, None]
        cur = t // SEL_BLOCK
        jj = jnp.arange(n_sel)[None, :]
        forced = (jj == 0) | (jj == cur[:, None]) | (jj == cur[:, None] - 1)
        score = jnp.where(valid, imp + jnp.where(forced, FORCE_BONUS, 0.0), NEG)
        _, sel = lax.top_k(score, k_sel)
        sel_valid = jnp.take_along_axis(jnp.broadcast_to(valid, score.shape), sel, axis=-1)
        ks_g = gather_blocks(ks_blocks, sel).reshape(B, G, QBLK, k_sel * SEL_BLOCK, D)
        vs_g = gather_blocks(vs_blocks, sel).reshape(B, G, QBLK, k_sel * SEL_BLOCK, D)
        kpos = sel[..., None] * SEL_BLOCK + jnp.arange(SEL_BLOCK)
        smask = (sel_valid[..., None] & (kpos <= t[:, None, None])).reshape(B, G, QBLK, k_sel * SEL_BLOCK)
        s = jnp.einsum('bghqd,bgqkd->bghqk', qb, ks_g) * scale
        p = masked_softmax(s, smask[:, :, None])
        o_slc = jnp.einsum('bghqk,bgqkd->bghqd', p, vs_g)
        kwb = lax.dynamic_slice_in_dim(kw_pad, t0, QBLK + NSA_WINDOW, axis=2)
        vwb = lax.dynamic_slice_in_dim(vw_pad, t0, QBLK + NSA_WINDOW, axis=2)
        kpos_w = t0 - NSA_WINDOW + jnp.arange(QBLK + NSA_WINDOW)
        dist = t[:, None] - kpos_w[None, :]
        wmask = (dist >= 0) & (dist < NSA_WINDOW) & (kpos_w[None, :] >= 0)
        s = jnp.einsum('bghqd,bgkd->bghqk', qb, kwb) * scale
        p = masked_softmax(s, wmask)
        o_win = jnp.einsum('bghqk,bgkd->bghqd', p, vwb)
        return gb[..., 0:1] * o_cmp + gb[..., 1:2] * o_slc + gb[..., 2:3] * o_win

    out = lax.map(block_fn, jnp.arange(nb))
    return out.transpose(1, 0, 4, 2, 3, 5).reshape(B, S, HN, D).astype(q.dtype)


def split_in_proj(proj):
    sizes = (D_DIL, D_DIL, D_DIL, D_NSA, KV_W, KV_W, KV_W, KV_W, KV_W, KV_W,
             N_HEADS_NSA * N_NSA_BRANCHES)
    offs = [int(o) for o in np.cumsum(sizes)[:-1]]
    return jnp.split(proj, offs, axis=-1)


def hybrid_mixer(h, cos, sin, positions, w_in, gate_b, cmp_pe, cmp_w1, cmp_b1, cmp_w2, cmp_b2, w_out):
    B, S, _ = h.shape
    proj = h @ w_in
    qa, ka, va, qn, kc, vc, ks, vs, kw, vw, g = split_in_proj(proj)

    def heads(t, n):
        return t.reshape(B, S, n, HEAD_DIM)

    out_a = dilated_attention(apply_rope(heads(qa, N_HEADS_DIL), cos, sin),
                              apply_rope(heads(ka, N_HEADS_DIL), cos, sin),
                              heads(va, N_HEADS_DIL))
    gates = jax.nn.sigmoid((g + gate_b).astype(jnp.float32)).reshape(B, S, N_HEADS_NSA, N_NSA_BRANCHES)
    out_b = native_sparse_attention(
        apply_rope(heads(qn, N_HEADS_NSA), cos, sin),
        heads(kc, NSA_KV_GROUPS), heads(vc, NSA_KV_GROUPS),
        apply_rope(heads(ks, NSA_KV_GROUPS), cos, sin), heads(vs, NSA_KV_GROUPS),
        apply_rope(heads(kw, NSA_KV_GROUPS), cos, sin), heads(vw, NSA_KV_GROUPS),
        gates, positions, cmp_pe, cmp_w1, cmp_b1, cmp_w2, cmp_b2)
    mixed = jnp.concatenate([out_a, out_b.astype(out_a.dtype)], axis=2).reshape(B, S, D_MODEL)
    return (mixed @ w_out).astype(h.dtype)


def setup_inputs(seed: int = 0) -> dict:
    key = jax.random.key(seed)
    ks = jax.random.split(key, 24)
    beta = (8.0 * DEPTH) ** -0.25
    nrm = jax.random.normal
    f32 = jnp.float32
    return {
        "x": nrm(ks[0], (BATCH, SEQ, D_MODEL), f32),
        "positions": jnp.broadcast_to(jnp.arange(SEQ, dtype=jnp.int32), (BATCH, SEQ)),
        "ln1_g": 1.0 + 0.02 * nrm(ks[1], (DEPTH, D_MODEL), f32),
        "ln1_b": 0.02 * nrm(ks[2], (DEPTH, D_MODEL), f32),
        "ffn1_w1": nrm(ks[3], (DEPTH, D_MODEL, D_FF), f32) * D_MODEL ** -0.5,
        "ffn1_w3": nrm(ks[4], (DEPTH, D_MODEL, D_FF), f32) * D_MODEL ** -0.5,
        "ffn1_w2": nrm(ks[5], (DEPTH, D_FF, D_MODEL), f32) * D_FF ** -0.5 * beta,
        "w_in": nrm(ks[6], (DEPTH, D_MODEL, IN_COLS), f32) * D_MODEL ** -0.5,
        "gate_b": 0.1 * nrm(ks[7], (DEPTH, N_HEADS_NSA * N_NSA_BRANCHES), f32),
        "cmp_pe": 0.1 * nrm(ks[8], (DEPTH, 2, CMP_BLOCK, HEAD_DIM), f32),
        "cmp_w1": nrm(ks[9], (DEPTH, 2, CMP_BLOCK * HEAD_DIM, CMP_HIDDEN), f32) * (CMP_BLOCK * HEAD_DIM) ** -0.5,
        "cmp_b1": 0.02 * nrm(ks[10], (DEPTH, 2, CMP_HIDDEN), f32),
        "cmp_w2": nrm(ks[11], (DEPTH, 2, CMP_HIDDEN, HEAD_DIM), f32) * CMP_HIDDEN ** -0.5,
        "cmp_b2": 0.02 * nrm(ks[12], (DEPTH, 2, HEAD_DIM), f32),
        "w_out": nrm(ks[13], (DEPTH, D_MODEL, D_MODEL), f32) * D_MODEL ** -0.5 * beta,
        "ln2_g": 1.0 + 0.02 * nrm(ks[14], (DEPTH, D_MODEL), f32),
        "ln2_b": 0.02 * nrm(ks[15], (DEPTH, D_MODEL), f32),
        "ffn2_w1": nrm(ks[16], (DEPTH, D_MODEL, D_FF), f32) * D_MODEL ** -0.5,
        "ffn2_w3": nrm(ks[17], (DEPTH, D_MODEL, D_FF), f32) * D_MODEL ** -0.5,
        "ffn2_w2": nrm(ks[18], (DEPTH, D_FF, D_MODEL), f32) * D_FF ** -0.5 * beta,
        "ln3_g": 1.0 + 0.02 * nrm(ks[19], (DEPTH, D_MODEL), f32),
        "ln3_b": 0.02 * nrm(ks[20], (DEPTH, D_MODEL), f32),
    }


def reference(x, positions, ln1_g, ln1_b, ffn1_w1, ffn1_w3, ffn1_w2, w_in, gate_b,
              cmp_pe, cmp_w1, cmp_b1, cmp_w2, cmp_b2, w_out, ln2_g, ln2_b,
              ffn2_w1, ffn2_w3, ffn2_w2, ln3_g, ln3_b):
    alpha = (2.0 * DEPTH) ** 0.25
    cos, sin = rope_tables(positions.astype(jnp.float32))
    h = x
    for l in range(DEPTH):
        h = layer_norm(alpha * h + 0.5 * swiglu(h, ffn1_w1[l], ffn1_w3[l], ffn1_w2[l]), ln1_g[l], ln1_b[l])
        mix = hybrid_mixer(h, cos, sin, positions, w_in[l], gate_b[l], cmp_pe[l], cmp_w1[l],
                           cmp_b1[l], cmp_w2[l], cmp_b2[l], w_out[l])
        h = layer_norm(alpha * h + mix, ln2_g[l], ln2_b[l])
        h = layer_norm(alpha * h + 0.5 * swiglu(h, ffn2_w1[l], ffn2_w3[l], ffn2_w2[l]), ln3_g[l], ln3_b[l])
    return h
```

```python
import functools

import jax
import jax.numpy as jnp
import numpy as np
from jax import lax
from jax.experimental import pallas as pl
from jax.experimental.pallas import tpu as pltpu

HEAD_DIM = 128
N_HEADS_DIL = 6
N_HEADS_NSA = 10
NSA_KV_GROUPS = 2
NSA_GROUP = N_HEADS_NSA // NSA_KV_GROUPS
N_NSA_BRANCHES = 3
DIL_PAIRS = ((128, 1), (512, 4), (2048, 16))
CMP_BLOCK = 32
CMP_STRIDE = 16
CMP_HIDDEN = 256
SEL_BLOCK = 64
N_SELECT = 16
NSA_WINDOW = 512
ROPE_THETA = 10000.0
QBLK = 128
LN_EPS = 1e-5
NEG = -1e30
FORCE_BONUS = 1e4
SEL_OFF = -1e9
DEPTH = 1

D_DIL = N_HEADS_DIL * HEAD_DIM
D_NSA = N_HEADS_NSA * HEAD_DIM
KV_W = NSA_KV_GROUPS * HEAD_DIM
N_GATES = N_HEADS_NSA * N_NSA_BRANCHES

H_QN, H_QA, H_KA, H_VA = 0, 10, 16, 22
H_KC, H_VC, H_KS, H_VS, H_KW, H_VW = 28, 30, 32, 34, 36, 38
N_PROJ_HEADS = 40

VMEM_LIMIT_BYTES = 56 * 1024 * 1024

F32 = jnp.float32
BF16 = jnp.bfloat16


def _params(sem, vmem=VMEM_LIMIT_BYTES):
    return pltpu.CompilerParams(dimension_semantics=sem, vmem_limit_bytes=vmem)


def _nt_dot(a, b):
    return lax.dot_general(a, b, (((1,), (1,)), ((), ())), preferred_element_type=F32)


def _layer_norm(y, g, b):
    mu = jnp.mean(y, axis=-1, keepdims=True)
    yc = y - mu
    var = jnp.mean(yc * yc, axis=-1, keepdims=True)
    return yc * lax.rsqrt(var + LN_EPS) * g + b


def _rope_table_kernel(pa_ref, pb_ref, invf_ref, cos_ref, sin_ref):
    pos = (pa_ref[...] + pb_ref[...]) * 0.5
    ang = pos * invf_ref[...]
    lane = lax.broadcasted_iota(jnp.int32, ang.shape, 1)
    cos_ref[...] = jnp.cos(ang)
    sin_ref[...] = jnp.where(lane < HEAD_DIM // 2, -1.0, 1.0) * jnp.sin(ang)


def _rope_tables(pos_a, pos_b, invf, tile):
    n = pos_a.shape[0]
    spec_p = pl.BlockSpec((tile, 1), lambda i: (i, 0))
    spec_t = pl.BlockSpec((tile, HEAD_DIM), lambda i: (i, 0))
    return pl.pallas_call(
        _rope_table_kernel,
        grid=(n // tile,),
        in_specs=[spec_p, spec_p, pl.BlockSpec((1, HEAD_DIM), lambda i: (0, 0))],
        out_specs=[spec_t, spec_t],
        out_shape=[jax.ShapeDtypeStruct((n, HEAD_DIM), F32)] * 2,
        compiler_params=_params(("arbitrary",)),
        name="rope_tables",
    )(pos_a, pos_b, invf)


def _ffn_ln_kernel(h_ref, w1_ref, w3_ref, w2_ref, g_ref, b_ref, o_ref, ob_ref,
                   hb_ref, acc_ref, *, alpha, nf):
    f = pl.program_id(1)

    @pl.when(f == 0)
    def _():
        hb_ref[...] = h_ref[...].astype(BF16)
        acc_ref[...] = jnp.zeros_like(acc_ref)

    hb = hb_ref[...]
    a = jnp.dot(hb, w1_ref[...], preferred_element_type=F32)
    b = jnp.dot(hb, w3_ref[...], preferred_element_type=F32)
    act = (a * jax.nn.sigmoid(a)) * b
    acc_ref[...] += jnp.dot(act.astype(BF16), w2_ref[...], preferred_element_type=F32)

    @pl.when(f == nf - 1)
    def _():
        y = alpha * h_ref[...] + 0.5 * acc_ref[...]
        out = _layer_norm(y, g_ref[...], b_ref[...])
        o_ref[...] = out
        ob_ref[...] = out.astype(BF16)


def _ffn_ln(h, w1, w3, w2, g, b, alpha, tm, tf):
    s, d = h.shape
    dff = w1.shape[1]
    nf = dff // tf
    return pl.pallas_call(
        functools.partial(_ffn_ln_kernel, alpha=alpha, nf=nf),
        grid=(s // tm, nf),
        in_specs=[
            pl.BlockSpec((tm, d), lambda i, f: (i, 0)),
            pl.BlockSpec((d, tf), lambda i, f: (0, f)),
            pl.BlockSpec((d, tf), lambda i, f: (0, f)),
            pl.BlockSpec((tf, d), lambda i, f: (f, 0)),
            pl.BlockSpec((1, d), lambda i, f: (0, 0)),
            pl.BlockSpec((1, d), lambda i, f: (0, 0)),
        ],
        out_specs=[pl.BlockSpec((tm, d), lambda i, f: (i, 0)),
                   pl.BlockSpec((tm, d), lambda i, f: (i, 0))],
        out_shape=[jax.ShapeDtypeStruct((s, d), F32), jax.ShapeDtypeStruct((s, d), BF16)],
        scratch_shapes=[pltpu.VMEM((tm, d), BF16), pltpu.VMEM((tm, d), F32)],
        compiler_params=_params(("arbitrary", "arbitrary")),
        name="ffn_ln",
    )(h, w1, w3, w2, g, b)


def _in_proj_kernel(h_ref, w_ref, cos_ref, sin_ref, o_ref, *, scale):
    j = pl.program_id(1)
    acc = jnp.dot(h_ref[...], w_ref[...], preferred_element_type=F32)
    rope = (j <= 10) | (j == 16) | (j == 18)
    qscale = jnp.where(j <= 7, scale, 1.0).astype(F32)
    cos = cos_ref[...]
    sin = sin_ref[...]
    for half in range(2):
        x = acc[:, half * HEAD_DIM:(half + 1) * HEAD_DIM]
        xr = x * cos + pltpu.roll(x, HEAD_DIM // 2, 1) * sin
        y = jnp.where(rope, xr, x) * qscale
        o_ref[half] = y.astype(BF16)


def _in_proj(hb, w, cosf, sinf, scale, tm):
    s, d = hb.shape
    nj = w.shape[1] // (2 * HEAD_DIM)
    return pl.pallas_call(
        functools.partial(_in_proj_kernel, scale=scale),
        grid=(s // tm, nj),
        in_specs=[
            pl.BlockSpec((tm, d), lambda i, j: (i, 0)),
            pl.BlockSpec((d, 2 * HEAD_DIM), lambda i, j: (0, j)),
            pl.BlockSpec((tm, HEAD_DIM), lambda i, j: (i, 0)),
            pl.BlockSpec((tm, HEAD_DIM), lambda i, j: (i, 0)),
        ],
        out_specs=pl.BlockSpec((2, tm, HEAD_DIM), lambda i, j: (j, i, 0)),
        out_shape=jax.ShapeDtypeStruct((2 * nj, s, HEAD_DIM), BF16),
        compiler_params=_params(("arbitrary", "arbitrary")),
        name="in_proj",
    )(hb, w, cosf, sinf)


def _gates_kernel(h_ref, w_ref, b_ref, o_ref):
    z = jnp.dot(h_ref[...], w_ref[...], preferred_element_type=F32) + b_ref[...]
    o_ref[...] = jax.nn.sigmoid(z)


def _gates(hb, wg, bg, tm):
    s, d = hb.shape
    n = wg.shape[1]
    return pl.pallas_call(
        _gates_kernel,
        grid=(s // tm,),
        in_specs=[pl.BlockSpec((tm, d), lambda i: (i, 0)),
                  pl.BlockSpec((d, n), lambda i: (0, 0)),
                  pl.BlockSpec((1, n), lambda i: (0, 0))],
        out_specs=pl.BlockSpec((tm, n), lambda i: (i, 0)),
        out_shape=jax.ShapeDtypeStruct((s, n), F32),
        compiler_params=_params(("arbitrary",)),
        name="nsa_gates",
    )(hb, wg, bg)


def _compress_kernel(tok_ref, pe_ref, w1_ref, b1_ref, w2_ref, b2_ref, cos_ref, sin_ref,
                     o_ref, ot_ref):
    j = pl.program_id(0)
    half = CMP_STRIDE * HEAD_DIM
    tok = tok_ref[0].astype(F32)
    pe = pe_ref[0]
    w1 = w1_ref[0]
    top = (tok + pe[:, :half]).astype(BF16)
    bot = (tok + pe[:, half:]).astype(BF16)
    u = jnp.dot(top, w1[:half], preferred_element_type=F32)
    v = jnp.dot(bot, w1[half:], preferred_element_type=F32)
    nc = u.shape[0]
    hid = u + pltpu.roll(v, nc - 1, 0) + b1_ref[0]
    hid = jax.nn.gelu(hid)
    out = jnp.dot(hid.astype(BF16), w2_ref[0], preferred_element_type=F32) + b2_ref[0]
    roped = out * cos_ref[...] + pltpu.roll(out, HEAD_DIM // 2, 1) * sin_ref[...]
    out = jnp.where(j == 0, roped, out)
    o_ref[0, 0] = out.astype(BF16)
    ot_ref[0, 0] = out.T.astype(BF16)


def _compress(tok16, pe, w1, b1, w2, b2, cos_c, sin_c):
    nc = tok16.shape[1]
    blk = CMP_BLOCK * HEAD_DIM
    g = NSA_KV_GROUPS
    return pl.pallas_call(
        _compress_kernel,
        grid=(2, g),
        in_specs=[
            pl.BlockSpec((1, nc, CMP_STRIDE * HEAD_DIM), lambda j, gi: (H_KC + g * j + gi, 0, 0)),
            pl.BlockSpec((1, 1, blk), lambda j, gi: (j, 0, 0)),
            pl.BlockSpec((1, blk, CMP_HIDDEN), lambda j, gi: (j, 0, 0)),
            pl.BlockSpec((1, 1, CMP_HIDDEN), lambda j, gi: (j, 0, 0)),
            pl.BlockSpec((1, CMP_HIDDEN, HEAD_DIM), lambda j, gi: (j, 0, 0)),
            pl.BlockSpec((1, 1, HEAD_DIM), lambda j, gi: (j, 0, 0)),
            pl.BlockSpec((nc, HEAD_DIM), lambda j, gi: (0, 0)),
            pl.BlockSpec((nc, HEAD_DIM), lambda j, gi: (0, 0)),
        ],
        out_specs=[pl.BlockSpec((1, 1, nc, HEAD_DIM), lambda j, gi: (j, gi, 0, 0)),
                   pl.BlockSpec((1, 1, HEAD_DIM, nc), lambda j, gi: (j, gi, 0, 0))],
        out_shape=[jax.ShapeDtypeStruct((2, g, nc, HEAD_DIM), BF16),
                   jax.ShapeDtypeStruct((2, g, HEAD_DIM, nc), BF16)],
        compiler_params=_params(("arbitrary", "arbitrary")),
        name="nsa_compress",
    )(tok16, pe, w1, b1, w2, b2, cos_c, sin_c)


def _dil_kernel(q_ref, k_ref, v_ref, o_ref, kpad, vpad, bias_ref, *, tq, padk):
    h = pl.program_id(0)
    i = pl.program_id(1)
    w = padk + tq

    @pl.when((h == 0) & (i == 0))
    def _():
        r = lax.broadcasted_iota(jnp.int32, (tq, w), 0)
        c = lax.broadcasted_iota(jnp.int32, (tq, w), 1)
        d = r + padk - c
        cnt = jnp.zeros((tq, w), F32)
        for window, dil in DIL_PAIRS:
            hit = (d >= 0) & (d <= window) & ((d & (dil - 1)) == 0)
            cnt = cnt + jnp.where(hit, 1.0, 0.0)
        bias_ref[...] = jnp.where(cnt > 0.0, jnp.log(jnp.maximum(cnt, 1.0)), NEG)

    @pl.when(i == 0)
    def _():
        kpad[0:padk, :] = jnp.zeros((padk, HEAD_DIM), BF16)
        vpad[0:padk, :] = jnp.zeros((padk, HEAD_DIM), BF16)
        kpad[padk:, :] = k_ref[0]
        vpad[padk:, :] = v_ref[0]

    q0 = pl.multiple_of(i * tq, tq)
    s = _nt_dot(q_ref[0], kpad[pl.ds(q0, w), :]) + bias_ref[...]
    col = lax.broadcasted_iota(jnp.int32, (1, w), 1)
    s = jnp.where(col >= padk - q0, s, NEG)
    m = jnp.max(s, axis=-1, keepdims=True)
    p = jnp.exp(s - m)
    l = jnp.sum(p, axis=-1, keepdims=True)
    o = jnp.dot(p.astype(BF16), vpad[pl.ds(q0, w), :], preferred_element_type=F32)
    o_ref[...] = (o / l).astype(BF16)


def _dilated(heads, tq):
    _, s, _ = heads.shape
    padk = max(wd for wd, _ in DIL_PAIRS)
    return pl.pallas_call(
        functools.partial(_dil_kernel, tq=tq, padk=padk),
        grid=(N_HEADS_DIL, s // tq),
        in_specs=[
            pl.BlockSpec((1, tq, HEAD_DIM), lambda h, i: (H_QA + h, i, 0)),
            pl.BlockSpec((1, s, HEAD_DIM), lambda h, i: (H_KA + h, 0, 0)),
            pl.BlockSpec((1, s, HEAD_DIM), lambda h, i: (H_VA + h, 0, 0)),
        ],
        out_specs=pl.BlockSpec((tq, HEAD_DIM), lambda h, i: (i, h)),
        out_shape=jax.ShapeDtypeStruct((s, D_DIL), BF16),
        scratch_shapes=[pltpu.VMEM((padk + s, HEAD_DIM), BF16),
                        pltpu.VMEM((padk + s, HEAD_DIM), BF16),
                        pltpu.VMEM((tq, padk + tq), F32)],
        compiler_params=_params(("arbitrary", "arbitrary")),
        name="dilated_attn",
    )(heads, heads, heads)


def _nsa_kernel(q_ref, kc_ref, vct_ref, ks_ref, vs_ref, kw_ref, vw_ref, gate_ref, ovt_ref,
                o_ref, kaug, *, s_len, nsp, kt, k_sel):
    qb = pl.program_id(1)
    t0 = qb * QBLK
    hg = NSA_GROUP
    rows = hg * QBLK
    ncp = kc_ref.shape[2]

    @pl.when(qb == 0)
    def _():
        kaug[:, 0:HEAD_DIM] = ks_ref[0]
        key = lax.broadcasted_iota(jnp.int32, (s_len, nsp), 0)
        blk = lax.broadcasted_iota(jnp.int32, (s_len, nsp), 1)
        kaug[:, HEAD_DIM:] = jnp.where(blk == key // SEL_BLOCK, 1.0, 0.0).astype(BF16)

    kc = kc_ref[0, 0]
    vct = vct_ref[0, 0]
    n_io = lax.broadcasted_iota(jnp.int32, (ncp, QBLK), 0)
    t_io = t0 + lax.broadcasted_iota(jnp.int32, (ncp, QBLK), 1)
    cmask = (n_io * CMP_STRIDE + (CMP_BLOCK - 1) <= t_io) & (n_io < ncp - 1)
    psum = jnp.zeros((ncp, QBLK), F32)
    o_cmp = []
    for u in range(hg):
        st = jnp.where(cmask, _nt_dot(kc, q_ref[u]), NEG)
        m = jnp.max(st, axis=0, keepdims=True)
        e = jnp.where(cmask, jnp.exp(st - m), 0.0)
        pt = e / jnp.maximum(jnp.sum(e, axis=0, keepdims=True), 1e-30)
        psum = psum + pt
        o_cmp.append(jnp.dot(vct, pt.astype(BF16), preferred_element_type=F32).T)

    p_hi = psum.astype(BF16)
    p_lo = (psum - p_hi.astype(F32)).astype(BF16)
    ovt = ovt_ref[...]
    imp = (jnp.dot(ovt, p_hi, preferred_element_type=F32)
           + jnp.dot(ovt, p_lo, preferred_element_type=F32))
    j_io = lax.broadcasted_iota(jnp.int32, (nsp, QBLK), 0)
    t_sel = t0 + lax.broadcasted_iota(jnp.int32, (nsp, QBLK), 1)
    valid = j_io * SEL_BLOCK <= t_sel
    cur = t_sel // SEL_BLOCK
    forced = (j_io == 0) | (j_io == cur) | (j_io == cur - 1)
    score = jnp.where(valid, imp + jnp.where(forced, FORCE_BONUS, 0.0), NEG)
    picked = jnp.zeros((nsp, QBLK), F32)
    for _ in range(k_sel):
        m = jnp.max(score, axis=0, keepdims=True)
        first = jnp.min(jnp.where(score == m, j_io, nsp), axis=0, keepdims=True)
        hit = j_io == first
        picked = jnp.where(hit, 1.0, picked)
        score = jnp.where(hit, -jnp.inf, score)
    sel_t = jnp.where(valid, picked, 0.0)
    sel_off = jnp.where(sel_t.T > 0.0, 0.0, SEL_OFF).astype(BF16)
    q5 = q_ref[...].reshape(rows, HEAD_DIM)
    qaug = jnp.concatenate([q5, jnp.concatenate([sel_off] * hg, axis=0)], axis=1)

    def sel_tile(kti, carry, causal):
        m_i, l_i, acc = carry
        k0 = pl.multiple_of(kti * kt, kt)
        s = _nt_dot(qaug, kaug[pl.ds(k0, kt), :])
        if causal:
            kpos = k0 + lax.broadcasted_iota(jnp.int32, (rows, kt), 1)
            tq = t0 + (lax.broadcasted_iota(jnp.int32, (rows, kt), 0) & (QBLK - 1))
            s = jnp.where(kpos <= tq, s, NEG)
        m_new = jnp.maximum(m_i, jnp.max(s, axis=-1, keepdims=True))
        p = jnp.exp(s - m_new)
        a = jnp.exp(m_i - m_new)
        l_new = a * l_i + jnp.sum(p, axis=-1, keepdims=True)
        pv = jnp.dot(p.astype(BF16), vs_ref[0, pl.ds(k0, kt), :], preferred_element_type=F32)
        return m_new, l_new, a * acc + pv

    n_kt = (t0 + QBLK + kt - 1) // kt
    init = (jnp.full((rows, 1), NEG, F32), jnp.zeros((rows, 1), F32),
            jnp.zeros((rows, HEAD_DIM), F32))
    carry = lax.fori_loop(0, n_kt - 1, lambda kti, c: sel_tile(kti, c, False), init)
    _, l_s, acc_s = sel_tile(n_kt - 1, carry, True)
    o_slc = acc_s / l_s

    ww = NSA_WINDOW + QBLK
    w0 = pl.multiple_of(jnp.maximum(t0 - NSA_WINDOW, 0), QBLK)
    sw = _nt_dot(q5, kw_ref[0, pl.ds(w0, ww), :])
    kpos = w0 + lax.broadcasted_iota(jnp.int32, (rows, ww), 1)
    dist = t0 + (lax.broadcasted_iota(jnp.int32, (rows, ww), 0) & (QBLK - 1)) - kpos
    wmask = (dist >= 0) & (dist < NSA_WINDOW)
    sw = jnp.where(wmask, sw, NEG)
    mw = jnp.max(sw, axis=-1, keepdims=True)
    pw = jnp.exp(sw - mw)
    lw = jnp.sum(pw, axis=-1, keepdims=True)
    o_win = jnp.dot(pw.astype(BF16), vw_ref[0, pl.ds(w0, ww), :],
                    preferred_element_type=F32) / lw

    gate = gate_ref[...]
    for u in range(hg):
        c = N_NSA_BRANCHES * u
        r = slice(u * QBLK, (u + 1) * QBLK)
        out = (gate[:, c:c + 1] * o_cmp[u] + gate[:, c + 1:c + 2] * o_slc[r]
               + gate[:, c + 2:c + 3] * o_win[r])
        o_ref[:, u * HEAD_DIM:(u + 1) * HEAD_DIM] = out.astype(BF16)


def _nsa(heads, kcv, vct, gates, ovt, kt):
    _, s, _ = heads.shape
    ncp = kcv.shape[2]
    nsp = ovt.shape[0]
    hg = NSA_GROUP
    k_sel = min(N_SELECT, s // SEL_BLOCK)
    full = lambda hbase: pl.BlockSpec((1, s, HEAD_DIM), lambda g, qb: (hbase + g, 0, 0))
    return pl.pallas_call(
        functools.partial(_nsa_kernel, s_len=s, nsp=nsp, kt=kt, k_sel=k_sel),
        grid=(NSA_KV_GROUPS, s // QBLK),
        in_specs=[
            pl.BlockSpec((hg, QBLK, HEAD_DIM), lambda g, qb: (g, qb, 0)),
            pl.BlockSpec((1, 1, ncp, HEAD_DIM), lambda g, qb: (0, g, 0, 0)),
            pl.BlockSpec((1, 1, HEAD_DIM, ncp), lambda g, qb: (1, g, 0, 0)),
            full(H_KS), full(H_VS), full(H_KW), full(H_VW),
            pl.BlockSpec((QBLK, HEAD_DIM), lambda g, qb: (qb, g)),
            pl.BlockSpec((nsp, ncp), lambda g, qb: (0, 0)),
        ],
        out_specs=pl.BlockSpec((QBLK, hg * HEAD_DIM), lambda g, qb: (qb, g)),
        out_shape=jax.ShapeDtypeStruct((s, D_NSA), BF16),
        scratch_shapes=[pltpu.VMEM((s, HEAD_DIM + nsp), BF16)],
        compiler_params=_params(("arbitrary", "arbitrary")),
        name="nsa_attn",
    )(heads, kcv, vct, heads, heads, heads, heads, gates, ovt)


def _out_proj_ln_kernel(a_ref, b_ref, wa_ref, wb_ref, h_ref, g_ref, be_ref, o_ref, *, alpha):
    mix = (jnp.dot(a_ref[...], wa_ref[...], preferred_element_type=F32)
           + jnp.dot(b_ref[...], wb_ref[...], preferred_element_type=F32))
    o_ref[...] = _layer_norm(alpha * h_ref[...] + mix, g_ref[...], be_ref[...])


def _out_proj_ln(mix_a, mix_b, wa, wb, h, g, b, alpha, tm):
    s, d = h.shape
    return pl.pallas_call(
        functools.partial(_out_proj_ln_kernel, alpha=alpha),
        grid=(s // tm,),
        in_specs=[
            pl.BlockSpec((tm, D_DIL), lambda i: (i, 0)),
            pl.BlockSpec((tm, D_NSA), lambda i: (i, 0)),
            pl.BlockSpec((D_DIL, d), lambda i: (0, 0)),
            pl.BlockSpec((D_NSA, d), lambda i: (0, 0)),
            pl.BlockSpec((tm, d), lambda i: (i, 0)),
            pl.BlockSpec((1, d), lambda i: (0, 0)),
            pl.BlockSpec((1, d), lambda i: (0, 0)),
        ],
        out_specs=pl.BlockSpec((tm, d), lambda i: (i, 0)),
        out_shape=jax.ShapeDtypeStruct((s, d), F32),
        compiler_params=_params(("arbitrary",)),
        name="out_proj_ln",
    )(mix_a, mix_b, wa, wb, h, g, b)


def _overlap_t(s):
    n_cmp = (s - CMP_BLOCK) // CMP_STRIDE + 1
    n_sel = s // SEL_BLOCK
    ncp = s // CMP_STRIDE
    nsp = -(-n_sel // HEAD_DIM) * HEAD_DIM
    c_lo = np.arange(n_cmp) * CMP_STRIDE
    c_hi = c_lo + CMP_BLOCK - 1
    s_lo = (np.arange(n_sel) * SEL_BLOCK)[:, None]
    ov = np.zeros((nsp, ncp), np.float32)
    ov[:n_sel, :n_cmp] = (c_lo[None, :] <= s_lo + SEL_BLOCK - 1) & (c_hi[None, :] >= s_lo)
    return jnp.asarray(ov, BF16)


def kernel(x, positions, ln1_g, ln1_b, ffn1_w1, ffn1_w3, ffn1_w2, w_in, gate_b, cmp_pe, cmp_w1,
           cmp_b1, cmp_w2, cmp_b2, w_out, ln2_g, ln2_b, ffn2_w1, ffn2_w3, ffn2_w2, ln3_g, ln3_b):
    bsz, s, d = x.shape
    assert bsz == 1 and d == (N_HEADS_DIL + N_HEADS_NSA) * HEAD_DIM
    assert s % 512 == 0 and s >= NSA_WINDOW + QBLK
    alpha = (2.0 * DEPTH) ** 0.25
    scale = HEAD_DIM ** -0.5
    row = lambda v: v.reshape(1, -1)
    tm = 512
    tf = 512 if ffn1_w1.shape[2] % 512 == 0 else 256

    inv_freq = ROPE_THETA ** (-jnp.arange(0, HEAD_DIM, 2, dtype=F32) / HEAD_DIM)
    invf = jnp.concatenate([inv_freq, inv_freq]).reshape(1, HEAD_DIM)
    pos = positions[0].astype(F32).reshape(s, 1)
    cosf, sinf = _rope_tables(pos, pos, invf, 512)
    ncp = s // CMP_STRIDE
    n_cmp = (s - CMP_BLOCK) // CMP_STRIDE + 1
    pos_lo = jnp.pad(pos[0:n_cmp * CMP_STRIDE:CMP_STRIDE], ((0, ncp - n_cmp), (0, 0)))
    pos_hi = jnp.pad(pos[CMP_BLOCK - 1::CMP_STRIDE][:n_cmp], ((0, ncp - n_cmp), (0, 0)))
    cos_c, sin_c = _rope_tables(pos_lo, pos_hi, invf, ncp)

    w_heads = jnp.concatenate(
        [w_in[0][:, 3 * D_DIL:3 * D_DIL + D_NSA], w_in[0][:, :3 * D_DIL],
         w_in[0][:, 3 * D_DIL + D_NSA:3 * D_DIL + D_NSA + 6 * KV_W]], axis=1).astype(BF16)
    g_off = 3 * D_DIL + D_NSA + 6 * KV_W
    per_g = NSA_GROUP * N_NSA_BRANCHES
    wg = jnp.zeros((d, NSA_KV_GROUPS * HEAD_DIM), F32)
    bg = jnp.zeros((1, NSA_KV_GROUPS * HEAD_DIM), F32)
    for g in range(NSA_KV_GROUPS):
        wg = wg.at[:, g * HEAD_DIM:g * HEAD_DIM + per_g].set(
            w_in[0][:, g_off + g * per_g:g_off + (g + 1) * per_g])
        bg = bg.at[0, g * HEAD_DIM:g * HEAD_DIM + per_g].set(gate_b[0][g * per_g:(g + 1) * per_g])
    wg = wg.astype(BF16)

    h0 = x[0]
    h1, h1b = _ffn_ln(h0, ffn1_w1[0].astype(BF16), ffn1_w3[0].astype(BF16),
                      ffn1_w2[0].astype(BF16), row(ln1_g[0]), row(ln1_b[0]), alpha, tm, tf)

    heads = _in_proj(h1b, w_heads, cosf, sinf, scale, tm)
    gates = _gates(h1b, wg, bg, tm)
    tok16 = heads.reshape(N_PROJ_HEADS, ncp, CMP_STRIDE * HEAD_DIM)
    kcv, kcv_t = _compress(
        tok16, cmp_pe[0].reshape(2, 1, CMP_BLOCK * HEAD_DIM), cmp_w1[0].astype(BF16),
        cmp_b1[0].reshape(2, 1, CMP_HIDDEN), cmp_w2[0].astype(BF16),
        cmp_b2[0].reshape(2, 1, HEAD_DIM), cos_c, sin_c)
    mix_a = _dilated(heads, 256)
    mix_b = _nsa(heads, kcv, kcv_t, gates, _overlap_t(s), 512)

    wo = w_out[0].astype(BF16)
    h2 = _out_proj_ln(mix_a, mix_b, wo[:D_DIL], wo[D_DIL:], h1, row(ln2_g[0]), row(ln2_b[0]),
                      alpha, tm)
    h3, _ = _ffn_ln(h2, ffn2_w1[0].astype(BF16), ffn2_w3[0].astype(BF16),
                    ffn2_w2[0].astype(BF16), row(ln3_g[0]), row(ln3_b[0]), alpha, tm, tf)
    return h3.reshape(bsz, s, d)
```

```python
import functools

import jax
import jax.numpy as jnp
import numpy as np
from jax import lax
from jax.experimental import pallas as pl
from jax.experimental.pallas import tpu as pltpu

HEAD_DIM = 128
N_HEADS_DIL = 6
N_HEADS_NSA = 10
NSA_KV_GROUPS = 2
NSA_GROUP = N_HEADS_NSA // NSA_KV_GROUPS
N_NSA_BRANCHES = 3
DIL_PAIRS = ((128, 1), (512, 4), (2048, 16))
CMP_BLOCK = 32
CMP_STRIDE = 16
CMP_HIDDEN = 256
SEL_BLOCK = 64
N_SELECT = 16
NSA_WINDOW = 512
ROPE_THETA = 10000.0
QBLK = 128
LN_EPS = 1e-5
NEG = -1e30
FORCE_BONUS = 1e4
SEL_OFF = -1e9
DEPTH = 1

D_DIL = N_HEADS_DIL * HEAD_DIM
D_NSA = N_HEADS_NSA * HEAD_DIM
KV_W = NSA_KV_GROUPS * HEAD_DIM
N_GATES = N_HEADS_NSA * N_NSA_BRANCHES

H_QA, H_KA, H_VA, H_QN = 0, 6, 12, 18
H_KC, H_VC, H_KS, H_VS, H_KW, H_VW = 28, 30, 32, 34, 36, 38
N_PROJ_HEADS = 40
HEADS_PER_PROJ_BLOCK = 4

VMEM_LIMIT_BYTES = 56 * 1024 * 1024

F32 = jnp.float32
BF16 = jnp.bfloat16


def _params(sem, vmem=VMEM_LIMIT_BYTES):
    return pltpu.CompilerParams(dimension_semantics=sem, vmem_limit_bytes=vmem)


def _nt_dot(a, b):
    return lax.dot_general(a, b, (((1,), (1,)), ((), ())), preferred_element_type=F32)


def _layer_norm(y, g, b):
    mu = jnp.mean(y, axis=-1, keepdims=True)
    yc = y - mu
    var = jnp.mean(yc * yc, axis=-1, keepdims=True)
    return yc * lax.rsqrt(var + LN_EPS) * g + b


def _rope_table_kernel(pa_ref, pb_ref, invf_ref, cos_ref, sin_ref):
    pos = (pa_ref[...] + pb_ref[...]) * 0.5
    ang = pos * invf_ref[...]
    lane = lax.broadcasted_iota(jnp.int32, ang.shape, 1)
    cos_ref[...] = jnp.cos(ang)
    sin_ref[...] = jnp.where(lane < HEAD_DIM // 2, -1.0, 1.0) * jnp.sin(ang)


def _rope_tables(pos_a, pos_b, invf, tile):
    n = pos_a.shape[0]
    spec_p = pl.BlockSpec((tile, 1), lambda i: (i, 0))
    spec_t = pl.BlockSpec((tile, HEAD_DIM), lambda i: (i, 0))
    return pl.pallas_call(
        _rope_table_kernel,
        grid=(n // tile,),
        in_specs=[spec_p, spec_p, pl.BlockSpec((1, HEAD_DIM), lambda i: (0, 0))],
        out_specs=[spec_t, spec_t],
        out_shape=[jax.ShapeDtypeStruct((n, HEAD_DIM), F32)] * 2,
        compiler_params=_params(("arbitrary",)),
        name="rope_tables",
    )(pos_a, pos_b, invf)


def _ffn_ln_kernel(h_ref, w1_ref, w3_ref, w2_ref, g_ref, b_ref, o_ref, ob_ref,
                   hb_ref, acc_ref, *, alpha, nf):
    f = pl.program_id(1)

    @pl.when(f == 0)
    def _():
        hb_ref[...] = h_ref[...].astype(BF16)
        acc_ref[...] = jnp.zeros_like(acc_ref)

    hb = hb_ref[...]
    a = jnp.dot(hb, w1_ref[...], preferred_element_type=F32)
    b = jnp.dot(hb, w3_ref[...], preferred_element_type=F32)
    act = (a * jax.nn.sigmoid(a)) * b
    acc_ref[...] += jnp.dot(act.astype(BF16), w2_ref[...], preferred_element_type=F32)

    @pl.when(f == nf - 1)
    def _():
        y = alpha * h_ref[...] + 0.5 * acc_ref[...]
        out = _layer_norm(y, g_ref[...], b_ref[...])
        o_ref[...] = out
        ob_ref[...] = out.astype(BF16)


def _ffn_ln(h, w1, w3, w2, g, b, alpha, tm, tf):
    s, d = h.shape
    dff = w1.shape[1]
    nf = dff // tf
    return pl.pallas_call(
        functools.partial(_ffn_ln_kernel, alpha=alpha, nf=nf),
        grid=(s // tm, nf),
        in_specs=[
            pl.BlockSpec((tm, d), lambda i, f: (i, 0)),
            pl.BlockSpec((d, tf), lambda i, f: (0, f)),
            pl.BlockSpec((d, tf), lambda i, f: (0, f)),
            pl.BlockSpec((tf, d), lambda i, f: (f, 0)),
            pl.BlockSpec((1, d), lambda i, f: (0, 0)),
            pl.BlockSpec((1, d), lambda i, f: (0, 0)),
        ],
        out_specs=[pl.BlockSpec((tm, d), lambda i, f: (i, 0)),
                   pl.BlockSpec((tm, d), lambda i, f: (i, 0))],
        out_shape=[jax.ShapeDtypeStruct((s, d), F32), jax.ShapeDtypeStruct((s, d), BF16)],
        scratch_shapes=[pltpu.VMEM((tm, d), BF16), pltpu.VMEM((tm, d), F32)],
        compiler_params=_params(("arbitrary", "arbitrary")),
        name="ffn_ln",
    )(h, w1, w3, w2, g, b)


def _in_proj_kernel(h_ref, w_ref, cos_ref, sin_ref, o_ref, *, scale):
    j = pl.program_id(1)
    hpb = HEADS_PER_PROJ_BLOCK
    acc = jnp.dot(h_ref[...], w_ref[...].astype(BF16), preferred_element_type=F32)
    cos = cos_ref[...]
    sin = sin_ref[...]
    for k in range(hpb):
        hh = j * hpb + k
        is_q = (hh < H_KA) | ((hh >= H_QN) & (hh < H_KC))
        rope = is_q | (hh < H_VA) | ((hh >= H_KS) & (hh < H_VS)) | ((hh >= H_KW) & (hh < H_VW))
        x = acc[:, k * HEAD_DIM:(k + 1) * HEAD_DIM]
        xr = x * cos + pltpu.roll(x, HEAD_DIM // 2, 1) * sin
        y = jnp.where(rope, xr, x) * jnp.where(is_q, scale, 1.0).astype(F32)
        o_ref[k] = y.astype(BF16)


def _in_proj(hb, w, cosf, sinf, scale, tm):
    s, d = hb.shape
    hpb = HEADS_PER_PROJ_BLOCK
    return pl.pallas_call(
        functools.partial(_in_proj_kernel, scale=scale),
        grid=(s // tm, N_PROJ_HEADS // hpb),
        in_specs=[
            pl.BlockSpec((tm, d), lambda i, j: (i, 0)),
            pl.BlockSpec((d, hpb * HEAD_DIM), lambda i, j: (0, j)),
            pl.BlockSpec((tm, HEAD_DIM), lambda i, j: (i, 0)),
            pl.BlockSpec((tm, HEAD_DIM), lambda i, j: (i, 0)),
        ],
        out_specs=pl.BlockSpec((hpb, tm, HEAD_DIM), lambda i, j: (j, i, 0)),
        out_shape=jax.ShapeDtypeStruct((N_PROJ_HEADS, s, HEAD_DIM), BF16),
        compiler_params=_params(("arbitrary", "arbitrary")),
        name="in_proj",
    )(hb, w, cosf, sinf)


def _gates_kernel(h_ref, w_ref, b_ref, o_ref):
    z = jnp.dot(h_ref[...], w_ref[...], preferred_element_type=F32) + b_ref[...]
    o_ref[...] = jax.nn.sigmoid(z)


def _gates(hb, wg, bg, tm):
    s, d = hb.shape
    n = wg.shape[1]
    return pl.pallas_call(
        _gates_kernel,
        grid=(s // tm,),
        in_specs=[pl.BlockSpec((tm, d), lambda i: (i, 0)),
                  pl.BlockSpec((d, n), lambda i: (0, 0)),
                  pl.BlockSpec((1, n), lambda i: (0, 0))],
        out_specs=pl.BlockSpec((tm, n), lambda i: (i, 0)),
        out_shape=jax.ShapeDtypeStruct((s, n), F32),
        compiler_params=_params(("arbitrary",)),
        name="nsa_gates",
    )(hb, wg, bg)


def _compress_kernel(tok_ref, pe_ref, w1_ref, b1_ref, w2_ref, b2_ref, cos_ref, sin_ref,
                     o_ref, ot_ref):
    j = pl.program_id(0)
    half = CMP_STRIDE * HEAD_DIM
    tok = tok_ref[0].astype(F32)
    pe = pe_ref[0]
    w1 = w1_ref[0]
    top = (tok + pe[:, :half]).astype(BF16)
    bot = (tok + pe[:, half:]).astype(BF16)
    u = jnp.dot(top, w1[:half], preferred_element_type=F32)
    v = jnp.dot(bot, w1[half:], preferred_element_type=F32)
    nc = u.shape[0]
    hid = u + pltpu.roll(v, nc - 1, 0) + b1_ref[0]
    hid = jax.nn.gelu(hid)
    out = jnp.dot(hid.astype(BF16), w2_ref[0], preferred_element_type=F32) + b2_ref[0]
    roped = out * cos_ref[...] + pltpu.roll(out, HEAD_DIM // 2, 1) * sin_ref[...]
    out = jnp.where(j == 0, roped, out)
    o_ref[0, 0] = out.astype(BF16)
    ot_ref[0, 0] = out.T.astype(BF16)


def _compress(tok16, pe, w1, b1, w2, b2, cos_c, sin_c):
    nc = tok16.shape[1]
    blk = CMP_BLOCK * HEAD_DIM
    g = NSA_KV_GROUPS
    return pl.pallas_call(
        _compress_kernel,
        grid=(2, g),
        in_specs=[
            pl.BlockSpec((1, nc, CMP_STRIDE * HEAD_DIM), lambda j, gi: (g * j + gi, 0, 0)),
            pl.BlockSpec((1, 1, blk), lambda j, gi: (j, 0, 0)),
            pl.BlockSpec((1, blk, CMP_HIDDEN), lambda j, gi: (j, 0, 0)),
            pl.BlockSpec((1, 1, CMP_HIDDEN), lambda j, gi: (j, 0, 0)),
            pl.BlockSpec((1, CMP_HIDDEN, HEAD_DIM), lambda j, gi: (j, 0, 0)),
            pl.BlockSpec((1, 1, HEAD_DIM), lambda j, gi: (j, 0, 0)),
            pl.BlockSpec((nc, HEAD_DIM), lambda j, gi: (0, 0)),
            pl.BlockSpec((nc, HEAD_DIM), lambda j, gi: (0, 0)),
        ],
        out_specs=[pl.BlockSpec((1, 1, nc, HEAD_DIM), lambda j, gi: (j, gi, 0, 0)),
                   pl.BlockSpec((1, 1, HEAD_DIM, nc), lambda j, gi: (j, gi, 0, 0))],
        out_shape=[jax.ShapeDtypeStruct((2, g, nc, HEAD_DIM), BF16),
                   jax.ShapeDtypeStruct((2, g, HEAD_DIM, nc), BF16)],
        compiler_params=_params(("arbitrary", "arbitrary")),
        name="nsa_compress",
    )(tok16, pe, w1, b1, w2, b2, cos_c, sin_c)


def _dil_kernel(q_ref, k_ref, v_ref, o_ref, kpad, vpad, bias_ref, *, tq, padk):
    h = pl.program_id(0)
    i = pl.program_id(1)
    w = padk + tq

    @pl.when((h == 0) & (i == 0))
    def _():
        r = lax.broadcasted_iota(jnp.int32, (tq, w), 0)
        c = lax.broadcasted_iota(jnp.int32, (tq, w), 1)
        d = r + padk - c
        cnt = jnp.zeros((tq, w), F32)
        for window, dil in DIL_PAIRS:
            hit = (d >= 0) & (d <= window) & ((d & (dil - 1)) == 0)
            cnt = cnt + jnp.where(hit, 1.0, 0.0)
        bias_ref[...] = jnp.where(cnt > 0.0, jnp.log(jnp.maximum(cnt, 1.0)), NEG)

    @pl.when(i == 0)
    def _():
        kpad[0:padk, :] = jnp.zeros((padk, HEAD_DIM), BF16)
        vpad[0:padk, :] = jnp.zeros((padk, HEAD_DIM), BF16)
        kpad[padk:, :] = k_ref[0]
        vpad[padk:, :] = v_ref[0]

    q0 = pl.multiple_of(i * tq, tq)
    s = _nt_dot(q_ref[0], kpad[pl.ds(q0, w), :]) + bias_ref[...]
    col = lax.broadcasted_iota(jnp.int32, (1, w), 1)
    s = jnp.where(col >= padk - q0, s, NEG)
    m = jnp.max(s, axis=-1, keepdims=True)
    p = jnp.exp(s - m)
    l = jnp.sum(p, axis=-1, keepdims=True)
    o = jnp.dot(p.astype(BF16), vpad[pl.ds(q0, w), :], preferred_element_type=F32)
    o_ref[...] = (o / l).astype(BF16)


def _dilated(heads, tq):
    _, s, _ = heads.shape
    padk = max(wd for wd, _ in DIL_PAIRS)
    return pl.pallas_call(
        functools.partial(_dil_kernel, tq=tq, padk=padk),
        grid=(N_HEADS_DIL, s // tq),
        in_specs=[
            pl.BlockSpec((1, tq, HEAD_DIM), lambda h, i: (H_QA + h, i, 0)),
            pl.BlockSpec((1, s, HEAD_DIM), lambda h, i: (H_KA + h, 0, 0)),
            pl.BlockSpec((1, s, HEAD_DIM), lambda h, i: (H_VA + h, 0, 0)),
        ],
        out_specs=pl.BlockSpec((tq, HEAD_DIM), lambda h, i: (i, h)),
        out_shape=jax.ShapeDtypeStruct((s, D_DIL), BF16),
        scratch_shapes=[pltpu.VMEM((padk + s, HEAD_DIM), BF16),
                        pltpu.VMEM((padk + s, HEAD_DIM), BF16),
                        pltpu.VMEM((tq, padk + tq), F32)],
        compiler_params=_params(("arbitrary", "arbitrary")),
        name="dilated_attn",
    )(heads, heads, heads)


def _nsa_kernel(*refs, s_len, nsp, kt, k_sel):
    q_refs = refs[:NSA_GROUP]
    (kc_ref, vct_ref, ks_ref, vs_ref, kw_ref, vw_ref, gate_ref, ovt_ref,
     o_ref, kaug) = refs[NSA_GROUP:]
    qb = pl.program_id(1)
    t0 = qb * QBLK
    hg = NSA_GROUP
    rows = hg * QBLK
    ncp = kc_ref.shape[2]

    @pl.when(qb == 0)
    def _():
        kaug[:, 0:HEAD_DIM] = ks_ref[0]
        key = lax.broadcasted_iota(jnp.int32, (s_len, nsp), 0)
        blk = lax.broadcasted_iota(jnp.int32, (s_len, nsp), 1)
        kaug[:, HEAD_DIM:] = jnp.where(blk == key // SEL_BLOCK, 1.0, 0.0).astype(BF16)

    kc = kc_ref[0, 0]
    vct = vct_ref[0, 0]
    n_io = lax.broadcasted_iota(jnp.int32, (ncp, QBLK), 0)
    t_io = t0 + lax.broadcasted_iota(jnp.int32, (ncp, QBLK), 1)
    cmask = (n_io * CMP_STRIDE + (CMP_BLOCK - 1) <= t_io) & (n_io < ncp - 1)
    psum = jnp.zeros((ncp, QBLK), F32)
    o_cmp = []
    for u in range(hg):
        st = jnp.where(cmask, _nt_dot(kc, q_refs[u][0]), NEG)
        m = jnp.max(st, axis=0, keepdims=True)
        e = jnp.where(cmask, jnp.exp(st - m), 0.0)
        pt = e / jnp.maximum(jnp.sum(e, axis=0, keepdims=True), 1e-30)
        psum = psum + pt
        o_cmp.append(jnp.dot(vct, pt.astype(BF16), preferred_element_type=F32).T)

    p_hi = psum.astype(BF16)
    p_lo = (psum - p_hi.astype(F32)).astype(BF16)
    ovt = ovt_ref[...]
    imp = (jnp.dot(ovt, p_hi, preferred_element_type=F32)
           + jnp.dot(ovt, p_lo, preferred_element_type=F32))
    j_io = lax.broadcasted_iota(jnp.int32, (nsp, QBLK), 0)
    t_sel = t0 + lax.broadcasted_iota(jnp.int32, (nsp, QBLK), 1)
    valid = j_io * SEL_BLOCK <= t_sel
    cur = t_sel // SEL_BLOCK
    forced = (j_io == 0) | (j_io == cur) | (j_io == cur - 1)
    score = jnp.where(valid, imp + jnp.where(forced, FORCE_BONUS, 0.0), NEG)
    picked = jnp.zeros((nsp, QBLK), F32)
    for _ in range(k_sel):
        m = jnp.max(score, axis=0, keepdims=True)
        first = jnp.min(jnp.where(score == m, j_io, nsp), axis=0, keepdims=True)
        hit = j_io == first
        picked = jnp.where(hit, 1.0, picked)
        score = jnp.where(hit, -jnp.inf, score)
    sel_t = jnp.where(valid, picked, 0.0)
    sel_off = jnp.where(sel_t.T > 0.0, 0.0, SEL_OFF).astype(BF16)
    q5 = jnp.concatenate([r[0] for r in q_refs], axis=0)
    qaug = jnp.concatenate([q5, jnp.concatenate([sel_off] * hg, axis=0)], axis=1)

    def sel_tile(kti, carry, causal):
        m_i, l_i, acc = carry
        k0 = pl.multiple_of(kti * kt, kt)
        s = _nt_dot(qaug, kaug[pl.ds(k0, kt), :])
        if causal:
            kpos = k0 + lax.broadcasted_iota(jnp.int32, (rows, kt), 1)
            tq = t0 + (lax.broadcasted_iota(jnp.int32, (rows, kt), 0) & (QBLK - 1))
            s = jnp.where(kpos <= tq, s, NEG)
        m_new = jnp.maximum(m_i, jnp.max(s, axis=-1, keepdims=True))
        p = jnp.exp(s - m_new)
        a = jnp.exp(m_i - m_new)
        l_new = a * l_i + jnp.sum(p, axis=-1, keepdims=True)
        pv = jnp.dot(p.astype(BF16), vs_ref[0, pl.ds(k0, kt), :], preferred_element_type=F32)
        return m_new, l_new, a * acc + pv

    n_kt = (t0 + QBLK + kt - 1) // kt
    init = (jnp.full((rows, 1), NEG, F32), jnp.zeros((rows, 1), F32),
            jnp.zeros((rows, HEAD_DIM), F32))
    carry = lax.fori_loop(0, n_kt - 1, lambda kti, c: sel_tile(kti, c, False), init)
    _, l_s, acc_s = sel_tile(n_kt - 1, carry, True)
    o_slc = acc_s / l_s

    ww = NSA_WINDOW + QBLK
    w0 = pl.multiple_of(jnp.maximum(t0 - NSA_WINDOW, 0), QBLK)
    sw = _nt_dot(q5, kw_ref[0, pl.ds(w0, ww), :])
    kpos = w0 + lax.broadcasted_iota(jnp.int32, (rows, ww), 1)
    dist = t0 + (lax.broadcasted_iota(jnp.int32, (rows, ww), 0) & (QBLK - 1)) - kpos
    wmask = (dist >= 0) & (dist < NSA_WINDOW)
    sw = jnp.where(wmask, sw, NEG)
    mw = jnp.max(sw, axis=-1, keepdims=True)
    pw = jnp.exp(sw - mw)
    lw = jnp.sum(pw, axis=-1, keepdims=True)
    o_win = jnp.dot(pw.astype(BF16), vw_ref[0, pl.ds(w0, ww), :],
                    preferred_element_type=F32) / lw

    gate = gate_ref[...]
    for u in range(hg):
        c = N_NSA_BRANCHES * u
        r = slice(u * QBLK, (u + 1) * QBLK)
        out = (gate[:, c:c + 1] * o_cmp[u] + gate[:, c + 1:c + 2] * o_slc[r]
               + gate[:, c + 2:c + 3] * o_win[r])
        o_ref[:, u * HEAD_DIM:(u + 1) * HEAD_DIM] = out.astype(BF16)


def _nsa(heads, kcv, vct, gates, ovt, kt):
    _, s, _ = heads.shape
    ncp = kcv.shape[2]
    nsp = ovt.shape[0]
    hg = NSA_GROUP
    k_sel = min(N_SELECT, s // SEL_BLOCK)
    q_map = lambda u, g, qb: (H_QN + hg * g + u, qb, 0)
    full = lambda hbase: pl.BlockSpec((1, s, HEAD_DIM), lambda g, qb: (hbase + g, 0, 0))
    return pl.pallas_call(
        functools.partial(_nsa_kernel, s_len=s, nsp=nsp, kt=kt, k_sel=k_sel),
        grid=(NSA_KV_GROUPS, s // QBLK),
        in_specs=[
            *[pl.BlockSpec((1, QBLK, HEAD_DIM), functools.partial(q_map, u)) for u in range(hg)],
            pl.BlockSpec((1, 1, ncp, HEAD_DIM), lambda g, qb: (0, g, 0, 0)),
            pl.BlockSpec((1, 1, HEAD_DIM, ncp), lambda g, qb: (1, g, 0, 0)),
            full(H_KS), full(H_VS), full(H_KW), full(H_VW),
            pl.BlockSpec((QBLK, HEAD_DIM), lambda g, qb: (qb, g)),
            pl.BlockSpec((nsp, ncp), lambda g, qb: (0, 0)),
        ],
        out_specs=pl.BlockSpec((QBLK, hg * HEAD_DIM), lambda g, qb: (qb, g)),
        out_shape=jax.ShapeDtypeStruct((s, D_NSA), BF16),
        scratch_shapes=[pltpu.VMEM((s, HEAD_DIM + nsp), BF16)],
        compiler_params=_params(("arbitrary", "arbitrary")),
        name="nsa_attn",
    )(*([heads] * hg), kcv, vct, heads, heads, heads, heads, gates, ovt)


def _out_proj_ln_kernel(a_ref, b_ref, wa_ref, wb_ref, h_ref, g_ref, be_ref, o_ref, *, alpha):
    mix = (jnp.dot(a_ref[...], wa_ref[...], preferred_element_type=F32)
           + jnp.dot(b_ref[...], wb_ref[...], preferred_element_type=F32))
    o_ref[...] = _layer_norm(alpha * h_ref[...] + mix, g_ref[...], be_ref[...])


def _out_proj_ln(mix_a, mix_b, wa, wb, h, g, b, alpha, tm):
    s, d = h.shape
    return pl.pallas_call(
        functools.partial(_out_proj_ln_kernel, alpha=alpha),
        grid=(s // tm,),
        in_specs=[
            pl.BlockSpec((tm, D_DIL), lambda i: (i, 0)),
            pl.BlockSpec((tm, D_NSA), lambda i: (i, 0)),
            pl.BlockSpec((D_DIL, d), lambda i: (0, 0)),
            pl.BlockSpec((D_NSA, d), lambda i: (0, 0)),
            pl.BlockSpec((tm, d), lambda i: (i, 0)),
            pl.BlockSpec((1, d), lambda i: (0, 0)),
            pl.BlockSpec((1, d), lambda i: (0, 0)),
        ],
        out_specs=pl.BlockSpec((tm, d), lambda i: (i, 0)),
        out_shape=jax.ShapeDtypeStruct((s, d), F32),
        compiler_params=_params(("arbitrary",)),
        name="out_proj_ln",
    )(mix_a, mix_b, wa, wb, h, g, b)


def _overlap_t(s):
    n_cmp = (s - CMP_BLOCK) // CMP_STRIDE + 1
    n_sel = s // SEL_BLOCK
    ncp = s // CMP_STRIDE
    nsp = -(-n_sel // HEAD_DIM) * HEAD_DIM
    c_lo = np.arange(n_cmp) * CMP_STRIDE
    c_hi = c_lo + CMP_BLOCK - 1
    s_lo = (np.arange(n_sel) * SEL_BLOCK)[:, None]
    ov = np.zeros((nsp, ncp), np.float32)
    ov[:n_sel, :n_cmp] = (c_lo[None, :] <= s_lo + SEL_BLOCK - 1) & (c_hi[None, :] >= s_lo)
    return jnp.asarray(ov, BF16)


def kernel(x, positions, ln1_g, ln1_b, ffn1_w1, ffn1_w3, ffn1_w2, w_in, gate_b, cmp_pe, cmp_w1,
           cmp_b1, cmp_w2, cmp_b2, w_out, ln2_g, ln2_b, ffn2_w1, ffn2_w3, ffn2_w2, ln3_g, ln3_b):
    bsz, s, d = x.shape
    assert bsz == 1 and d == (N_HEADS_DIL + N_HEADS_NSA) * HEAD_DIM
    assert s % 512 == 0 and s >= NSA_WINDOW + QBLK
    alpha = (2.0 * DEPTH) ** 0.25
    scale = HEAD_DIM ** -0.5
    row = lambda v: v.reshape(1, -1)
    tm = 512
    tf = 512 if ffn1_w1.shape[2] % 512 == 0 else 256

    inv_freq = ROPE_THETA ** (-jnp.arange(0, HEAD_DIM, 2, dtype=F32) / HEAD_DIM)
    invf = jnp.concatenate([inv_freq, inv_freq]).reshape(1, HEAD_DIM)
    pos = positions[0].astype(F32).reshape(s, 1)
    cosf, sinf = _rope_tables(pos, pos, invf, 512)
    ncp = s // CMP_STRIDE
    n_cmp = (s - CMP_BLOCK) // CMP_STRIDE + 1
    pos_lo = jnp.pad(pos[0:n_cmp * CMP_STRIDE:CMP_STRIDE], ((0, ncp - n_cmp), (0, 0)))
    pos_hi = jnp.pad(pos[CMP_BLOCK - 1::CMP_STRIDE][:n_cmp], ((0, ncp - n_cmp), (0, 0)))
    cos_c, sin_c = _rope_tables(pos_lo, pos_hi, invf, ncp)

    g_off = 3 * D_DIL + D_NSA + 6 * KV_W
    per_g = NSA_GROUP * N_NSA_BRANCHES
    wg = jnp.zeros((d, NSA_KV_GROUPS * HEAD_DIM), F32)
    bg = jnp.zeros((1, NSA_KV_GROUPS * HEAD_DIM), F32)
    for g in range(NSA_KV_GROUPS):
        wg = wg.at[:, g * HEAD_DIM:g * HEAD_DIM + per_g].set(
            w_in[0][:, g_off + g * per_g:g_off + (g + 1) * per_g])
        bg = bg.at[0, g * HEAD_DIM:g * HEAD_DIM + per_g].set(gate_b[0][g * per_g:(g + 1) * per_g])
    wg = wg.astype(BF16)

    h0 = x[0]
    h1, h1b = _ffn_ln(h0, ffn1_w1[0].astype(BF16), ffn1_w3[0].astype(BF16),
                      ffn1_w2[0].astype(BF16), row(ln1_g[0]), row(ln1_b[0]), alpha, tm, tf)

    heads = _in_proj(h1b, w_in[0], cosf, sinf, scale, 1024)
    gates = _gates(h1b, wg, bg, tm)
    tok16 = heads[H_KC:H_KS].reshape(2 * NSA_KV_GROUPS, ncp, CMP_STRIDE * HEAD_DIM)
    kcv, kcv_t = _compress(
        tok16, cmp_pe[0].reshape(2, 1, CMP_BLOCK * HEAD_DIM), cmp_w1[0].astype(BF16),
        cmp_b1[0].reshape(2, 1, CMP_HIDDEN), cmp_w2[0].astype(BF16),
        cmp_b2[0].reshape(2, 1, HEAD_DIM), cos_c, sin_c)
    mix_a = _dilated(heads, 256)
    mix_b = _nsa(heads, kcv, kcv_t, gates, _overlap_t(s), 512)

    wo = w_out[0].astype(BF16)
    h2 = _out_proj_ln(mix_a, mix_b, wo[:D_DIL], wo[D_DIL:], h1, row(ln2_g[0]), row(ln2_b[0]),
                      alpha, tm)
    h3, _ = _ffn_ln(h2, ffn2_w1[0].astype(BF16), ffn2_w3[0].astype(BF16),
                    ffn2_w2[0].astype(BF16), row(ln3_g[0]), row(ln3_b[0]), alpha, tm, tf)
    return h3.reshape(bsz, s, d)
```

```python
import functools

import jax
import jax.numpy as jnp
import numpy as np
from jax import lax
from jax.experimental import pallas as pl
from jax.experimental.pallas import tpu as pltpu

HEAD_DIM = 128
N_HEADS_DIL = 6
N_HEADS_NSA = 10
NSA_KV_GROUPS = 2
NSA_GROUP = N_HEADS_NSA // NSA_KV_GROUPS
N_NSA_BRANCHES = 3
DIL_PAIRS = ((128, 1), (512, 4), (2048, 16))
CMP_BLOCK = 32
CMP_STRIDE = 16
CMP_HIDDEN = 256
SEL_BLOCK = 64
N_SELECT = 16
NSA_WINDOW = 512
ROPE_THETA = 10000.0
QBLK = 128
LN_EPS = 1e-5
NEG = -1e30
FORCE_BONUS = 1e4
SEL_OFF = -1e9
DEPTH = 1

D_DIL = N_HEADS_DIL * HEAD_DIM
D_NSA = N_HEADS_NSA * HEAD_DIM
KV_W = NSA_KV_GROUPS * HEAD_DIM
N_GATES = N_HEADS_NSA * N_NSA_BRANCHES

H_QA, H_KA, H_VA, H_QN = 0, 6, 12, 18
H_KC, H_VC, H_KS, H_VS, H_KW, H_VW = 28, 30, 32, 34, 36, 38
N_PROJ_HEADS = 40
HEADS_PER_PROJ_BLOCK = 4

VMEM_LIMIT_BYTES = 56 * 1024 * 1024

F32 = jnp.float32
BF16 = jnp.bfloat16


def _params(sem, vmem=VMEM_LIMIT_BYTES):
    return pltpu.CompilerParams(dimension_semantics=sem, vmem_limit_bytes=vmem)


def _nt_dot(a, b):
    return lax.dot_general(a, b, (((1,), (1,)), ((), ())), preferred_element_type=F32)


def _layer_norm(y, g, b):
    mu = jnp.mean(y, axis=-1, keepdims=True)
    yc = y - mu
    var = jnp.mean(yc * yc, axis=-1, keepdims=True)
    return yc * lax.rsqrt(var + LN_EPS) * g + b


def _rope_table_kernel(pa_ref, pb_ref, invf_ref, cos_ref, sin_ref):
    pos = (pa_ref[...] + pb_ref[...]) * 0.5
    ang = pos * invf_ref[...]
    lane = lax.broadcasted_iota(jnp.int32, ang.shape, 1)
    cos_ref[...] = jnp.cos(ang)
    sin_ref[...] = jnp.where(lane < HEAD_DIM // 2, -1.0, 1.0) * jnp.sin(ang)


def _rope_tables(pos_a, pos_b, invf, tile):
    n = pos_a.shape[0]
    spec_p = pl.BlockSpec((tile, 1), lambda i: (i, 0))
    spec_t = pl.BlockSpec((tile, HEAD_DIM), lambda i: (i, 0))
    return pl.pallas_call(
        _rope_table_kernel,
        grid=(n // tile,),
        in_specs=[spec_p, spec_p, pl.BlockSpec((1, HEAD_DIM), lambda i: (0, 0))],
        out_specs=[spec_t, spec_t],
        out_shape=[jax.ShapeDtypeStruct((n, HEAD_DIM), F32)] * 2,
        compiler_params=_params(("arbitrary",)),
        name="rope_tables",
    )(pos_a, pos_b, invf)


def _ffn_ln_kernel(h_ref, w1_ref, w3_ref, w2_ref, g_ref, b_ref, o_ref, ob_ref,
                   hb_ref, acc_ref, *, alpha, nf):
    f = pl.program_id(1)

    @pl.when(f == 0)
    def _():
        hb_ref[...] = h_ref[...].astype(BF16)
        acc_ref[...] = jnp.zeros_like(acc_ref)

    hb = hb_ref[...]
    a = jnp.dot(hb, w1_ref[...], preferred_element_type=F32)
    b = jnp.dot(hb, w3_ref[...], preferred_element_type=F32)
    act = (a * jax.nn.sigmoid(a)) * b
    acc_ref[...] += jnp.dot(act.astype(BF16), w2_ref[...], preferred_element_type=F32)

    @pl.when(f == nf - 1)
    def _():
        y = alpha * h_ref[...] + 0.5 * acc_ref[...]
        out = _layer_norm(y, g_ref[...], b_ref[...])
        o_ref[...] = out
        ob_ref[...] = out.astype(BF16)


def _ffn_ln(h, w1, w3, w2, g, b, alpha, tm, tf):
    s, d = h.shape
    dff = w1.shape[1]
    nf = dff // tf
    return pl.pallas_call(
        functools.partial(_ffn_ln_kernel, alpha=alpha, nf=nf),
        grid=(s // tm, nf),
        in_specs=[
            pl.BlockSpec((tm, d), lambda i, f: (i, 0)),
            pl.BlockSpec((d, tf), lambda i, f: (0, f)),
            pl.BlockSpec((d, tf), lambda i, f: (0, f)),
            pl.BlockSpec((tf, d), lambda i, f: (f, 0)),
            pl.BlockSpec((1, d), lambda i, f: (0, 0)),
            pl.BlockSpec((1, d), lambda i, f: (0, 0)),
        ],
        out_specs=[pl.BlockSpec((tm, d), lambda i, f: (i, 0)),
                   pl.BlockSpec((tm, d), lambda i, f: (i, 0))],
        out_shape=[jax.ShapeDtypeStruct((s, d), F32), jax.ShapeDtypeStruct((s, d), BF16)],
        scratch_shapes=[pltpu.VMEM((tm, d), BF16), pltpu.VMEM((tm, d), F32)],
        compiler_params=_params(("arbitrary", "arbitrary")),
        name="ffn_ln",
    )(h, w1, w3, w2, g, b)


def _in_proj_kernel(h_ref, w_ref, cos_ref, sin_ref, o_ref, *, scale):
    j = pl.program_id(1)
    hpb = HEADS_PER_PROJ_BLOCK
    acc = _nt_dot(h_ref[...], w_ref[...].astype(BF16))
    cos = cos_ref[...]
    sin = sin_ref[...]
    for k in range(hpb):
        hh = j * hpb + k
        is_q = (hh < H_KA) | ((hh >= H_QN) & (hh < H_KC))
        rope = is_q | (hh < H_VA) | ((hh >= H_KS) & (hh < H_VS)) | ((hh >= H_KW) & (hh < H_VW))
        x = acc[:, k * HEAD_DIM:(k + 1) * HEAD_DIM]
        xr = x * cos + pltpu.roll(x, HEAD_DIM // 2, 1) * sin
        y = jnp.where(rope, xr, x) * jnp.where(is_q, scale, 1.0).astype(F32)
        o_ref[k] = y.astype(BF16)


def _in_proj(hb, w, cosf, sinf, scale, tm):
    s, d = hb.shape
    hpb = HEADS_PER_PROJ_BLOCK
    return pl.pallas_call(
        functools.partial(_in_proj_kernel, scale=scale),
        grid=(s // tm, N_PROJ_HEADS // hpb),
        in_specs=[
            pl.BlockSpec((tm, d), lambda i, j: (i, 0)),
            pl.BlockSpec((hpb * HEAD_DIM, d), lambda i, j: (j, 0)),
            pl.BlockSpec((tm, HEAD_DIM), lambda i, j: (i, 0)),
            pl.BlockSpec((tm, HEAD_DIM), lambda i, j: (i, 0)),
        ],
        out_specs=pl.BlockSpec((hpb, tm, HEAD_DIM), lambda i, j: (j, i, 0)),
        out_shape=jax.ShapeDtypeStruct((N_PROJ_HEADS, s, HEAD_DIM), BF16),
        compiler_params=_params(("arbitrary", "arbitrary")),
        name="in_proj",
    )(hb, w, cosf, sinf)


def _gates_kernel(h_ref, w_ref, b_ref, o_ref):
    z = _nt_dot(h_ref[...], w_ref[...]) + b_ref[...]
    o_ref[...] = jax.nn.sigmoid(z)


def _gates(hb, wg, bg, tm):
    s, d = hb.shape
    n = wg.shape[0]
    return pl.pallas_call(
        _gates_kernel,
        grid=(s // tm,),
        in_specs=[pl.BlockSpec((tm, d), lambda i: (i, 0)),
                  pl.BlockSpec((n, d), lambda i: (0, 0)),
                  pl.BlockSpec((1, n), lambda i: (0, 0))],
        out_specs=pl.BlockSpec((tm, n), lambda i: (i, 0)),
        out_shape=jax.ShapeDtypeStruct((s, n), F32),
        compiler_params=_params(("arbitrary",)),
        name="nsa_gates",
    )(hb, wg, bg)


def _compress_kernel(tok_ref, pe_ref, w1_ref, b1_ref, w2_ref, b2_ref, cos_ref, sin_ref,
                     o_ref, ot_ref):
    j = pl.program_id(0)
    half = CMP_STRIDE * HEAD_DIM
    tok = tok_ref[0].astype(F32)
    pe = pe_ref[0]
    w1 = w1_ref[0]
    top = (tok + pe[:, :half]).astype(BF16)
    bot = (tok + pe[:, half:]).astype(BF16)
    u = jnp.dot(top, w1[:half], preferred_element_type=F32)
    v = jnp.dot(bot, w1[half:], preferred_element_type=F32)
    nc = u.shape[0]
    hid = u + pltpu.roll(v, nc - 1, 0) + b1_ref[0]
    hid = jax.nn.gelu(hid)
    out = jnp.dot(hid.astype(BF16), w2_ref[0], preferred_element_type=F32) + b2_ref[0]
    roped = out * cos_ref[...] + pltpu.roll(out, HEAD_DIM // 2, 1) * sin_ref[...]
    out = jnp.where(j == 0, roped, out)
    o_ref[0, 0] = out.astype(BF16)
    ot_ref[0, 0] = out.T.astype(BF16)


def _compress(tok16, pe, w1, b1, w2, b2, cos_c, sin_c):
    nc = tok16.shape[1]
    blk = CMP_BLOCK * HEAD_DIM
    g = NSA_KV_GROUPS
    return pl.pallas_call(
        _compress_kernel,
        grid=(2, g),
        in_specs=[
            pl.BlockSpec((1, nc, CMP_STRIDE * HEAD_DIM), lambda j, gi: (g * j + gi, 0, 0)),
            pl.BlockSpec((1, 1, blk), lambda j, gi: (j, 0, 0)),
            pl.BlockSpec((1, blk, CMP_HIDDEN), lambda j, gi: (j, 0, 0)),
            pl.BlockSpec((1, 1, CMP_HIDDEN), lambda j, gi: (j, 0, 0)),
            pl.BlockSpec((1, CMP_HIDDEN, HEAD_DIM), lambda j, gi: (j, 0, 0)),
            pl.BlockSpec((1, 1, HEAD_DIM), lambda j, gi: (j, 0, 0)),
            pl.BlockSpec((nc, HEAD_DIM), lambda j, gi: (0, 0)),
            pl.BlockSpec((nc, HEAD_DIM), lambda j, gi: (0, 0)),
        ],
        out_specs=[pl.BlockSpec((1, 1, nc, HEAD_DIM), lambda j, gi: (j, gi, 0, 0)),
                   pl.BlockSpec((1, 1, HEAD_DIM, nc), lambda j, gi: (j, gi, 0, 0))],
        out_shape=[jax.ShapeDtypeStruct((2, g, nc, HEAD_DIM), BF16),
                   jax.ShapeDtypeStruct((2, g, HEAD_DIM, nc), BF16)],
        compiler_params=_params(("arbitrary", "arbitrary")),
        name="nsa_compress",
    )(tok16, pe, w1, b1, w2, b2, cos_c, sin_c)


def _dil_kernel(q_ref, k_ref, v_ref, o_ref, kpad, vpad, bias_ref, *, tq, padk):
    h = pl.program_id(0)
    i = pl.program_id(1)
    w = padk + tq

    @pl.when((h == 0) & (i == 0))
    def _():
        r = lax.broadcasted_iota(jnp.int32, (tq, w), 0)
        c = lax.broadcasted_iota(jnp.int32, (tq, w), 1)
        d = r + padk - c
        cnt = jnp.zeros((tq, w), F32)
        for window, dil in DIL_PAIRS:
            hit = (d >= 0) & (d <= window) & ((d & (dil - 1)) == 0)
            cnt = cnt + jnp.where(hit, 1.0, 0.0)
        bias_ref[...] = jnp.where(cnt > 0.0, jnp.log(jnp.maximum(cnt, 1.0)), NEG)

    @pl.when(i == 0)
    def _():
        kpad[0:padk, :] = jnp.zeros((padk, HEAD_DIM), BF16)
        vpad[0:padk, :] = jnp.zeros((padk, 2 * HEAD_DIM), BF16)
        kpad[padk:, :] = k_ref[0]
        vpad[padk:, 0:HEAD_DIM] = v_ref[0]
        vpad[padk:, HEAD_DIM:] = jnp.ones((vpad.shape[0] - padk, HEAD_DIM), BF16)

    q0 = pl.multiple_of(i * tq, tq)
    s = _nt_dot(q_ref[0], kpad[pl.ds(q0, w), :]) + bias_ref[...]
    col = lax.broadcasted_iota(jnp.int32, (1, w), 1)
    s = jnp.where(col >= padk - q0, s, NEG)
    p = jnp.exp(s - jnp.max(s, axis=-1, keepdims=True))
    acc = jnp.dot(p.astype(BF16), vpad[pl.ds(q0, w), :], preferred_element_type=F32)
    o_ref[...] = (acc[:, :HEAD_DIM] / acc[:, HEAD_DIM:]).astype(BF16)


def _dilated(heads, tq):
    _, s, _ = heads.shape
    padk = max(wd for wd, _ in DIL_PAIRS)
    return pl.pallas_call(
        functools.partial(_dil_kernel, tq=tq, padk=padk),
        grid=(N_HEADS_DIL, s // tq),
        in_specs=[
            pl.BlockSpec((1, tq, HEAD_DIM), lambda h, i: (H_QA + h, i, 0)),
            pl.BlockSpec((1, s, HEAD_DIM), lambda h, i: (H_KA + h, 0, 0)),
            pl.BlockSpec((1, s, HEAD_DIM), lambda h, i: (H_VA + h, 0, 0)),
        ],
        out_specs=pl.BlockSpec((tq, HEAD_DIM), lambda h, i: (i, h)),
        out_shape=jax.ShapeDtypeStruct((s, D_DIL), BF16),
        scratch_shapes=[pltpu.VMEM((padk + s, HEAD_DIM), BF16),
                        pltpu.VMEM((padk + s, 2 * HEAD_DIM), BF16),
                        pltpu.VMEM((tq, padk + tq), F32)],
        compiler_params=_params(("arbitrary", "arbitrary")),
        name="dilated_attn",
    )(heads, heads, heads)


def _add_per_head(s, bias, hg):
    return jnp.concatenate([s[u * QBLK:(u + 1) * QBLK] + bias for u in range(hg)], axis=0)


def _nsa_kernel(*refs, s_len, nsp, kt, k_sel):
    q_refs = refs[:NSA_GROUP]
    (kc_ref, vct_ref, ks_ref, vs_ref, kw_ref, vw_ref, gate_ref, ovt_ref,
     o_ref, kaug, vaug, vwaug, cbias, wbias, s_a, s_b, m_ref, acc_ref) = refs[NSA_GROUP:]
    g = pl.program_id(0)
    qb = pl.program_id(1)
    t0 = qb * QBLK
    hg = NSA_GROUP
    rows = hg * QBLK
    ncp = kc_ref.shape[2]
    ww = NSA_WINDOW + QBLK

    @pl.when((g == 0) & (qb == 0))
    def _():
        for r4 in range(kt // QBLK):
            d = (lax.broadcasted_iota(jnp.int32, (QBLK, kt), 0) + r4 * QBLK
                 - lax.broadcasted_iota(jnp.int32, (QBLK, kt), 1))
            cbias[r4] = jnp.where(d >= 0, 0.0, NEG)
        for w in range(NSA_WINDOW // QBLK + 1):
            d = (lax.broadcasted_iota(jnp.int32, (QBLK, ww), 0) + w * QBLK
                 - lax.broadcasted_iota(jnp.int32, (QBLK, ww), 1))
            wbias[w] = jnp.where((d >= 0) & (d < NSA_WINDOW), 0.0, NEG)

    @pl.when(qb == 0)
    def _():
        kaug[:, 0:HEAD_DIM] = ks_ref[0]
        key = lax.broadcasted_iota(jnp.int32, (s_len, nsp), 0)
        blk = lax.broadcasted_iota(jnp.int32, (s_len, nsp), 1)
        kaug[:, HEAD_DIM:] = jnp.where(blk == key // SEL_BLOCK, 1.0, 0.0).astype(BF16)
        ones = jnp.ones((s_len, HEAD_DIM), BF16)
        vaug[:, 0:HEAD_DIM] = vs_ref[0]
        vaug[:, HEAD_DIM:] = ones
        vwaug[:, 0:HEAD_DIM] = vw_ref[0]
        vwaug[:, HEAD_DIM:] = ones

    kc = kc_ref[0, 0]
    vct = vct_ref[0, 0]
    n_io = lax.broadcasted_iota(jnp.int32, (ncp, QBLK), 0)
    t_io = t0 + lax.broadcasted_iota(jnp.int32, (ncp, QBLK), 1)
    cmask = (n_io * CMP_STRIDE + (CMP_BLOCK - 1) <= t_io) & (n_io < ncp - 1)
    psum = jnp.zeros((ncp, QBLK), F32)
    o_cmp = []
    for u in range(hg):
        st = jnp.where(cmask, _nt_dot(kc, q_refs[u][0]), NEG)
        m = jnp.max(st, axis=0, keepdims=True)
        e = jnp.where(cmask, jnp.exp(st - m), 0.0)
        pt = e / jnp.maximum(jnp.sum(e, axis=0, keepdims=True), 1e-30)
        psum = psum + pt
        o_cmp.append(jnp.dot(vct, pt.astype(BF16), preferred_element_type=F32).T)

    p_hi = psum.astype(BF16)
    p_lo = (psum - p_hi.astype(F32)).astype(BF16)
    ovt = ovt_ref[...]
    imp = (jnp.dot(ovt, p_hi, preferred_element_type=F32)
           + jnp.dot(ovt, p_lo, preferred_element_type=F32))
    j_io = lax.broadcasted_iota(jnp.int32, (nsp, QBLK), 0)
    t_sel = t0 + lax.broadcasted_iota(jnp.int32, (nsp, QBLK), 1)
    valid = j_io * SEL_BLOCK <= t_sel
    cur = t_sel // SEL_BLOCK
    forced = (j_io == 0) | (j_io == cur) | (j_io == cur - 1)
    score = jnp.where(valid, imp + jnp.where(forced, FORCE_BONUS, 0.0), NEG)
    picked = jnp.zeros((nsp, QBLK), F32)
    for _ in range(k_sel):
        m = jnp.max(score, axis=0, keepdims=True)
        first = jnp.min(jnp.where(score == m, j_io, nsp), axis=0, keepdims=True)
        hit = j_io == first
        picked = jnp.where(hit, 1.0, picked)
        score = jnp.where(hit, -jnp.inf, score)
    sel_t = jnp.where(valid, picked, 0.0)
    sel_off = jnp.where(sel_t.T > 0.0, 0.0, SEL_OFF).astype(BF16)
    q5 = jnp.concatenate([r[0] for r in q_refs], axis=0)
    qaug = jnp.concatenate([q5, jnp.concatenate([sel_off] * hg, axis=0)], axis=1)

    def sel_scores(kti):
        k0 = pl.multiple_of(kti * kt, kt)
        return _nt_dot(qaug, kaug[pl.ds(k0, kt), :])

    def sel_update(s_ref, kti, diagonal):
        s = s_ref[...]
        if diagonal:
            s = _add_per_head(s, cbias[qb % (kt // QBLK)], hg)
        k0 = pl.multiple_of(kti * kt, kt)
        m_i = m_ref[...]
        m_new = jnp.maximum(m_i, jnp.max(s, axis=-1, keepdims=True))
        p = jnp.exp(s - m_new)
        pv = jnp.dot(p.astype(BF16), vaug[pl.ds(k0, kt), :], preferred_element_type=F32)
        acc_ref[...] = jnp.exp(m_i - m_new) * acc_ref[...] + pv
        m_ref[...] = m_new

    def sel_pair(pi, _):
        t = 2 * pi
        s_b[...] = sel_scores(t + 1)
        sel_update(s_a, t, False)
        s_a[...] = sel_scores(t + 2)
        sel_update(s_b, t + 1, False)
        return 0

    last = (t0 + QBLK + kt - 1) // kt - 1
    m_ref[...] = jnp.full((rows, 1), NEG, F32)
    acc_ref[...] = jnp.zeros((rows, 2 * HEAD_DIM), F32)
    s_a[...] = sel_scores(0)
    lax.fori_loop(0, last // 2, sel_pair, 0)
    t_even = 2 * (last // 2)

    @pl.when(last % 2 == 1)
    def _():
        s_b[...] = sel_scores(t_even + 1)
        sel_update(s_a, t_even, False)
        sel_update(s_b, t_even + 1, True)

    @pl.when(last % 2 == 0)
    def _():
        sel_update(s_a, t_even, True)

    acc_s = acc_ref[...]
    o_slc = acc_s[:, :HEAD_DIM] / acc_s[:, HEAD_DIM:]

    w0 = pl.multiple_of(jnp.maximum(t0 - NSA_WINDOW, 0), QBLK)
    sw = _nt_dot(q5, kw_ref[0, pl.ds(w0, ww), :])
    sw = _add_per_head(sw, wbias[jnp.minimum(qb, NSA_WINDOW // QBLK)], hg)
    pw = jnp.exp(sw - jnp.max(sw, axis=-1, keepdims=True))
    acc_w = jnp.dot(pw.astype(BF16), vwaug[pl.ds(w0, ww), :], preferred_element_type=F32)
    o_win = acc_w[:, :HEAD_DIM] / acc_w[:, HEAD_DIM:]

    gate = gate_ref[...]
    for u in range(hg):
        c = N_NSA_BRANCHES * u
        r = slice(u * QBLK, (u + 1) * QBLK)
        out = (gate[:, c:c + 1] * o_cmp[u] + gate[:, c + 1:c + 2] * o_slc[r]
               + gate[:, c + 2:c + 3] * o_win[r])
        o_ref[:, u * HEAD_DIM:(u + 1) * HEAD_DIM] = out.astype(BF16)


def _nsa(heads, kcv, vct, gates, ovt, kt):
    _, s, _ = heads.shape
    ncp = kcv.shape[2]
    nsp = ovt.shape[0]
    hg = NSA_GROUP
    k_sel = min(N_SELECT, s // SEL_BLOCK)
    q_map = lambda u, g, qb: (H_QN + hg * g + u, qb, 0)
    full = lambda hbase: pl.BlockSpec((1, s, HEAD_DIM), lambda g, qb: (hbase + g, 0, 0))
    return pl.pallas_call(
        functools.partial(_nsa_kernel, s_len=s, nsp=nsp, kt=kt, k_sel=k_sel),
        grid=(NSA_KV_GROUPS, s // QBLK),
        in_specs=[
            *[pl.BlockSpec((1, QBLK, HEAD_DIM), functools.partial(q_map, u)) for u in range(hg)],
            pl.BlockSpec((1, 1, ncp, HEAD_DIM), lambda g, qb: (0, g, 0, 0)),
            pl.BlockSpec((1, 1, HEAD_DIM, ncp), lambda g, qb: (1, g, 0, 0)),
            full(H_KS), full(H_VS), full(H_KW), full(H_VW),
            pl.BlockSpec((QBLK, HEAD_DIM), lambda g, qb: (qb, g)),
            pl.BlockSpec((nsp, ncp), lambda g, qb: (0, 0)),
        ],
        out_specs=pl.BlockSpec((QBLK, hg * HEAD_DIM), lambda g, qb: (qb, g)),
        out_shape=jax.ShapeDtypeStruct((s, D_NSA), BF16),
        scratch_shapes=[pltpu.VMEM((s, HEAD_DIM + nsp), BF16),
                        pltpu.VMEM((s, 2 * HEAD_DIM), BF16),
                        pltpu.VMEM((s, 2 * HEAD_DIM), BF16),
                        pltpu.VMEM((kt // QBLK, QBLK, kt), F32),
                        pltpu.VMEM((NSA_WINDOW // QBLK + 1, QBLK, NSA_WINDOW + QBLK), F32),
                        pltpu.VMEM((hg * QBLK, kt), F32),
                        pltpu.VMEM((hg * QBLK, kt), F32),
                        pltpu.VMEM((hg * QBLK, 1), F32),
                        pltpu.VMEM((hg * QBLK, 2 * HEAD_DIM), F32)],
        compiler_params=_params(("arbitrary", "arbitrary")),
        name="nsa_attn",
    )(*([heads] * hg), kcv, vct, heads, heads, heads, heads, gates, ovt)


def _out_proj_ln_kernel(a_ref, b_ref, wa_ref, wb_ref, h_ref, g_ref, be_ref, o_ref, *, alpha):
    mix = (jnp.dot(a_ref[...], wa_ref[...], preferred_element_type=F32)
           + jnp.dot(b_ref[...], wb_ref[...], preferred_element_type=F32))
    o_ref[...] = _layer_norm(alpha * h_ref[...] + mix, g_ref[...], be_ref[...])


def _out_proj_ln(mix_a, mix_b, wa, wb, h, g, b, alpha, tm):
    s, d = h.shape
    return pl.pallas_call(
        functools.partial(_out_proj_ln_kernel, alpha=alpha),
        grid=(s // tm,),
        in_specs=[
            pl.BlockSpec((tm, D_DIL), lambda i: (i, 0)),
            pl.BlockSpec((tm, D_NSA), lambda i: (i, 0)),
            pl.BlockSpec((D_DIL, d), lambda i: (0, 0)),
            pl.BlockSpec((D_NSA, d), lambda i: (0, 0)),
            pl.BlockSpec((tm, d), lambda i: (i, 0)),
            pl.BlockSpec((1, d), lambda i: (0, 0)),
            pl.BlockSpec((1, d), lambda i: (0, 0)),
        ],
        out_specs=pl.BlockSpec((tm, d), lambda i: (i, 0)),
        out_shape=jax.ShapeDtypeStruct((s, d), F32),
        compiler_params=_params(("arbitrary",)),
        name="out_proj_ln",
    )(mix_a, mix_b, wa, wb, h, g, b)


def _overlap_t(s):
    n_cmp = (s - CMP_BLOCK) // CMP_STRIDE + 1
    n_sel = s // SEL_BLOCK
    ncp = s // CMP_STRIDE
    nsp = -(-n_sel // HEAD_DIM) * HEAD_DIM
    c_lo = np.arange(n_cmp) * CMP_STRIDE
    c_hi = c_lo + CMP_BLOCK - 1
    s_lo = (np.arange(n_sel) * SEL_BLOCK)[:, None]
    ov = np.zeros((nsp, ncp), np.float32)
    ov[:n_sel, :n_cmp] = (c_lo[None, :] <= s_lo + SEL_BLOCK - 1) & (c_hi[None, :] >= s_lo)
    return jnp.asarray(ov, BF16)


def kernel(x, positions, ln1_g, ln1_b, ffn1_w1, ffn1_w3, ffn1_w2, w_in, gate_b, cmp_pe, cmp_w1,
           cmp_b1, cmp_w2, cmp_b2, w_out, ln2_g, ln2_b, ffn2_w1, ffn2_w3, ffn2_w2, ln3_g, ln3_b):
    bsz, s, d = x.shape
    assert bsz == 1 and d == (N_HEADS_DIL + N_HEADS_NSA) * HEAD_DIM
    assert s % 512 == 0 and s >= NSA_WINDOW + QBLK
    alpha = (2.0 * DEPTH) ** 0.25
    scale = HEAD_DIM ** -0.5
    row = lambda v: v.reshape(1, -1)
    tm = 512
    tf = 512 if ffn1_w1.shape[2] % 512 == 0 else 256

    inv_freq = ROPE_THETA ** (-jnp.arange(0, HEAD_DIM, 2, dtype=F32) / HEAD_DIM)
    invf = jnp.concatenate([inv_freq, inv_freq]).reshape(1, HEAD_DIM)
    pos = positions[0].astype(F32).reshape(s, 1)
    cosf, sinf = _rope_tables(pos, pos, invf, 512)
    ncp = s // CMP_STRIDE
    n_cmp = (s - CMP_BLOCK) // CMP_STRIDE + 1
    pos_lo = jnp.pad(pos[0:n_cmp * CMP_STRIDE:CMP_STRIDE], ((0, ncp - n_cmp), (0, 0)))
    pos_hi = jnp.pad(pos[CMP_BLOCK - 1::CMP_STRIDE][:n_cmp], ((0, ncp - n_cmp), (0, 0)))
    cos_c, sin_c = _rope_tables(pos_lo, pos_hi, invf, ncp)

    g_off = 3 * D_DIL + D_NSA + 6 * KV_W
    per_g = NSA_GROUP * N_NSA_BRANCHES
    w_in_t = w_in[0].T
    wg = jnp.zeros((NSA_KV_GROUPS * HEAD_DIM, d), F32)
    bg = jnp.zeros((1, NSA_KV_GROUPS * HEAD_DIM), F32)
    for g in range(NSA_KV_GROUPS):
        wg = wg.at[g * HEAD_DIM:g * HEAD_DIM + per_g].set(
            w_in_t[g_off + g * per_g:g_off + (g + 1) * per_g])
        bg = bg.at[0, g * HEAD_DIM:g * HEAD_DIM + per_g].set(gate_b[0][g * per_g:(g + 1) * per_g])
    wg = wg.astype(BF16)

    h0 = x[0]
    h1, h1b = _ffn_ln(h0, ffn1_w1[0].astype(BF16), ffn1_w3[0].astype(BF16),
                      ffn1_w2[0].astype(BF16), row(ln1_g[0]), row(ln1_b[0]), alpha, tm, tf)

    heads = _in_proj(h1b, w_in_t, cosf, sinf, scale, 1024)
    gates = _gates(h1b, wg, bg, tm)
    tok16 = heads[H_KC:H_KS].reshape(2 * NSA_KV_GROUPS, ncp, CMP_STRIDE * HEAD_DIM)
    kcv, kcv_t = _compress(
        tok16, cmp_pe[0].reshape(2, 1, CMP_BLOCK * HEAD_DIM), cmp_w1[0].astype(BF16),
        cmp_b1[0].reshape(2, 1, CMP_HIDDEN), cmp_w2[0].astype(BF16),
        cmp_b2[0].reshape(2, 1, HEAD_DIM), cos_c, sin_c)
    mix_a = _dilated(heads, 256)
    mix_b = _nsa(heads, kcv, kcv_t, gates, _overlap_t(s), 512)

    wo = w_out[0].astype(BF16)
    h2 = _out_proj_ln(mix_a, mix_b, wo[:D_DIL], wo[D_DIL:], h1, row(ln2_g[0]), row(ln2_b[0]),
                      alpha, tm)
    h3, _ = _ffn_ln(h2, ffn2_w1[0].astype(BF16), ffn2_w3[0].astype(BF16),
                    ffn2_w2[0].astype(BF16), row(ln3_g[0]), row(ln3_b[0]), alpha, tm, tf)
    return h3.reshape(bsz, s, d)
```

```python
import functools

import jax
import jax.numpy as jnp
import numpy as np
from jax import lax
from jax.experimental import pallas as pl
from jax.experimental.pallas import tpu as pltpu

HEAD_DIM = 128
N_HEADS_DIL = 6
N_HEADS_NSA = 10
NSA_KV_GROUPS = 2
NSA_GROUP = N_HEADS_NSA // NSA_KV_GROUPS
N_NSA_BRANCHES = 3
DIL_PAIRS = ((128, 1), (512, 4), (2048, 16))
CMP_BLOCK = 32
CMP_STRIDE = 16
CMP_HIDDEN = 256
SEL_BLOCK = 64
N_SELECT = 16
NSA_WINDOW = 512
ROPE_THETA = 10000.0
QBLK = 128
LN_EPS = 1e-5
NEG = -1e30
FORCE_BONUS = 1e4
SEL_OFF = -1e9
DEPTH = 1

D_DIL = N_HEADS_DIL * HEAD_DIM
D_NSA = N_HEADS_NSA * HEAD_DIM
KV_W = NSA_KV_GROUPS * HEAD_DIM
N_GATES = N_HEADS_NSA * N_NSA_BRANCHES

H_QA, H_KA, H_VA, H_QN = 0, 6, 12, 18
H_KC, H_VC, H_KS, H_VS, H_KW, H_VW = 28, 30, 32, 34, 36, 38
N_PROJ_HEADS = 40
HEADS_PER_PROJ_BLOCK = 4

VMEM_LIMIT_BYTES = 56 * 1024 * 1024

F32 = jnp.float32
BF16 = jnp.bfloat16


def _params(sem, vmem=VMEM_LIMIT_BYTES):
    return pltpu.CompilerParams(dimension_semantics=sem, vmem_limit_bytes=vmem)


def _nt_dot(a, b):
    return lax.dot_general(a, b, (((1,), (1,)), ((), ())), preferred_element_type=F32)


def _layer_norm(y, g, b):
    mu = jnp.mean(y, axis=-1, keepdims=True)
    yc = y - mu
    var = jnp.mean(yc * yc, axis=-1, keepdims=True)
    return yc * lax.rsqrt(var + LN_EPS) * g + b


def _rope_table_kernel(pa_ref, pb_ref, invf_ref, cos_ref, sin_ref):
    pos = (pa_ref[...] + pb_ref[...]) * 0.5
    ang = pos * invf_ref[...]
    lane = lax.broadcasted_iota(jnp.int32, ang.shape, 1)
    cos_ref[...] = jnp.cos(ang)
    sin_ref[...] = jnp.where(lane < HEAD_DIM // 2, -1.0, 1.0) * jnp.sin(ang)


def _rope_tables(pos_a, pos_b, invf, tile):
    n = pos_a.shape[0]
    spec_p = pl.BlockSpec((tile, 1), lambda i: (i, 0))
    spec_t = pl.BlockSpec((tile, HEAD_DIM), lambda i: (i, 0))
    return pl.pallas_call(
        _rope_table_kernel,
        grid=(n // tile,),
        in_specs=[spec_p, spec_p, pl.BlockSpec((1, HEAD_DIM), lambda i: (0, 0))],
        out_specs=[spec_t, spec_t],
        out_shape=[jax.ShapeDtypeStruct((n, HEAD_DIM), F32)] * 2,
        compiler_params=_params(("arbitrary",)),
        name="rope_tables",
    )(pos_a, pos_b, invf)


def _ffn_ln_kernel(h_ref, w1_ref, w3_ref, w2_ref, g_ref, b_ref, o_ref, ob_ref,
                   hb_ref, acc_ref, *, alpha, nf):
    f = pl.program_id(1)

    @pl.when(f == 0)
    def _():
        hb_ref[...] = h_ref[...].astype(BF16)
        acc_ref[...] = jnp.zeros_like(acc_ref)

    hb = hb_ref[...]
    a = jnp.dot(hb, w1_ref[...], preferred_element_type=F32)
    b = jnp.dot(hb, w3_ref[...], preferred_element_type=F32)
    act = (a * jax.nn.sigmoid(a)) * b
    acc_ref[...] += jnp.dot(act.astype(BF16), w2_ref[...], preferred_element_type=F32)

    @pl.when(f == nf - 1)
    def _():
        y = alpha * h_ref[...] + 0.5 * acc_ref[...]
        out = _layer_norm(y, g_ref[...], b_ref[...])
        o_ref[...] = out
        ob_ref[...] = out.astype(BF16)


def _ffn_ln(h, w1, w3, w2, g, b, alpha, tm, tf):
    s, d = h.shape
    dff = w1.shape[1]
    nf = dff // tf
    return pl.pallas_call(
        functools.partial(_ffn_ln_kernel, alpha=alpha, nf=nf),
        grid=(s // tm, nf),
        in_specs=[
            pl.BlockSpec((tm, d), lambda i, f: (i, 0)),
            pl.BlockSpec((d, tf), lambda i, f: (0, f)),
            pl.BlockSpec((d, tf), lambda i, f: (0, f)),
            pl.BlockSpec((tf, d), lambda i, f: (f, 0)),
            pl.BlockSpec((1, d), lambda i, f: (0, 0)),
            pl.BlockSpec((1, d), lambda i, f: (0, 0)),
        ],
        out_specs=[pl.BlockSpec((tm, d), lambda i, f: (i, 0)),
                   pl.BlockSpec((tm, d), lambda i, f: (i, 0))],
        out_shape=[jax.ShapeDtypeStruct((s, d), F32), jax.ShapeDtypeStruct((s, d), BF16)],
        scratch_shapes=[pltpu.VMEM((tm, d), BF16), pltpu.VMEM((tm, d), F32)],
        compiler_params=_params(("arbitrary", "arbitrary")),
        name="ffn_ln",
    )(h, w1, w3, w2, g, b)


def _in_proj_kernel(h_ref, w_ref, cos_ref, sin_ref, o_ref, *, scale):
    j = pl.program_id(1)
    hpb = HEADS_PER_PROJ_BLOCK
    acc = _nt_dot(h_ref[...], w_ref[...].astype(BF16))
    cos = cos_ref[...]
    sin = sin_ref[...]
    for k in range(hpb):
        hh = j * hpb + k
        is_q = (hh < H_KA) | ((hh >= H_QN) & (hh < H_KC))
        rope = is_q | (hh < H_VA) | ((hh >= H_KS) & (hh < H_VS)) | ((hh >= H_KW) & (hh < H_VW))
        x = acc[:, k * HEAD_DIM:(k + 1) * HEAD_DIM]
        xr = x * cos + pltpu.roll(x, HEAD_DIM // 2, 1) * sin
        y = jnp.where(rope, xr, x) * jnp.where(is_q, scale, 1.0).astype(F32)
        o_ref[k] = y.astype(BF16)


def _in_proj(hb, w, cosf, sinf, scale, tm):
    s, d = hb.shape
    hpb = HEADS_PER_PROJ_BLOCK
    return pl.pallas_call(
        functools.partial(_in_proj_kernel, scale=scale),
        grid=(s // tm, N_PROJ_HEADS // hpb),
        in_specs=[
            pl.BlockSpec((tm, d), lambda i, j: (i, 0)),
            pl.BlockSpec((hpb * HEAD_DIM, d), lambda i, j: (j, 0)),
            pl.BlockSpec((tm, HEAD_DIM), lambda i, j: (i, 0)),
            pl.BlockSpec((tm, HEAD_DIM), lambda i, j: (i, 0)),
        ],
        out_specs=pl.BlockSpec((hpb, tm, HEAD_DIM), lambda i, j: (j, i, 0)),
        out_shape=jax.ShapeDtypeStruct((N_PROJ_HEADS, s, HEAD_DIM), BF16),
        compiler_params=_params(("arbitrary", "arbitrary")),
        name="in_proj",
    )(hb, w, cosf, sinf)


def _gates_kernel(h_ref, w_ref, b_ref, o_ref):
    z = _nt_dot(h_ref[...], w_ref[...]) + b_ref[...]
    o_ref[...] = jax.nn.sigmoid(z)


def _gates(hb, wg, bg, tm):
    s, d = hb.shape
    n = wg.shape[0]
    return pl.pallas_call(
        _gates_kernel,
        grid=(s // tm,),
        in_specs=[pl.BlockSpec((tm, d), lambda i: (i, 0)),
                  pl.BlockSpec((n, d), lambda i: (0, 0)),
                  pl.BlockSpec((1, n), lambda i: (0, 0))],
        out_specs=pl.BlockSpec((tm, n), lambda i: (i, 0)),
        out_shape=jax.ShapeDtypeStruct((s, n), F32),
        compiler_params=_params(("arbitrary",)),
        name="nsa_gates",
    )(hb, wg, bg)


def _compress_kernel(tok_ref, pe_ref, w1_ref, b1_ref, w2_ref, b2_ref, cos_ref, sin_ref,
                     o_ref, ot_ref):
    j = pl.program_id(0)
    half = CMP_STRIDE * HEAD_DIM
    tok = tok_ref[0].astype(F32)
    pe = pe_ref[0]
    w1 = w1_ref[0]
    top = (tok + pe[:, :half]).astype(BF16)
    bot = (tok + pe[:, half:]).astype(BF16)
    u = jnp.dot(top, w1[:half], preferred_element_type=F32)
    v = jnp.dot(bot, w1[half:], preferred_element_type=F32)
    nc = u.shape[0]
    hid = u + pltpu.roll(v, nc - 1, 0) + b1_ref[0]
    hid = jax.nn.gelu(hid)
    out = jnp.dot(hid.astype(BF16), w2_ref[0], preferred_element_type=F32) + b2_ref[0]
    roped = out * cos_ref[...] + pltpu.roll(out, HEAD_DIM // 2, 1) * sin_ref[...]
    out = jnp.where(j == 0, roped, out)
    o_ref[0, 0] = out.astype(BF16)
    ot_ref[0, 0] = out.T.astype(BF16)


def _compress(tok16, pe, w1, b1, w2, b2, cos_c, sin_c):
    nc = tok16.shape[1]
    blk = CMP_BLOCK * HEAD_DIM
    g = NSA_KV_GROUPS
    return pl.pallas_call(
        _compress_kernel,
        grid=(2, g),
        in_specs=[
            pl.BlockSpec((1, nc, CMP_STRIDE * HEAD_DIM), lambda j, gi: (g * j + gi, 0, 0)),
            pl.BlockSpec((1, 1, blk), lambda j, gi: (j, 0, 0)),
            pl.BlockSpec((1, blk, CMP_HIDDEN), lambda j, gi: (j, 0, 0)),
            pl.BlockSpec((1, 1, CMP_HIDDEN), lambda j, gi: (j, 0, 0)),
            pl.BlockSpec((1, CMP_HIDDEN, HEAD_DIM), lambda j, gi: (j, 0, 0)),
            pl.BlockSpec((1, 1, HEAD_DIM), lambda j, gi: (j, 0, 0)),
            pl.BlockSpec((nc, HEAD_DIM), lambda j, gi: (0, 0)),
            pl.BlockSpec((nc, HEAD_DIM), lambda j, gi: (0, 0)),
        ],
        out_specs=[pl.BlockSpec((1, 1, nc, HEAD_DIM), lambda j, gi: (j, gi, 0, 0)),
                   pl.BlockSpec((1, 1, HEAD_DIM, nc), lambda j, gi: (j, gi, 0, 0))],
        out_shape=[jax.ShapeDtypeStruct((2, g, nc, HEAD_DIM), BF16),
                   jax.ShapeDtypeStruct((2, g, HEAD_DIM, nc), BF16)],
        compiler_params=_params(("arbitrary", "arbitrary")),
        name="nsa_compress",
    )(tok16, pe, w1, b1, w2, b2, cos_c, sin_c)


def _dil_kernel(q_ref, k_ref, v_ref, o_ref, kpad, vpad, bias_ref, *, tq, padk):
    h = pl.program_id(0)
    i = pl.program_id(1)
    w = padk + tq

    @pl.when((h == 0) & (i == 0))
    def _():
        r = lax.broadcasted_iota(jnp.int32, (tq, w), 0)
        c = lax.broadcasted_iota(jnp.int32, (tq, w), 1)
        d = r + padk - c
        cnt = jnp.zeros((tq, w), F32)
        for window, dil in DIL_PAIRS:
            hit = (d >= 0) & (d <= window) & ((d & (dil - 1)) == 0)
            cnt = cnt + jnp.where(hit, 1.0, 0.0)
        bias_ref[...] = jnp.where(cnt > 0.0, jnp.log(jnp.maximum(cnt, 1.0)), NEG)

    @pl.when(i == 0)
    def _():
        kpad[0:padk, :] = jnp.zeros((padk, HEAD_DIM), BF16)
        vpad[0:padk, :] = jnp.zeros((padk, HEAD_DIM), BF16)
        kpad[padk:, :] = k_ref[0]
        vpad[padk:, :] = v_ref[0]

    q0 = pl.multiple_of(i * tq, tq)
    s = _nt_dot(q_ref[0], kpad[pl.ds(q0, w), :]) + bias_ref[...]
    col = lax.broadcasted_iota(jnp.int32, (1, w), 1)
    s = jnp.where(col >= padk - q0, s, NEG)
    p = jnp.exp(s - jnp.max(s, axis=-1, keepdims=True))
    l = jnp.sum(p, axis=-1, keepdims=True)
    o = jnp.dot(p.astype(BF16), vpad[pl.ds(q0, w), :], preferred_element_type=F32)
    o_ref[...] = (o / l).astype(BF16)


def _dilated(heads, tq):
    _, s, _ = heads.shape
    padk = max(wd for wd, _ in DIL_PAIRS)
    return pl.pallas_call(
        functools.partial(_dil_kernel, tq=tq, padk=padk),
        grid=(N_HEADS_DIL, s // tq),
        in_specs=[
            pl.BlockSpec((1, tq, HEAD_DIM), lambda h, i: (H_QA + h, i, 0)),
            pl.BlockSpec((1, s, HEAD_DIM), lambda h, i: (H_KA + h, 0, 0)),
            pl.BlockSpec((1, s, HEAD_DIM), lambda h, i: (H_VA + h, 0, 0)),
        ],
        out_specs=pl.BlockSpec((tq, HEAD_DIM), lambda h, i: (i, h)),
        out_shape=jax.ShapeDtypeStruct((s, D_DIL), BF16),
        scratch_shapes=[pltpu.VMEM((padk + s, HEAD_DIM), BF16),
                        pltpu.VMEM((padk + s, HEAD_DIM), BF16),
                        pltpu.VMEM((tq, padk + tq), F32)],
        compiler_params=_params(("arbitrary", "arbitrary")),
        name="dilated_attn",
    )(heads, heads, heads)


def _add_per_head(s, bias, hg):
    tq = bias.shape[0]
    return jnp.concatenate([s[u * tq:(u + 1) * tq] + bias for u in range(hg)], axis=0)


def _nsa_kernel(*refs, s_len, nsp, tq, kt, k_sel):
    q_refs = refs[:NSA_GROUP]
    (kc_ref, vct_ref, ks_ref, vs_ref, kw_ref, vw_ref, gate_ref, ovt_ref,
     o_ref, kaug, vaug, vwaug, cbias, wbias, s_a, s_b, m_ref, acc_ref,
     part_ref) = refs[NSA_GROUP:]
    g = pl.program_id(0)
    qb = pl.program_id(1)
    t0 = qb * tq
    hg = NSA_GROUP
    rows = hg * tq
    ncp = kc_ref.shape[2]
    ww = NSA_WINDOW + tq

    @pl.when((g == 0) & (qb == 0))
    def _():
        for r in range(kt // tq):
            d = (lax.broadcasted_iota(jnp.int32, (tq, kt), 0) + r * tq
                 - lax.broadcasted_iota(jnp.int32, (tq, kt), 1))
            cbias[r] = jnp.where(d >= 0, 0.0, NEG)
        for w in range(NSA_WINDOW // tq + 1):
            d = (lax.broadcasted_iota(jnp.int32, (tq, ww), 0) + w * tq
                 - lax.broadcasted_iota(jnp.int32, (tq, ww), 1))
            wbias[w] = jnp.where((d >= 0) & (d < NSA_WINDOW), 0.0, NEG)

    @pl.when(qb == 0)
    def _():
        kaug[:, 0:HEAD_DIM] = ks_ref[0]
        key = lax.broadcasted_iota(jnp.int32, (s_len, nsp), 0)
        blk = lax.broadcasted_iota(jnp.int32, (s_len, nsp), 1)
        kaug[:, HEAD_DIM:] = jnp.where(blk == key // SEL_BLOCK, 1.0, 0.0).astype(BF16)
        ones = jnp.ones((s_len, HEAD_DIM), BF16)
        vaug[:, 0:HEAD_DIM] = vs_ref[0]
        vaug[:, HEAD_DIM:] = ones
        vwaug[:, 0:HEAD_DIM] = vw_ref[0]
        vwaug[:, HEAD_DIM:] = ones

    gate = gate_ref[...]
    q5 = jnp.concatenate([r[0] for r in q_refs], axis=0)

    w0 = pl.multiple_of(jnp.maximum(t0 - NSA_WINDOW, 0), tq)
    sw = _nt_dot(q5, kw_ref[0, pl.ds(w0, ww), :])
    sw = _add_per_head(sw, wbias[jnp.minimum(qb, NSA_WINDOW // tq)], hg)
    pw = jnp.exp(sw - jnp.max(sw, axis=-1, keepdims=True))
    acc_w = jnp.dot(pw.astype(BF16), vwaug[pl.ds(w0, ww), :], preferred_element_type=F32)
    o_win = acc_w[:, :HEAD_DIM] / acc_w[:, HEAD_DIM:]

    kc = kc_ref[0, 0]
    vct = vct_ref[0, 0]
    n_io = lax.broadcasted_iota(jnp.int32, (ncp, tq), 0)
    t_io = t0 + lax.broadcasted_iota(jnp.int32, (ncp, tq), 1)
    cmask = (n_io * CMP_STRIDE + (CMP_BLOCK - 1) <= t_io) & (n_io < ncp - 1)
    psum = jnp.zeros((ncp, tq), F32)
    for u in range(hg):
        st = jnp.where(cmask, _nt_dot(kc, q_refs[u][0]), NEG)
        m = jnp.maximum(jnp.max(st, axis=0, keepdims=True), 0.1 * NEG)
        e = jnp.exp(st - m)
        pt = e / jnp.maximum(jnp.sum(e, axis=0, keepdims=True), 1e-30)
        psum = psum + pt
        o_cmp = jnp.dot(vct, pt.astype(BF16), preferred_element_type=F32).T
        c = N_NSA_BRANCHES * u
        part_ref[:, u * HEAD_DIM:(u + 1) * HEAD_DIM] = (
            gate[:, c:c + 1] * o_cmp + gate[:, c + 2:c + 3] * o_win[u * tq:(u + 1) * tq])

    p_hi = psum.astype(BF16)
    p_lo = (psum - p_hi.astype(F32)).astype(BF16)
    ovt = ovt_ref[...]
    imp = (jnp.dot(ovt, p_hi, preferred_element_type=F32)
           + jnp.dot(ovt, p_lo, preferred_element_type=F32))
    j_io = lax.broadcasted_iota(jnp.int32, (nsp, tq), 0)
    t_sel = t0 + lax.broadcasted_iota(jnp.int32, (nsp, tq), 1)
    valid = j_io * SEL_BLOCK <= t_sel
    cur = t_sel // SEL_BLOCK
    forced = (j_io == 0) | (j_io == cur) | (j_io == cur - 1)
    score = jnp.where(valid & jnp.logical_not(forced), imp, NEG)
    picked = jnp.where(forced, 1.0, 0.0)
    for _ in range(k_sel - 3):
        m = jnp.max(score, axis=0, keepdims=True)
        first = jnp.min(jnp.where(score == m, j_io, nsp), axis=0, keepdims=True)
        hit = j_io == first
        picked = jnp.where(hit, 1.0, picked)
        score = jnp.where(hit, -jnp.inf, score)
    sel_t = jnp.where(valid, picked, 0.0)
    sel_off = jnp.where(sel_t.T > 0.0, 0.0, SEL_OFF).astype(BF16)
    qaug = jnp.concatenate([q5, jnp.concatenate([sel_off] * hg, axis=0)], axis=1)

    def sel_scores(kti):
        k0 = pl.multiple_of(kti * kt, kt)
        return _nt_dot(qaug, kaug[pl.ds(k0, kt), :])

    def sel_update(s_ref, kti, diagonal):
        s = s_ref[...]
        if diagonal:
            s = _add_per_head(s, cbias[qb % (kt // tq)], hg)
        k0 = pl.multiple_of(kti * kt, kt)
        m_i = m_ref[...]
        m_new = jnp.maximum(m_i, jnp.max(s, axis=-1, keepdims=True))
        p = jnp.exp(s - m_new)
        pv = jnp.dot(p.astype(BF16), vaug[pl.ds(k0, kt), :], preferred_element_type=F32)
        acc_ref[...] = jnp.exp(m_i - m_new) * acc_ref[...] + pv
        m_ref[...] = m_new

    def sel_pair(pi, _):
        t = 2 * pi
        s_b[...] = sel_scores(t + 1)
        sel_update(s_a, t, False)
        s_a[...] = sel_scores(t + 2)
        sel_update(s_b, t + 1, False)
        return 0

    last = (t0 + tq + kt - 1) // kt - 1
    m_ref[...] = jnp.full((rows, 1), NEG, F32)
    acc_ref[...] = jnp.zeros((rows, 2 * HEAD_DIM), F32)
    s_a[...] = sel_scores(0)
    lax.fori_loop(0, last // 2, sel_pair, 0)
    t_even = 2 * (last // 2)

    @pl.when(last % 2 == 1)
    def _():
        s_b[...] = sel_scores(t_even + 1)
        sel_update(s_a, t_even, False)
        sel_update(s_b, t_even + 1, True)

    @pl.when(last % 2 == 0)
    def _():
        sel_update(s_a, t_even, True)

    gate = gate_ref[...]
    for u in range(hg):
        c = N_NSA_BRANCHES * u + 1
        acc_u = acc_ref[u * tq:(u + 1) * tq, :]
        o_slc = acc_u[:, :HEAD_DIM] / acc_u[:, HEAD_DIM:]
        cols = slice(u * HEAD_DIM, (u + 1) * HEAD_DIM)
        o_ref[:, cols] = (part_ref[:, cols] + gate[:, c:c + 1] * o_slc).astype(BF16)


def _nsa(heads, kcv, vct, gates, ovt, tq, kt):
    _, s, _ = heads.shape
    ncp = kcv.shape[2]
    nsp = ovt.shape[0]
    hg = NSA_GROUP
    k_sel = min(N_SELECT, s // SEL_BLOCK)
    assert k_sel >= 3 and kt % tq == 0 and NSA_WINDOW % tq == 0 and s % kt == 0
    q_map = lambda u, g, qb: (H_QN + hg * g + u, qb, 0)
    full = lambda hbase: pl.BlockSpec((1, s, HEAD_DIM), lambda g, qb: (hbase + g, 0, 0),
                                      pipeline_mode=pl.Buffered(1))
    return pl.pallas_call(
        functools.partial(_nsa_kernel, s_len=s, nsp=nsp, tq=tq, kt=kt, k_sel=k_sel),
        grid=(NSA_KV_GROUPS, s // tq),
        in_specs=[
            *[pl.BlockSpec((1, tq, HEAD_DIM), functools.partial(q_map, u)) for u in range(hg)],
            pl.BlockSpec((1, 1, ncp, HEAD_DIM), lambda g, qb: (0, g, 0, 0)),
            pl.BlockSpec((1, 1, HEAD_DIM, ncp), lambda g, qb: (1, g, 0, 0)),
            full(H_KS), full(H_VS), full(H_KW), full(H_VW),
            pl.BlockSpec((tq, HEAD_DIM), lambda g, qb: (qb, g)),
            pl.BlockSpec((nsp, ncp), lambda g, qb: (0, 0)),
        ],
        out_specs=pl.BlockSpec((tq, hg * HEAD_DIM), lambda g, qb: (qb, g)),
        out_shape=jax.ShapeDtypeStruct((s, D_NSA), BF16),
        scratch_shapes=[pltpu.VMEM((s, HEAD_DIM + nsp), BF16),
                        pltpu.VMEM((s, 2 * HEAD_DIM), BF16),
                        pltpu.VMEM((s, 2 * HEAD_DIM), BF16),
                        pltpu.VMEM((kt // tq, tq, kt), F32),
                        pltpu.VMEM((NSA_WINDOW // tq + 1, tq, NSA_WINDOW + tq), F32),
                        pltpu.VMEM((hg * tq, kt), F32),
                        pltpu.VMEM((hg * tq, kt), F32),
                        pltpu.VMEM((hg * tq, 1), F32),
                        pltpu.VMEM((hg * tq, 2 * HEAD_DIM), F32),
                        pltpu.VMEM((tq, hg * HEAD_DIM), F32)],
        compiler_params=_params(("arbitrary", "arbitrary")),
        name="nsa_attn",
    )(*([heads] * hg), kcv, vct, heads, heads, heads, heads, gates, ovt)


def _out_proj_ln_kernel(a_ref, b_ref, wa_ref, wb_ref, h_ref, g_ref, be_ref, o_ref, *, alpha):
    mix = (jnp.dot(a_ref[...], wa_ref[...], preferred_element_type=F32)
           + jnp.dot(b_ref[...], wb_ref[...], preferred_element_type=F32))
    o_ref[...] = _layer_norm(alpha * h_ref[...] + mix, g_ref[...], be_ref[...])


def _out_proj_ln(mix_a, mix_b, wa, wb, h, g, b, alpha, tm):
    s, d = h.shape
    return pl.pallas_call(
        functools.partial(_out_proj_ln_kernel, alpha=alpha),
        grid=(s // tm,),
        in_specs=[
            pl.BlockSpec((tm, D_DIL), lambda i: (i, 0)),
            pl.BlockSpec((tm, D_NSA), lambda i: (i, 0)),
            pl.BlockSpec((D_DIL, d), lambda i: (0, 0)),
            pl.BlockSpec((D_NSA, d), lambda i: (0, 0)),
            pl.BlockSpec((tm, d), lambda i: (i, 0)),
            pl.BlockSpec((1, d), lambda i: (0, 0)),
            pl.BlockSpec((1, d), lambda i: (0, 0)),
        ],
        out_specs=pl.BlockSpec((tm, d), lambda i: (i, 0)),
        out_shape=jax.ShapeDtypeStruct((s, d), F32),
        compiler_params=_params(("arbitrary",)),
        name="out_proj_ln",
    )(mix_a, mix_b, wa, wb, h, g, b)


def _overlap_t(s):
    n_cmp = (s - CMP_BLOCK) // CMP_STRIDE + 1
    n_sel = s // SEL_BLOCK
    ncp = s // CMP_STRIDE
    nsp = -(-n_sel // HEAD_DIM) * HEAD_DIM
    c_lo = np.arange(n_cmp) * CMP_STRIDE
    c_hi = c_lo + CMP_BLOCK - 1
    s_lo = (np.arange(n_sel) * SEL_BLOCK)[:, None]
    ov = np.zeros((nsp, ncp), np.float32)
    ov[:n_sel, :n_cmp] = (c_lo[None, :] <= s_lo + SEL_BLOCK - 1) & (c_hi[None, :] >= s_lo)
    return jnp.asarray(ov, BF16)


def kernel(x, positions, ln1_g, ln1_b, ffn1_w1, ffn1_w3, ffn1_w2, w_in, gate_b, cmp_pe, cmp_w1,
           cmp_b1, cmp_w2, cmp_b2, w_out, ln2_g, ln2_b, ffn2_w1, ffn2_w3, ffn2_w2, ln3_g, ln3_b):
    bsz, s, d = x.shape
    assert bsz == 1 and d == (N_HEADS_DIL + N_HEADS_NSA) * HEAD_DIM
    assert s % 512 == 0 and s >= NSA_WINDOW + QBLK
    alpha = (2.0 * DEPTH) ** 0.25
    scale = HEAD_DIM ** -0.5
    row = lambda v: v.reshape(1, -1)
    tm = 512
    tf = 512 if ffn1_w1.shape[2] % 512 == 0 else 256

    inv_freq = ROPE_THETA ** (-jnp.arange(0, HEAD_DIM, 2, dtype=F32) / HEAD_DIM)
    invf = jnp.concatenate([inv_freq, inv_freq]).reshape(1, HEAD_DIM)
    pos = positions[0].astype(F32).reshape(s, 1)
    cosf, sinf = _rope_tables(pos, pos, invf, 512)
    ncp = s // CMP_STRIDE
    n_cmp = (s - CMP_BLOCK) // CMP_STRIDE + 1
    pos_lo = jnp.pad(pos[0:n_cmp * CMP_STRIDE:CMP_STRIDE], ((0, ncp - n_cmp), (0, 0)))
    pos_hi = jnp.pad(pos[CMP_BLOCK - 1::CMP_STRIDE][:n_cmp], ((0, ncp - n_cmp), (0, 0)))
    cos_c, sin_c = _rope_tables(pos_lo, pos_hi, invf, ncp)

    g_off = 3 * D_DIL + D_NSA + 6 * KV_W
    per_g = NSA_GROUP * N_NSA_BRANCHES
    w_in_t = w_in[0].T
    wg = jnp.zeros((NSA_KV_GROUPS * HEAD_DIM, d), F32)
    bg = jnp.zeros((1, NSA_KV_GROUPS * HEAD_DIM), F32)
    for g in range(NSA_KV_GROUPS):
        wg = wg.at[g * HEAD_DIM:g * HEAD_DIM + per_g].set(
            w_in_t[g_off + g * per_g:g_off + (g + 1) * per_g])
        bg = bg.at[0, g * HEAD_DIM:g * HEAD_DIM + per_g].set(gate_b[0][g * per_g:(g + 1) * per_g])
    wg = wg.astype(BF16)

    h0 = x[0]
    h1, h1b = _ffn_ln(h0, ffn1_w1[0].astype(BF16), ffn1_w3[0].astype(BF16),
                      ffn1_w2[0].astype(BF16), row(ln1_g[0]), row(ln1_b[0]), alpha, tm, tf)

    heads = _in_proj(h1b, w_in_t, cosf, sinf, scale, 1024)
    gates = _gates(h1b, wg, bg, tm)
    tok16 = heads[H_KC:H_KS].reshape(2 * NSA_KV_GROUPS, ncp, CMP_STRIDE * HEAD_DIM)
    kcv, kcv_t = _compress(
        tok16, cmp_pe[0].reshape(2, 1, CMP_BLOCK * HEAD_DIM), cmp_w1[0].astype(BF16),
        cmp_b1[0].reshape(2, 1, CMP_HIDDEN), cmp_w2[0].astype(BF16),
        cmp_b2[0].reshape(2, 1, HEAD_DIM), cos_c, sin_c)
    mix_a = _dilated(heads, 256)
    mix_b = _nsa(heads, kcv, kcv_t, gates, _overlap_t(s), 256, 1024)

    wo = w_out[0].astype(BF16)
    h2 = _out_proj_ln(mix_a, mix_b, wo[:D_DIL], wo[D_DIL:], h1, row(ln2_g[0]), row(ln2_b[0]),
                      alpha, tm)
    h3, _ = _ffn_ln(h2, ffn2_w1[0].astype(BF16), ffn2_w3[0].astype(BF16),
                    ffn2_w2[0].astype(BF16), row(ln3_g[0]), row(ln3_b[0]), alpha, tm, tf)
    return h3.reshape(bsz, s, d)
```

```python
import functools

import jax
import jax.numpy as jnp
import numpy as np
from jax import lax
from jax.experimental import pallas as pl
from jax.experimental.pallas import tpu as pltpu

HEAD_DIM = 128
N_HEADS_DIL = 6
N_HEADS_NSA = 10
NSA_KV_GROUPS = 2
NSA_GROUP = N_HEADS_NSA // NSA_KV_GROUPS
N_NSA_BRANCHES = 3
DIL_PAIRS = ((128, 1), (512, 4), (2048, 16))
CMP_BLOCK = 32
CMP_STRIDE = 16
CMP_HIDDEN = 256
SEL_BLOCK = 64
N_SELECT = 16
NSA_WINDOW = 512
ROPE_THETA = 10000.0
QBLK = 128
LN_EPS = 1e-5
NEG = -1e30
FORCE_BONUS = 1e4
SEL_OFF = -1e9
DEPTH = 1

D_DIL = N_HEADS_DIL * HEAD_DIM
D_NSA = N_HEADS_NSA * HEAD_DIM
KV_W = NSA_KV_GROUPS * HEAD_DIM
N_GATES = N_HEADS_NSA * N_NSA_BRANCHES

H_QA, H_KA, H_VA, H_QN = 0, 6, 12, 18
H_KC, H_VC, H_KS, H_VS, H_KW, H_VW = 28, 30, 32, 34, 36, 38
N_PROJ_HEADS = 40
HEADS_PER_PROJ_BLOCK = 4

VMEM_LIMIT_BYTES = 56 * 1024 * 1024

F32 = jnp.float32
BF16 = jnp.bfloat16


def _params(sem, vmem=VMEM_LIMIT_BYTES):
    return pltpu.CompilerParams(dimension_semantics=sem, vmem_limit_bytes=vmem)


def _nt_dot(a, b):
    return lax.dot_general(a, b, (((1,), (1,)), ((), ())), preferred_element_type=F32)


def _layer_norm(y, g, b):
    mu = jnp.mean(y, axis=-1, keepdims=True)
    yc = y - mu
    var = jnp.mean(yc * yc, axis=-1, keepdims=True)
    return yc * lax.rsqrt(var + LN_EPS) * g + b


def _rope_table_kernel(pa_ref, pb_ref, invf_ref, cos_ref, sin_ref):
    pos = (pa_ref[...] + pb_ref[...]) * 0.5
    ang = pos * invf_ref[...]
    lane = lax.broadcasted_iota(jnp.int32, ang.shape, 1)
    cos_ref[...] = jnp.cos(ang)
    sin_ref[...] = jnp.where(lane < HEAD_DIM // 2, -1.0, 1.0) * jnp.sin(ang)


def _rope_tables(pos_a, pos_b, invf, tile):
    n = pos_a.shape[0]
    spec_p = pl.BlockSpec((tile, 1), lambda i: (i, 0))
    spec_t = pl.BlockSpec((tile, HEAD_DIM), lambda i: (i, 0))
    return pl.pallas_call(
        _rope_table_kernel,
        grid=(n // tile,),
        in_specs=[spec_p, spec_p, pl.BlockSpec((1, HEAD_DIM), lambda i: (0, 0))],
        out_specs=[spec_t, spec_t],
        out_shape=[jax.ShapeDtypeStruct((n, HEAD_DIM), F32)] * 2,
        compiler_params=_params(("arbitrary",)),
        name="rope_tables",
    )(pos_a, pos_b, invf)


def _ffn_ln_kernel(h_ref, w1_ref, w3_ref, w2_ref, g_ref, b_ref, o_ref, *rest, alpha, nf):
    hb_ref = rest[-1]
    f = pl.program_id(1)

    @pl.when(f == 0)
    def _():
        hb_ref[...] = h_ref[...].astype(BF16)
        o_ref[...] = jnp.zeros_like(o_ref)

    hb = hb_ref[...]
    a = jnp.dot(hb, w1_ref[...].astype(BF16), preferred_element_type=F32)
    b = jnp.dot(hb, w3_ref[...].astype(BF16), preferred_element_type=F32)
    act = (a * jax.nn.sigmoid(a)) * b
    o_ref[...] += jnp.dot(act.astype(BF16), w2_ref[...].astype(BF16),
                          preferred_element_type=F32)

    @pl.when(f == nf - 1)
    def _():
        y = alpha * h_ref[...] + 0.5 * o_ref[...]
        out = _layer_norm(y, g_ref[...], b_ref[...])
        o_ref[...] = out
        if len(rest) == 2:
            rest[0][...] = out.astype(BF16)


def _ffn_ln(h, w1, w3, w2, g, b, alpha, tm, tf, emit_bf16):
    s, d = h.shape
    dff = w1.shape[1]
    nf = dff // tf
    once = pl.Buffered(1)
    row_spec = lambda: pl.BlockSpec((tm, d), lambda i, f: (i, 0), pipeline_mode=once)
    out_specs = [row_spec()]
    out_shape = [jax.ShapeDtypeStruct((s, d), F32)]
    if emit_bf16:
        out_specs.append(row_spec())
        out_shape.append(jax.ShapeDtypeStruct((s, d), BF16))
    return pl.pallas_call(
        functools.partial(_ffn_ln_kernel, alpha=alpha, nf=nf),
        grid=(s // tm, nf),
        in_specs=[
            row_spec(),
            pl.BlockSpec((d, tf), lambda i, f: (0, f)),
            pl.BlockSpec((d, tf), lambda i, f: (0, f)),
            pl.BlockSpec((tf, d), lambda i, f: (f, 0)),
            pl.BlockSpec((1, d), lambda i, f: (0, 0)),
            pl.BlockSpec((1, d), lambda i, f: (0, 0)),
        ],
        out_specs=out_specs,
        out_shape=out_shape,
        scratch_shapes=[pltpu.VMEM((tm, d), BF16)],
        compiler_params=_params(("arbitrary", "arbitrary")),
        name="ffn_ln",
    )(h, w1, w3, w2, g, b)


def _in_proj_kernel(h_ref, w_ref, cos_ref, sin_ref, o_ref, *, scale):
    j = pl.program_id(1)
    hpb = HEADS_PER_PROJ_BLOCK
    acc = _nt_dot(h_ref[...], w_ref[...].astype(BF16))
    cos = cos_ref[...]
    sin = sin_ref[...]
    for k in range(hpb):
        hh = j * hpb + k
        is_q = (hh < H_KA) | ((hh >= H_QN) & (hh < H_KC))
        rope = is_q | (hh < H_VA) | ((hh >= H_KS) & (hh < H_VS)) | ((hh >= H_KW) & (hh < H_VW))
        x = acc[:, k * HEAD_DIM:(k + 1) * HEAD_DIM]
        xr = x * cos + pltpu.roll(x, HEAD_DIM // 2, 1) * sin
        y = jnp.where(rope, xr, x) * jnp.where(is_q, scale, 1.0).astype(F32)
        o_ref[k] = y.astype(BF16)


def _in_proj(hb, w, cosf, sinf, scale, tm):
    s, d = hb.shape
    hpb = HEADS_PER_PROJ_BLOCK
    return pl.pallas_call(
        functools.partial(_in_proj_kernel, scale=scale),
        grid=(s // tm, N_PROJ_HEADS // hpb),
        in_specs=[
            pl.BlockSpec((tm, d), lambda i, j: (i, 0)),
            pl.BlockSpec((hpb * HEAD_DIM, d), lambda i, j: (j, 0)),
            pl.BlockSpec((tm, HEAD_DIM), lambda i, j: (i, 0)),
            pl.BlockSpec((tm, HEAD_DIM), lambda i, j: (i, 0)),
        ],
        out_specs=pl.BlockSpec((hpb, tm, HEAD_DIM), lambda i, j: (j, i, 0)),
        out_shape=jax.ShapeDtypeStruct((N_PROJ_HEADS, s, HEAD_DIM), BF16),
        compiler_params=_params(("arbitrary", "arbitrary")),
        name="in_proj",
    )(hb, w, cosf, sinf)


def _gates_kernel(h_ref, w_ref, b_ref, o_ref):
    z = _nt_dot(h_ref[...], w_ref[...]) + b_ref[...]
    o_ref[...] = jax.nn.sigmoid(z)


def _gates(hb, wg, bg, tm):
    s, d = hb.shape
    n = wg.shape[0]
    return pl.pallas_call(
        _gates_kernel,
        grid=(s // tm,),
        in_specs=[pl.BlockSpec((tm, d), lambda i: (i, 0)),
                  pl.BlockSpec((n, d), lambda i: (0, 0)),
                  pl.BlockSpec((1, n), lambda i: (0, 0))],
        out_specs=pl.BlockSpec((tm, n), lambda i: (i, 0)),
        out_shape=jax.ShapeDtypeStruct((s, n), F32),
        compiler_params=_params(("arbitrary",)),
        name="nsa_gates",
    )(hb, wg, bg)


def _compress_kernel(tok_ref, pe_ref, w1_ref, b1_ref, w2_ref, b2_ref, cos_ref, sin_ref,
                     o_ref, ot_ref):
    j = pl.program_id(0)
    half = CMP_STRIDE * HEAD_DIM
    tok = tok_ref[0].astype(F32)
    pe = pe_ref[0]
    w1 = w1_ref[0]
    top = (tok + pe[:, :half]).astype(BF16)
    bot = (tok + pe[:, half:]).astype(BF16)
    u = jnp.dot(top, w1[:half], preferred_element_type=F32)
    v = jnp.dot(bot, w1[half:], preferred_element_type=F32)
    nc = u.shape[0]
    hid = u + pltpu.roll(v, nc - 1, 0) + b1_ref[0]
    hid = jax.nn.gelu(hid)
    out = jnp.dot(hid.astype(BF16), w2_ref[0], preferred_element_type=F32) + b2_ref[0]
    roped = out * cos_ref[...] + pltpu.roll(out, HEAD_DIM // 2, 1) * sin_ref[...]
    out = jnp.where(j == 0, roped, out)
    o_ref[0, 0] = out.astype(BF16)
    ot_ref[0, 0] = out.T.astype(BF16)


def _compress(tok16, pe, w1, b1, w2, b2, cos_c, sin_c):
    nc = tok16.shape[1]
    blk = CMP_BLOCK * HEAD_DIM
    g = NSA_KV_GROUPS
    return pl.pallas_call(
        _compress_kernel,
        grid=(2, g),
        in_specs=[
            pl.BlockSpec((1, nc, CMP_STRIDE * HEAD_DIM), lambda j, gi: (g * j + gi, 0, 0)),
            pl.BlockSpec((1, 1, blk), lambda j, gi: (j, 0, 0)),
            pl.BlockSpec((1, blk, CMP_HIDDEN), lambda j, gi: (j, 0, 0)),
            pl.BlockSpec((1, 1, CMP_HIDDEN), lambda j, gi: (j, 0, 0)),
            pl.BlockSpec((1, CMP_HIDDEN, HEAD_DIM), lambda j, gi: (j, 0, 0)),
            pl.BlockSpec((1, 1, HEAD_DIM), lambda j, gi: (j, 0, 0)),
            pl.BlockSpec((nc, HEAD_DIM), lambda j, gi: (0, 0)),
            pl.BlockSpec((nc, HEAD_DIM), lambda j, gi: (0, 0)),
        ],
        out_specs=[pl.BlockSpec((1, 1, nc, HEAD_DIM), lambda j, gi: (j, gi, 0, 0)),
                   pl.BlockSpec((1, 1, HEAD_DIM, nc), lambda j, gi: (j, gi, 0, 0))],
        out_shape=[jax.ShapeDtypeStruct((2, g, nc, HEAD_DIM), BF16),
                   jax.ShapeDtypeStruct((2, g, HEAD_DIM, nc), BF16)],
        compiler_params=_params(("arbitrary", "arbitrary")),
        name="nsa_compress",
    )(tok16, pe, w1, b1, w2, b2, cos_c, sin_c)


def _dil_kernel(q_ref, k_ref, v_ref, o_ref, kpad, vpad, bias_ref, *, tq, padk):
    h = pl.program_id(0)
    i = pl.program_id(1)
    w = padk + tq

    @pl.when((h == 0) & (i == 0))
    def _():
        r = lax.broadcasted_iota(jnp.int32, (tq, w), 0)
        c = lax.broadcasted_iota(jnp.int32, (tq, w), 1)
        d = r + padk - c
        cnt = jnp.zeros((tq, w), F32)
        for window, dil in DIL_PAIRS:
            hit = (d >= 0) & (d <= window) & ((d & (dil - 1)) == 0)
            cnt = cnt + jnp.where(hit, 1.0, 0.0)
        bias_ref[...] = jnp.where(cnt > 0.0, jnp.log(jnp.maximum(cnt, 1.0)), NEG)

    @pl.when(i == 0)
    def _():
        kpad[0:padk, :] = jnp.zeros((padk, HEAD_DIM), BF16)
        vpad[0:padk, :] = jnp.zeros((padk, HEAD_DIM), BF16)
        kpad[padk:, :] = k_ref[0]
        vpad[padk:, :] = v_ref[0]

    q0 = pl.multiple_of(i * tq, tq)
    s = _nt_dot(q_ref[0], kpad[pl.ds(q0, w), :]) + bias_ref[...]
    col = lax.broadcasted_iota(jnp.int32, (1, w), 1)
    s = jnp.where(col >= padk - q0, s, NEG)
    p = jnp.exp(s - jnp.max(s, axis=-1, keepdims=True))
    l = jnp.sum(p, axis=-1, keepdims=True)
    o = jnp.dot(p.astype(BF16), vpad[pl.ds(q0, w), :], preferred_element_type=F32)
    o_ref[...] = (o / l).astype(BF16)


def _dilated(heads, tq):
    _, s, _ = heads.shape
    padk = max(wd for wd, _ in DIL_PAIRS)
    return pl.pallas_call(
        functools.partial(_dil_kernel, tq=tq, padk=padk),
        grid=(N_HEADS_DIL, s // tq),
        in_specs=[
            pl.BlockSpec((1, tq, HEAD_DIM), lambda h, i: (H_QA + h, i, 0)),
            pl.BlockSpec((1, s, HEAD_DIM), lambda h, i: (H_KA + h, 0, 0)),
            pl.BlockSpec((1, s, HEAD_DIM), lambda h, i: (H_VA + h, 0, 0)),
        ],
        out_specs=pl.BlockSpec((tq, HEAD_DIM), lambda h, i: (i, h)),
        out_shape=jax.ShapeDtypeStruct((s, D_DIL), BF16),
        scratch_shapes=[pltpu.VMEM((padk + s, HEAD_DIM), BF16),
                        pltpu.VMEM((padk + s, HEAD_DIM), BF16),
                        pltpu.VMEM((tq, padk + tq), F32)],
        compiler_params=_params(("arbitrary", "arbitrary")),
        name="dilated_attn",
    )(heads, heads, heads)


def _add_per_head(s, bias, hg):
    tq = bias.shape[0]
    return jnp.concatenate([s[u * tq:(u + 1) * tq] + bias for u in range(hg)], axis=0)


def _nsa_kernel(*refs, s_len, nsp, tq, kt, k_sel):
    q_refs = refs[:NSA_GROUP]
    (kc_ref, vct_ref, ks_ref, vs_ref, kw_ref, vw_ref, gate_ref, ovt_ref,
     o_ref, kaug, vaug, vwaug, cbias, wbias, s_a, s_b, m_ref, acc_ref,
     part_ref) = refs[NSA_GROUP:]
    g = pl.program_id(0)
    qb = pl.program_id(1)
    t0 = qb * tq
    hg = NSA_GROUP
    rows = hg * tq
    ncp = kc_ref.shape[2]
    ww = NSA_WINDOW + tq

    @pl.when((g == 0) & (qb == 0))
    def _():
        for r in range(kt // tq):
            d = (lax.broadcasted_iota(jnp.int32, (tq, kt), 0) + r * tq
                 - lax.broadcasted_iota(jnp.int32, (tq, kt), 1))
            cbias[r] = jnp.where(d >= 0, 0.0, NEG)
        for w in range(NSA_WINDOW // tq + 1):
            d = (lax.broadcasted_iota(jnp.int32, (tq, ww), 0) + w * tq
                 - lax.broadcasted_iota(jnp.int32, (tq, ww), 1))
            wbias[w] = jnp.where((d >= 0) & (d < NSA_WINDOW), 0.0, NEG)

    @pl.when(qb == 0)
    def _():
        kaug[:, 0:HEAD_DIM] = ks_ref[0]
        key = lax.broadcasted_iota(jnp.int32, (s_len, nsp), 0)
        blk = lax.broadcasted_iota(jnp.int32, (s_len, nsp), 1)
        kaug[:, HEAD_DIM:] = jnp.where(blk == key // SEL_BLOCK, 1.0, 0.0).astype(BF16)
        ones = jnp.ones((s_len, HEAD_DIM), BF16)
        vaug[:, 0:HEAD_DIM] = vs_ref[0]
        vaug[:, HEAD_DIM:] = ones
        vwaug[:, 0:HEAD_DIM] = vw_ref[0]
        vwaug[:, HEAD_DIM:] = ones

    gate = gate_ref[...]
    q5 = jnp.concatenate([r[0] for r in q_refs], axis=0)

    w0 = pl.multiple_of(jnp.maximum(t0 - NSA_WINDOW, 0), tq)
    sw = _nt_dot(q5, kw_ref[0, pl.ds(w0, ww), :])
    sw = _add_per_head(sw, wbias[jnp.minimum(qb, NSA_WINDOW // tq)], hg)
    pw = jnp.exp(sw - jnp.max(sw, axis=-1, keepdims=True))
    acc_w = jnp.dot(pw.astype(BF16), vwaug[pl.ds(w0, ww), :], preferred_element_type=F32)
    o_win = acc_w[:, :HEAD_DIM] / acc_w[:, HEAD_DIM:]

    kc = kc_ref[0, 0]
    vct = vct_ref[0, 0]
    n_io = lax.broadcasted_iota(jnp.int32, (ncp, tq), 0)
    t_io = t0 + lax.broadcasted_iota(jnp.int32, (ncp, tq), 1)
    cmask = (n_io * CMP_STRIDE + (CMP_BLOCK - 1) <= t_io) & (n_io < ncp - 1)
    psum = jnp.zeros((ncp, tq), F32)
    for u in range(hg):
        st = jnp.where(cmask, _nt_dot(kc, q_refs[u][0]), NEG)
        m = jnp.maximum(jnp.max(st, axis=0, keepdims=True), 0.1 * NEG)
        e = jnp.exp(st - m)
        pt = e / jnp.maximum(jnp.sum(e, axis=0, keepdims=True), 1e-30)
        psum = psum + pt
        o_cmp = jnp.dot(vct, pt.astype(BF16), preferred_element_type=F32).T
        c = N_NSA_BRANCHES * u
        part_ref[:, u * HEAD_DIM:(u + 1) * HEAD_DIM] = (
            gate[:, c:c + 1] * o_cmp + gate[:, c + 2:c + 3] * o_win[u * tq:(u + 1) * tq])

    p_hi = psum.astype(BF16)
    p_lo = (psum - p_hi.astype(F32)).astype(BF16)
    ovt = ovt_ref[...]
    imp = (jnp.dot(ovt, p_hi, preferred_element_type=F32)
           + jnp.dot(ovt, p_lo, preferred_element_type=F32))
    j_io = lax.broadcasted_iota(jnp.int32, (nsp, tq), 0)
    t_sel = t0 + lax.broadcasted_iota(jnp.int32, (nsp, tq), 1)
    valid = j_io * SEL_BLOCK <= t_sel
    cur = t_sel // SEL_BLOCK
    forced = (j_io == 0) | (j_io == cur) | (j_io == cur - 1)
    score = jnp.where(valid & jnp.logical_not(forced), imp, NEG)
    picked = jnp.where(forced, 1.0, 0.0)
    for _ in range(k_sel - 3):
        m = jnp.max(score, axis=0, keepdims=True)
        first = jnp.min(jnp.where(score == m, j_io, nsp), axis=0, keepdims=True)
        hit = j_io == first
        picked = jnp.where(hit, 1.0, picked)
        score = jnp.where(hit, -jnp.inf, score)
    sel_t = jnp.where(valid, picked, 0.0)
    sel_off = jnp.where(sel_t.T > 0.0, 0.0, SEL_OFF).astype(BF16)
    qaug = jnp.concatenate([q5, jnp.concatenate([sel_off] * hg, axis=0)], axis=1)

    def sel_scores(kti):
        k0 = pl.multiple_of(kti * kt, kt)
        return _nt_dot(qaug, kaug[pl.ds(k0, kt), :])

    def sel_update(s_ref, kti, diagonal):
        s = s_ref[...]
        if diagonal:
            s = _add_per_head(s, cbias[qb % (kt // tq)], hg)
        k0 = pl.multiple_of(kti * kt, kt)
        m_i = m_ref[...]
        m_new = jnp.maximum(m_i, jnp.max(s, axis=-1, keepdims=True))
        p = jnp.exp(s - m_new)
        pv = jnp.dot(p.astype(BF16), vaug[pl.ds(k0, kt), :], preferred_element_type=F32)
        acc_ref[...] = jnp.exp(m_i - m_new) * acc_ref[...] + pv
        m_ref[...] = m_new

    def sel_pair(pi, _):
        t = 2 * pi
        s_b[...] = sel_scores(t + 1)
        sel_update(s_a, t, False)
        s_a[...] = sel_scores(t + 2)
        sel_update(s_b, t + 1, False)
        return 0

    last = (t0 + tq + kt - 1) // kt - 1
    m_ref[...] = jnp.full((rows, 1), NEG, F32)
    acc_ref[...] = jnp.zeros((rows, 2 * HEAD_DIM), F32)
    s_a[...] = sel_scores(0)
    lax.fori_loop(0, last // 2, sel_pair, 0)
    t_even = 2 * (last // 2)

    @pl.when(last % 2 == 1)
    def _():
        s_b[...] = sel_scores(t_even + 1)
        sel_update(s_a, t_even, False)
        sel_update(s_b, t_even + 1, True)

    @pl.when(last % 2 == 0)
    def _():
        sel_update(s_a, t_even, True)

    gate = gate_ref[...]
    for u in range(hg):
        c = N_NSA_BRANCHES * u + 1
        acc_u = acc_ref[u * tq:(u + 1) * tq, :]
        o_slc = acc_u[:, :HEAD_DIM] / acc_u[:, HEAD_DIM:]
        cols = slice(u * HEAD_DIM, (u + 1) * HEAD_DIM)
        o_ref[:, cols] = (part_ref[:, cols] + gate[:, c:c + 1] * o_slc).astype(BF16)


def _nsa(heads, kcv, vct, gates, ovt, tq, kt):
    _, s, _ = heads.shape
    ncp = kcv.shape[2]
    nsp = ovt.shape[0]
    hg = NSA_GROUP
    k_sel = min(N_SELECT, s // SEL_BLOCK)
    assert k_sel >= 3 and kt % tq == 0 and NSA_WINDOW % tq == 0 and s % kt == 0
    q_map = lambda u, g, qb: (H_QN + hg * g + u, qb, 0)
    full = lambda hbase: pl.BlockSpec((1, s, HEAD_DIM), lambda g, qb: (hbase + g, 0, 0),
                                      pipeline_mode=pl.Buffered(1))
    return pl.pallas_call(
        functools.partial(_nsa_kernel, s_len=s, nsp=nsp, tq=tq, kt=kt, k_sel=k_sel),
        grid=(NSA_KV_GROUPS, s // tq),
        in_specs=[
            *[pl.BlockSpec((1, tq, HEAD_DIM), functools.partial(q_map, u)) for u in range(hg)],
            pl.BlockSpec((1, 1, ncp, HEAD_DIM), lambda g, qb: (0, g, 0, 0)),
            pl.BlockSpec((1, 1, HEAD_DIM, ncp), lambda g, qb: (1, g, 0, 0)),
            full(H_KS), full(H_VS), full(H_KW), full(H_VW),
            pl.BlockSpec((tq, HEAD_DIM), lambda g, qb: (qb, g)),
            pl.BlockSpec((nsp, ncp), lambda g, qb: (0, 0)),
        ],
        out_specs=pl.BlockSpec((tq, hg * HEAD_DIM), lambda g, qb: (qb, g)),
        out_shape=jax.ShapeDtypeStruct((s, D_NSA), BF16),
        scratch_shapes=[pltpu.VMEM((s, HEAD_DIM + nsp), BF16),
                        pltpu.VMEM((s, 2 * HEAD_DIM), BF16),
                        pltpu.VMEM((s, 2 * HEAD_DIM), BF16),
                        pltpu.VMEM((kt // tq, tq, kt), F32),
                        pltpu.VMEM((NSA_WINDOW // tq + 1, tq, NSA_WINDOW + tq), F32),
                        pltpu.VMEM((hg * tq, kt), F32),
                        pltpu.VMEM((hg * tq, kt), F32),
                        pltpu.VMEM((hg * tq, 1), F32),
                        pltpu.VMEM((hg * tq, 2 * HEAD_DIM), F32),
                        pltpu.VMEM((tq, hg * HEAD_DIM), F32)],
        compiler_params=_params(("arbitrary", "arbitrary")),
        name="nsa_attn",
    )(*([heads] * hg), kcv, vct, heads, heads, heads, heads, gates, ovt)


def _out_proj_ln_kernel(a_ref, b_ref, wa_ref, wb_ref, h_ref, g_ref, be_ref, o_ref, *, alpha):
    mix = (jnp.dot(a_ref[...], wa_ref[...], preferred_element_type=F32)
           + jnp.dot(b_ref[...], wb_ref[...], preferred_element_type=F32))
    o_ref[...] = _layer_norm(alpha * h_ref[...] + mix, g_ref[...], be_ref[...])


def _out_proj_ln(mix_a, mix_b, wa, wb, h, g, b, alpha, tm):
    s, d = h.shape
    return pl.pallas_call(
        functools.partial(_out_proj_ln_kernel, alpha=alpha),
        grid=(s // tm,),
        in_specs=[
            pl.BlockSpec((tm, D_DIL), lambda i: (i, 0)),
            pl.BlockSpec((tm, D_NSA), lambda i: (i, 0)),
            pl.BlockSpec((D_DIL, d), lambda i: (0, 0)),
            pl.BlockSpec((D_NSA, d), lambda i: (0, 0)),
            pl.BlockSpec((tm, d), lambda i: (i, 0)),
            pl.BlockSpec((1, d), lambda i: (0, 0)),
            pl.BlockSpec((1, d), lambda i: (0, 0)),
        ],
        out_specs=pl.BlockSpec((tm, d), lambda i: (i, 0)),
        out_shape=jax.ShapeDtypeStruct((s, d), F32),
        compiler_params=_params(("arbitrary",)),
        name="out_proj_ln",
    )(mix_a, mix_b, wa, wb, h, g, b)


def _overlap_t(s):
    n_cmp = (s - CMP_BLOCK) // CMP_STRIDE + 1
    n_sel = s // SEL_BLOCK
    ncp = s // CMP_STRIDE
    nsp = -(-n_sel // HEAD_DIM) * HEAD_DIM
    c_lo = np.arange(n_cmp) * CMP_STRIDE
    c_hi = c_lo + CMP_BLOCK - 1
    s_lo = (np.arange(n_sel) * SEL_BLOCK)[:, None]
    ov = np.zeros((nsp, ncp), np.float32)
    ov[:n_sel, :n_cmp] = (c_lo[None, :] <= s_lo + SEL_BLOCK - 1) & (c_hi[None, :] >= s_lo)
    return jnp.asarray(ov, BF16)


def kernel(x, positions, ln1_g, ln1_b, ffn1_w1, ffn1_w3, ffn1_w2, w_in, gate_b, cmp_pe, cmp_w1,
           cmp_b1, cmp_w2, cmp_b2, w_out, ln2_g, ln2_b, ffn2_w1, ffn2_w3, ffn2_w2, ln3_g, ln3_b):
    bsz, s, d = x.shape
    assert bsz == 1 and d == (N_HEADS_DIL + N_HEADS_NSA) * HEAD_DIM
    assert s % 512 == 0 and s >= NSA_WINDOW + QBLK
    alpha = (2.0 * DEPTH) ** 0.25
    scale = HEAD_DIM ** -0.5
    row = lambda v: v.reshape(1, -1)
    tm = 512
    ffn_tm, ffn_tf = 1024, 256

    inv_freq = ROPE_THETA ** (-jnp.arange(0, HEAD_DIM, 2, dtype=F32) / HEAD_DIM)
    invf = jnp.concatenate([inv_freq, inv_freq]).reshape(1, HEAD_DIM)
    pos = positions[0].astype(F32).reshape(s, 1)
    cosf, sinf = _rope_tables(pos, pos, invf, 512)
    ncp = s // CMP_STRIDE
    n_cmp = (s - CMP_BLOCK) // CMP_STRIDE + 1
    pos_lo = jnp.pad(pos[0:n_cmp * CMP_STRIDE:CMP_STRIDE], ((0, ncp - n_cmp), (0, 0)))
    pos_hi = jnp.pad(pos[CMP_BLOCK - 1::CMP_STRIDE][:n_cmp], ((0, ncp - n_cmp), (0, 0)))
    cos_c, sin_c = _rope_tables(pos_lo, pos_hi, invf, ncp)

    g_off = 3 * D_DIL + D_NSA + 6 * KV_W
    per_g = NSA_GROUP * N_NSA_BRANCHES
    w_in_t = w_in[0].T
    wg = jnp.zeros((NSA_KV_GROUPS * HEAD_DIM, d), F32)
    bg = jnp.zeros((1, NSA_KV_GROUPS * HEAD_DIM), F32)
    for g in range(NSA_KV_GROUPS):
        wg = wg.at[g * HEAD_DIM:g * HEAD_DIM + per_g].set(
            w_in_t[g_off + g * per_g:g_off + (g + 1) * per_g])
        bg = bg.at[0, g * HEAD_DIM:g * HEAD_DIM + per_g].set(gate_b[0][g * per_g:(g + 1) * per_g])
    wg = wg.astype(BF16)

    h0 = x[0]
    h1, h1b = _ffn_ln(h0, ffn1_w1[0], ffn1_w3[0], ffn1_w2[0], row(ln1_g[0]), row(ln1_b[0]),
                      alpha, ffn_tm, ffn_tf, True)

    heads = _in_proj(h1b, w_in_t, cosf, sinf, scale, 1024)
    gates = _gates(h1b, wg, bg, tm)
    tok16 = heads[H_KC:H_KS].reshape(2 * NSA_KV_GROUPS, ncp, CMP_STRIDE * HEAD_DIM)
    kcv, kcv_t = _compress(
        tok16, cmp_pe[0].reshape(2, 1, CMP_BLOCK * HEAD_DIM), cmp_w1[0].astype(BF16),
        cmp_b1[0].reshape(2, 1, CMP_HIDDEN), cmp_w2[0].astype(BF16),
        cmp_b2[0].reshape(2, 1, HEAD_DIM), cos_c, sin_c)
    mix_a = _dilated(heads, 256)
    mix_b = _nsa(heads, kcv, kcv_t, gates, _overlap_t(s), 256, 1024)

    wo = w_out[0].astype(BF16)
    h2 = _out_proj_ln(mix_a, mix_b, wo[:D_DIL], wo[D_DIL:], h1, row(ln2_g[0]), row(ln2_b[0]),
                      alpha, tm)
    (h3,) = _ffn_ln(h2, ffn2_w1[0], ffn2_w3[0], ffn2_w2[0], row(ln3_g[0]), row(ln3_b[0]),
                    alpha, ffn_tm, ffn_tf, False)
    return h3.reshape(bsz, s, d)
```

```python
import functools

import jax
import jax.numpy as jnp
import numpy as np
from jax import lax
from jax.experimental import pallas as pl
from jax.experimental.pallas import tpu as pltpu

HEAD_DIM = 128
N_HEADS_DIL = 6
N_HEADS_NSA = 10
NSA_KV_GROUPS = 2
NSA_GROUP = N_HEADS_NSA // NSA_KV_GROUPS
N_NSA_BRANCHES = 3
DIL_PAIRS = ((128, 1), (512, 4), (2048, 16))
CMP_BLOCK = 32
CMP_STRIDE = 16
CMP_HIDDEN = 256
SEL_BLOCK = 64
N_SELECT = 16
NSA_WINDOW = 512
ROPE_THETA = 10000.0
QBLK = 128
LN_EPS = 1e-5
NEG = -1e30
FORCE_BONUS = 1e4
SEL_OFF = -1e9
DEPTH = 1

D_DIL = N_HEADS_DIL * HEAD_DIM
D_NSA = N_HEADS_NSA * HEAD_DIM
KV_W = NSA_KV_GROUPS * HEAD_DIM
N_GATES = N_HEADS_NSA * N_NSA_BRANCHES

H_QA, H_KA, H_VA, H_QN = 0, 6, 12, 18
H_KC, H_VC, H_KS, H_VS, H_KW, H_VW = 28, 30, 32, 34, 36, 38
N_PROJ_HEADS = 40
HEADS_PER_PROJ_BLOCK = 4

VMEM_LIMIT_BYTES = 56 * 1024 * 1024

F32 = jnp.float32
BF16 = jnp.bfloat16


def _params(sem, vmem=VMEM_LIMIT_BYTES):
    return pltpu.CompilerParams(dimension_semantics=sem, vmem_limit_bytes=vmem)


def _nt_dot(a, b):
    return lax.dot_general(a, b, (((1,), (1,)), ((), ())), preferred_element_type=F32)


def _layer_norm(y, g, b):
    mu = jnp.mean(y, axis=-1, keepdims=True)
    yc = y - mu
    var = jnp.mean(yc * yc, axis=-1, keepdims=True)
    return yc * lax.rsqrt(var + LN_EPS) * g + b


def _rope_table_kernel(pa_ref, pb_ref, invf_ref, cos_ref, sin_ref):
    pos = (pa_ref[...] + pb_ref[...]) * 0.5
    ang = pos * invf_ref[...]
    lane = lax.broadcasted_iota(jnp.int32, ang.shape, 1)
    cos_ref[...] = jnp.cos(ang)
    sin_ref[...] = jnp.where(lane < HEAD_DIM // 2, -1.0, 1.0) * jnp.sin(ang)


def _rope_tables(pos_a, pos_b, invf, tile):
    n = pos_a.shape[0]
    spec_p = pl.BlockSpec((tile, 1), lambda i: (i, 0))
    spec_t = pl.BlockSpec((tile, HEAD_DIM), lambda i: (i, 0))
    return pl.pallas_call(
        _rope_table_kernel,
        grid=(n // tile,),
        in_specs=[spec_p, spec_p, pl.BlockSpec((1, HEAD_DIM), lambda i: (0, 0))],
        out_specs=[spec_t, spec_t],
        out_shape=[jax.ShapeDtypeStruct((n, HEAD_DIM), F32)] * 2,
        compiler_params=_params(("arbitrary",)),
        name="rope_tables",
    )(pos_a, pos_b, invf)


def _ffn_ln_kernel(h_ref, w1_ref, w3_ref, w2_ref, g_ref, b_ref, o_ref, *rest, alpha, nf):
    hb_ref = rest[-1]
    f = pl.program_id(1)

    @pl.when(f == 0)
    def _():
        hb_ref[...] = h_ref[...].astype(BF16)
        o_ref[...] = jnp.zeros_like(o_ref)

    hb = hb_ref[...]
    a = jnp.dot(hb, w1_ref[...].astype(BF16), preferred_element_type=F32)
    b = jnp.dot(hb, w3_ref[...].astype(BF16), preferred_element_type=F32)
    act = (a * jax.nn.sigmoid(a)) * b
    o_ref[...] += jnp.dot(act.astype(BF16), w2_ref[...].astype(BF16),
                          preferred_element_type=F32)

    @pl.when(f == nf - 1)
    def _():
        y = alpha * h_ref[...] + 0.5 * o_ref[...]
        out = _layer_norm(y, g_ref[...], b_ref[...])
        o_ref[...] = out
        if len(rest) == 2:
            rest[0][...] = out.astype(BF16)


def _ffn_ln(h, w1, w3, w2, g, b, alpha, tm, tf, emit_bf16):
    s, d = h.shape
    dff = w1.shape[1]
    nf = dff // tf
    once = pl.Buffered(1)
    row_spec = lambda: pl.BlockSpec((tm, d), lambda i, f: (i, 0), pipeline_mode=once)
    out_specs = [row_spec()]
    out_shape = [jax.ShapeDtypeStruct((s, d), F32)]
    if emit_bf16:
        out_specs.append(row_spec())
        out_shape.append(jax.ShapeDtypeStruct((s, d), BF16))
    return pl.pallas_call(
        functools.partial(_ffn_ln_kernel, alpha=alpha, nf=nf),
        grid=(s // tm, nf),
        in_specs=[
            pl.BlockSpec((tm, d), lambda i, f: (i, 0)),
            pl.BlockSpec((d, tf), lambda i, f: (0, f)),
            pl.BlockSpec((d, tf), lambda i, f: (0, f)),
            pl.BlockSpec((tf, d), lambda i, f: (f, 0)),
            pl.BlockSpec((1, d), lambda i, f: (0, 0)),
            pl.BlockSpec((1, d), lambda i, f: (0, 0)),
        ],
        out_specs=out_specs,
        out_shape=out_shape,
        scratch_shapes=[pltpu.VMEM((tm, d), BF16)],
        compiler_params=_params(("arbitrary", "arbitrary")),
        name="ffn_ln",
    )(h, w1, w3, w2, g, b)


def _in_proj_kernel(h_ref, w_ref, cos_ref, sin_ref, o_ref, *, scale):
    j = pl.program_id(1)
    hpb = HEADS_PER_PROJ_BLOCK
    acc = _nt_dot(h_ref[...], w_ref[...].astype(BF16))
    cos = cos_ref[...]
    sin = sin_ref[...]
    for k in range(hpb):
        hh = j * hpb + k
        is_q = (hh < H_KA) | ((hh >= H_QN) & (hh < H_KC))
        rope = is_q | (hh < H_VA) | ((hh >= H_KS) & (hh < H_VS)) | ((hh >= H_KW) & (hh < H_VW))
        x = acc[:, k * HEAD_DIM:(k + 1) * HEAD_DIM]
        xr = x * cos + pltpu.roll(x, HEAD_DIM // 2, 1) * sin
        y = jnp.where(rope, xr, x) * jnp.where(is_q, scale, 1.0).astype(F32)
        o_ref[k] = y.astype(BF16)


def _in_proj(hb, w, cosf, sinf, scale, tm):
    s, d = hb.shape
    hpb = HEADS_PER_PROJ_BLOCK
    return pl.pallas_call(
        functools.partial(_in_proj_kernel, scale=scale),
        grid=(s // tm, N_PROJ_HEADS // hpb),
        in_specs=[
            pl.BlockSpec((tm, d), lambda i, j: (i, 0)),
            pl.BlockSpec((hpb * HEAD_DIM, d), lambda i, j: (j, 0)),
            pl.BlockSpec((tm, HEAD_DIM), lambda i, j: (i, 0)),
            pl.BlockSpec((tm, HEAD_DIM), lambda i, j: (i, 0)),
        ],
        out_specs=pl.BlockSpec((hpb, tm, HEAD_DIM), lambda i, j: (j, i, 0)),
        out_shape=jax.ShapeDtypeStruct((N_PROJ_HEADS, s, HEAD_DIM), BF16),
        compiler_params=_params(("arbitrary", "arbitrary")),
        name="in_proj",
    )(hb, w, cosf, sinf)


def _gates_kernel(h_ref, w_ref, b_ref, o_ref):
    z = _nt_dot(h_ref[...], w_ref[...]) + b_ref[...]
    o_ref[...] = jax.nn.sigmoid(z)


def _gates(hb, wg, bg, tm):
    s, d = hb.shape
    n = wg.shape[0]
    return pl.pallas_call(
        _gates_kernel,
        grid=(s // tm,),
        in_specs=[pl.BlockSpec((tm, d), lambda i: (i, 0)),
                  pl.BlockSpec((n, d), lambda i: (0, 0)),
                  pl.BlockSpec((1, n), lambda i: (0, 0))],
        out_specs=pl.BlockSpec((tm, n), lambda i: (i, 0)),
        out_shape=jax.ShapeDtypeStruct((s, n), F32),
        compiler_params=_params(("arbitrary",)),
        name="nsa_gates",
    )(hb, wg, bg)


def _compress_kernel(tok_ref, pe_ref, w1_ref, b1_ref, w2_ref, b2_ref, cos_ref, sin_ref,
                     o_ref, ot_ref):
    j = pl.program_id(0)
    half = CMP_STRIDE * HEAD_DIM
    tok = tok_ref[0].astype(F32)
    pe = pe_ref[0]
    w1 = w1_ref[0]
    top = (tok + pe[:, :half]).astype(BF16)
    bot = (tok + pe[:, half:]).astype(BF16)
    u = jnp.dot(top, w1[:half], preferred_element_type=F32)
    v = jnp.dot(bot, w1[half:], preferred_element_type=F32)
    nc = u.shape[0]
    hid = u + pltpu.roll(v, nc - 1, 0) + b1_ref[0]
    hid = jax.nn.gelu(hid)
    out = jnp.dot(hid.astype(BF16), w2_ref[0], preferred_element_type=F32) + b2_ref[0]
    roped = out * cos_ref[...] + pltpu.roll(out, HEAD_DIM // 2, 1) * sin_ref[...]
    out = jnp.where(j == 0, roped, out)
    o_ref[0, 0] = out.astype(BF16)
    ot_ref[0, 0] = out.T.astype(BF16)


def _compress(tok16, pe, w1, b1, w2, b2, cos_c, sin_c):
    nc = tok16.shape[1]
    blk = CMP_BLOCK * HEAD_DIM
    g = NSA_KV_GROUPS
    return pl.pallas_call(
        _compress_kernel,
        grid=(2, g),
        in_specs=[
            pl.BlockSpec((1, nc, CMP_STRIDE * HEAD_DIM), lambda j, gi: (g * j + gi, 0, 0)),
            pl.BlockSpec((1, 1, blk), lambda j, gi: (j, 0, 0)),
            pl.BlockSpec((1, blk, CMP_HIDDEN), lambda j, gi: (j, 0, 0)),
            pl.BlockSpec((1, 1, CMP_HIDDEN), lambda j, gi: (j, 0, 0)),
            pl.BlockSpec((1, CMP_HIDDEN, HEAD_DIM), lambda j, gi: (j, 0, 0)),
            pl.BlockSpec((1, 1, HEAD_DIM), lambda j, gi: (j, 0, 0)),
            pl.BlockSpec((nc, HEAD_DIM), lambda j, gi: (0, 0)),
            pl.BlockSpec((nc, HEAD_DIM), lambda j, gi: (0, 0)),
        ],
        out_specs=[pl.BlockSpec((1, 1, nc, HEAD_DIM), lambda j, gi: (j, gi, 0, 0)),
                   pl.BlockSpec((1, 1, HEAD_DIM, nc), lambda j, gi: (j, gi, 0, 0))],
        out_shape=[jax.ShapeDtypeStruct((2, g, nc, HEAD_DIM), BF16),
                   jax.ShapeDtypeStruct((2, g, HEAD_DIM, nc), BF16)],
        compiler_params=_params(("arbitrary", "arbitrary")),
        name="nsa_compress",
    )(tok16, pe, w1, b1, w2, b2, cos_c, sin_c)


def _dil_kernel(q_ref, k_ref, v_ref, o_ref, kpad, vpad, bias_ref, *, tq, n_sub, padk):
    h = pl.program_id(0)
    i = pl.program_id(1)
    w = padk + tq

    @pl.when((h == 0) & (i == 0))
    def _():
        r = lax.broadcasted_iota(jnp.int32, (tq, w), 0)
        c = lax.broadcasted_iota(jnp.int32, (tq, w), 1)
        d = r + padk - c
        cnt = jnp.zeros((tq, w), F32)
        for window, dil in DIL_PAIRS:
            hit = (d >= 0) & (d <= window) & ((d & (dil - 1)) == 0)
            cnt = cnt + jnp.where(hit, 1.0, 0.0)
        bias_ref[...] = jnp.where(cnt > 0.0, jnp.log(jnp.maximum(cnt, 1.0)), NEG)

    @pl.when(i == 0)
    def _():
        kpad[0:padk, :] = jnp.zeros((padk, HEAD_DIM), BF16)
        vpad[0:padk, :] = jnp.zeros((padk, HEAD_DIM), BF16)
        kpad[padk:, :] = k_ref[0]
        vpad[padk:, :] = v_ref[0]

    col = lax.broadcasted_iota(jnp.int32, (1, w), 1)
    for part in range(n_sub):
        q0 = pl.multiple_of((i * n_sub + part) * tq, tq)
        q = q_ref[0, part * tq:(part + 1) * tq, :]
        s = _nt_dot(q, kpad[pl.ds(q0, w), :]) + bias_ref[...]
        s = jnp.where(col >= padk - q0, s, NEG)
        p = jnp.exp(s - jnp.max(s, axis=-1, keepdims=True))
        l = jnp.sum(p, axis=-1, keepdims=True)
        o = jnp.dot(p.astype(BF16), vpad[pl.ds(q0, w), :], preferred_element_type=F32)
        o_ref[part * tq:(part + 1) * tq, :] = (o / l).astype(BF16)


def _dilated(heads, tq, n_sub):
    _, s, _ = heads.shape
    padk = max(wd for wd, _ in DIL_PAIRS)
    tstep = tq * n_sub
    return pl.pallas_call(
        functools.partial(_dil_kernel, tq=tq, n_sub=n_sub, padk=padk),
        grid=(N_HEADS_DIL, s // tstep),
        in_specs=[
            pl.BlockSpec((1, tstep, HEAD_DIM), lambda h, i: (H_QA + h, i, 0)),
            pl.BlockSpec((1, s, HEAD_DIM), lambda h, i: (H_KA + h, 0, 0)),
            pl.BlockSpec((1, s, HEAD_DIM), lambda h, i: (H_VA + h, 0, 0)),
        ],
        out_specs=pl.BlockSpec((tstep, HEAD_DIM), lambda h, i: (i, h)),
        out_shape=jax.ShapeDtypeStruct((s, D_DIL), BF16),
        scratch_shapes=[pltpu.VMEM((padk + s, HEAD_DIM), BF16),
                        pltpu.VMEM((padk + s, HEAD_DIM), BF16),
                        pltpu.VMEM((tq, padk + tq), F32)],
        compiler_params=_params(("arbitrary", "arbitrary")),
        name="dilated_attn",
    )(heads, heads, heads)


def _add_per_head(s, bias, hg):
    tq = bias.shape[0]
    return jnp.concatenate([s[u * tq:(u + 1) * tq] + bias for u in range(hg)], axis=0)


def _nsa_kernel(*refs, s_len, nsp, tq, kt, k_sel):
    q_refs = refs[:NSA_GROUP]
    (kc_ref, vct_ref, ks_ref, vs_ref, kw_ref, vw_ref, gate_ref, ovt_ref,
     o_ref, kaug, vaug, vwaug, cbias, wbias, s_a, s_b, m_ref, acc_ref,
     part_ref) = refs[NSA_GROUP:]
    g = pl.program_id(0)
    qb = pl.program_id(1)
    t0 = qb * tq
    hg = NSA_GROUP
    rows = hg * tq
    ncp = kc_ref.shape[2]
    ww = NSA_WINDOW + tq

    @pl.when((g == 0) & (qb == 0))
    def _():
        for r in range(kt // tq):
            d = (lax.broadcasted_iota(jnp.int32, (tq, kt), 0) + r * tq
                 - lax.broadcasted_iota(jnp.int32, (tq, kt), 1))
            cbias[r] = jnp.where(d >= 0, 0.0, NEG)
        for w in range(NSA_WINDOW // tq + 1):
            d = (lax.broadcasted_iota(jnp.int32, (tq, ww), 0) + w * tq
                 - lax.broadcasted_iota(jnp.int32, (tq, ww), 1))
            wbias[w] = jnp.where((d >= 0) & (d < NSA_WINDOW), 0.0, NEG)

    @pl.when(qb == 0)
    def _():
        kaug[:, 0:HEAD_DIM] = ks_ref[0]
        key = lax.broadcasted_iota(jnp.int32, (s_len, nsp), 0)
        blk = lax.broadcasted_iota(jnp.int32, (s_len, nsp), 1)
        kaug[:, HEAD_DIM:] = jnp.where(blk == key // SEL_BLOCK, 1.0, 0.0).astype(BF16)
        ones = jnp.ones((s_len, HEAD_DIM), BF16)
        vaug[:, 0:HEAD_DIM] = vs_ref[0]
        vaug[:, HEAD_DIM:] = ones
        vwaug[:, 0:HEAD_DIM] = vw_ref[0]
        vwaug[:, HEAD_DIM:] = ones

    gate = gate_ref[...]
    q5 = jnp.concatenate([r[0] for r in q_refs], axis=0)

    w0 = pl.multiple_of(jnp.maximum(t0 - NSA_WINDOW, 0), tq)
    sw = _nt_dot(q5, kw_ref[0, pl.ds(w0, ww), :])
    sw = _add_per_head(sw, wbias[jnp.minimum(qb, NSA_WINDOW // tq)], hg)
    pw = jnp.exp(sw - jnp.max(sw, axis=-1, keepdims=True))
    acc_w = jnp.dot(pw.astype(BF16), vwaug[pl.ds(w0, ww), :], preferred_element_type=F32)
    o_win = acc_w[:, :HEAD_DIM] / acc_w[:, HEAD_DIM:]

    kc = kc_ref[0, 0]
    vct = vct_ref[0, 0]
    n_io = lax.broadcasted_iota(jnp.int32, (ncp, tq), 0)
    t_io = t0 + lax.broadcasted_iota(jnp.int32, (ncp, tq), 1)
    cmask = (n_io * CMP_STRIDE + (CMP_BLOCK - 1) <= t_io) & (n_io < ncp - 1)
    psum = jnp.zeros((ncp, tq), F32)
    for u in range(hg):
        st = jnp.where(cmask, _nt_dot(kc, q_refs[u][0]), NEG)
        m = jnp.maximum(jnp.max(st, axis=0, keepdims=True), 0.1 * NEG)
        e = jnp.exp(st - m)
        pt = e / jnp.maximum(jnp.sum(e, axis=0, keepdims=True), 1e-30)
        psum = psum + pt
        o_cmp = jnp.dot(vct, pt.astype(BF16), preferred_element_type=F32).T
        c = N_NSA_BRANCHES * u
        part_ref[:, u * HEAD_DIM:(u + 1) * HEAD_DIM] = (
            gate[:, c:c + 1] * o_cmp + gate[:, c + 2:c + 3] * o_win[u * tq:(u + 1) * tq])

    p_hi = psum.astype(BF16)
    p_lo = (psum - p_hi.astype(F32)).astype(BF16)
    ovt = ovt_ref[...]
    imp = (jnp.dot(ovt, p_hi, preferred_element_type=F32)
           + jnp.dot(ovt, p_lo, preferred_element_type=F32))
    j_io = lax.broadcasted_iota(jnp.int32, (nsp, tq), 0)
    t_sel = t0 + lax.broadcasted_iota(jnp.int32, (nsp, tq), 1)
    valid = j_io * SEL_BLOCK <= t_sel
    cur = t_sel // SEL_BLOCK
    forced = (j_io == 0) | (j_io == cur) | (j_io == cur - 1)
    score = jnp.where(valid & jnp.logical_not(forced), imp, NEG)
    picked = jnp.where(forced, 1.0, 0.0)
    for _ in range(k_sel - 3):
        m = jnp.max(score, axis=0, keepdims=True)
        first = jnp.min(jnp.where(score == m, j_io, nsp), axis=0, keepdims=True)
        hit = j_io == first
        picked = jnp.where(hit, 1.0, picked)
        score = jnp.where(hit, -jnp.inf, score)
    sel_t = jnp.where(valid, picked, 0.0)
    sel_off = jnp.where(sel_t.T > 0.0, 0.0, SEL_OFF).astype(BF16)
    qaug = jnp.concatenate([q5, jnp.concatenate([sel_off] * hg, axis=0)], axis=1)

    def sel_scores(kti):
        k0 = pl.multiple_of(kti * kt, kt)
        return _nt_dot(qaug, kaug[pl.ds(k0, kt), :])

    def sel_update(s_ref, kti, diagonal):
        s = s_ref[...]
        if diagonal:
            s = _add_per_head(s, cbias[qb % (kt // tq)], hg)
        k0 = pl.multiple_of(kti * kt, kt)
        m_i = m_ref[...]
        m_new = jnp.maximum(m_i, jnp.max(s, axis=-1, keepdims=True))
        p = jnp.exp(s - m_new)
        pv = jnp.dot(p.astype(BF16), vaug[pl.ds(k0, kt), :], preferred_element_type=F32)
        acc_ref[...] = jnp.exp(m_i - m_new) * acc_ref[...] + pv
        m_ref[...] = m_new

    def sel_pair(pi, _):
        t = 2 * pi
        s_b[...] = sel_scores(t + 1)
        sel_update(s_a, t, False)
        s_a[...] = sel_scores(t + 2)
        sel_update(s_b, t + 1, False)
        return 0

    last = (t0 + tq + kt - 1) // kt - 1
    m_ref[...] = jnp.full((rows, 1), NEG, F32)
    acc_ref[...] = jnp.zeros((rows, 2 * HEAD_DIM), F32)
    s_a[...] = sel_scores(0)
    lax.fori_loop(0, last // 2, sel_pair, 0)
    t_even = 2 * (last // 2)

    @pl.when(last % 2 == 1)
    def _():
        s_b[...] = sel_scores(t_even + 1)
        sel_update(s_a, t_even, False)
        sel_update(s_b, t_even + 1, True)

    @pl.when(last % 2 == 0)
    def _():
        sel_update(s_a, t_even, True)

    gate = gate_ref[...]
    for u in range(hg):
        c = N_NSA_BRANCHES * u + 1
        acc_u = acc_ref[u * tq:(u + 1) * tq, :]
        o_slc = acc_u[:, :HEAD_DIM] / acc_u[:, HEAD_DIM:]
        cols = slice(u * HEAD_DIM, (u + 1) * HEAD_DIM)
        o_ref[:, cols] = (part_ref[:, cols] + gate[:, c:c + 1] * o_slc).astype(BF16)


def _nsa(heads, kcv, vct, gates, ovt, tq, kt):
    _, s, _ = heads.shape
    ncp = kcv.shape[2]
    nsp = ovt.shape[0]
    hg = NSA_GROUP
    k_sel = min(N_SELECT, s // SEL_BLOCK)
    assert k_sel >= 3 and kt % tq == 0 and NSA_WINDOW % tq == 0 and s % kt == 0
    q_map = lambda u, g, qb: (H_QN + hg * g + u, qb, 0)
    full = lambda hbase: pl.BlockSpec((1, s, HEAD_DIM), lambda g, qb: (hbase + g, 0, 0),
                                      pipeline_mode=pl.Buffered(1))
    return pl.pallas_call(
        functools.partial(_nsa_kernel, s_len=s, nsp=nsp, tq=tq, kt=kt, k_sel=k_sel),
        grid=(NSA_KV_GROUPS, s // tq),
        in_specs=[
            *[pl.BlockSpec((1, tq, HEAD_DIM), functools.partial(q_map, u)) for u in range(hg)],
            pl.BlockSpec((1, 1, ncp, HEAD_DIM), lambda g, qb: (0, g, 0, 0)),
            pl.BlockSpec((1, 1, HEAD_DIM, ncp), lambda g, qb: (1, g, 0, 0)),
            full(H_KS), full(H_VS), full(H_KW), full(H_VW),
            pl.BlockSpec((tq, HEAD_DIM), lambda g, qb: (qb, g)),
            pl.BlockSpec((nsp, ncp), lambda g, qb: (0, 0)),
        ],
        out_specs=pl.BlockSpec((tq, hg * HEAD_DIM), lambda g, qb: (qb, g)),
        out_shape=jax.ShapeDtypeStruct((s, D_NSA), BF16),
        scratch_shapes=[pltpu.VMEM((s, HEAD_DIM + nsp), BF16),
                        pltpu.VMEM((s, 2 * HEAD_DIM), BF16),
                        pltpu.VMEM((s, 2 * HEAD_DIM), BF16),
                        pltpu.VMEM((kt // tq, tq, kt), F32),
                        pltpu.VMEM((NSA_WINDOW // tq + 1, tq, NSA_WINDOW + tq), F32),
                        pltpu.VMEM((hg * tq, kt), F32),
                        pltpu.VMEM((hg * tq, kt), F32),
                        pltpu.VMEM((hg * tq, 1), F32),
                        pltpu.VMEM((hg * tq, 2 * HEAD_DIM), F32),
                        pltpu.VMEM((tq, hg * HEAD_DIM), F32)],
        compiler_params=_params(("arbitrary", "arbitrary")),
        name="nsa_attn",
    )(*([heads] * hg), kcv, vct, heads, heads, heads, heads, gates, ovt)


def _out_proj_ln_kernel(a_ref, b_ref, wa_ref, wb_ref, h_ref, g_ref, be_ref, o_ref, *, alpha):
    mix = (jnp.dot(a_ref[...], wa_ref[...], preferred_element_type=F32)
           + jnp.dot(b_ref[...], wb_ref[...], preferred_element_type=F32))
    o_ref[...] = _layer_norm(alpha * h_ref[...] + mix, g_ref[...], be_ref[...])


def _out_proj_ln(mix_a, mix_b, wa, wb, h, g, b, alpha, tm):
    s, d = h.shape
    return pl.pallas_call(
        functools.partial(_out_proj_ln_kernel, alpha=alpha),
        grid=(s // tm,),
        in_specs=[
            pl.BlockSpec((tm, D_DIL), lambda i: (i, 0)),
            pl.BlockSpec((tm, D_NSA), lambda i: (i, 0)),
            pl.BlockSpec((D_DIL, d), lambda i: (0, 0)),
            pl.BlockSpec((D_NSA, d), lambda i: (0, 0)),
            pl.BlockSpec((tm, d), lambda i: (i, 0)),
            pl.BlockSpec((1, d), lambda i: (0, 0)),
            pl.BlockSpec((1, d), lambda i: (0, 0)),
        ],
        out_specs=pl.BlockSpec((tm, d), lambda i: (i, 0)),
        out_shape=jax.ShapeDtypeStruct((s, d), F32),
        compiler_params=_params(("arbitrary",)),
        name="out_proj_ln",
    )(mix_a, mix_b, wa, wb, h, g, b)


def _overlap_t(s):
    n_cmp = (s - CMP_BLOCK) // CMP_STRIDE + 1
    n_sel = s // SEL_BLOCK
    ncp = s // CMP_STRIDE
    nsp = -(-n_sel // HEAD_DIM) * HEAD_DIM
    c_lo = np.arange(n_cmp) * CMP_STRIDE
    c_hi = c_lo + CMP_BLOCK - 1
    s_lo = (np.arange(n_sel) * SEL_BLOCK)[:, None]
    ov = np.zeros((nsp, ncp), np.float32)
    ov[:n_sel, :n_cmp] = (c_lo[None, :] <= s_lo + SEL_BLOCK - 1) & (c_hi[None, :] >= s_lo)
    return jnp.asarray(ov, BF16)


def kernel(x, positions, ln1_g, ln1_b, ffn1_w1, ffn1_w3, ffn1_w2, w_in, gate_b, cmp_pe, cmp_w1,
           cmp_b1, cmp_w2, cmp_b2, w_out, ln2_g, ln2_b, ffn2_w1, ffn2_w3, ffn2_w2, ln3_g, ln3_b):
    bsz, s, d = x.shape
    assert bsz == 1 and d == (N_HEADS_DIL + N_HEADS_NSA) * HEAD_DIM
    assert s % 512 == 0 and s >= NSA_WINDOW + QBLK
    alpha = (2.0 * DEPTH) ** 0.25
    scale = HEAD_DIM ** -0.5
    row = lambda v: v.reshape(1, -1)
    tm = 512
    ffn_tm, ffn_tf = 1024, 256

    inv_freq = ROPE_THETA ** (-jnp.arange(0, HEAD_DIM, 2, dtype=F32) / HEAD_DIM)
    invf = jnp.concatenate([inv_freq, inv_freq]).reshape(1, HEAD_DIM)
    pos = positions[0].astype(F32).reshape(s, 1)
    cosf, sinf = _rope_tables(pos, pos, invf, 512)
    ncp = s // CMP_STRIDE
    n_cmp = (s - CMP_BLOCK) // CMP_STRIDE + 1
    pos_lo = jnp.pad(pos[0:n_cmp * CMP_STRIDE:CMP_STRIDE], ((0, ncp - n_cmp), (0, 0)))
    pos_hi = jnp.pad(pos[CMP_BLOCK - 1::CMP_STRIDE][:n_cmp], ((0, ncp - n_cmp), (0, 0)))
    cos_c, sin_c = _rope_tables(pos_lo, pos_hi, invf, ncp)

    g_off = 3 * D_DIL + D_NSA + 6 * KV_W
    per_g = NSA_GROUP * N_NSA_BRANCHES
    w_in_t = w_in[0].T
    wg = jnp.zeros((NSA_KV_GROUPS * HEAD_DIM, d), F32)
    bg = jnp.zeros((1, NSA_KV_GROUPS * HEAD_DIM), F32)
    for g in range(NSA_KV_GROUPS):
        wg = wg.at[g * HEAD_DIM:g * HEAD_DIM + per_g].set(
            w_in_t[g_off + g * per_g:g_off + (g + 1) * per_g])
        bg = bg.at[0, g * HEAD_DIM:g * HEAD_DIM + per_g].set(gate_b[0][g * per_g:(g + 1) * per_g])
    wg = wg.astype(BF16)

    h0 = x[0]
    h1, h1b = _ffn_ln(h0, ffn1_w1[0], ffn1_w3[0], ffn1_w2[0], row(ln1_g[0]), row(ln1_b[0]),
                      alpha, ffn_tm, ffn_tf, True)

    heads = _in_proj(h1b, w_in_t, cosf, sinf, scale, 1024)
    gates = _gates(h1b, wg, bg, tm)
    tok16 = heads[H_KC:H_KS].reshape(2 * NSA_KV_GROUPS, ncp, CMP_STRIDE * HEAD_DIM)
    kcv, kcv_t = _compress(
        tok16, cmp_pe[0].reshape(2, 1, CMP_BLOCK * HEAD_DIM), cmp_w1[0].astype(BF16),
        cmp_b1[0].reshape(2, 1, CMP_HIDDEN), cmp_w2[0].astype(BF16),
        cmp_b2[0].reshape(2, 1, HEAD_DIM), cos_c, sin_c)
    mix_a = _dilated(heads, 256, 2)
    mix_b = _nsa(heads, kcv, kcv_t, gates, _overlap_t(s), 256, 1024)

    wo = w_out[0].astype(BF16)
    h2 = _out_proj_ln(mix_a, mix_b, wo[:D_DIL], wo[D_DIL:], h1, row(ln2_g[0]), row(ln2_b[0]),
                      alpha, tm)
    (h3,) = _ffn_ln(h2, ffn2_w1[0], ffn2_w3[0], ffn2_w2[0], row(ln3_g[0]), row(ln3_b[0]),
                    alpha, ffn_tm, ffn_tf, False)
    return h3.reshape(bsz, s, d)
```

```python
import functools

import jax
import jax.numpy as jnp
import numpy as np
from jax import lax
from jax.experimental import pallas as pl
from jax.experimental.pallas import tpu as pltpu

HEAD_DIM = 128
N_HEADS_DIL = 6
N_HEADS_NSA = 10
NSA_KV_GROUPS = 2
NSA_GROUP = N_HEADS_NSA // NSA_KV_GROUPS
N_NSA_BRANCHES = 3
DIL_PAIRS = ((128, 1), (512, 4), (2048, 16))
CMP_BLOCK = 32
CMP_STRIDE = 16
CMP_HIDDEN = 256
SEL_BLOCK = 64
N_SELECT = 16
NSA_WINDOW = 512
ROPE_THETA = 10000.0
QBLK = 128
LN_EPS = 1e-5
NEG = -1e30
FORCE_BONUS = 1e4
SEL_OFF = -1e9
DEPTH = 1

D_DIL = N_HEADS_DIL * HEAD_DIM
D_NSA = N_HEADS_NSA * HEAD_DIM
KV_W = NSA_KV_GROUPS * HEAD_DIM
N_GATES = N_HEADS_NSA * N_NSA_BRANCHES

H_QA, H_KA, H_VA, H_QN = 0, 6, 12, 18
H_KC, H_VC, H_KS, H_VS, H_KW, H_VW = 28, 30, 32, 34, 36, 38
N_PROJ_HEADS = 40
HEADS_PER_PROJ_BLOCK = 8
PROJ_SUB_HEADS = 4
ROW_CHUNKS = 2
FFN_LAST_CHUNKS = 4

VMEM_LIMIT_BYTES = 56 * 1024 * 1024

F32 = jnp.float32
BF16 = jnp.bfloat16


def _params(sem, vmem=VMEM_LIMIT_BYTES):
    return pltpu.CompilerParams(dimension_semantics=sem, vmem_limit_bytes=vmem)


def _nt_dot(a, b):
    return lax.dot_general(a, b, (((1,), (1,)), ((), ())), preferred_element_type=F32)


def _layer_norm(y, g, b):
    mu = jnp.mean(y, axis=-1, keepdims=True)
    yc = y - mu
    var = jnp.mean(yc * yc, axis=-1, keepdims=True)
    return yc * lax.rsqrt(var + LN_EPS) * g + b


def _rope_table_kernel(pa_ref, pb_ref, invf_ref, cos_ref, sin_ref):
    pos = (pa_ref[...] + pb_ref[...]) * 0.5
    ang = pos * invf_ref[...]
    lane = lax.broadcasted_iota(jnp.int32, ang.shape, 1)
    cos_ref[...] = jnp.cos(ang)
    sin_ref[...] = jnp.where(lane < HEAD_DIM // 2, -1.0, 1.0) * jnp.sin(ang)


def _rope_tables(pos_a, pos_b, invf, tile):
    n = pos_a.shape[0]
    spec_p = pl.BlockSpec((tile, 1), lambda i: (i, 0))
    spec_t = pl.BlockSpec((tile, HEAD_DIM), lambda i: (i, 0))
    return pl.pallas_call(
        _rope_table_kernel,
        grid=(n // tile,),
        in_specs=[spec_p, spec_p, pl.BlockSpec((1, HEAD_DIM), lambda i: (0, 0))],
        out_specs=[spec_t, spec_t],
        out_shape=[jax.ShapeDtypeStruct((n, HEAD_DIM), F32)] * 2,
        compiler_params=_params(("arbitrary",)),
        name="rope_tables",
    )(pos_a, pos_b, invf)


def _ffn_ln_kernel(h_ref, w1_ref, w3_ref, w2_ref, g_ref, b_ref, o_ref, *rest, alpha, nf):
    hb_ref = rest[-1]
    f = pl.program_id(1)

    @pl.when(f == 0)
    def _():
        hb_ref[...] = h_ref[...].astype(BF16)
        o_ref[...] = jnp.zeros_like(o_ref)

    w1 = w1_ref[...].astype(BF16)
    w3 = w3_ref[...].astype(BF16)
    w2 = w2_ref[...].astype(BF16)

    def hidden_tile(rows):
        hb = hb_ref[rows, :]
        a = jnp.dot(hb, w1, preferred_element_type=F32)
        b = jnp.dot(hb, w3, preferred_element_type=F32)
        act = ((a * jax.nn.sigmoid(a)) * b).astype(BF16)
        return jnp.dot(act, w2, preferred_element_type=F32)

    @pl.when(f < nf - 1)
    def _():
        o_ref[...] += hidden_tile(slice(None))

    @pl.when(f == nf - 1)
    def _():
        rc = o_ref.shape[0] // FFN_LAST_CHUNKS
        for c in range(FFN_LAST_CHUNKS):
            r = slice(c * rc, (c + 1) * rc)
            ffn = o_ref[r, :] + hidden_tile(r)
            out = _layer_norm(alpha * h_ref[r, :] + 0.5 * ffn, g_ref[...], b_ref[...])
            o_ref[r, :] = out
            if len(rest) == 2:
                rest[0][r, :] = out.astype(BF16)


def _ffn_ln(h, w1, w3, w2, g, b, alpha, tm, tf, emit_bf16):
    s, d = h.shape
    dff = w1.shape[1]
    nf = dff // tf
    once = pl.Buffered(1)
    row_spec = lambda: pl.BlockSpec((tm, d), lambda i, f: (i, 0), pipeline_mode=once)
    out_specs = [row_spec()]
    out_shape = [jax.ShapeDtypeStruct((s, d), F32)]
    if emit_bf16:
        out_specs.append(row_spec())
        out_shape.append(jax.ShapeDtypeStruct((s, d), BF16))
    return pl.pallas_call(
        functools.partial(_ffn_ln_kernel, alpha=alpha, nf=nf),
        grid=(s // tm, nf),
        in_specs=[
            pl.BlockSpec((tm, d), lambda i, f: (i, 0)),
            pl.BlockSpec((d, tf), lambda i, f: (0, f)),
            pl.BlockSpec((d, tf), lambda i, f: (0, f)),
            pl.BlockSpec((tf, d), lambda i, f: (f, 0)),
            pl.BlockSpec((1, d), lambda i, f: (0, 0)),
            pl.BlockSpec((1, d), lambda i, f: (0, 0)),
        ],
        out_specs=out_specs,
        out_shape=out_shape,
        scratch_shapes=[pltpu.VMEM((tm, d), BF16)],
        compiler_params=_params(("arbitrary", "arbitrary")),
        name="ffn_ln",
    )(h, w1, w3, w2, g, b)


def _in_proj_kernel(h_ref, w_ref, cos_ref, sin_ref, o_ref, *, scale):
    j = pl.program_id(1)
    hpb = HEADS_PER_PROJ_BLOCK
    sub = PROJ_SUB_HEADS
    cos = cos_ref[...]
    sin = sin_ref[...]
    h = h_ref[...]
    for k0 in range(0, hpb, sub):
        w = w_ref[k0 * HEAD_DIM:(k0 + sub) * HEAD_DIM, :].astype(BF16)
        acc = _nt_dot(h, w)
        for k in range(sub):
            hh = j * hpb + k0 + k
            is_q = (hh < H_KA) | ((hh >= H_QN) & (hh < H_KC))
            rope = (is_q | (hh < H_VA) | ((hh >= H_KS) & (hh < H_VS))
                    | ((hh >= H_KW) & (hh < H_VW)))
            x = acc[:, k * HEAD_DIM:(k + 1) * HEAD_DIM]
            xr = x * cos + pltpu.roll(x, HEAD_DIM // 2, 1) * sin
            y = jnp.where(rope, xr, x) * jnp.where(is_q, scale, 1.0).astype(F32)
            o_ref[k0 + k] = y.astype(BF16)


def _in_proj(hb, w, cosf, sinf, scale, tm):
    s, d = hb.shape
    hpb = HEADS_PER_PROJ_BLOCK
    return pl.pallas_call(
        functools.partial(_in_proj_kernel, scale=scale),
        grid=(s // tm, N_PROJ_HEADS // hpb),
        in_specs=[
            pl.BlockSpec((tm, d), lambda i, j: (i, 0)),
            pl.BlockSpec((hpb * HEAD_DIM, d), lambda i, j: (j, 0)),
            pl.BlockSpec((tm, HEAD_DIM), lambda i, j: (i, 0)),
            pl.BlockSpec((tm, HEAD_DIM), lambda i, j: (i, 0)),
        ],
        out_specs=pl.BlockSpec((hpb, tm, HEAD_DIM), lambda i, j: (j, i, 0)),
        out_shape=jax.ShapeDtypeStruct((N_PROJ_HEADS, s, HEAD_DIM), BF16),
        compiler_params=_params(("arbitrary", "arbitrary")),
        name="in_proj",
    )(hb, w, cosf, sinf)


def _gates_kernel(h_ref, w_ref, b_ref, o_ref):
    z = _nt_dot(h_ref[...], w_ref[...]) + b_ref[...]
    o_ref[...] = jax.nn.sigmoid(z)


def _gates(hb, wg, bg, tm):
    s, d = hb.shape
    n = wg.shape[0]
    return pl.pallas_call(
        _gates_kernel,
        grid=(s // tm,),
        in_specs=[pl.BlockSpec((tm, d), lambda i: (i, 0)),
                  pl.BlockSpec((n, d), lambda i: (0, 0)),
                  pl.BlockSpec((1, n), lambda i: (0, 0))],
        out_specs=pl.BlockSpec((tm, n), lambda i: (i, 0)),
        out_shape=jax.ShapeDtypeStruct((s, n), F32),
        compiler_params=_params(("arbitrary",)),
        name="nsa_gates",
    )(hb, wg, bg)


def _compress_kernel(tok_ref, pe_ref, w1_ref, b1_ref, w2_ref, b2_ref, cos_ref, sin_ref,
                     o_ref, ot_ref):
    j = pl.program_id(0)
    half = CMP_STRIDE * HEAD_DIM
    tok = tok_ref[0].astype(F32)
    pe = pe_ref[0]
    w1 = w1_ref[0]
    top = (tok + pe[:, :half]).astype(BF16)
    bot = (tok + pe[:, half:]).astype(BF16)
    u = jnp.dot(top, w1[:half], preferred_element_type=F32)
    v = jnp.dot(bot, w1[half:], preferred_element_type=F32)
    nc = u.shape[0]
    hid = u + pltpu.roll(v, nc - 1, 0) + b1_ref[0]
    hid = jax.nn.gelu(hid)
    out = jnp.dot(hid.astype(BF16), w2_ref[0], preferred_element_type=F32) + b2_ref[0]
    roped = out * cos_ref[...] + pltpu.roll(out, HEAD_DIM // 2, 1) * sin_ref[...]
    out = jnp.where(j == 0, roped, out)
    o_ref[0, 0] = out.astype(BF16)
    ot_ref[0, 0] = out.T.astype(BF16)


def _compress(tok16, pe, w1, b1, w2, b2, cos_c, sin_c):
    nc = tok16.shape[1]
    blk = CMP_BLOCK * HEAD_DIM
    g = NSA_KV_GROUPS
    return pl.pallas_call(
        _compress_kernel,
        grid=(2, g),
        in_specs=[
            pl.BlockSpec((1, nc, CMP_STRIDE * HEAD_DIM), lambda j, gi: (g * j + gi, 0, 0)),
            pl.BlockSpec((1, 1, blk), lambda j, gi: (j, 0, 0)),
            pl.BlockSpec((1, blk, CMP_HIDDEN), lambda j, gi: (j, 0, 0)),
            pl.BlockSpec((1, 1, CMP_HIDDEN), lambda j, gi: (j, 0, 0)),
            pl.BlockSpec((1, CMP_HIDDEN, HEAD_DIM), lambda j, gi: (j, 0, 0)),
            pl.BlockSpec((1, 1, HEAD_DIM), lambda j, gi: (j, 0, 0)),
            pl.BlockSpec((nc, HEAD_DIM), lambda j, gi: (0, 0)),
            pl.BlockSpec((nc, HEAD_DIM), lambda j, gi: (0, 0)),
        ],
        out_specs=[pl.BlockSpec((1, 1, nc, HEAD_DIM), lambda j, gi: (j, gi, 0, 0)),
                   pl.BlockSpec((1, 1, HEAD_DIM, nc), lambda j, gi: (j, gi, 0, 0))],
        out_shape=[jax.ShapeDtypeStruct((2, g, nc, HEAD_DIM), BF16),
                   jax.ShapeDtypeStruct((2, g, HEAD_DIM, nc), BF16)],
        compiler_params=_params(("arbitrary", "arbitrary")),
        name="nsa_compress",
    )(tok16, pe, w1, b1, w2, b2, cos_c, sin_c)


def _dil_kernel(q_ref, k_ref, v_ref, o_ref, kpad, vpad, bias_ref, *, tq, n_sub, padk):
    h = pl.program_id(0)
    i = pl.program_id(1)
    w = padk + tq

    @pl.when((h == 0) & (i == 0))
    def _():
        r = lax.broadcasted_iota(jnp.int32, (tq, w), 0)
        c = lax.broadcasted_iota(jnp.int32, (tq, w), 1)
        d = r + padk - c
        cnt = jnp.zeros((tq, w), F32)
        for window, dil in DIL_PAIRS:
            hit = (d >= 0) & (d <= window) & ((d & (dil - 1)) == 0)
            cnt = cnt + jnp.where(hit, 1.0, 0.0)
        bias_ref[...] = jnp.where(cnt > 0.0, jnp.log(jnp.maximum(cnt, 1.0)), NEG)

    @pl.when(i == 0)
    def _():
        kpad[0:padk, :] = jnp.zeros((padk, HEAD_DIM), BF16)
        vpad[0:padk, :] = jnp.zeros((padk, HEAD_DIM), BF16)
        kpad[padk:, :] = k_ref[0]
        vpad[padk:, :] = v_ref[0]

    col = lax.broadcasted_iota(jnp.int32, (1, w), 1)
    for part in range(n_sub):
        q0 = pl.multiple_of((i * n_sub + part) * tq, tq)
        q = q_ref[0, part * tq:(part + 1) * tq, :]
        s = _nt_dot(q, kpad[pl.ds(q0, w), :]) + bias_ref[...]
        s = jnp.where(col >= padk - q0, s, NEG)
        p = jnp.exp(s - jnp.max(s, axis=-1, keepdims=True))
        l = jnp.sum(p, axis=-1, keepdims=True)
        o = jnp.dot(p.astype(BF16), vpad[pl.ds(q0, w), :], preferred_element_type=F32)
        o_ref[part * tq:(part + 1) * tq, :] = (o / l).astype(BF16)


def _dilated(heads, tq, n_sub):
    _, s, _ = heads.shape
    padk = max(wd for wd, _ in DIL_PAIRS)
    tstep = tq * n_sub
    return pl.pallas_call(
        functools.partial(_dil_kernel, tq=tq, n_sub=n_sub, padk=padk),
        grid=(N_HEADS_DIL, s // tstep),
        in_specs=[
            pl.BlockSpec((1, tstep, HEAD_DIM), lambda h, i: (H_QA + h, i, 0)),
            pl.BlockSpec((1, s, HEAD_DIM), lambda h, i: (H_KA + h, 0, 0)),
            pl.BlockSpec((1, s, HEAD_DIM), lambda h, i: (H_VA + h, 0, 0)),
        ],
        out_specs=pl.BlockSpec((tstep, HEAD_DIM), lambda h, i: (i, h)),
        out_shape=jax.ShapeDtypeStruct((s, D_DIL), BF16),
        scratch_shapes=[pltpu.VMEM((padk + s, HEAD_DIM), BF16),
                        pltpu.VMEM((padk + s, HEAD_DIM), BF16),
                        pltpu.VMEM((tq, padk + tq), F32)],
        compiler_params=_params(("arbitrary", "arbitrary")),
        name="dilated_attn",
    )(heads, heads, heads)


def _add_per_head(s, bias, hg):
    tq = bias.shape[0]
    return jnp.concatenate([s[u * tq:(u + 1) * tq] + bias for u in range(hg)], axis=0)


def _nsa_kernel(*refs, s_len, nsp, tq, kt, k_sel):
    q_refs = refs[:NSA_GROUP]
    (kc_ref, vct_ref, ks_ref, vs_ref, kw_ref, vw_ref, gate_ref, ovt_ref,
     o_ref, kaug, vaug, vwaug, cbias, wbias, s_a, s_b, m_ref, acc_ref,
     part_ref) = refs[NSA_GROUP:]
    g = pl.program_id(0)
    qb = pl.program_id(1)
    t0 = qb * tq
    hg = NSA_GROUP
    rows = hg * tq
    ncp = kc_ref.shape[2]
    ww = NSA_WINDOW + tq

    @pl.when((g == 0) & (qb == 0))
    def _():
        for r in range(kt // tq):
            d = (lax.broadcasted_iota(jnp.int32, (tq, kt), 0) + r * tq
                 - lax.broadcasted_iota(jnp.int32, (tq, kt), 1))
            cbias[r] = jnp.where(d >= 0, 0.0, NEG)
        for w in range(NSA_WINDOW // tq + 1):
            d = (lax.broadcasted_iota(jnp.int32, (tq, ww), 0) + w * tq
                 - lax.broadcasted_iota(jnp.int32, (tq, ww), 1))
            wbias[w] = jnp.where((d >= 0) & (d < NSA_WINDOW), 0.0, NEG)

    @pl.when(qb == 0)
    def _():
        kaug[:, 0:HEAD_DIM] = ks_ref[0]
        key = lax.broadcasted_iota(jnp.int32, (s_len, nsp), 0)
        blk = lax.broadcasted_iota(jnp.int32, (s_len, nsp), 1)
        kaug[:, HEAD_DIM:] = jnp.where(blk == key // SEL_BLOCK, 1.0, 0.0).astype(BF16)
        ones = jnp.ones((s_len, HEAD_DIM), BF16)
        vaug[:, 0:HEAD_DIM] = vs_ref[0]
        vaug[:, HEAD_DIM:] = ones
        vwaug[:, 0:HEAD_DIM] = vw_ref[0]
        vwaug[:, HEAD_DIM:] = ones

    gate = gate_ref[...]
    q5 = jnp.concatenate([r[0] for r in q_refs], axis=0)

    w0 = pl.multiple_of(jnp.maximum(t0 - NSA_WINDOW, 0), tq)
    sw = _nt_dot(q5, kw_ref[0, pl.ds(w0, ww), :])
    sw = _add_per_head(sw, wbias[jnp.minimum(qb, NSA_WINDOW // tq)], hg)
    pw = jnp.exp(sw - jnp.max(sw, axis=-1, keepdims=True))
    acc_w = jnp.dot(pw.astype(BF16), vwaug[pl.ds(w0, ww), :], preferred_element_type=F32)
    o_win = acc_w[:, :HEAD_DIM] / acc_w[:, HEAD_DIM:]

    kc = kc_ref[0, 0]
    vct = vct_ref[0, 0]
    n_io = lax.broadcasted_iota(jnp.int32, (ncp, tq), 0)
    t_io = t0 + lax.broadcasted_iota(jnp.int32, (ncp, tq), 1)
    cmask = (n_io * CMP_STRIDE + (CMP_BLOCK - 1) <= t_io) & (n_io < ncp - 1)
    psum = jnp.zeros((ncp, tq), F32)
    for u in range(hg):
        st = jnp.where(cmask, _nt_dot(kc, q_refs[u][0]), NEG)
        m = jnp.maximum(jnp.max(st, axis=0, keepdims=True), 0.1 * NEG)
        e = jnp.exp(st - m)
        pt = e / jnp.maximum(jnp.sum(e, axis=0, keepdims=True), 1e-30)
        psum = psum + pt
        o_cmp = jnp.dot(vct, pt.astype(BF16), preferred_element_type=F32).T
        c = N_NSA_BRANCHES * u
        part_ref[:, u * HEAD_DIM:(u + 1) * HEAD_DIM] = (
            gate[:, c:c + 1] * o_cmp + gate[:, c + 2:c + 3] * o_win[u * tq:(u + 1) * tq])

    p_hi = psum.astype(BF16)
    p_lo = (psum - p_hi.astype(F32)).astype(BF16)
    ovt = ovt_ref[...]
    imp = (jnp.dot(ovt, p_hi, preferred_element_type=F32)
           + jnp.dot(ovt, p_lo, preferred_element_type=F32))
    j_io = lax.broadcasted_iota(jnp.int32, (nsp, tq), 0)
    t_sel = t0 + lax.broadcasted_iota(jnp.int32, (nsp, tq), 1)
    valid = j_io * SEL_BLOCK <= t_sel
    cur = t_sel // SEL_BLOCK
    forced = (j_io == 0) | (j_io == cur) | (j_io == cur - 1)
    score = jnp.where(valid & jnp.logical_not(forced), imp, NEG)
    picked = jnp.where(forced, 1.0, 0.0)
    for _ in range(k_sel - 3):
        m = jnp.max(score, axis=0, keepdims=True)
        first = jnp.min(jnp.where(score == m, j_io, nsp), axis=0, keepdims=True)
        hit = j_io == first
        picked = jnp.where(hit, 1.0, picked)
        score = jnp.where(hit, -jnp.inf, score)
    sel_t = jnp.where(valid, picked, 0.0)
    sel_off = jnp.where(sel_t.T > 0.0, 0.0, SEL_OFF).astype(BF16)
    qaug = jnp.concatenate([q5, jnp.concatenate([sel_off] * hg, axis=0)], axis=1)

    def sel_scores(kti):
        k0 = pl.multiple_of(kti * kt, kt)
        return _nt_dot(qaug, kaug[pl.ds(k0, kt), :])

    def sel_update(s_ref, kti, diagonal):
        s = s_ref[...]
        if diagonal:
            s = _add_per_head(s, cbias[qb % (kt // tq)], hg)
        k0 = pl.multiple_of(kti * kt, kt)
        m_i = m_ref[...]
        m_new = jnp.maximum(m_i, jnp.max(s, axis=-1, keepdims=True))
        p = jnp.exp(s - m_new)
        pv = jnp.dot(p.astype(BF16), vaug[pl.ds(k0, kt), :], preferred_element_type=F32)
        acc_ref[...] = jnp.exp(m_i - m_new) * acc_ref[...] + pv
        m_ref[...] = m_new

    def sel_pair(pi, _):
        t = 2 * pi
        s_b[...] = sel_scores(t + 1)
        sel_update(s_a, t, False)
        s_a[...] = sel_scores(t + 2)
        sel_update(s_b, t + 1, False)
        return 0

    last = (t0 + tq + kt - 1) // kt - 1
    m_ref[...] = jnp.full((rows, 1), NEG, F32)
    acc_ref[...] = jnp.zeros((rows, 2 * HEAD_DIM), F32)
    s_a[...] = sel_scores(0)
    lax.fori_loop(0, last // 2, sel_pair, 0)
    t_even = 2 * (last // 2)

    @pl.when(last % 2 == 1)
    def _():
        s_b[...] = sel_scores(t_even + 1)
        sel_update(s_a, t_even, False)
        sel_update(s_b, t_even + 1, True)

    @pl.when(last % 2 == 0)
    def _():
        sel_update(s_a, t_even, True)

    gate = gate_ref[...]
    for u in range(hg):
        c = N_NSA_BRANCHES * u + 1
        acc_u = acc_ref[u * tq:(u + 1) * tq, :]
        o_slc = acc_u[:, :HEAD_DIM] / acc_u[:, HEAD_DIM:]
        cols = slice(u * HEAD_DIM, (u + 1) * HEAD_DIM)
        o_ref[:, cols] = (part_ref[:, cols] + gate[:, c:c + 1] * o_slc).astype(BF16)


def _nsa(heads, kcv, vct, gates, ovt, tq, kt):
    _, s, _ = heads.shape
    ncp = kcv.shape[2]
    nsp = ovt.shape[0]
    hg = NSA_GROUP
    k_sel = min(N_SELECT, s // SEL_BLOCK)
    assert k_sel >= 3 and kt % tq == 0 and NSA_WINDOW % tq == 0 and s % kt == 0
    q_map = lambda u, g, qb: (H_QN + hg * g + u, qb, 0)
    full = lambda hbase: pl.BlockSpec((1, s, HEAD_DIM), lambda g, qb: (hbase + g, 0, 0),
                                      pipeline_mode=pl.Buffered(1))
    return pl.pallas_call(
        functools.partial(_nsa_kernel, s_len=s, nsp=nsp, tq=tq, kt=kt, k_sel=k_sel),
        grid=(NSA_KV_GROUPS, s // tq),
        in_specs=[
            *[pl.BlockSpec((1, tq, HEAD_DIM), functools.partial(q_map, u)) for u in range(hg)],
            pl.BlockSpec((1, 1, ncp, HEAD_DIM), lambda g, qb: (0, g, 0, 0)),
            pl.BlockSpec((1, 1, HEAD_DIM, ncp), lambda g, qb: (1, g, 0, 0)),
            full(H_KS), full(H_VS), full(H_KW), full(H_VW),
            pl.BlockSpec((tq, HEAD_DIM), lambda g, qb: (qb, g)),
            pl.BlockSpec((nsp, ncp), lambda g, qb: (0, 0)),
        ],
        out_specs=pl.BlockSpec((tq, hg * HEAD_DIM), lambda g, qb: (qb, g)),
        out_shape=jax.ShapeDtypeStruct((s, D_NSA), BF16),
        scratch_shapes=[pltpu.VMEM((s, HEAD_DIM + nsp), BF16),
                        pltpu.VMEM((s, 2 * HEAD_DIM), BF16),
                        pltpu.VMEM((s, 2 * HEAD_DIM), BF16),
                        pltpu.VMEM((kt // tq, tq, kt), F32),
                        pltpu.VMEM((NSA_WINDOW // tq + 1, tq, NSA_WINDOW + tq), F32),
                        pltpu.VMEM((hg * tq, kt), F32),
                        pltpu.VMEM((hg * tq, kt), F32),
                        pltpu.VMEM((hg * tq, 1), F32),
                        pltpu.VMEM((hg * tq, 2 * HEAD_DIM), F32),
                        pltpu.VMEM((tq, hg * HEAD_DIM), F32)],
        compiler_params=_params(("arbitrary", "arbitrary")),
        name="nsa_attn",
    )(*([heads] * hg), kcv, vct, heads, heads, heads, heads, gates, ovt)


def _out_proj_ln_kernel(a_ref, b_ref, wa_ref, wb_ref, h_ref, g_ref, be_ref, o_ref, *, alpha):
    tm = o_ref.shape[0]
    rc = tm // ROW_CHUNKS
    for c in range(ROW_CHUNKS):
        r = slice(c * rc, (c + 1) * rc)
        mix = (jnp.dot(a_ref[r, :], wa_ref[...], preferred_element_type=F32)
               + jnp.dot(b_ref[r, :], wb_ref[...], preferred_element_type=F32))
        o_ref[r, :] = _layer_norm(alpha * h_ref[r, :] + mix, g_ref[...], be_ref[...])


def _out_proj_ln(mix_a, mix_b, wa, wb, h, g, b, alpha, tm):
    s, d = h.shape
    return pl.pallas_call(
        functools.partial(_out_proj_ln_kernel, alpha=alpha),
        grid=(s // tm,),
        in_specs=[
            pl.BlockSpec((tm, D_DIL), lambda i: (i, 0)),
            pl.BlockSpec((tm, D_NSA), lambda i: (i, 0)),
            pl.BlockSpec((D_DIL, d), lambda i: (0, 0)),
            pl.BlockSpec((D_NSA, d), lambda i: (0, 0)),
            pl.BlockSpec((tm, d), lambda i: (i, 0)),
            pl.BlockSpec((1, d), lambda i: (0, 0)),
            pl.BlockSpec((1, d), lambda i: (0, 0)),
        ],
        out_specs=pl.BlockSpec((tm, d), lambda i: (i, 0)),
        out_shape=jax.ShapeDtypeStruct((s, d), F32),
        compiler_params=_params(("arbitrary",)),
        name="out_proj_ln",
    )(mix_a, mix_b, wa, wb, h, g, b)


def _overlap_t(s):
    n_cmp = (s - CMP_BLOCK) // CMP_STRIDE + 1
    n_sel = s // SEL_BLOCK
    ncp = s // CMP_STRIDE
    nsp = -(-n_sel // HEAD_DIM) * HEAD_DIM
    c_lo = np.arange(n_cmp) * CMP_STRIDE
    c_hi = c_lo + CMP_BLOCK - 1
    s_lo = (np.arange(n_sel) * SEL_BLOCK)[:, None]
    ov = np.zeros((nsp, ncp), np.float32)
    ov[:n_sel, :n_cmp] = (c_lo[None, :] <= s_lo + SEL_BLOCK - 1) & (c_hi[None, :] >= s_lo)
    return jnp.asarray(ov, BF16)


def kernel(x, positions, ln1_g, ln1_b, ffn1_w1, ffn1_w3, ffn1_w2, w_in, gate_b, cmp_pe, cmp_w1,
           cmp_b1, cmp_w2, cmp_b2, w_out, ln2_g, ln2_b, ffn2_w1, ffn2_w3, ffn2_w2, ln3_g, ln3_b):
    bsz, s, d = x.shape
    assert bsz == 1 and d == (N_HEADS_DIL + N_HEADS_NSA) * HEAD_DIM
    assert s % 512 == 0 and s >= NSA_WINDOW + QBLK
    alpha = (2.0 * DEPTH) ** 0.25
    scale = HEAD_DIM ** -0.5
    row = lambda v: v.reshape(1, -1)
    tm = 512
    ffn_tm, ffn_tf = 1024, 256

    inv_freq = ROPE_THETA ** (-jnp.arange(0, HEAD_DIM, 2, dtype=F32) / HEAD_DIM)
    invf = jnp.concatenate([inv_freq, inv_freq]).reshape(1, HEAD_DIM)
    pos = positions[0].astype(F32).reshape(s, 1)
    cosf, sinf = _rope_tables(pos, pos, invf, 512)
    ncp = s // CMP_STRIDE
    n_cmp = (s - CMP_BLOCK) // CMP_STRIDE + 1
    pos_lo = jnp.pad(pos[0:n_cmp * CMP_STRIDE:CMP_STRIDE], ((0, ncp - n_cmp), (0, 0)))
    pos_hi = jnp.pad(pos[CMP_BLOCK - 1::CMP_STRIDE][:n_cmp], ((0, ncp - n_cmp), (0, 0)))
    cos_c, sin_c = _rope_tables(pos_lo, pos_hi, invf, ncp)

    g_off = 3 * D_DIL + D_NSA + 6 * KV_W
    per_g = NSA_GROUP * N_NSA_BRANCHES
    w_in_t = w_in[0].T
    wg = jnp.zeros((NSA_KV_GROUPS * HEAD_DIM, d), F32)
    bg = jnp.zeros((1, NSA_KV_GROUPS * HEAD_DIM), F32)
    for g in range(NSA_KV_GROUPS):
        wg = wg.at[g * HEAD_DIM:g * HEAD_DIM + per_g].set(
            w_in_t[g_off + g * per_g:g_off + (g + 1) * per_g])
        bg = bg.at[0, g * HEAD_DIM:g * HEAD_DIM + per_g].set(gate_b[0][g * per_g:(g + 1) * per_g])
    wg = wg.astype(BF16)

    h0 = x[0]
    h1, h1b = _ffn_ln(h0, ffn1_w1[0], ffn1_w3[0], ffn1_w2[0], row(ln1_g[0]), row(ln1_b[0]),
                      alpha, ffn_tm, ffn_tf, True)

    heads = _in_proj(h1b, w_in_t, cosf, sinf, scale, 1024)
    gates = _gates(h1b, wg, bg, tm)
    tok16 = heads[H_KC:H_KS].reshape(2 * NSA_KV_GROUPS, ncp, CMP_STRIDE * HEAD_DIM)
    kcv, kcv_t = _compress(
        tok16, cmp_pe[0].reshape(2, 1, CMP_BLOCK * HEAD_DIM), cmp_w1[0].astype(BF16),
        cmp_b1[0].reshape(2, 1, CMP_HIDDEN), cmp_w2[0].astype(BF16),
        cmp_b2[0].reshape(2, 1, HEAD_DIM), cos_c, sin_c)
    mix_a = _dilated(heads, 256, 2)
    mix_b = _nsa(heads, kcv, kcv_t, gates, _overlap_t(s), 256, 1024)

    wo = w_out[0].astype(BF16)
    h2 = _out_proj_ln(mix_a, mix_b, wo[:D_DIL], wo[D_DIL:], h1, row(ln2_g[0]), row(ln2_b[0]),
                      alpha, tm)
    (h3,) = _ffn_ln(h2, ffn2_w1[0], ffn2_w3[0], ffn2_w2[0], row(ln3_g[0]), row(ln3_b[0]),
                    alpha, ffn_tm, ffn_tf, False)
    return h3.reshape(bsz, s, d)
```

```python
import functools

import jax
import jax.numpy as jnp
import numpy as np
from jax import lax
from jax.experimental import pallas as pl
from jax.experimental.pallas import tpu as pltpu

HEAD_DIM = 128
N_HEADS_DIL = 6
N_HEADS_NSA = 10
NSA_KV_GROUPS = 2
NSA_GROUP = N_HEADS_NSA // NSA_KV_GROUPS
N_NSA_BRANCHES = 3
DIL_PAIRS = ((128, 1), (512, 4), (2048, 16))
CMP_BLOCK = 32
CMP_STRIDE = 16
CMP_HIDDEN = 256
SEL_BLOCK = 64
N_SELECT = 16
NSA_WINDOW = 512
ROPE_THETA = 10000.0
QBLK = 128
LN_EPS = 1e-5
NEG = -1e30
FORCE_BONUS = 1e4
SEL_OFF = -1e9
LOG2_E = 1.4426950408889634
DEPTH = 1

D_DIL = N_HEADS_DIL * HEAD_DIM
D_NSA = N_HEADS_NSA * HEAD_DIM
KV_W = NSA_KV_GROUPS * HEAD_DIM
N_GATES = N_HEADS_NSA * N_NSA_BRANCHES

H_QA, H_KA, H_VA, H_QN = 0, 6, 12, 18
H_KC, H_VC, H_KS, H_VS, H_KW, H_VW = 28, 30, 32, 34, 36, 38
N_PROJ_HEADS = 40
HEADS_PER_PROJ_BLOCK = 8
PROJ_SUB_HEADS = 4
ROW_CHUNKS = 2

VMEM_LIMIT_BYTES = 56 * 1024 * 1024

F32 = jnp.float32
BF16 = jnp.bfloat16


def _params(sem, vmem=VMEM_LIMIT_BYTES):
    return pltpu.CompilerParams(dimension_semantics=sem, vmem_limit_bytes=vmem)


def _nt_dot(a, b):
    return lax.dot_general(a, b, (((1,), (1,)), ((), ())), preferred_element_type=F32)


def _layer_norm(y, g, b):
    mu = jnp.mean(y, axis=-1, keepdims=True)
    yc = y - mu
    var = jnp.mean(yc * yc, axis=-1, keepdims=True)
    return yc * lax.rsqrt(var + LN_EPS) * g + b


def _rope_table_kernel(pa_ref, pb_ref, invf_ref, cos_ref, sin_ref):
    pos = (pa_ref[...] + pb_ref[...]) * 0.5
    ang = pos * invf_ref[...]
    lane = lax.broadcasted_iota(jnp.int32, ang.shape, 1)
    cos_ref[...] = jnp.cos(ang)
    sin_ref[...] = jnp.where(lane < HEAD_DIM // 2, -1.0, 1.0) * jnp.sin(ang)


def _rope_tables(pos_a, pos_b, invf, tile):
    n = pos_a.shape[0]
    spec_p = pl.BlockSpec((tile, 1), lambda i: (i, 0))
    spec_t = pl.BlockSpec((tile, HEAD_DIM), lambda i: (i, 0))
    return pl.pallas_call(
        _rope_table_kernel,
        grid=(n // tile,),
        in_specs=[spec_p, spec_p, pl.BlockSpec((1, HEAD_DIM), lambda i: (0, 0))],
        out_specs=[spec_t, spec_t],
        out_shape=[jax.ShapeDtypeStruct((n, HEAD_DIM), F32)] * 2,
        compiler_params=_params(("arbitrary",)),
        name="rope_tables",
    )(pos_a, pos_b, invf)


def _ffn_ln_kernel(h_ref, w1_ref, w3_ref, w2_ref, g_ref, b_ref, o_ref, *rest, alpha, nf):
    hb_ref = rest[-1]
    f = pl.program_id(1)

    @pl.when(f == 0)
    def _():
        hb_ref[...] = h_ref[...].astype(BF16)
        o_ref[...] = jnp.zeros_like(o_ref)

    hb = hb_ref[...]
    a = jnp.dot(hb, w1_ref[...].astype(BF16), preferred_element_type=F32)
    b = jnp.dot(hb, w3_ref[...].astype(BF16), preferred_element_type=F32)
    act = (a * jax.nn.sigmoid(a)) * b
    o_ref[...] += jnp.dot(act.astype(BF16), w2_ref[...].astype(BF16),
                          preferred_element_type=F32)

    @pl.when(f == nf - 1)
    def _():
        y = alpha * h_ref[...] + 0.5 * o_ref[...]
        out = _layer_norm(y, g_ref[...], b_ref[...])
        o_ref[...] = out
        if len(rest) == 2:
            rest[0][...] = out.astype(BF16)


def _ffn_ln(h, w1, w3, w2, g, b, alpha, tm, tf, emit_bf16):
    s, d = h.shape
    dff = w1.shape[1]
    nf = dff // tf
    once = pl.Buffered(1)
    row_spec = lambda: pl.BlockSpec((tm, d), lambda i, f: (i, 0), pipeline_mode=once)
    out_specs = [row_spec()]
    out_shape = [jax.ShapeDtypeStruct((s, d), F32)]
    if emit_bf16:
        out_specs.append(row_spec())
        out_shape.append(jax.ShapeDtypeStruct((s, d), BF16))
    return pl.pallas_call(
        functools.partial(_ffn_ln_kernel, alpha=alpha, nf=nf),
        grid=(s // tm, nf),
        in_specs=[
            pl.BlockSpec((tm, d), lambda i, f: (i, 0)),
            pl.BlockSpec((d, tf), lambda i, f: (0, f)),
            pl.BlockSpec((d, tf), lambda i, f: (0, f)),
            pl.BlockSpec((tf, d), lambda i, f: (f, 0)),
            pl.BlockSpec((1, d), lambda i, f: (0, 0)),
            pl.BlockSpec((1, d), lambda i, f: (0, 0)),
        ],
        out_specs=out_specs,
        out_shape=out_shape,
        scratch_shapes=[pltpu.VMEM((tm, d), BF16)],
        compiler_params=_params(("arbitrary", "arbitrary")),
        name="ffn_ln",
    )(h, w1, w3, w2, g, b)


def _in_proj_kernel(h_ref, w_ref, cos_ref, sin_ref, o_ref, *, scale):
    j = pl.program_id(1)
    hpb = HEADS_PER_PROJ_BLOCK
    sub = PROJ_SUB_HEADS
    cos = cos_ref[...]
    sin = sin_ref[...]
    h = h_ref[...]
    for k0 in range(0, hpb, sub):
        w = w_ref[k0 * HEAD_DIM:(k0 + sub) * HEAD_DIM, :].astype(BF16)
        acc = _nt_dot(h, w)
        for k in range(sub):
            hh = j * hpb + k0 + k
            is_q = (hh < H_KA) | ((hh >= H_QN) & (hh < H_KC))
            rope = (is_q | (hh < H_VA) | ((hh >= H_KS) & (hh < H_VS))
                    | ((hh >= H_KW) & (hh < H_VW)))
            x = acc[:, k * HEAD_DIM:(k + 1) * HEAD_DIM]
            xr = x * cos + pltpu.roll(x, HEAD_DIM // 2, 1) * sin
            y = jnp.where(rope, xr, x) * jnp.where(is_q, scale, 1.0).astype(F32)
            o_ref[k0 + k] = y.astype(BF16)


def _in_proj(hb, w, cosf, sinf, scale, tm):
    s, d = hb.shape
    hpb = HEADS_PER_PROJ_BLOCK
    return pl.pallas_call(
        functools.partial(_in_proj_kernel, scale=scale),
        grid=(s // tm, N_PROJ_HEADS // hpb),
        in_specs=[
            pl.BlockSpec((tm, d), lambda i, j: (i, 0)),
            pl.BlockSpec((hpb * HEAD_DIM, d), lambda i, j: (j, 0)),
            pl.BlockSpec((tm, HEAD_DIM), lambda i, j: (i, 0)),
            pl.BlockSpec((tm, HEAD_DIM), lambda i, j: (i, 0)),
        ],
        out_specs=pl.BlockSpec((hpb, tm, HEAD_DIM), lambda i, j: (j, i, 0)),
        out_shape=jax.ShapeDtypeStruct((N_PROJ_HEADS, s, HEAD_DIM), BF16),
        compiler_params=_params(("arbitrary", "arbitrary")),
        name="in_proj",
    )(hb, w, cosf, sinf)


def _gates_kernel(h_ref, w_ref, b_ref, o_ref):
    z = _nt_dot(h_ref[...], w_ref[...]) + b_ref[...]
    o_ref[...] = jax.nn.sigmoid(z)


def _gates(hb, wg, bg, tm):
    s, d = hb.shape
    n = wg.shape[0]
    return pl.pallas_call(
        _gates_kernel,
        grid=(s // tm,),
        in_specs=[pl.BlockSpec((tm, d), lambda i: (i, 0)),
                  pl.BlockSpec((n, d), lambda i: (0, 0)),
                  pl.BlockSpec((1, n), lambda i: (0, 0))],
        out_specs=pl.BlockSpec((tm, n), lambda i: (i, 0)),
        out_shape=jax.ShapeDtypeStruct((s, n), F32),
        compiler_params=_params(("arbitrary",)),
        name="nsa_gates",
    )(hb, wg, bg)


def _compress_kernel(tok_ref, pe_ref, w1_ref, b1_ref, w2_ref, b2_ref, cos_ref, sin_ref,
                     o_ref, ot_ref):
    j = pl.program_id(0)
    half = CMP_STRIDE * HEAD_DIM
    tok = tok_ref[0].astype(F32)
    pe = pe_ref[0]
    w1 = w1_ref[0]
    top = (tok + pe[:, :half]).astype(BF16)
    bot = (tok + pe[:, half:]).astype(BF16)
    u = jnp.dot(top, w1[:half], preferred_element_type=F32)
    v = jnp.dot(bot, w1[half:], preferred_element_type=F32)
    nc = u.shape[0]
    hid = u + pltpu.roll(v, nc - 1, 0) + b1_ref[0]
    hid = jax.nn.gelu(hid)
    out = jnp.dot(hid.astype(BF16), w2_ref[0], preferred_element_type=F32) + b2_ref[0]
    roped = out * cos_ref[...] + pltpu.roll(out, HEAD_DIM // 2, 1) * sin_ref[...]
    out = jnp.where(j == 0, roped, out)
    o_ref[0, 0] = out.astype(BF16)
    ot_ref[0, 0] = out.T.astype(BF16)


def _compress(tok16, pe, w1, b1, w2, b2, cos_c, sin_c):
    nc = tok16.shape[1]
    blk = CMP_BLOCK * HEAD_DIM
    g = NSA_KV_GROUPS
    return pl.pallas_call(
        _compress_kernel,
        grid=(2, g),
        in_specs=[
            pl.BlockSpec((1, nc, CMP_STRIDE * HEAD_DIM), lambda j, gi: (g * j + gi, 0, 0)),
            pl.BlockSpec((1, 1, blk), lambda j, gi: (j, 0, 0)),
            pl.BlockSpec((1, blk, CMP_HIDDEN), lambda j, gi: (j, 0, 0)),
            pl.BlockSpec((1, 1, CMP_HIDDEN), lambda j, gi: (j, 0, 0)),
            pl.BlockSpec((1, CMP_HIDDEN, HEAD_DIM), lambda j, gi: (j, 0, 0)),
            pl.BlockSpec((1, 1, HEAD_DIM), lambda j, gi: (j, 0, 0)),
            pl.BlockSpec((nc, HEAD_DIM), lambda j, gi: (0, 0)),
            pl.BlockSpec((nc, HEAD_DIM), lambda j, gi: (0, 0)),
        ],
        out_specs=[pl.BlockSpec((1, 1, nc, HEAD_DIM), lambda j, gi: (j, gi, 0, 0)),
                   pl.BlockSpec((1, 1, HEAD_DIM, nc), lambda j, gi: (j, gi, 0, 0))],
        out_shape=[jax.ShapeDtypeStruct((2, g, nc, HEAD_DIM), BF16),
                   jax.ShapeDtypeStruct((2, g, HEAD_DIM, nc), BF16)],
        compiler_params=_params(("arbitrary", "arbitrary")),
        name="nsa_compress",
    )(tok16, pe, w1, b1, w2, b2, cos_c, sin_c)


def _dil_kernel(q_ref, k_ref, v_ref, o_ref, kpad, vpad, bias_ref, *, tq, n_sub, padk):
    h = pl.program_id(0)
    i = pl.program_id(1)
    w = padk + tq

    @pl.when((h == 0) & (i == 0))
    def _():
        r = lax.broadcasted_iota(jnp.int32, (tq, w), 0)
        c = lax.broadcasted_iota(jnp.int32, (tq, w), 1)
        d = r + padk - c
        cnt = jnp.zeros((tq, w), F32)
        for window, dil in DIL_PAIRS:
            hit = (d >= 0) & (d <= window) & ((d & (dil - 1)) == 0)
            cnt = cnt + jnp.where(hit, 1.0, 0.0)
        bias_ref[...] = jnp.where(cnt > 0.0, jnp.log2(jnp.maximum(cnt, 1.0)), NEG)

    @pl.when(i == 0)
    def _():
        kpad[0:padk, :] = jnp.zeros((padk, HEAD_DIM), BF16)
        vpad[0:padk, :] = jnp.zeros((padk, HEAD_DIM), BF16)
        kpad[padk:, :] = k_ref[0]
        vpad[padk:, :] = v_ref[0]

    col = lax.broadcasted_iota(jnp.int32, (1, w), 1)
    for part in range(n_sub):
        q0 = pl.multiple_of((i * n_sub + part) * tq, tq)
        q = q_ref[0, part * tq:(part + 1) * tq, :]
        s = _nt_dot(q, kpad[pl.ds(q0, w), :]) + bias_ref[...]
        s = jnp.where(col >= padk - q0, s, NEG)
        p = jnp.exp2(s - jnp.max(s, axis=-1, keepdims=True))
        l = jnp.sum(p, axis=-1, keepdims=True)
        o = jnp.dot(p.astype(BF16), vpad[pl.ds(q0, w), :], preferred_element_type=F32)
        o_ref[part * tq:(part + 1) * tq, :] = (o / l).astype(BF16)


def _dilated(heads, tq, n_sub):
    _, s, _ = heads.shape
    padk = max(wd for wd, _ in DIL_PAIRS)
    tstep = tq * n_sub
    return pl.pallas_call(
        functools.partial(_dil_kernel, tq=tq, n_sub=n_sub, padk=padk),
        grid=(N_HEADS_DIL, s // tstep),
        in_specs=[
            pl.BlockSpec((1, tstep, HEAD_DIM), lambda h, i: (H_QA + h, i, 0)),
            pl.BlockSpec((1, s, HEAD_DIM), lambda h, i: (H_KA + h, 0, 0)),
            pl.BlockSpec((1, s, HEAD_DIM), lambda h, i: (H_VA + h, 0, 0)),
        ],
        out_specs=pl.BlockSpec((tstep, HEAD_DIM), lambda h, i: (i, h)),
        out_shape=jax.ShapeDtypeStruct((s, D_DIL), BF16),
        scratch_shapes=[pltpu.VMEM((padk + s, HEAD_DIM), BF16),
                        pltpu.VMEM((padk + s, HEAD_DIM), BF16),
                        pltpu.VMEM((tq, padk + tq), F32)],
        compiler_params=_params(("arbitrary", "arbitrary")),
        name="dilated_attn",
    )(heads, heads, heads)


def _add_per_head(s, bias, hg):
    tq = bias.shape[0]
    return jnp.concatenate([s[u * tq:(u + 1) * tq] + bias for u in range(hg)], axis=0)


def _nsa_select_kernel(*refs, nsp, tq, k_sel):
    q_refs = refs[:NSA_GROUP]
    kc_ref, vct_ref, gate_ref, ovt_ref, cmp_ref, sel_ref = refs[NSA_GROUP:]
    t0 = pl.program_id(1) * tq
    hg = NSA_GROUP
    ncp = kc_ref.shape[2]
    gate = gate_ref[...]

    kc = kc_ref[0, 0]
    vct = vct_ref[0, 0]
    n_io = lax.broadcasted_iota(jnp.int32, (ncp, tq), 0)
    t_io = t0 + lax.broadcasted_iota(jnp.int32, (ncp, tq), 1)
    cmask = (n_io * CMP_STRIDE + (CMP_BLOCK - 1) <= t_io) & (n_io < ncp - 1)
    psum = jnp.zeros((ncp, tq), F32)
    for u in range(hg):
        st = jnp.where(cmask, _nt_dot(kc, q_refs[u][0]), NEG)
        m = jnp.maximum(jnp.max(st, axis=0, keepdims=True), 0.1 * NEG)
        e = jnp.exp2(st - m)
        pt = e / jnp.maximum(jnp.sum(e, axis=0, keepdims=True), 1e-30)
        psum = psum + pt
        o_cmp = jnp.dot(vct, pt.astype(BF16), preferred_element_type=F32).T
        c = N_NSA_BRANCHES * u
        cmp_ref[:, u * HEAD_DIM:(u + 1) * HEAD_DIM] = gate[:, c:c + 1] * o_cmp

    p_hi = psum.astype(BF16)
    p_lo = (psum - p_hi.astype(F32)).astype(BF16)
    ovt = ovt_ref[...]
    imp = (jnp.dot(ovt, p_hi, preferred_element_type=F32)
           + jnp.dot(ovt, p_lo, preferred_element_type=F32))
    j_io = lax.broadcasted_iota(jnp.int32, (nsp, tq), 0)
    t_sel = t0 + lax.broadcasted_iota(jnp.int32, (nsp, tq), 1)
    valid = j_io * SEL_BLOCK <= t_sel
    cur = t_sel // SEL_BLOCK
    forced = (j_io == 0) | (j_io == cur) | (j_io == cur - 1)
    score = jnp.where(valid & jnp.logical_not(forced), imp, NEG)
    picked = jnp.where(forced, 1.0, 0.0)
    for _ in range(k_sel - 3):
        m = jnp.max(score, axis=0, keepdims=True)
        first = jnp.min(jnp.where(score == m, j_io, nsp), axis=0, keepdims=True)
        hit = j_io == first
        picked = jnp.where(hit, 1.0, picked)
        score = jnp.where(hit, -jnp.inf, score)
    sel_t = jnp.where(valid, picked, 0.0)
    sel_ref[0] = jnp.where(sel_t.T > 0.0, 0.0, SEL_OFF).astype(BF16)


def _nsa_select(heads, kcv, vct, gates, ovt, tq):
    _, s, _ = heads.shape
    ncp = kcv.shape[2]
    nsp = ovt.shape[0]
    hg = NSA_GROUP
    k_sel = min(N_SELECT, s // SEL_BLOCK)
    assert k_sel >= 3
    q_map = lambda u, g, qb: (H_QN + hg * g + u, qb, 0)
    return pl.pallas_call(
        functools.partial(_nsa_select_kernel, nsp=nsp, tq=tq, k_sel=k_sel),
        grid=(NSA_KV_GROUPS, s // tq),
        in_specs=[
            *[pl.BlockSpec((1, tq, HEAD_DIM), functools.partial(q_map, u)) for u in range(hg)],
            pl.BlockSpec((1, 1, ncp, HEAD_DIM), lambda g, qb: (0, g, 0, 0)),
            pl.BlockSpec((1, 1, HEAD_DIM, ncp), lambda g, qb: (1, g, 0, 0)),
            pl.BlockSpec((tq, HEAD_DIM), lambda g, qb: (qb, g)),
            pl.BlockSpec((nsp, ncp), lambda g, qb: (0, 0)),
        ],
        out_specs=[pl.BlockSpec((tq, hg * HEAD_DIM), lambda g, qb: (qb, g)),
                   pl.BlockSpec((1, tq, nsp), lambda g, qb: (g, qb, 0))],
        out_shape=[jax.ShapeDtypeStruct((s, D_NSA), F32),
                   jax.ShapeDtypeStruct((NSA_KV_GROUPS, s, nsp), BF16)],
        compiler_params=_params(("arbitrary", "arbitrary")),
        name="nsa_select",
    )(*([heads] * hg), kcv, vct, gates, ovt)


def _nsa_kernel(*refs, s_len, nsp, tq, kt):
    q_refs = refs[:NSA_GROUP]
    (ks_ref, vs_ref, kw_ref, vw_ref, gate_ref, cmp_ref, sel_ref,
     o_ref, kaug, vaug, vwaug, cbias, wbias, s_a, s_b, m_ref, acc_ref,
     part_ref) = refs[NSA_GROUP:]
    g = pl.program_id(0)
    qb = pl.program_id(1)
    t0 = qb * tq
    hg = NSA_GROUP
    rows = hg * tq
    ww = NSA_WINDOW + tq

    @pl.when((g == 0) & (qb == 0))
    def _():
        for r in range(kt // tq):
            d = (lax.broadcasted_iota(jnp.int32, (tq, kt), 0) + r * tq
                 - lax.broadcasted_iota(jnp.int32, (tq, kt), 1))
            cbias[r] = jnp.where(d >= 0, 0.0, NEG)
        for w in range(NSA_WINDOW // tq + 1):
            d = (lax.broadcasted_iota(jnp.int32, (tq, ww), 0) + w * tq
                 - lax.broadcasted_iota(jnp.int32, (tq, ww), 1))
            wbias[w] = jnp.where((d >= 0) & (d < NSA_WINDOW), 0.0, NEG)

    @pl.when(qb == 0)
    def _():
        kaug[:, 0:HEAD_DIM] = ks_ref[0]
        key = lax.broadcasted_iota(jnp.int32, (s_len, nsp), 0)
        blk = lax.broadcasted_iota(jnp.int32, (s_len, nsp), 1)
        kaug[:, HEAD_DIM:] = jnp.where(blk == key // SEL_BLOCK, 1.0, 0.0).astype(BF16)
        ones = jnp.ones((s_len, HEAD_DIM), BF16)
        vaug[:, 0:HEAD_DIM] = vs_ref[0]
        vaug[:, HEAD_DIM:] = ones
        vwaug[:, 0:HEAD_DIM] = vw_ref[0]
        vwaug[:, HEAD_DIM:] = ones

    gate = gate_ref[...]
    q5 = jnp.concatenate([r[0] for r in q_refs], axis=0)

    qaug = jnp.concatenate([q5, jnp.concatenate([sel_ref[0]] * hg, axis=0)], axis=1)

    w0 = pl.multiple_of(jnp.maximum(t0 - NSA_WINDOW, 0), tq)
    sw = _nt_dot(q5, kw_ref[0, pl.ds(w0, ww), :])
    sw = _add_per_head(sw, wbias[jnp.minimum(qb, NSA_WINDOW // tq)], hg)
    pw = jnp.exp2(sw - jnp.max(sw, axis=-1, keepdims=True))
    acc_w = jnp.dot(pw.astype(BF16), vwaug[pl.ds(w0, ww), :], preferred_element_type=F32)
    o_win = acc_w[:, :HEAD_DIM] / acc_w[:, HEAD_DIM:]
    for u in range(hg):
        c = N_NSA_BRANCHES * u + 2
        cols = slice(u * HEAD_DIM, (u + 1) * HEAD_DIM)
        part_ref[:, cols] = cmp_ref[:, cols] + gate[:, c:c + 1] * o_win[u * tq:(u + 1) * tq]

    def sel_scores(kti):
        k0 = pl.multiple_of(kti * kt, kt)
        return _nt_dot(qaug, kaug[pl.ds(k0, kt), :])

    def sel_update(s_ref, kti, diagonal):
        s = s_ref[...]
        if diagonal:
            s = _add_per_head(s, cbias[qb % (kt // tq)], hg)
        k0 = pl.multiple_of(kti * kt, kt)
        m_i = m_ref[...]
        m_new = jnp.maximum(m_i, jnp.max(s, axis=-1, keepdims=True))
        p = jnp.exp2(s - m_new)
        pv = jnp.dot(p.astype(BF16), vaug[pl.ds(k0, kt), :], preferred_element_type=F32)
        acc_ref[...] = jnp.exp2(m_i - m_new) * acc_ref[...] + pv
        m_ref[...] = m_new

    def sel_pair(pi, _):
        t = 2 * pi
        s_b[...] = sel_scores(t + 1)
        sel_update(s_a, t, False)
        s_a[...] = sel_scores(t + 2)
        sel_update(s_b, t + 1, False)
        return 0

    last = (t0 + tq + kt - 1) // kt - 1
    m_ref[...] = jnp.full((rows, 1), NEG, F32)
    acc_ref[...] = jnp.zeros((rows, 2 * HEAD_DIM), F32)
    s_a[...] = sel_scores(0)
    lax.fori_loop(0, last // 2, sel_pair, 0)
    t_even = 2 * (last // 2)

    @pl.when(last % 2 == 1)
    def _():
        s_b[...] = sel_scores(t_even + 1)
        sel_update(s_a, t_even, False)
        sel_update(s_b, t_even + 1, True)

    @pl.when(last % 2 == 0)
    def _():
        sel_update(s_a, t_even, True)

    gate = gate_ref[...]
    for u in range(hg):
        c = N_NSA_BRANCHES * u + 1
        acc_u = acc_ref[u * tq:(u + 1) * tq, :]
        o_slc = acc_u[:, :HEAD_DIM] / acc_u[:, HEAD_DIM:]
        cols = slice(u * HEAD_DIM, (u + 1) * HEAD_DIM)
        o_ref[:, cols] = (part_ref[:, cols] + gate[:, c:c + 1] * o_slc).astype(BF16)


def _nsa(heads, gates, part_cmp, sel_off, tq, kt):
    _, s, _ = heads.shape
    nsp = sel_off.shape[2]
    hg = NSA_GROUP
    assert kt % tq == 0 and NSA_WINDOW % tq == 0 and s % kt == 0
    q_map = lambda u, g, qb: (H_QN + hg * g + u, qb, 0)
    full = lambda hbase: pl.BlockSpec((1, s, HEAD_DIM), lambda g, qb: (hbase + g, 0, 0),
                                      pipeline_mode=pl.Buffered(1))
    return pl.pallas_call(
        functools.partial(_nsa_kernel, s_len=s, nsp=nsp, tq=tq, kt=kt),
        grid=(NSA_KV_GROUPS, s // tq),
        in_specs=[
            *[pl.BlockSpec((1, tq, HEAD_DIM), functools.partial(q_map, u)) for u in range(hg)],
            full(H_KS), full(H_VS), full(H_KW), full(H_VW),
            pl.BlockSpec((tq, HEAD_DIM), lambda g, qb: (qb, g)),
            pl.BlockSpec((tq, hg * HEAD_DIM), lambda g, qb: (qb, g)),
            pl.BlockSpec((1, tq, nsp), lambda g, qb: (g, qb, 0)),
        ],
        out_specs=pl.BlockSpec((tq, hg * HEAD_DIM), lambda g, qb: (qb, g)),
        out_shape=jax.ShapeDtypeStruct((s, D_NSA), BF16),
        scratch_shapes=[pltpu.VMEM((s, HEAD_DIM + nsp), BF16),
                        pltpu.VMEM((s, 2 * HEAD_DIM), BF16),
                        pltpu.VMEM((s, 2 * HEAD_DIM), BF16),
                        pltpu.VMEM((kt // tq, tq, kt), F32),
                        pltpu.VMEM((NSA_WINDOW // tq + 1, tq, NSA_WINDOW + tq), F32),
                        pltpu.VMEM((hg * tq, kt), F32),
                        pltpu.VMEM((hg * tq, kt), F32),
                        pltpu.VMEM((hg * tq, 1), F32),
                        pltpu.VMEM((hg * tq, 2 * HEAD_DIM), F32),
                        pltpu.VMEM((tq, hg * HEAD_DIM), F32)],
        compiler_params=_params(("arbitrary", "arbitrary")),
        name="nsa_attn",
    )(*([heads] * hg), heads, heads, heads, heads, gates, part_cmp, sel_off)


def _out_proj_ln_kernel(a_ref, b_ref, wa_ref, wb_ref, h_ref, g_ref, be_ref, o_ref, *, alpha):
    tm = o_ref.shape[0]
    rc = tm // ROW_CHUNKS
    for c in range(ROW_CHUNKS):
        r = slice(c * rc, (c + 1) * rc)
        mix = (jnp.dot(a_ref[r, :], wa_ref[...], preferred_element_type=F32)
               + jnp.dot(b_ref[r, :], wb_ref[...], preferred_element_type=F32))
        o_ref[r, :] = _layer_norm(alpha * h_ref[r, :] + mix, g_ref[...], be_ref[...])


def _out_proj_ln(mix_a, mix_b, wa, wb, h, g, b, alpha, tm):
    s, d = h.shape
    return pl.pallas_call(
        functools.partial(_out_proj_ln_kernel, alpha=alpha),
        grid=(s // tm,),
        in_specs=[
            pl.BlockSpec((tm, D_DIL), lambda i: (i, 0)),
            pl.BlockSpec((tm, D_NSA), lambda i: (i, 0)),
            pl.BlockSpec((D_DIL, d), lambda i: (0, 0)),
            pl.BlockSpec((D_NSA, d), lambda i: (0, 0)),
            pl.BlockSpec((tm, d), lambda i: (i, 0)),
            pl.BlockSpec((1, d), lambda i: (0, 0)),
            pl.BlockSpec((1, d), lambda i: (0, 0)),
        ],
        out_specs=pl.BlockSpec((tm, d), lambda i: (i, 0)),
        out_shape=jax.ShapeDtypeStruct((s, d), F32),
        compiler_params=_params(("arbitrary",)),
        name="out_proj_ln",
    )(mix_a, mix_b, wa, wb, h, g, b)


def _overlap_t(s):
    n_cmp = (s - CMP_BLOCK) // CMP_STRIDE + 1
    n_sel = s // SEL_BLOCK
    ncp = s // CMP_STRIDE
    nsp = -(-n_sel // HEAD_DIM) * HEAD_DIM
    c_lo = np.arange(n_cmp) * CMP_STRIDE
    c_hi = c_lo + CMP_BLOCK - 1
    s_lo = (np.arange(n_sel) * SEL_BLOCK)[:, None]
    ov = np.zeros((nsp, ncp), np.float32)
    ov[:n_sel, :n_cmp] = (c_lo[None, :] <= s_lo + SEL_BLOCK - 1) & (c_hi[None, :] >= s_lo)
    return jnp.asarray(ov, BF16)


def kernel(x, positions, ln1_g, ln1_b, ffn1_w1, ffn1_w3, ffn1_w2, w_in, gate_b, cmp_pe, cmp_w1,
           cmp_b1, cmp_w2, cmp_b2, w_out, ln2_g, ln2_b, ffn2_w1, ffn2_w3, ffn2_w2, ln3_g, ln3_b):
    bsz, s, d = x.shape
    assert bsz == 1 and d == (N_HEADS_DIL + N_HEADS_NSA) * HEAD_DIM
    assert s % 512 == 0 and s >= NSA_WINDOW + QBLK
    alpha = (2.0 * DEPTH) ** 0.25
    scale = HEAD_DIM ** -0.5 * LOG2_E
    row = lambda v: v.reshape(1, -1)
    tm = 512
    ffn_tm, ffn_tf = 1024, 256

    inv_freq = ROPE_THETA ** (-jnp.arange(0, HEAD_DIM, 2, dtype=F32) / HEAD_DIM)
    invf = jnp.concatenate([inv_freq, inv_freq]).reshape(1, HEAD_DIM)
    pos = positions[0].astype(F32).reshape(s, 1)
    cosf, sinf = _rope_tables(pos, pos, invf, 512)
    ncp = s // CMP_STRIDE
    n_cmp = (s - CMP_BLOCK) // CMP_STRIDE + 1
    pos_lo = jnp.pad(pos[0:n_cmp * CMP_STRIDE:CMP_STRIDE], ((0, ncp - n_cmp), (0, 0)))
    pos_hi = jnp.pad(pos[CMP_BLOCK - 1::CMP_STRIDE][:n_cmp], ((0, ncp - n_cmp), (0, 0)))
    cos_c, sin_c = _rope_tables(pos_lo, pos_hi, invf, ncp)

    g_off = 3 * D_DIL + D_NSA + 6 * KV_W
    per_g = NSA_GROUP * N_NSA_BRANCHES
    w_in_t = w_in[0].T
    wg = jnp.zeros((NSA_KV_GROUPS * HEAD_DIM, d), F32)
    bg = jnp.zeros((1, NSA_KV_GROUPS * HEAD_DIM), F32)
    for g in range(NSA_KV_GROUPS):
        wg = wg.at[g * HEAD_DIM:g * HEAD_DIM + per_g].set(
            w_in_t[g_off + g * per_g:g_off + (g + 1) * per_g])
        bg = bg.at[0, g * HEAD_DIM:g * HEAD_DIM + per_g].set(gate_b[0][g * per_g:(g + 1) * per_g])
    wg = wg.astype(BF16)

    h0 = x[0]
    h1, h1b = _ffn_ln(h0, ffn1_w1[0], ffn1_w3[0], ffn1_w2[0], row(ln1_g[0]), row(ln1_b[0]),
                      alpha, ffn_tm, ffn_tf, True)

    heads = _in_proj(h1b, w_in_t, cosf, sinf, scale, 1024)
    gates = _gates(h1b, wg, bg, tm)
    tok16 = heads[H_KC:H_KS].reshape(2 * NSA_KV_GROUPS, ncp, CMP_STRIDE * HEAD_DIM)
    kcv, kcv_t = _compress(
        tok16, cmp_pe[0].reshape(2, 1, CMP_BLOCK * HEAD_DIM), cmp_w1[0].astype(BF16),
        cmp_b1[0].reshape(2, 1, CMP_HIDDEN), cmp_w2[0].astype(BF16),
        cmp_b2[0].reshape(2, 1, HEAD_DIM), cos_c, sin_c)
    mix_a = _dilated(heads, 256, 2)
    part_cmp, sel_off = _nsa_select(heads, kcv, kcv_t, gates, _overlap_t(s), 512)
    mix_b = _nsa(heads, gates, part_cmp, sel_off, 256, 1024)

    wo = w_out[0].astype(BF16)
    h2 = _out_proj_ln(mix_a, mix_b, wo[:D_DIL], wo[D_DIL:], h1, row(ln2_g[0]), row(ln2_b[0]),
                      alpha, tm)
    (h3,) = _ffn_ln(h2, ffn2_w1[0], ffn2_w3[0], ffn2_w2[0], row(ln3_g[0]), row(ln3_b[0]),
                    alpha, ffn_tm, ffn_tf, False)
    return h3.reshape(bsz, s, d)
```

```python
import functools

import jax
import jax.numpy as jnp
import numpy as np
from jax import lax
from jax.experimental import pallas as pl
from jax.experimental.pallas import tpu as pltpu

HEAD_DIM = 128
N_HEADS_DIL = 6
N_HEADS_NSA = 10
NSA_KV_GROUPS = 2
NSA_GROUP = N_HEADS_NSA // NSA_KV_GROUPS
N_NSA_BRANCHES = 3
DIL_PAIRS = ((128, 1), (512, 4), (2048, 16))
CMP_BLOCK = 32
CMP_STRIDE = 16
CMP_HIDDEN = 256
SEL_BLOCK = 64
N_SELECT = 16
NSA_WINDOW = 512
ROPE_THETA = 10000.0
QBLK = 128
LN_EPS = 1e-5
NEG = -1e30
FORCE_BONUS = 1e4
SEL_OFF = -1e9
LOG2_E = 1.4426950408889634
DEPTH = 1

D_DIL = N_HEADS_DIL * HEAD_DIM
D_NSA = N_HEADS_NSA * HEAD_DIM
KV_W = NSA_KV_GROUPS * HEAD_DIM
N_GATES = N_HEADS_NSA * N_NSA_BRANCHES

H_QA, H_KA, H_VA, H_QN = 0, 6, 12, 18
H_KC, H_VC, H_KS, H_VS, H_KW, H_VW = 28, 30, 32, 34, 36, 38
N_PROJ_HEADS = 40
HEADS_PER_PROJ_BLOCK = 8
PROJ_SUB_HEADS = 4
ROW_CHUNKS = 2

VMEM_LIMIT_BYTES = 56 * 1024 * 1024
FFN_VMEM_LIMIT_BYTES = 60 * 1024 * 1024

F32 = jnp.float32
BF16 = jnp.bfloat16


def _params(sem, vmem=VMEM_LIMIT_BYTES):
    return pltpu.CompilerParams(dimension_semantics=sem, vmem_limit_bytes=vmem)


def _nt_dot(a, b):
    return lax.dot_general(a, b, (((1,), (1,)), ((), ())), preferred_element_type=F32)


def _layer_norm(y, g, b):
    mu = jnp.mean(y, axis=-1, keepdims=True)
    yc = y - mu
    var = jnp.mean(yc * yc, axis=-1, keepdims=True)
    return yc * lax.rsqrt(var + LN_EPS) * g + b


def _rope_table_kernel(pa_ref, pb_ref, invf_ref, cos_ref, sin_ref):
    pos = (pa_ref[...] + pb_ref[...]) * 0.5
    ang = pos * invf_ref[...]
    lane = lax.broadcasted_iota(jnp.int32, ang.shape, 1)
    cos_ref[...] = jnp.cos(ang)
    sin_ref[...] = jnp.where(lane < HEAD_DIM // 2, -1.0, 1.0) * jnp.sin(ang)


def _rope_tables(pos_a, pos_b, invf, tile):
    n = pos_a.shape[0]
    spec_p = pl.BlockSpec((tile, 1), lambda i: (i, 0))
    spec_t = pl.BlockSpec((tile, HEAD_DIM), lambda i: (i, 0))
    return pl.pallas_call(
        _rope_table_kernel,
        grid=(n // tile,),
        in_specs=[spec_p, spec_p, pl.BlockSpec((1, HEAD_DIM), lambda i: (0, 0))],
        out_specs=[spec_t, spec_t],
        out_shape=[jax.ShapeDtypeStruct((n, HEAD_DIM), F32)] * 2,
        compiler_params=_params(("arbitrary",)),
        name="rope_tables",
    )(pos_a, pos_b, invf)


def _ffn_ln_kernel(h_ref, w1_ref, w3_ref, w2_ref, g_ref, b_ref, o_ref, *rest, alpha, nf):
    hb_ref = rest[-1]
    f = pl.program_id(1)

    @pl.when(f == 0)
    def _():
        hb_ref[...] = h_ref[...].astype(BF16)
        o_ref[...] = jnp.zeros_like(o_ref)

    hb = hb_ref[...]
    a = jnp.dot(hb, w1_ref[...].astype(BF16), preferred_element_type=F32)
    b = jnp.dot(hb, w3_ref[...].astype(BF16), preferred_element_type=F32)
    act = (a * jax.nn.sigmoid(a)) * b
    o_ref[...] += jnp.dot(act.astype(BF16), w2_ref[...].astype(BF16),
                          preferred_element_type=F32)

    @pl.when(f == nf - 1)
    def _():
        y = alpha * h_ref[...] + 0.5 * o_ref[...]
        out = _layer_norm(y, g_ref[...], b_ref[...])
        o_ref[...] = out
        if len(rest) == 2:
            rest[0][...] = out.astype(BF16)


def _ffn_ln(h, w1, w3, w2, g, b, alpha, tm, tf, emit_bf16):
    s, d = h.shape
    dff = w1.shape[1]
    nf = dff // tf
    once = pl.Buffered(1)
    if emit_bf16:
        out_specs = [pl.BlockSpec((tm, d), lambda i, f: (i, 0), pipeline_mode=once)] * 2
        out_shape = [jax.ShapeDtypeStruct((s, d), F32), jax.ShapeDtypeStruct((s, d), BF16)]
    else:
        out_specs = [pl.BlockSpec((tm, d), lambda i, f: (i, 0))]
        out_shape = [jax.ShapeDtypeStruct((s, d), F32)]
    return pl.pallas_call(
        functools.partial(_ffn_ln_kernel, alpha=alpha, nf=nf),
        grid=(s // tm, nf),
        in_specs=[
            pl.BlockSpec((tm, d), lambda i, f: (i, 0)),
            pl.BlockSpec((d, tf), lambda i, f: (0, f)),
            pl.BlockSpec((d, tf), lambda i, f: (0, f)),
            pl.BlockSpec((tf, d), lambda i, f: (f, 0)),
            pl.BlockSpec((1, d), lambda i, f: (0, 0)),
            pl.BlockSpec((1, d), lambda i, f: (0, 0)),
        ],
        out_specs=out_specs,
        out_shape=out_shape,
        scratch_shapes=[pltpu.VMEM((tm, d), BF16)],
        compiler_params=_params(("arbitrary", "arbitrary"), FFN_VMEM_LIMIT_BYTES),
        name="ffn_ln",
    )(h, w1, w3, w2, g, b)


def _in_proj_kernel(h_ref, w_ref, cos_ref, sin_ref, o_ref, *, scale):
    j = pl.program_id(1)
    hpb = HEADS_PER_PROJ_BLOCK
    sub = PROJ_SUB_HEADS
    cos = cos_ref[...]
    sin = sin_ref[...]
    h = h_ref[...]
    for k0 in range(0, hpb, sub):
        w = w_ref[k0 * HEAD_DIM:(k0 + sub) * HEAD_DIM, :].astype(BF16)
        acc = _nt_dot(h, w)
        for k in range(sub):
            hh = j * hpb + k0 + k
            is_q = (hh < H_KA) | ((hh >= H_QN) & (hh < H_KC))
            rope = (is_q | (hh < H_VA) | ((hh >= H_KS) & (hh < H_VS))
                    | ((hh >= H_KW) & (hh < H_VW)))
            x = acc[:, k * HEAD_DIM:(k + 1) * HEAD_DIM]
            xr = x * cos + pltpu.roll(x, HEAD_DIM // 2, 1) * sin
            y = jnp.where(rope, xr, x) * jnp.where(is_q, scale, 1.0).astype(F32)
            o_ref[k0 + k] = y.astype(BF16)


def _in_proj(hb, w, cosf, sinf, scale, tm):
    s, d = hb.shape
    hpb = HEADS_PER_PROJ_BLOCK
    return pl.pallas_call(
        functools.partial(_in_proj_kernel, scale=scale),
        grid=(s // tm, N_PROJ_HEADS // hpb),
        in_specs=[
            pl.BlockSpec((tm, d), lambda i, j: (i, 0)),
            pl.BlockSpec((hpb * HEAD_DIM, d), lambda i, j: (j, 0)),
            pl.BlockSpec((tm, HEAD_DIM), lambda i, j: (i, 0)),
            pl.BlockSpec((tm, HEAD_DIM), lambda i, j: (i, 0)),
        ],
        out_specs=pl.BlockSpec((hpb, tm, HEAD_DIM), lambda i, j: (j, i, 0)),
        out_shape=jax.ShapeDtypeStruct((N_PROJ_HEADS, s, HEAD_DIM), BF16),
        compiler_params=_params(("arbitrary", "arbitrary")),
        name="in_proj",
    )(hb, w, cosf, sinf)


def _gates_kernel(h_ref, w_ref, b_ref, o_ref):
    z = _nt_dot(h_ref[...], w_ref[...]) + b_ref[...]
    o_ref[...] = jax.nn.sigmoid(z)


def _gates(hb, wg, bg, tm):
    s, d = hb.shape
    n = wg.shape[0]
    return pl.pallas_call(
        _gates_kernel,
        grid=(s // tm,),
        in_specs=[pl.BlockSpec((tm, d), lambda i: (i, 0)),
                  pl.BlockSpec((n, d), lambda i: (0, 0)),
                  pl.BlockSpec((1, n), lambda i: (0, 0))],
        out_specs=pl.BlockSpec((tm, n), lambda i: (i, 0)),
        out_shape=jax.ShapeDtypeStruct((s, n), F32),
        compiler_params=_params(("arbitrary",)),
        name="nsa_gates",
    )(hb, wg, bg)


def _compress_kernel(tok_ref, pe_ref, w1_ref, b1_ref, w2_ref, b2_ref, cos_ref, sin_ref,
                     o_ref, ot_ref):
    j = pl.program_id(0)
    half = CMP_STRIDE * HEAD_DIM
    tok = tok_ref[0].astype(F32)
    pe = pe_ref[0]
    w1 = w1_ref[0]
    top = (tok + pe[:, :half]).astype(BF16)
    bot = (tok + pe[:, half:]).astype(BF16)
    u = jnp.dot(top, w1[:half], preferred_element_type=F32)
    v = jnp.dot(bot, w1[half:], preferred_element_type=F32)
    nc = u.shape[0]
    hid = u + pltpu.roll(v, nc - 1, 0) + b1_ref[0]
    hid = jax.nn.gelu(hid)
    out = jnp.dot(hid.astype(BF16), w2_ref[0], preferred_element_type=F32) + b2_ref[0]
    roped = out * cos_ref[...] + pltpu.roll(out, HEAD_DIM // 2, 1) * sin_ref[...]
    out = jnp.where(j == 0, roped, out)
    o_ref[0, 0] = out.astype(BF16)
    ot_ref[0, 0] = out.T.astype(BF16)


def _compress(tok16, pe, w1, b1, w2, b2, cos_c, sin_c):
    nc = tok16.shape[1]
    blk = CMP_BLOCK * HEAD_DIM
    g = NSA_KV_GROUPS
    return pl.pallas_call(
        _compress_kernel,
        grid=(2, g),
        in_specs=[
            pl.BlockSpec((1, nc, CMP_STRIDE * HEAD_DIM), lambda j, gi: (g * j + gi, 0, 0)),
            pl.BlockSpec((1, 1, blk), lambda j, gi: (j, 0, 0)),
            pl.BlockSpec((1, blk, CMP_HIDDEN), lambda j, gi: (j, 0, 0)),
            pl.BlockSpec((1, 1, CMP_HIDDEN), lambda j, gi: (j, 0, 0)),
            pl.BlockSpec((1, CMP_HIDDEN, HEAD_DIM), lambda j, gi: (j, 0, 0)),
            pl.BlockSpec((1, 1, HEAD_DIM), lambda j, gi: (j, 0, 0)),
            pl.BlockSpec((nc, HEAD_DIM), lambda j, gi: (0, 0)),
            pl.BlockSpec((nc, HEAD_DIM), lambda j, gi: (0, 0)),
        ],
        out_specs=[pl.BlockSpec((1, 1, nc, HEAD_DIM), lambda j, gi: (j, gi, 0, 0)),
                   pl.BlockSpec((1, 1, HEAD_DIM, nc), lambda j, gi: (j, gi, 0, 0))],
        out_shape=[jax.ShapeDtypeStruct((2, g, nc, HEAD_DIM), BF16),
                   jax.ShapeDtypeStruct((2, g, HEAD_DIM, nc), BF16)],
        compiler_params=_params(("arbitrary", "arbitrary")),
        name="nsa_compress",
    )(tok16, pe, w1, b1, w2, b2, cos_c, sin_c)


def _dil_kernel(q_ref, k_ref, v_ref, o_ref, kpad, vpad, bias_ref, *, tq, n_sub, padk):
    h = pl.program_id(0)
    i = pl.program_id(1)
    w = padk + tq

    @pl.when((h == 0) & (i == 0))
    def _():
        r = lax.broadcasted_iota(jnp.int32, (tq, w), 0)
        c = lax.broadcasted_iota(jnp.int32, (tq, w), 1)
        d = r + padk - c
        cnt = jnp.zeros((tq, w), F32)
        for window, dil in DIL_PAIRS:
            hit = (d >= 0) & (d <= window) & ((d & (dil - 1)) == 0)
            cnt = cnt + jnp.where(hit, 1.0, 0.0)
        bias_ref[...] = jnp.where(cnt > 0.0, jnp.log2(jnp.maximum(cnt, 1.0)), NEG)

    @pl.when(i == 0)
    def _():
        kpad[0:padk, :] = jnp.zeros((padk, HEAD_DIM), BF16)
        vpad[0:padk, :] = jnp.zeros((padk, HEAD_DIM), BF16)
        kpad[padk:, :] = k_ref[0]
        vpad[padk:, :] = v_ref[0]

    col = lax.broadcasted_iota(jnp.int32, (1, w), 1)
    for part in range(n_sub):
        q0 = pl.multiple_of((i * n_sub + part) * tq, tq)
        q = q_ref[0, part * tq:(part + 1) * tq, :]
        s = _nt_dot(q, kpad[pl.ds(q0, w), :]) + bias_ref[...]
        s = jnp.where(col >= padk - q0, s, NEG)
        p = jnp.exp2(s - jnp.max(s, axis=-1, keepdims=True))
        l = jnp.sum(p, axis=-1, keepdims=True)
        o = jnp.dot(p.astype(BF16), vpad[pl.ds(q0, w), :], preferred_element_type=F32)
        o_ref[part * tq:(part + 1) * tq, :] = (o / l).astype(BF16)


def _dilated(heads, tq, n_sub):
    _, s, _ = heads.shape
    padk = max(wd for wd, _ in DIL_PAIRS)
    tstep = tq * n_sub
    return pl.pallas_call(
        functools.partial(_dil_kernel, tq=tq, n_sub=n_sub, padk=padk),
        grid=(N_HEADS_DIL, s // tstep),
        in_specs=[
            pl.BlockSpec((1, tstep, HEAD_DIM), lambda h, i: (H_QA + h, i, 0)),
            pl.BlockSpec((1, s, HEAD_DIM), lambda h, i: (H_KA + h, 0, 0)),
            pl.BlockSpec((1, s, HEAD_DIM), lambda h, i: (H_VA + h, 0, 0)),
        ],
        out_specs=pl.BlockSpec((tstep, HEAD_DIM), lambda h, i: (i, h)),
        out_shape=jax.ShapeDtypeStruct((s, D_DIL), BF16),
        scratch_shapes=[pltpu.VMEM((padk + s, HEAD_DIM), BF16),
                        pltpu.VMEM((padk + s, HEAD_DIM), BF16),
                        pltpu.VMEM((tq, padk + tq), F32)],
        compiler_params=_params(("arbitrary", "arbitrary")),
        name="dilated_attn",
    )(heads, heads, heads)


def _add_per_head(s, bias, hg):
    tq = bias.shape[0]
    return jnp.concatenate([s[u * tq:(u + 1) * tq] + bias for u in range(hg)], axis=0)


def _nsa_select_kernel(*refs, nsp, tq, k_sel):
    q_refs = refs[:NSA_GROUP]
    kc_ref, vct_ref, gate_ref, ovt_ref, cmp_ref, sel_ref = refs[NSA_GROUP:]
    t0 = pl.program_id(1) * tq
    hg = NSA_GROUP
    ncp = kc_ref.shape[2]
    gate = gate_ref[...]

    kc = kc_ref[0, 0]
    vct = vct_ref[0, 0]
    n_io = lax.broadcasted_iota(jnp.int32, (ncp, tq), 0)
    t_io = t0 + lax.broadcasted_iota(jnp.int32, (ncp, tq), 1)
    cmask = (n_io * CMP_STRIDE + (CMP_BLOCK - 1) <= t_io) & (n_io < ncp - 1)
    ovt = ovt_ref[...]
    imp = jnp.zeros((nsp, tq), F32)
    for u in range(hg):
        st = jnp.where(cmask, _nt_dot(kc, q_refs[u][0]), NEG)
        m = jnp.maximum(jnp.max(st, axis=0, keepdims=True), 0.1 * NEG)
        e = jnp.exp2(st - m)
        r = 1.0 / jnp.maximum(jnp.sum(e, axis=0, keepdims=True), 1e-30)
        eb = e.astype(BF16)
        imp = imp + jnp.dot(ovt, eb, preferred_element_type=F32) * r
        o_cmp = (jnp.dot(vct, eb, preferred_element_type=F32) * r).T
        c = N_NSA_BRANCHES * u
        cmp_ref[:, u * HEAD_DIM:(u + 1) * HEAD_DIM] = gate[:, c:c + 1] * o_cmp

    j_io = lax.broadcasted_iota(jnp.int32, (nsp, tq), 0)
    t_sel = t0 + lax.broadcasted_iota(jnp.int32, (nsp, tq), 1)
    valid = j_io * SEL_BLOCK <= t_sel
    cur = t_sel // SEL_BLOCK
    forced = (j_io == 0) | (j_io == cur) | (j_io == cur - 1)
    score = jnp.where(valid & jnp.logical_not(forced), imp, NEG)
    picked = jnp.where(forced, 1.0, 0.0)
    for _ in range(k_sel - 3):
        m = jnp.max(score, axis=0, keepdims=True)
        first = jnp.min(jnp.where(score == m, j_io, nsp), axis=0, keepdims=True)
        hit = j_io == first
        picked = jnp.where(hit, 1.0, picked)
        score = jnp.where(hit, -jnp.inf, score)
    sel_t = jnp.where(valid, picked, 0.0)
    sel_ref[0] = jnp.where(sel_t.T > 0.0, 0.0, SEL_OFF).astype(BF16)


def _nsa_select(heads, kcv, vct, gates, ovt, tq):
    _, s, _ = heads.shape
    ncp = kcv.shape[2]
    nsp = ovt.shape[0]
    hg = NSA_GROUP
    k_sel = min(N_SELECT, s // SEL_BLOCK)
    assert k_sel >= 3
    q_map = lambda u, g, qb: (H_QN + hg * g + u, qb, 0)
    return pl.pallas_call(
        functools.partial(_nsa_select_kernel, nsp=nsp, tq=tq, k_sel=k_sel),
        grid=(NSA_KV_GROUPS, s // tq),
        in_specs=[
            *[pl.BlockSpec((1, tq, HEAD_DIM), functools.partial(q_map, u)) for u in range(hg)],
            pl.BlockSpec((1, 1, ncp, HEAD_DIM), lambda g, qb: (0, g, 0, 0)),
            pl.BlockSpec((1, 1, HEAD_DIM, ncp), lambda g, qb: (1, g, 0, 0)),
            pl.BlockSpec((tq, HEAD_DIM), lambda g, qb: (qb, g)),
            pl.BlockSpec((nsp, ncp), lambda g, qb: (0, 0)),
        ],
        out_specs=[pl.BlockSpec((tq, hg * HEAD_DIM), lambda g, qb: (qb, g)),
                   pl.BlockSpec((1, tq, nsp), lambda g, qb: (g, qb, 0))],
        out_shape=[jax.ShapeDtypeStruct((s, D_NSA), F32),
                   jax.ShapeDtypeStruct((NSA_KV_GROUPS, s, nsp), BF16)],
        compiler_params=_params(("arbitrary", "arbitrary")),
        name="nsa_select",
    )(*([heads] * hg), kcv, vct, gates, ovt)


def _nsa_kernel(*refs, s_len, nsp, tq, kt):
    q_refs = refs[:NSA_GROUP]
    (ks_ref, vs_ref, kw_ref, vw_ref, gate_ref, cmp_ref, sel_ref,
     o_ref, kaug, vaug, vwaug, cbias, wbias, s_a, s_b, m_ref, acc_ref,
     part_ref) = refs[NSA_GROUP:]
    g = pl.program_id(0)
    qb = pl.program_id(1)
    t0 = qb * tq
    hg = NSA_GROUP
    rows = hg * tq
    ww = NSA_WINDOW + tq

    @pl.when((g == 0) & (qb == 0))
    def _():
        for r in range(kt // tq):
            d = (lax.broadcasted_iota(jnp.int32, (tq, kt), 0) + r * tq
                 - lax.broadcasted_iota(jnp.int32, (tq, kt), 1))
            cbias[r] = jnp.where(d >= 0, 0.0, NEG)
        for w in range(NSA_WINDOW // tq + 1):
            d = (lax.broadcasted_iota(jnp.int32, (tq, ww), 0) + w * tq
                 - lax.broadcasted_iota(jnp.int32, (tq, ww), 1))
            wbias[w] = jnp.where((d >= 0) & (d < NSA_WINDOW), 0.0, NEG)

    @pl.when(qb == 0)
    def _():
        kaug[:, 0:HEAD_DIM] = ks_ref[0]
        key = lax.broadcasted_iota(jnp.int32, (s_len, nsp), 0)
        blk = lax.broadcasted_iota(jnp.int32, (s_len, nsp), 1)
        kaug[:, HEAD_DIM:] = jnp.where(blk == key // SEL_BLOCK, 1.0, 0.0).astype(BF16)
        ones = jnp.ones((s_len, HEAD_DIM), BF16)
        vaug[:, 0:HEAD_DIM] = vs_ref[0]
        vaug[:, HEAD_DIM:] = ones
        vwaug[:, 0:HEAD_DIM] = vw_ref[0]
        vwaug[:, HEAD_DIM:] = ones

    gate = gate_ref[...]
    q5 = jnp.concatenate([r[0] for r in q_refs], axis=0)

    qaug = jnp.concatenate([q5, jnp.concatenate([sel_ref[0]] * hg, axis=0)], axis=1)

    w0 = pl.multiple_of(jnp.maximum(t0 - NSA_WINDOW, 0), tq)
    sw = _nt_dot(q5, kw_ref[0, pl.ds(w0, ww), :])
    sw = _add_per_head(sw, wbias[jnp.minimum(qb, NSA_WINDOW // tq)], hg)
    pw = jnp.exp2(sw - jnp.max(sw, axis=-1, keepdims=True))
    acc_w = jnp.dot(pw.astype(BF16), vwaug[pl.ds(w0, ww), :], preferred_element_type=F32)
    o_win = acc_w[:, :HEAD_DIM] / acc_w[:, HEAD_DIM:]
    for u in range(hg):
        c = N_NSA_BRANCHES * u + 2
        cols = slice(u * HEAD_DIM, (u + 1) * HEAD_DIM)
        part_ref[:, cols] = cmp_ref[:, cols] + gate[:, c:c + 1] * o_win[u * tq:(u + 1) * tq]

    def sel_scores(kti):
        k0 = pl.multiple_of(kti * kt, kt)
        return _nt_dot(qaug, kaug[pl.ds(k0, kt), :])

    def sel_update(s_ref, kti, diagonal):
        s = s_ref[...]
        if diagonal:
            s = _add_per_head(s, cbias[qb % (kt // tq)], hg)
        k0 = pl.multiple_of(kti * kt, kt)
        m_i = m_ref[...]
        m_new = jnp.maximum(m_i, jnp.max(s, axis=-1, keepdims=True))
        p = jnp.exp2(s - m_new)
        pv = jnp.dot(p.astype(BF16), vaug[pl.ds(k0, kt), :], preferred_element_type=F32)
        acc_ref[...] = jnp.exp2(m_i - m_new) * acc_ref[...] + pv
        m_ref[...] = m_new

    def sel_pair(pi, _):
        t = 2 * pi
        s_b[...] = sel_scores(t + 1)
        sel_update(s_a, t, False)
        s_a[...] = sel_scores(t + 2)
        sel_update(s_b, t + 1, False)
        return 0

    last = (t0 + tq + kt - 1) // kt - 1
    m_ref[...] = jnp.full((rows, 1), NEG, F32)
    acc_ref[...] = jnp.zeros((rows, 2 * HEAD_DIM), F32)
    s_a[...] = sel_scores(0)
    lax.fori_loop(0, last // 2, sel_pair, 0)
    t_even = 2 * (last // 2)

    @pl.when(last % 2 == 1)
    def _():
        s_b[...] = sel_scores(t_even + 1)
        sel_update(s_a, t_even, False)
        sel_update(s_b, t_even + 1, True)

    @pl.when(last % 2 == 0)
    def _():
        sel_update(s_a, t_even, True)

    gate = gate_ref[...]
    for u in range(hg):
        c = N_NSA_BRANCHES * u + 1
        acc_u = acc_ref[u * tq:(u + 1) * tq, :]
        o_slc = acc_u[:, :HEAD_DIM] / acc_u[:, HEAD_DIM:]
        cols = slice(u * HEAD_DIM, (u + 1) * HEAD_DIM)
        o_ref[:, cols] = (part_ref[:, cols] + gate[:, c:c + 1] * o_slc).astype(BF16)


def _nsa(heads, gates, part_cmp, sel_off, tq, kt):
    _, s, _ = heads.shape
    nsp = sel_off.shape[2]
    hg = NSA_GROUP
    assert kt % tq == 0 and NSA_WINDOW % tq == 0 and s % kt == 0
    q_map = lambda u, g, qb: (H_QN + hg * g + u, qb, 0)
    full = lambda hbase: pl.BlockSpec((1, s, HEAD_DIM), lambda g, qb: (hbase + g, 0, 0),
                                      pipeline_mode=pl.Buffered(1))
    return pl.pallas_call(
        functools.partial(_nsa_kernel, s_len=s, nsp=nsp, tq=tq, kt=kt),
        grid=(NSA_KV_GROUPS, s // tq),
        in_specs=[
            *[pl.BlockSpec((1, tq, HEAD_DIM), functools.partial(q_map, u)) for u in range(hg)],
            full(H_KS), full(H_VS), full(H_KW), full(H_VW),
            pl.BlockSpec((tq, HEAD_DIM), lambda g, qb: (qb, g)),
            pl.BlockSpec((tq, hg * HEAD_DIM), lambda g, qb: (qb, g)),
            pl.BlockSpec((1, tq, nsp), lambda g, qb: (g, qb, 0)),
        ],
        out_specs=pl.BlockSpec((tq, hg * HEAD_DIM), lambda g, qb: (qb, g)),
        out_shape=jax.ShapeDtypeStruct((s, D_NSA), BF16),
        scratch_shapes=[pltpu.VMEM((s, HEAD_DIM + nsp), BF16),
                        pltpu.VMEM((s, 2 * HEAD_DIM), BF16),
                        pltpu.VMEM((s, 2 * HEAD_DIM), BF16),
                        pltpu.VMEM((kt // tq, tq, kt), F32),
                        pltpu.VMEM((NSA_WINDOW // tq + 1, tq, NSA_WINDOW + tq), F32),
                        pltpu.VMEM((hg * tq, kt), F32),
                        pltpu.VMEM((hg * tq, kt), F32),
                        pltpu.VMEM((hg * tq, 1), F32),
                        pltpu.VMEM((hg * tq, 2 * HEAD_DIM), F32),
                        pltpu.VMEM((tq, hg * HEAD_DIM), F32)],
        compiler_params=_params(("arbitrary", "arbitrary")),
        name="nsa_attn",
    )(*([heads] * hg), heads, heads, heads, heads, gates, part_cmp, sel_off)


def _out_proj_ln_kernel(a_ref, b_ref, wa_ref, wb_ref, h_ref, g_ref, be_ref, o_ref, *, alpha):
    tm = o_ref.shape[0]
    rc = tm // ROW_CHUNKS
    for c in range(ROW_CHUNKS):
        r = slice(c * rc, (c + 1) * rc)
        mix = (jnp.dot(a_ref[r, :], wa_ref[...], preferred_element_type=F32)
               + jnp.dot(b_ref[r, :], wb_ref[...], preferred_element_type=F32))
        o_ref[r, :] = _layer_norm(alpha * h_ref[r, :] + mix, g_ref[...], be_ref[...])


def _out_proj_ln(mix_a, mix_b, wa, wb, h, g, b, alpha, tm):
    s, d = h.shape
    return pl.pallas_call(
        functools.partial(_out_proj_ln_kernel, alpha=alpha),
        grid=(s // tm,),
        in_specs=[
            pl.BlockSpec((tm, D_DIL), lambda i: (i, 0)),
            pl.BlockSpec((tm, D_NSA), lambda i: (i, 0)),
            pl.BlockSpec((D_DIL, d), lambda i: (0, 0)),
            pl.BlockSpec((D_NSA, d), lambda i: (0, 0)),
            pl.BlockSpec((tm, d), lambda i: (i, 0)),
            pl.BlockSpec((1, d), lambda i: (0, 0)),
            pl.BlockSpec((1, d), lambda i: (0, 0)),
        ],
        out_specs=pl.BlockSpec((tm, d), lambda i: (i, 0)),
        out_shape=jax.ShapeDtypeStruct((s, d), F32),
        compiler_params=_params(("arbitrary",)),
        name="out_proj_ln",
    )(mix_a, mix_b, wa, wb, h, g, b)


def _overlap_t(s):
    n_cmp = (s - CMP_BLOCK) // CMP_STRIDE + 1
    n_sel = s // SEL_BLOCK
    ncp = s // CMP_STRIDE
    nsp = -(-n_sel // HEAD_DIM) * HEAD_DIM
    c_lo = np.arange(n_cmp) * CMP_STRIDE
    c_hi = c_lo + CMP_BLOCK - 1
    s_lo = (np.arange(n_sel) * SEL_BLOCK)[:, None]
    ov = np.zeros((nsp, ncp), np.float32)
    ov[:n_sel, :n_cmp] = (c_lo[None, :] <= s_lo + SEL_BLOCK - 1) & (c_hi[None, :] >= s_lo)
    return jnp.asarray(ov, BF16)


def kernel(x, positions, ln1_g, ln1_b, ffn1_w1, ffn1_w3, ffn1_w2, w_in, gate_b, cmp_pe, cmp_w1,
           cmp_b1, cmp_w2, cmp_b2, w_out, ln2_g, ln2_b, ffn2_w1, ffn2_w3, ffn2_w2, ln3_g, ln3_b):
    bsz, s, d = x.shape
    assert bsz == 1 and d == (N_HEADS_DIL + N_HEADS_NSA) * HEAD_DIM
    assert s % 512 == 0 and s >= NSA_WINDOW + QBLK
    alpha = (2.0 * DEPTH) ** 0.25
    scale = HEAD_DIM ** -0.5 * LOG2_E
    row = lambda v: v.reshape(1, -1)
    tm = 512
    ffn_tm, ffn_tf = 1024, 256

    inv_freq = ROPE_THETA ** (-jnp.arange(0, HEAD_DIM, 2, dtype=F32) / HEAD_DIM)
    invf = jnp.concatenate([inv_freq, inv_freq]).reshape(1, HEAD_DIM)
    pos = positions[0].astype(F32).reshape(s, 1)
    cosf, sinf = _rope_tables(pos, pos, invf, 512)
    ncp = s // CMP_STRIDE
    n_cmp = (s - CMP_BLOCK) // CMP_STRIDE + 1
    pos_lo = jnp.pad(pos[0:n_cmp * CMP_STRIDE:CMP_STRIDE], ((0, ncp - n_cmp), (0, 0)))
    pos_hi = jnp.pad(pos[CMP_BLOCK - 1::CMP_STRIDE][:n_cmp], ((0, ncp - n_cmp), (0, 0)))
    cos_c, sin_c = _rope_tables(pos_lo, pos_hi, invf, ncp)

    g_off = 3 * D_DIL + D_NSA + 6 * KV_W
    per_g = NSA_GROUP * N_NSA_BRANCHES
    w_in_t = w_in[0].T
    wg = jnp.zeros((NSA_KV_GROUPS * HEAD_DIM, d), F32)
    bg = jnp.zeros((1, NSA_KV_GROUPS * HEAD_DIM), F32)
    for g in range(NSA_KV_GROUPS):
        wg = wg.at[g * HEAD_DIM:g * HEAD_DIM + per_g].set(
            w_in_t[g_off + g * per_g:g_off + (g + 1) * per_g])
        bg = bg.at[0, g * HEAD_DIM:g * HEAD_DIM + per_g].set(gate_b[0][g * per_g:(g + 1) * per_g])
    wg = wg.astype(BF16)

    h0 = x[0]
    h1, h1b = _ffn_ln(h0, ffn1_w1[0], ffn1_w3[0], ffn1_w2[0], row(ln1_g[0]), row(ln1_b[0]),
                      alpha, ffn_tm, ffn_tf, True)

    heads = _in_proj(h1b, w_in_t, cosf, sinf, scale, 1024)
    gates = _gates(h1b, wg, bg, tm)
    tok16 = heads[H_KC:H_KS].reshape(2 * NSA_KV_GROUPS, ncp, CMP_STRIDE * HEAD_DIM)
    kcv, kcv_t = _compress(
        tok16, cmp_pe[0].reshape(2, 1, CMP_BLOCK * HEAD_DIM), cmp_w1[0].astype(BF16),
        cmp_b1[0].reshape(2, 1, CMP_HIDDEN), cmp_w2[0].astype(BF16),
        cmp_b2[0].reshape(2, 1, HEAD_DIM), cos_c, sin_c)
    mix_a = _dilated(heads, 256, 2)
    part_cmp, sel_off = _nsa_select(heads, kcv, kcv_t, gates, _overlap_t(s), 512)
    mix_b = _nsa(heads, gates, part_cmp, sel_off, 256, 1024)

    wo = w_out[0].astype(BF16)
    h2 = _out_proj_ln(mix_a, mix_b, wo[:D_DIL], wo[D_DIL:], h1, row(ln2_g[0]), row(ln2_b[0]),
                      alpha, tm)
    (h3,) = _ffn_ln(h2, ffn2_w1[0], ffn2_w3[0], ffn2_w2[0], row(ln3_g[0]), row(ln3_b[0]),
                    alpha, ffn_tm, ffn_tf, False)
    return h3.reshape(bsz, s, d)
```

```python
import functools

import jax
import jax.numpy as jnp
import numpy as np
from jax import lax
from jax.experimental import pallas as pl
from jax.experimental.pallas import tpu as pltpu

HEAD_DIM = 128
N_HEADS_DIL = 6
N_HEADS_NSA = 10
NSA_KV_GROUPS = 2
NSA_GROUP = N_HEADS_NSA // NSA_KV_GROUPS
N_NSA_BRANCHES = 3
DIL_PAIRS = ((128, 1), (512, 4), (2048, 16))
CMP_BLOCK = 32
CMP_STRIDE = 16
CMP_HIDDEN = 256
SEL_BLOCK = 64
N_SELECT = 16
NSA_WINDOW = 512
ROPE_THETA = 10000.0
QBLK = 128
LN_EPS = 1e-5
NEG = -1e30
FORCE_BONUS = 1e4
SEL_OFF = -1e9
LOG2_E = 1.4426950408889634
DEPTH = 1

D_DIL = N_HEADS_DIL * HEAD_DIM
D_NSA = N_HEADS_NSA * HEAD_DIM
KV_W = NSA_KV_GROUPS * HEAD_DIM
N_GATES = N_HEADS_NSA * N_NSA_BRANCHES

H_QA, H_KA, H_VA, H_QN = 0, 6, 12, 18
H_KC, H_VC, H_KS, H_VS, H_KW, H_VW = 28, 30, 32, 34, 36, 38
N_PROJ_HEADS = 40
HEADS_PER_PROJ_BLOCK = 8
PROJ_SUB_HEADS = 4
ROW_CHUNKS = 2

VMEM_LIMIT_BYTES = 56 * 1024 * 1024
FFN_VMEM_LIMIT_BYTES = 60 * 1024 * 1024

F32 = jnp.float32
BF16 = jnp.bfloat16


def _params(sem, vmem=VMEM_LIMIT_BYTES):
    return pltpu.CompilerParams(dimension_semantics=sem, vmem_limit_bytes=vmem)


def _nt_dot(a, b):
    return lax.dot_general(a, b, (((1,), (1,)), ((), ())), preferred_element_type=F32)


def _layer_norm(y, g, b):
    mu = jnp.mean(y, axis=-1, keepdims=True)
    yc = y - mu
    var = jnp.mean(yc * yc, axis=-1, keepdims=True)
    return yc * lax.rsqrt(var + LN_EPS) * g + b


def _rope_table_kernel(pa_ref, pb_ref, invf_ref, cos_ref, sin_ref):
    pos = (pa_ref[...] + pb_ref[...]) * 0.5
    ang = pos * invf_ref[...]
    lane = lax.broadcasted_iota(jnp.int32, ang.shape, 1)
    cos_ref[...] = jnp.cos(ang)
    sin_ref[...] = jnp.where(lane < HEAD_DIM // 2, -1.0, 1.0) * jnp.sin(ang)


def _rope_tables(pos_a, pos_b, invf, tile):
    n = pos_a.shape[0]
    spec_p = pl.BlockSpec((tile, 1), lambda i: (i, 0))
    spec_t = pl.BlockSpec((tile, HEAD_DIM), lambda i: (i, 0))
    return pl.pallas_call(
        _rope_table_kernel,
        grid=(n // tile,),
        in_specs=[spec_p, spec_p, pl.BlockSpec((1, HEAD_DIM), lambda i: (0, 0))],
        out_specs=[spec_t, spec_t],
        out_shape=[jax.ShapeDtypeStruct((n, HEAD_DIM), F32)] * 2,
        compiler_params=_params(("arbitrary",)),
        name="rope_tables",
    )(pos_a, pos_b, invf)


def _ffn_ln_kernel(h_ref, w1_ref, w3_ref, w2_ref, g_ref, b_ref, o_ref, *rest, alpha, nf):
    hb_ref = rest[-1]
    f = pl.program_id(1)

    @pl.when(f == 0)
    def _():
        hb_ref[...] = h_ref[...].astype(BF16)
        o_ref[...] = jnp.zeros_like(o_ref)

    hb = hb_ref[...]
    a = jnp.dot(hb, w1_ref[...].astype(BF16), preferred_element_type=F32)
    b = jnp.dot(hb, w3_ref[...].astype(BF16), preferred_element_type=F32)
    act = (a * jax.nn.sigmoid(a)) * b
    o_ref[...] += jnp.dot(act.astype(BF16), w2_ref[...].astype(BF16),
                          preferred_element_type=F32)

    @pl.when(f == nf - 1)
    def _():
        y = alpha * h_ref[...] + 0.5 * o_ref[...]
        out = _layer_norm(y, g_ref[...], b_ref[...])
        o_ref[...] = out
        if len(rest) == 2:
            rest[0][...] = out.astype(BF16)


def _ffn_ln(h, w1, w3, w2, g, b, alpha, tm, tf, emit_bf16):
    s, d = h.shape
    dff = w1.shape[1]
    nf = dff // tf
    once = pl.Buffered(1)
    if emit_bf16:
        out_specs = [pl.BlockSpec((tm, d), lambda i, f: (i, 0), pipeline_mode=once)] * 2
        out_shape = [jax.ShapeDtypeStruct((s, d), F32), jax.ShapeDtypeStruct((s, d), BF16)]
    else:
        out_specs = [pl.BlockSpec((tm, d), lambda i, f: (i, 0))]
        out_shape = [jax.ShapeDtypeStruct((s, d), F32)]
    return pl.pallas_call(
        functools.partial(_ffn_ln_kernel, alpha=alpha, nf=nf),
        grid=(s // tm, nf),
        in_specs=[
            pl.BlockSpec((tm, d), lambda i, f: (i, 0)),
            pl.BlockSpec((d, tf), lambda i, f: (0, f)),
            pl.BlockSpec((d, tf), lambda i, f: (0, f)),
            pl.BlockSpec((tf, d), lambda i, f: (f, 0)),
            pl.BlockSpec((1, d), lambda i, f: (0, 0)),
            pl.BlockSpec((1, d), lambda i, f: (0, 0)),
        ],
        out_specs=out_specs,
        out_shape=out_shape,
        scratch_shapes=[pltpu.VMEM((tm, d), BF16)],
        compiler_params=_params(("arbitrary", "arbitrary"), FFN_VMEM_LIMIT_BYTES),
        name="ffn_ln",
    )(h, w1, w3, w2, g, b)


def _in_proj_kernel(h_ref, w_ref, cos_ref, sin_ref, o_ref, *, scale):
    j = pl.program_id(1)
    hpb = HEADS_PER_PROJ_BLOCK
    sub = PROJ_SUB_HEADS
    cos = cos_ref[...]
    sin = sin_ref[...]
    h = h_ref[...]
    for k0 in range(0, hpb, sub):
        w = w_ref[k0 * HEAD_DIM:(k0 + sub) * HEAD_DIM, :].astype(BF16)
        acc = _nt_dot(h, w)
        for k in range(sub):
            hh = j * hpb + k0 + k
            is_q = (hh < H_KA) | ((hh >= H_QN) & (hh < H_KC))
            rope = (is_q | (hh < H_VA) | ((hh >= H_KS) & (hh < H_VS))
                    | ((hh >= H_KW) & (hh < H_VW)))
            x = acc[:, k * HEAD_DIM:(k + 1) * HEAD_DIM]
            xr = x * cos + pltpu.roll(x, HEAD_DIM // 2, 1) * sin
            y = jnp.where(rope, xr, x) * jnp.where(is_q, scale, 1.0).astype(F32)
            o_ref[k0 + k] = y.astype(BF16)


def _in_proj(hb, w, cosf, sinf, scale, tm):
    s, d = hb.shape
    hpb = HEADS_PER_PROJ_BLOCK
    return pl.pallas_call(
        functools.partial(_in_proj_kernel, scale=scale),
        grid=(s // tm, N_PROJ_HEADS // hpb),
        in_specs=[
            pl.BlockSpec((tm, d), lambda i, j: (i, 0)),
            pl.BlockSpec((hpb * HEAD_DIM, d), lambda i, j: (j, 0)),
            pl.BlockSpec((tm, HEAD_DIM), lambda i, j: (i, 0)),
            pl.BlockSpec((tm, HEAD_DIM), lambda i, j: (i, 0)),
        ],
        out_specs=pl.BlockSpec((hpb, tm, HEAD_DIM), lambda i, j: (j, i, 0)),
        out_shape=jax.ShapeDtypeStruct((N_PROJ_HEADS, s, HEAD_DIM), BF16),
        compiler_params=_params(("arbitrary", "arbitrary")),
        name="in_proj",
    )(hb, w, cosf, sinf)


def _gates_kernel(h_ref, w_ref, b_ref, o_ref):
    z = _nt_dot(h_ref[...], w_ref[...]) + b_ref[...]
    o_ref[...] = jax.nn.sigmoid(z)


def _gates(hb, wg, bg, tm):
    s, d = hb.shape
    n = wg.shape[0]
    return pl.pallas_call(
        _gates_kernel,
        grid=(s // tm,),
        in_specs=[pl.BlockSpec((tm, d), lambda i: (i, 0)),
                  pl.BlockSpec((n, d), lambda i: (0, 0)),
                  pl.BlockSpec((1, n), lambda i: (0, 0))],
        out_specs=pl.BlockSpec((tm, n), lambda i: (i, 0)),
        out_shape=jax.ShapeDtypeStruct((s, n), F32),
        compiler_params=_params(("arbitrary",)),
        name="nsa_gates",
    )(hb, wg, bg)


def _compress_kernel(tok_ref, pe_ref, w1_ref, b1_ref, w2_ref, b2_ref, cos_ref, sin_ref,
                     o_ref, ot_ref):
    j = pl.program_id(0)
    half = CMP_STRIDE * HEAD_DIM
    tok = tok_ref[0].astype(F32)
    pe = pe_ref[0]
    w1 = w1_ref[0]
    top = (tok + pe[:, :half]).astype(BF16)
    bot = (tok + pe[:, half:]).astype(BF16)
    u = jnp.dot(top, w1[:half], preferred_element_type=F32)
    v = jnp.dot(bot, w1[half:], preferred_element_type=F32)
    nc = u.shape[0]
    hid = u + pltpu.roll(v, nc - 1, 0) + b1_ref[0]
    hid = jax.nn.gelu(hid)
    out = jnp.dot(hid.astype(BF16), w2_ref[0], preferred_element_type=F32) + b2_ref[0]
    roped = out * cos_ref[...] + pltpu.roll(out, HEAD_DIM // 2, 1) * sin_ref[...]
    out = jnp.where(j == 0, roped, out)
    o_ref[0, 0] = out.astype(BF16)
    ot_ref[0, 0] = out.T.astype(BF16)


def _compress(tok16, pe, w1, b1, w2, b2, cos_c, sin_c):
    nc = tok16.shape[1]
    blk = CMP_BLOCK * HEAD_DIM
    g = NSA_KV_GROUPS
    return pl.pallas_call(
        _compress_kernel,
        grid=(2, g),
        in_specs=[
            pl.BlockSpec((1, nc, CMP_STRIDE * HEAD_DIM), lambda j, gi: (g * j + gi, 0, 0)),
            pl.BlockSpec((1, 1, blk), lambda j, gi: (j, 0, 0)),
            pl.BlockSpec((1, blk, CMP_HIDDEN), lambda j, gi: (j, 0, 0)),
            pl.BlockSpec((1, 1, CMP_HIDDEN), lambda j, gi: (j, 0, 0)),
            pl.BlockSpec((1, CMP_HIDDEN, HEAD_DIM), lambda j, gi: (j, 0, 0)),
            pl.BlockSpec((1, 1, HEAD_DIM), lambda j, gi: (j, 0, 0)),
            pl.BlockSpec((nc, HEAD_DIM), lambda j, gi: (0, 0)),
            pl.BlockSpec((nc, HEAD_DIM), lambda j, gi: (0, 0)),
        ],
        out_specs=[pl.BlockSpec((1, 1, nc, HEAD_DIM), lambda j, gi: (j, gi, 0, 0)),
                   pl.BlockSpec((1, 1, HEAD_DIM, nc), lambda j, gi: (j, gi, 0, 0))],
        out_shape=[jax.ShapeDtypeStruct((2, g, nc, HEAD_DIM), BF16),
                   jax.ShapeDtypeStruct((2, g, HEAD_DIM, nc), BF16)],
        compiler_params=_params(("arbitrary", "arbitrary")),
        name="nsa_compress",
    )(tok16, pe, w1, b1, w2, b2, cos_c, sin_c)


def _dil_kernel(q_ref, k_ref, v_ref, o_ref, kpad, vpad, bias_ref, s_a, s_b, *, tq, padk):
    w = padk + tq
    n_tiles = q_ref.shape[1] // tq

    @pl.when(pl.program_id(0) == 0)
    def _():
        r = lax.broadcasted_iota(jnp.int32, (tq, w), 0)
        c = lax.broadcasted_iota(jnp.int32, (tq, w), 1)
        d = r + padk - c
        cnt = jnp.zeros((tq, w), F32)
        for window, dil in DIL_PAIRS:
            hit = (d >= 0) & (d <= window) & ((d & (dil - 1)) == 0)
            cnt = cnt + jnp.where(hit, 1.0, 0.0)
        bias_ref[...] = jnp.where(cnt > 0.0, jnp.log2(jnp.maximum(cnt, 1.0)), NEG)

    kpad[0:padk, :] = jnp.zeros((padk, HEAD_DIM), BF16)
    vpad[0:padk, :] = jnp.zeros((padk, HEAD_DIM), BF16)
    kpad[padk:, :] = k_ref[0]
    vpad[padk:, :] = v_ref[0]
    col = lax.broadcasted_iota(jnp.int32, (1, w), 1)

    def scores(t):
        q0 = pl.multiple_of(t * tq, tq)
        s = _nt_dot(q_ref[0, pl.ds(q0, tq), :], kpad[pl.ds(q0, w), :]) + bias_ref[...]
        return jnp.where(col >= padk - q0, s, NEG)

    def finish(s_ref, t):
        q0 = pl.multiple_of(t * tq, tq)
        s = s_ref[...]
        p = jnp.exp2(s - jnp.max(s, axis=-1, keepdims=True))
        l = jnp.sum(p, axis=-1, keepdims=True)
        o = jnp.dot(p.astype(BF16), vpad[pl.ds(q0, w), :], preferred_element_type=F32)
        o_ref[pl.ds(q0, tq), :] = (o / l).astype(BF16)

    def pair(pi, _):
        t = 2 * pi
        s_b[...] = scores(t + 1)
        finish(s_a, t)
        s_a[...] = scores(t + 2)
        finish(s_b, t + 1)
        return 0

    s_a[...] = scores(0)
    lax.fori_loop(0, n_tiles // 2 - 1, pair, 0)
    s_b[...] = scores(n_tiles - 1)
    finish(s_a, n_tiles - 2)
    finish(s_b, n_tiles - 1)


def _dilated(heads, tq):
    _, s, _ = heads.shape
    padk = max(wd for wd, _ in DIL_PAIRS)
    assert (s // tq) % 2 == 0
    return pl.pallas_call(
        functools.partial(_dil_kernel, tq=tq, padk=padk),
        grid=(N_HEADS_DIL,),
        in_specs=[
            pl.BlockSpec((1, s, HEAD_DIM), lambda h: (H_QA + h, 0, 0)),
            pl.BlockSpec((1, s, HEAD_DIM), lambda h: (H_KA + h, 0, 0)),
            pl.BlockSpec((1, s, HEAD_DIM), lambda h: (H_VA + h, 0, 0)),
        ],
        out_specs=pl.BlockSpec((s, HEAD_DIM), lambda h: (0, h)),
        out_shape=jax.ShapeDtypeStruct((s, D_DIL), BF16),
        scratch_shapes=[pltpu.VMEM((padk + s, HEAD_DIM), BF16),
                        pltpu.VMEM((padk + s, HEAD_DIM), BF16),
                        pltpu.VMEM((tq, padk + tq), F32),
                        pltpu.VMEM((tq, padk + tq), F32),
                        pltpu.VMEM((tq, padk + tq), F32)],
        compiler_params=_params(("arbitrary",)),
        name="dilated_attn",
    )(heads, heads, heads)


def _add_per_head(s, bias, hg):
    tq = bias.shape[0]
    return jnp.concatenate([s[u * tq:(u + 1) * tq] + bias for u in range(hg)], axis=0)


def _nsa_select_kernel(*refs, nsp, tq, k_sel):
    q_refs = refs[:NSA_GROUP]
    kc_ref, vct_ref, gate_ref, ovt_ref, cmp_ref, sel_ref = refs[NSA_GROUP:]
    t0 = pl.program_id(1) * tq
    hg = NSA_GROUP
    ncp = kc_ref.shape[2]
    gate = gate_ref[...]

    kc = kc_ref[0, 0]
    vct = vct_ref[0, 0]
    n_io = lax.broadcasted_iota(jnp.int32, (ncp, tq), 0)
    t_io = t0 + lax.broadcasted_iota(jnp.int32, (ncp, tq), 1)
    cmask = (n_io * CMP_STRIDE + (CMP_BLOCK - 1) <= t_io) & (n_io < ncp - 1)
    ovt = ovt_ref[...]
    imp = jnp.zeros((nsp, tq), F32)
    for u in range(hg):
        st = jnp.where(cmask, _nt_dot(kc, q_refs[u][0]), NEG)
        m = jnp.maximum(jnp.max(st, axis=0, keepdims=True), 0.1 * NEG)
        e = jnp.exp2(st - m)
        r = 1.0 / jnp.maximum(jnp.sum(e, axis=0, keepdims=True), 1e-30)
        eb = e.astype(BF16)
        imp = imp + jnp.dot(ovt, eb, preferred_element_type=F32) * r
        o_cmp = (jnp.dot(vct, eb, preferred_element_type=F32) * r).T
        c = N_NSA_BRANCHES * u
        cmp_ref[:, u * HEAD_DIM:(u + 1) * HEAD_DIM] = gate[:, c:c + 1] * o_cmp

    j_io = lax.broadcasted_iota(jnp.int32, (nsp, tq), 0)
    t_sel = t0 + lax.broadcasted_iota(jnp.int32, (nsp, tq), 1)
    valid = j_io * SEL_BLOCK <= t_sel
    cur = t_sel // SEL_BLOCK
    forced = (j_io == 0) | (j_io == cur) | (j_io == cur - 1)
    score = jnp.where(valid & jnp.logical_not(forced), imp, NEG)
    picked = jnp.where(forced, 1.0, 0.0)
    for _ in range(k_sel - 3):
        m = jnp.max(score, axis=0, keepdims=True)
        first = jnp.min(jnp.where(score == m, j_io, nsp), axis=0, keepdims=True)
        hit = j_io == first
        picked = jnp.where(hit, 1.0, picked)
        score = jnp.where(hit, -jnp.inf, score)
    sel_t = jnp.where(valid, picked, 0.0)
    sel_ref[0] = jnp.where(sel_t.T > 0.0, 0.0, SEL_OFF).astype(BF16)


def _nsa_select(heads, kcv, vct, gates, ovt, tq):
    _, s, _ = heads.shape
    ncp = kcv.shape[2]
    nsp = ovt.shape[0]
    hg = NSA_GROUP
    k_sel = min(N_SELECT, s // SEL_BLOCK)
    assert k_sel >= 3
    q_map = lambda u, g, qb: (H_QN + hg * g + u, qb, 0)
    return pl.pallas_call(
        functools.partial(_nsa_select_kernel, nsp=nsp, tq=tq, k_sel=k_sel),
        grid=(NSA_KV_GROUPS, s // tq),
        in_specs=[
            *[pl.BlockSpec((1, tq, HEAD_DIM), functools.partial(q_map, u)) for u in range(hg)],
            pl.BlockSpec((1, 1, ncp, HEAD_DIM), lambda g, qb: (0, g, 0, 0)),
            pl.BlockSpec((1, 1, HEAD_DIM, ncp), lambda g, qb: (1, g, 0, 0)),
            pl.BlockSpec((tq, HEAD_DIM), lambda g, qb: (qb, g)),
            pl.BlockSpec((nsp, ncp), lambda g, qb: (0, 0)),
        ],
        out_specs=[pl.BlockSpec((tq, hg * HEAD_DIM), lambda g, qb: (qb, g)),
                   pl.BlockSpec((1, tq, nsp), lambda g, qb: (g, qb, 0))],
        out_shape=[jax.ShapeDtypeStruct((s, D_NSA), F32),
                   jax.ShapeDtypeStruct((NSA_KV_GROUPS, s, nsp), BF16)],
        compiler_params=_params(("arbitrary", "arbitrary")),
        name="nsa_select",
    )(*([heads] * hg), kcv, vct, gates, ovt)


def _nsa_kernel(*refs, s_len, nsp, tq, kt):
    q_refs = refs[:NSA_GROUP]
    (ks_ref, vs_ref, kw_ref, vw_ref, gate_ref, cmp_ref, sel_ref,
     o_ref, kaug, vaug, vwaug, cbias, wbias, s_a, s_b, m_ref, acc_ref,
     part_ref) = refs[NSA_GROUP:]
    g = pl.program_id(0)
    qb = pl.program_id(1)
    t0 = qb * tq
    hg = NSA_GROUP
    rows = hg * tq
    ww = NSA_WINDOW + tq

    @pl.when((g == 0) & (qb == 0))
    def _():
        for r in range(kt // tq):
            d = (lax.broadcasted_iota(jnp.int32, (tq, kt), 0) + r * tq
                 - lax.broadcasted_iota(jnp.int32, (tq, kt), 1))
            cbias[r] = jnp.where(d >= 0, 0.0, NEG)
        for w in range(NSA_WINDOW // tq + 1):
            d = (lax.broadcasted_iota(jnp.int32, (tq, ww), 0) + w * tq
                 - lax.broadcasted_iota(jnp.int32, (tq, ww), 1))
            wbias[w] = jnp.where((d >= 0) & (d < NSA_WINDOW), 0.0, NEG)

    @pl.when(qb == 0)
    def _():
        kaug[:, 0:HEAD_DIM] = ks_ref[0]
        key = lax.broadcasted_iota(jnp.int32, (s_len, nsp), 0)
        blk = lax.broadcasted_iota(jnp.int32, (s_len, nsp), 1)
        kaug[:, HEAD_DIM:] = jnp.where(blk == key // SEL_BLOCK, 1.0, 0.0).astype(BF16)
        ones = jnp.ones((s_len, HEAD_DIM), BF16)
        vaug[:, 0:HEAD_DIM] = vs_ref[0]
        vaug[:, HEAD_DIM:] = ones
        vwaug[:, 0:HEAD_DIM] = vw_ref[0]
        vwaug[:, HEAD_DIM:] = ones

    gate = gate_ref[...]
    q5 = jnp.concatenate([r[0] for r in q_refs], axis=0)

    qaug = jnp.concatenate([q5, jnp.concatenate([sel_ref[0]] * hg, axis=0)], axis=1)

    w0 = pl.multiple_of(jnp.maximum(t0 - NSA_WINDOW, 0), tq)
    sw = _nt_dot(q5, kw_ref[0, pl.ds(w0, ww), :])
    sw = _add_per_head(sw, wbias[jnp.minimum(qb, NSA_WINDOW // tq)], hg)
    pw = jnp.exp2(sw - jnp.max(sw, axis=-1, keepdims=True))
    acc_w = jnp.dot(pw.astype(BF16), vwaug[pl.ds(w0, ww), :], preferred_element_type=F32)
    o_win = acc_w[:, :HEAD_DIM] / acc_w[:, HEAD_DIM:]
    for u in range(hg):
        c = N_NSA_BRANCHES * u + 2
        cols = slice(u * HEAD_DIM, (u + 1) * HEAD_DIM)
        part_ref[:, cols] = cmp_ref[:, cols] + gate[:, c:c + 1] * o_win[u * tq:(u + 1) * tq]

    def sel_scores(kti):
        k0 = pl.multiple_of(kti * kt, kt)
        return _nt_dot(qaug, kaug[pl.ds(k0, kt), :])

    def sel_update(s_ref, kti, diagonal):
        s = s_ref[...]
        if diagonal:
            s = _add_per_head(s, cbias[qb % (kt // tq)], hg)
        k0 = pl.multiple_of(kti * kt, kt)
        m_i = m_ref[...]
        m_new = jnp.maximum(m_i, jnp.max(s, axis=-1, keepdims=True))
        p = jnp.exp2(s - m_new)
        pv = jnp.dot(p.astype(BF16), vaug[pl.ds(k0, kt), :], preferred_element_type=F32)
        acc_ref[...] = jnp.exp2(m_i - m_new) * acc_ref[...] + pv
        m_ref[...] = m_new

    def sel_pair(pi, _):
        t = 2 * pi
        s_b[...] = sel_scores(t + 1)
        sel_update(s_a, t, False)
        s_a[...] = sel_scores(t + 2)
        sel_update(s_b, t + 1, False)
        return 0

    last = (t0 + tq + kt - 1) // kt - 1
    m_ref[...] = jnp.full((rows, 1), NEG, F32)
    acc_ref[...] = jnp.zeros((rows, 2 * HEAD_DIM), F32)
    s_a[...] = sel_scores(0)
    lax.fori_loop(0, last // 2, sel_pair, 0)
    t_even = 2 * (last // 2)

    @pl.when(last % 2 == 1)
    def _():
        s_b[...] = sel_scores(t_even + 1)
        sel_update(s_a, t_even, False)
        sel_update(s_b, t_even + 1, True)

    @pl.when(last % 2 == 0)
    def _():
        sel_update(s_a, t_even, True)

    gate = gate_ref[...]
    for u in range(hg):
        c = N_NSA_BRANCHES * u + 1
        acc_u = acc_ref[u * tq:(u + 1) * tq, :]
        o_slc = acc_u[:, :HEAD_DIM] / acc_u[:, HEAD_DIM:]
        cols = slice(u * HEAD_DIM, (u + 1) * HEAD_DIM)
        o_ref[:, cols] = (part_ref[:, cols] + gate[:, c:c + 1] * o_slc).astype(BF16)


def _nsa(heads, gates, part_cmp, sel_off, tq, kt):
    _, s, _ = heads.shape
    nsp = sel_off.shape[2]
    hg = NSA_GROUP
    assert kt % tq == 0 and NSA_WINDOW % tq == 0 and s % kt == 0
    q_map = lambda u, g, qb: (H_QN + hg * g + u, qb, 0)
    full = lambda hbase: pl.BlockSpec((1, s, HEAD_DIM), lambda g, qb: (hbase + g, 0, 0),
                                      pipeline_mode=pl.Buffered(1))
    return pl.pallas_call(
        functools.partial(_nsa_kernel, s_len=s, nsp=nsp, tq=tq, kt=kt),
        grid=(NSA_KV_GROUPS, s // tq),
        in_specs=[
            *[pl.BlockSpec((1, tq, HEAD_DIM), functools.partial(q_map, u)) for u in range(hg)],
            full(H_KS), full(H_VS), full(H_KW), full(H_VW),
            pl.BlockSpec((tq, HEAD_DIM), lambda g, qb: (qb, g)),
            pl.BlockSpec((tq, hg * HEAD_DIM), lambda g, qb: (qb, g)),
            pl.BlockSpec((1, tq, nsp), lambda g, qb: (g, qb, 0)),
        ],
        out_specs=pl.BlockSpec((tq, hg * HEAD_DIM), lambda g, qb: (qb, g)),
        out_shape=jax.ShapeDtypeStruct((s, D_NSA), BF16),
        scratch_shapes=[pltpu.VMEM((s, HEAD_DIM + nsp), BF16),
                        pltpu.VMEM((s, 2 * HEAD_DIM), BF16),
                        pltpu.VMEM((s, 2 * HEAD_DIM), BF16),
                        pltpu.VMEM((kt // tq, tq, kt), F32),
                        pltpu.VMEM((NSA_WINDOW // tq + 1, tq, NSA_WINDOW + tq), F32),
                        pltpu.VMEM((hg * tq, kt), F32),
                        pltpu.VMEM((hg * tq, kt), F32),
                        pltpu.VMEM((hg * tq, 1), F32),
                        pltpu.VMEM((hg * tq, 2 * HEAD_DIM), F32),
                        pltpu.VMEM((tq, hg * HEAD_DIM), F32)],
        compiler_params=_params(("arbitrary", "arbitrary")),
        name="nsa_attn",
    )(*([heads] * hg), heads, heads, heads, heads, gates, part_cmp, sel_off)


def _out_proj_ln_kernel(a_ref, b_ref, wa_ref, wb_ref, h_ref, g_ref, be_ref, o_ref, *, alpha):
    tm = o_ref.shape[0]
    rc = tm // ROW_CHUNKS
    for c in range(ROW_CHUNKS):
        r = slice(c * rc, (c + 1) * rc)
        mix = (jnp.dot(a_ref[r, :], wa_ref[...], preferred_element_type=F32)
               + jnp.dot(b_ref[r, :], wb_ref[...], preferred_element_type=F32))
        o_ref[r, :] = _layer_norm(alpha * h_ref[r, :] + mix, g_ref[...], be_ref[...])


def _out_proj_ln(mix_a, mix_b, wa, wb, h, g, b, alpha, tm):
    s, d = h.shape
    return pl.pallas_call(
        functools.partial(_out_proj_ln_kernel, alpha=alpha),
        grid=(s // tm,),
        in_specs=[
            pl.BlockSpec((tm, D_DIL), lambda i: (i, 0)),
            pl.BlockSpec((tm, D_NSA), lambda i: (i, 0)),
            pl.BlockSpec((D_DIL, d), lambda i: (0, 0)),
            pl.BlockSpec((D_NSA, d), lambda i: (0, 0)),
            pl.BlockSpec((tm, d), lambda i: (i, 0)),
            pl.BlockSpec((1, d), lambda i: (0, 0)),
            pl.BlockSpec((1, d), lambda i: (0, 0)),
        ],
        out_specs=pl.BlockSpec((tm, d), lambda i: (i, 0)),
        out_shape=jax.ShapeDtypeStruct((s, d), F32),
        compiler_params=_params(("arbitrary",)),
        name="out_proj_ln",
    )(mix_a, mix_b, wa, wb, h, g, b)


def _overlap_t(s):
    n_cmp = (s - CMP_BLOCK) // CMP_STRIDE + 1
    n_sel = s // SEL_BLOCK
    ncp = s // CMP_STRIDE
    nsp = -(-n_sel // HEAD_DIM) * HEAD_DIM
    c_lo = np.arange(n_cmp) * CMP_STRIDE
    c_hi = c_lo + CMP_BLOCK - 1
    s_lo = (np.arange(n_sel) * SEL_BLOCK)[:, None]
    ov = np.zeros((nsp, ncp), np.float32)
    ov[:n_sel, :n_cmp] = (c_lo[None, :] <= s_lo + SEL_BLOCK - 1) & (c_hi[None, :] >= s_lo)
    return jnp.asarray(ov, BF16)


def kernel(x, positions, ln1_g, ln1_b, ffn1_w1, ffn1_w3, ffn1_w2, w_in, gate_b, cmp_pe, cmp_w1,
           cmp_b1, cmp_w2, cmp_b2, w_out, ln2_g, ln2_b, ffn2_w1, ffn2_w3, ffn2_w2, ln3_g, ln3_b):
    bsz, s, d = x.shape
    assert bsz == 1 and d == (N_HEADS_DIL + N_HEADS_NSA) * HEAD_DIM
    assert s % 512 == 0 and s >= NSA_WINDOW + QBLK
    alpha = (2.0 * DEPTH) ** 0.25
    scale = HEAD_DIM ** -0.5 * LOG2_E
    row = lambda v: v.reshape(1, -1)
    tm = 512
    ffn_tm, ffn_tf = 1024, 256

    inv_freq = ROPE_THETA ** (-jnp.arange(0, HEAD_DIM, 2, dtype=F32) / HEAD_DIM)
    invf = jnp.concatenate([inv_freq, inv_freq]).reshape(1, HEAD_DIM)
    pos = positions[0].astype(F32).reshape(s, 1)
    cosf, sinf = _rope_tables(pos, pos, invf, 512)
    ncp = s // CMP_STRIDE
    n_cmp = (s - CMP_BLOCK) // CMP_STRIDE + 1
    pos_lo = jnp.pad(pos[0:n_cmp * CMP_STRIDE:CMP_STRIDE], ((0, ncp - n_cmp), (0, 0)))
    pos_hi = jnp.pad(pos[CMP_BLOCK - 1::CMP_STRIDE][:n_cmp], ((0, ncp - n_cmp), (0, 0)))
    cos_c, sin_c = _rope_tables(pos_lo, pos_hi, invf, ncp)

    g_off = 3 * D_DIL + D_NSA + 6 * KV_W
    per_g = NSA_GROUP * N_NSA_BRANCHES
    w_in_t = w_in[0].T
    wg = jnp.zeros((NSA_KV_GROUPS * HEAD_DIM, d), F32)
    bg = jnp.zeros((1, NSA_KV_GROUPS * HEAD_DIM), F32)
    for g in range(NSA_KV_GROUPS):
        wg = wg.at[g * HEAD_DIM:g * HEAD_DIM + per_g].set(
            w_in_t[g_off + g * per_g:g_off + (g + 1) * per_g])
        bg = bg.at[0, g * HEAD_DIM:g * HEAD_DIM + per_g].set(gate_b[0][g * per_g:(g + 1) * per_g])
    wg = wg.astype(BF16)

    h0 = x[0]
    h1, h1b = _ffn_ln(h0, ffn1_w1[0], ffn1_w3[0], ffn1_w2[0], row(ln1_g[0]), row(ln1_b[0]),
                      alpha, ffn_tm, ffn_tf, True)

    heads = _in_proj(h1b, w_in_t, cosf, sinf, scale, 1024)
    gates = _gates(h1b, wg, bg, tm)
    tok16 = heads[H_KC:H_KS].reshape(2 * NSA_KV_GROUPS, ncp, CMP_STRIDE * HEAD_DIM)
    kcv, kcv_t = _compress(
        tok16, cmp_pe[0].reshape(2, 1, CMP_BLOCK * HEAD_DIM), cmp_w1[0].astype(BF16),
        cmp_b1[0].reshape(2, 1, CMP_HIDDEN), cmp_w2[0].astype(BF16),
        cmp_b2[0].reshape(2, 1, HEAD_DIM), cos_c, sin_c)
    mix_a = _dilated(heads, 256)
    part_cmp, sel_off = _nsa_select(heads, kcv, kcv_t, gates, _overlap_t(s), 512)
    mix_b = _nsa(heads, gates, part_cmp, sel_off, 256, 1024)

    wo = w_out[0].astype(BF16)
    h2 = _out_proj_ln(mix_a, mix_b, wo[:D_DIL], wo[D_DIL:], h1, row(ln2_g[0]), row(ln2_b[0]),
                      alpha, tm)
    (h3,) = _ffn_ln(h2, ffn2_w1[0], ffn2_w3[0], ffn2_w2[0], row(ln3_g[0]), row(ln3_b[0]),
                    alpha, ffn_tm, ffn_tf, False)
    return h3.reshape(bsz, s, d)
```

```python
import functools

import jax
import jax.numpy as jnp
import numpy as np
from jax import lax
from jax.experimental import pallas as pl
from jax.experimental.pallas import tpu as pltpu

HEAD_DIM = 128
N_HEADS_DIL = 6
N_HEADS_NSA = 10
NSA_KV_GROUPS = 2
NSA_GROUP = N_HEADS_NSA // NSA_KV_GROUPS
N_NSA_BRANCHES = 3
DIL_PAIRS = ((128, 1), (512, 4), (2048, 16))
CMP_BLOCK = 32
CMP_STRIDE = 16
CMP_HIDDEN = 256
SEL_BLOCK = 64
N_SELECT = 16
NSA_WINDOW = 512
ROPE_THETA = 10000.0
QBLK = 128
LN_EPS = 1e-5
NEG = -1e30
FORCE_BONUS = 1e4
SEL_OFF = -1e9
LOG2_E = 1.4426950408889634
DEPTH = 1

D_DIL = N_HEADS_DIL * HEAD_DIM
D_NSA = N_HEADS_NSA * HEAD_DIM
KV_W = NSA_KV_GROUPS * HEAD_DIM
N_GATES = N_HEADS_NSA * N_NSA_BRANCHES

H_QA, H_KA, H_VA, H_QN = 0, 6, 12, 18
H_KC, H_VC, H_KS, H_VS, H_KW, H_VW = 28, 30, 32, 34, 36, 38
N_PROJ_HEADS = 40
HEADS_PER_PROJ_BLOCK = 8
PROJ_SUB_HEADS = 4
ROW_CHUNKS = 2
CMP_ROW_CHUNK = 128

VMEM_LIMIT_BYTES = 56 * 1024 * 1024
FFN_VMEM_LIMIT_BYTES = 60 * 1024 * 1024

F32 = jnp.float32
BF16 = jnp.bfloat16


def _params(sem, vmem=VMEM_LIMIT_BYTES):
    return pltpu.CompilerParams(dimension_semantics=sem, vmem_limit_bytes=vmem)


def _nt_dot(a, b):
    return lax.dot_general(a, b, (((1,), (1,)), ((), ())), preferred_element_type=F32)


def _layer_norm(y, g, b):
    mu = jnp.mean(y, axis=-1, keepdims=True)
    yc = y - mu
    var = jnp.mean(yc * yc, axis=-1, keepdims=True)
    return yc * lax.rsqrt(var + LN_EPS) * g + b


def _rope_table_kernel(pa_ref, pb_ref, invf_ref, cos_ref, sin_ref):
    pos = (pa_ref[...] + pb_ref[...]) * 0.5
    ang = pos * invf_ref[...]
    lane = lax.broadcasted_iota(jnp.int32, ang.shape, 1)
    cos_ref[...] = jnp.cos(ang)
    sin_ref[...] = jnp.where(lane < HEAD_DIM // 2, -1.0, 1.0) * jnp.sin(ang)


def _rope_tables(pos_a, pos_b, invf, tile):
    n = pos_a.shape[0]
    spec_p = pl.BlockSpec((tile, 1), lambda i: (i, 0))
    spec_t = pl.BlockSpec((tile, HEAD_DIM), lambda i: (i, 0))
    return pl.pallas_call(
        _rope_table_kernel,
        grid=(n // tile,),
        in_specs=[spec_p, spec_p, pl.BlockSpec((1, HEAD_DIM), lambda i: (0, 0))],
        out_specs=[spec_t, spec_t],
        out_shape=[jax.ShapeDtypeStruct((n, HEAD_DIM), F32)] * 2,
        compiler_params=_params(("arbitrary",)),
        name="rope_tables",
    )(pos_a, pos_b, invf)


def _ffn_ln_kernel(h_ref, w1_ref, w3_ref, w2_ref, g_ref, b_ref, o_ref, hb_ref, *, alpha, nf):
    f = pl.program_id(1)

    @pl.when(f == 0)
    def _():
        hb_ref[...] = h_ref[...].astype(BF16)
        o_ref[...] = jnp.zeros_like(o_ref)

    hb = hb_ref[...]
    a = jnp.dot(hb, w1_ref[...].astype(BF16), preferred_element_type=F32)
    b = jnp.dot(hb, w3_ref[...].astype(BF16), preferred_element_type=F32)
    act = (a * jax.nn.sigmoid(a)) * b
    o_ref[...] += jnp.dot(act.astype(BF16), w2_ref[...].astype(BF16),
                          preferred_element_type=F32)

    @pl.when(f == nf - 1)
    def _():
        y = alpha * h_ref[...] + 0.5 * o_ref[...]
        o_ref[...] = _layer_norm(y, g_ref[...], b_ref[...])


def _ffn_ln(h, w1, w3, w2, g, b, alpha, tm, tf):
    s, d = h.shape
    dff = w1.shape[1]
    nf = dff // tf
    return pl.pallas_call(
        functools.partial(_ffn_ln_kernel, alpha=alpha, nf=nf),
        grid=(s // tm, nf),
        in_specs=[
            pl.BlockSpec((tm, d), lambda i, f: (i, 0)),
            pl.BlockSpec((d, tf), lambda i, f: (0, f)),
            pl.BlockSpec((d, tf), lambda i, f: (0, f)),
            pl.BlockSpec((tf, d), lambda i, f: (f, 0)),
            pl.BlockSpec((1, d), lambda i, f: (0, 0)),
            pl.BlockSpec((1, d), lambda i, f: (0, 0)),
        ],
        out_specs=pl.BlockSpec((tm, d), lambda i, f: (i, 0)),
        out_shape=jax.ShapeDtypeStruct((s, d), F32),
        scratch_shapes=[pltpu.VMEM((tm, d), BF16)],
        compiler_params=_params(("arbitrary", "arbitrary"), FFN_VMEM_LIMIT_BYTES),
        name="ffn_ln",
    )(h, w1, w3, w2, g, b)


def _in_proj_kernel(h_ref, w_ref, cos_ref, sin_ref, wg_ref, bg_ref, o_ref, gate_ref, hb_ref,
                    *, scale):
    j = pl.program_id(1)
    hpb = HEADS_PER_PROJ_BLOCK
    sub = PROJ_SUB_HEADS

    @pl.when(j == 0)
    def _():
        hb_ref[...] = h_ref[...].astype(BF16)
        gate_ref[...] = jax.nn.sigmoid(_nt_dot(hb_ref[...], wg_ref[...]) + bg_ref[...])

    cos = cos_ref[...]
    sin = sin_ref[...]
    h = hb_ref[...]
    for k0 in range(0, hpb, sub):
        w = w_ref[k0 * HEAD_DIM:(k0 + sub) * HEAD_DIM, :].astype(BF16)
        acc = _nt_dot(h, w)
        for k in range(sub):
            hh = j * hpb + k0 + k
            is_q = (hh < H_KA) | ((hh >= H_QN) & (hh < H_KC))
            rope = (is_q | (hh < H_VA) | ((hh >= H_KS) & (hh < H_VS))
                    | ((hh >= H_KW) & (hh < H_VW)))
            x = acc[:, k * HEAD_DIM:(k + 1) * HEAD_DIM]
            xr = x * cos + pltpu.roll(x, HEAD_DIM // 2, 1) * sin
            y = jnp.where(rope, xr, x) * jnp.where(is_q, scale, 1.0).astype(F32)
            o_ref[k0 + k] = y.astype(BF16)


def _in_proj(h, w, cosf, sinf, wg, bg, scale, tm):
    s, d = h.shape
    n = wg.shape[0]
    hpb = HEADS_PER_PROJ_BLOCK
    return pl.pallas_call(
        functools.partial(_in_proj_kernel, scale=scale),
        grid=(s // tm, N_PROJ_HEADS // hpb),
        in_specs=[
            pl.BlockSpec((tm, d), lambda i, j: (i, 0)),
            pl.BlockSpec((hpb * HEAD_DIM, d), lambda i, j: (j, 0)),
            pl.BlockSpec((tm, HEAD_DIM), lambda i, j: (i, 0)),
            pl.BlockSpec((tm, HEAD_DIM), lambda i, j: (i, 0)),
            pl.BlockSpec((n, d), lambda i, j: (0, 0)),
            pl.BlockSpec((1, n), lambda i, j: (0, 0)),
        ],
        out_specs=[pl.BlockSpec((hpb, tm, HEAD_DIM), lambda i, j: (j, i, 0)),
                   pl.BlockSpec((tm, n), lambda i, j: (i, 0))],
        out_shape=[jax.ShapeDtypeStruct((N_PROJ_HEADS, s, HEAD_DIM), BF16),
                   jax.ShapeDtypeStruct((s, n), F32)],
        scratch_shapes=[pltpu.VMEM((tm, d), BF16)],
        compiler_params=_params(("arbitrary", "arbitrary")),
        name="in_proj",
    )(h, w, cosf, sinf, wg, bg)


def _compress_kernel(tok_ref, pe_ref, w1_ref, b1_ref, w2_ref, b2_ref, cos_ref, sin_ref,
                     o_ref, ot_ref):
    j = pl.program_id(0)
    half = CMP_STRIDE * HEAD_DIM
    tok = tok_ref[0].astype(F32)
    pe = pe_ref[0]
    w1 = w1_ref[0]
    top = (tok + pe[:, :half]).astype(BF16)
    bot = (tok + pe[:, half:]).astype(BF16)
    u = jnp.dot(top, w1[:half], preferred_element_type=F32)
    v = jnp.dot(bot, w1[half:], preferred_element_type=F32)
    nc = u.shape[0]
    hid = u + pltpu.roll(v, nc - 1, 0) + b1_ref[0]
    hid = jax.nn.gelu(hid)
    out = jnp.dot(hid.astype(BF16), w2_ref[0], preferred_element_type=F32) + b2_ref[0]
    roped = out * cos_ref[...] + pltpu.roll(out, HEAD_DIM // 2, 1) * sin_ref[...]
    out = jnp.where(j == 0, roped, out)
    o_ref[0, 0] = out.astype(BF16)
    ot_ref[0, 0] = out.T.astype(BF16)


def _compress(tok16, pe, w1, b1, w2, b2, cos_c, sin_c):
    nc = tok16.shape[1]
    blk = CMP_BLOCK * HEAD_DIM
    g = NSA_KV_GROUPS
    return pl.pallas_call(
        _compress_kernel,
        grid=(2, g),
        in_specs=[
            pl.BlockSpec((1, nc, CMP_STRIDE * HEAD_DIM), lambda j, gi: (g * j + gi, 0, 0)),
            pl.BlockSpec((1, 1, blk), lambda j, gi: (j, 0, 0)),
            pl.BlockSpec((1, blk, CMP_HIDDEN), lambda j, gi: (j, 0, 0)),
            pl.BlockSpec((1, 1, CMP_HIDDEN), lambda j, gi: (j, 0, 0)),
            pl.BlockSpec((1, CMP_HIDDEN, HEAD_DIM), lambda j, gi: (j, 0, 0)),
            pl.BlockSpec((1, 1, HEAD_DIM), lambda j, gi: (j, 0, 0)),
            pl.BlockSpec((nc, HEAD_DIM), lambda j, gi: (0, 0)),
            pl.BlockSpec((nc, HEAD_DIM), lambda j, gi: (0, 0)),
        ],
        out_specs=[pl.BlockSpec((1, 1, nc, HEAD_DIM), lambda j, gi: (j, gi, 0, 0)),
                   pl.BlockSpec((1, 1, HEAD_DIM, nc), lambda j, gi: (j, gi, 0, 0))],
        out_shape=[jax.ShapeDtypeStruct((2, g, nc, HEAD_DIM), BF16),
                   jax.ShapeDtypeStruct((2, g, HEAD_DIM, nc), BF16)],
        compiler_params=_params(("arbitrary", "arbitrary")),
        name="nsa_compress",
    )(tok16, pe, w1, b1, w2, b2, cos_c, sin_c)


def _dil_kernel(q_ref, k_ref, v_ref, o_ref, kpad, vpad, bias_ref, s_a, s_b, *, tq, padk):
    w = padk + tq
    n_tiles = q_ref.shape[1] // tq

    @pl.when(pl.program_id(0) == 0)
    def _():
        r = lax.broadcasted_iota(jnp.int32, (tq, w), 0)
        c = lax.broadcasted_iota(jnp.int32, (tq, w), 1)
        d = r + padk - c
        cnt = jnp.zeros((tq, w), F32)
        for window, dil in DIL_PAIRS:
            hit = (d >= 0) & (d <= window) & ((d & (dil - 1)) == 0)
            cnt = cnt + jnp.where(hit, 1.0, 0.0)
        bias_ref[...] = jnp.where(cnt > 0.0, jnp.log2(jnp.maximum(cnt, 1.0)), NEG)

    kpad[0:padk, :] = jnp.zeros((padk, HEAD_DIM), BF16)
    vpad[0:padk, :] = jnp.zeros((padk, HEAD_DIM), BF16)
    kpad[padk:, :] = k_ref[0]
    vpad[padk:, :] = v_ref[0]
    col = lax.broadcasted_iota(jnp.int32, (1, w), 1)

    def scores(t):
        q0 = pl.multiple_of(t * tq, tq)
        s = _nt_dot(q_ref[0, pl.ds(q0, tq), :], kpad[pl.ds(q0, w), :]) + bias_ref[...]
        return jnp.where(col >= padk - q0, s, NEG)

    def finish(s_ref, t):
        q0 = pl.multiple_of(t * tq, tq)
        s = s_ref[...]
        p = jnp.exp2(s - jnp.max(s, axis=-1, keepdims=True))
        l = jnp.sum(p, axis=-1, keepdims=True)
        o = jnp.dot(p.astype(BF16), vpad[pl.ds(q0, w), :], preferred_element_type=F32)
        o_ref[pl.ds(q0, tq), :] = (o / l).astype(BF16)

    def pair(pi, _):
        t = 2 * pi
        s_b[...] = scores(t + 1)
        finish(s_a, t)
        s_a[...] = scores(t + 2)
        finish(s_b, t + 1)
        return 0

    s_a[...] = scores(0)
    lax.fori_loop(0, n_tiles // 2 - 1, pair, 0)
    s_b[...] = scores(n_tiles - 1)
    finish(s_a, n_tiles - 2)
    finish(s_b, n_tiles - 1)


def _dilated(heads, tq):
    _, s, _ = heads.shape
    padk = max(wd for wd, _ in DIL_PAIRS)
    assert (s // tq) % 2 == 0
    return pl.pallas_call(
        functools.partial(_dil_kernel, tq=tq, padk=padk),
        grid=(N_HEADS_DIL,),
        in_specs=[
            pl.BlockSpec((1, s, HEAD_DIM), lambda h: (H_QA + h, 0, 0)),
            pl.BlockSpec((1, s, HEAD_DIM), lambda h: (H_KA + h, 0, 0)),
            pl.BlockSpec((1, s, HEAD_DIM), lambda h: (H_VA + h, 0, 0)),
        ],
        out_specs=pl.BlockSpec((s, HEAD_DIM), lambda h: (0, h)),
        out_shape=jax.ShapeDtypeStruct((s, D_DIL), BF16),
        scratch_shapes=[pltpu.VMEM((padk + s, HEAD_DIM), BF16),
                        pltpu.VMEM((padk + s, HEAD_DIM), BF16),
                        pltpu.VMEM((tq, padk + tq), F32),
                        pltpu.VMEM((tq, padk + tq), F32),
                        pltpu.VMEM((tq, padk + tq), F32)],
        compiler_params=_params(("arbitrary",)),
        name="dilated_attn",
    )(heads, heads, heads)


def _add_per_head(s, bias, hg):
    tq = bias.shape[0]
    return jnp.concatenate([s[u * tq:(u + 1) * tq] + bias for u in range(hg)], axis=0)


def _nsa_select_kernel(*refs, nsp, tq, k_sel):
    q_refs = refs[:NSA_GROUP]
    kc_ref, vct_ref, gate_ref, ovt_ref, cmp_ref, sel_ref, imp_ref = refs[NSA_GROUP:]
    t0 = pl.program_id(1) * tq
    hg = NSA_GROUP
    ncp = kc_ref.shape[2]
    gate = gate_ref[...]

    def cmp_branch(n_rows):
        kc = kc_ref[0, 0, :n_rows, :]
        vct = vct_ref[0, 0, :, :n_rows]
        ovt = ovt_ref[:, :n_rows]
        n_io = lax.broadcasted_iota(jnp.int32, (n_rows, tq), 0)
        t_io = t0 + lax.broadcasted_iota(jnp.int32, (n_rows, tq), 1)
        cmask = (n_io * CMP_STRIDE + (CMP_BLOCK - 1) <= t_io) & (n_io < ncp - 1)
        imp = jnp.zeros((nsp, tq), F32)
        for u in range(hg):
            st = jnp.where(cmask, _nt_dot(kc, q_refs[u][0]), NEG)
            m = jnp.maximum(jnp.max(st, axis=0, keepdims=True), 0.1 * NEG)
            e = jnp.exp2(st - m)
            r = 1.0 / jnp.maximum(jnp.sum(e, axis=0, keepdims=True), 1e-30)
            eb = e.astype(BF16)
            imp = imp + jnp.dot(ovt, eb, preferred_element_type=F32) * r
            o_cmp = (jnp.dot(vct, eb, preferred_element_type=F32) * r).T
            c = N_NSA_BRANCHES * u
            cmp_ref[:, u * HEAD_DIM:(u + 1) * HEAD_DIM] = gate[:, c:c + 1] * o_cmp
        imp_ref[...] = imp

    n_var = ncp // CMP_ROW_CHUNK
    need = (t0 + tq) // CMP_STRIDE
    var = jnp.minimum((need + CMP_ROW_CHUNK - 1) // CMP_ROW_CHUNK, n_var) - 1
    for v in range(n_var):
        pl.when(var == v)(functools.partial(cmp_branch, CMP_ROW_CHUNK * (v + 1)))
    imp = imp_ref[...]

    j_io = lax.broadcasted_iota(jnp.int32, (nsp, tq), 0)
    t_sel = t0 + lax.broadcasted_iota(jnp.int32, (nsp, tq), 1)
    valid = j_io * SEL_BLOCK <= t_sel
    cur = t_sel // SEL_BLOCK
    forced = (j_io == 0) | (j_io == cur) | (j_io == cur - 1)
    score = jnp.where(valid & jnp.logical_not(forced), imp, NEG)
    picked = jnp.where(forced, 1.0, 0.0)
    for _ in range(k_sel - 3):
        m = jnp.max(score, axis=0, keepdims=True)
        first = jnp.min(jnp.where(score == m, j_io, nsp), axis=0, keepdims=True)
        hit = j_io == first
        picked = jnp.where(hit, 1.0, picked)
        score = jnp.where(hit, -jnp.inf, score)
    sel_t = jnp.where(valid, picked, 0.0)
    sel_ref[0] = jnp.where(sel_t.T > 0.0, 0.0, SEL_OFF).astype(BF16)


def _nsa_select(heads, kcv, vct, gates, ovt, tq):
    _, s, _ = heads.shape
    ncp = kcv.shape[2]
    nsp = ovt.shape[0]
    hg = NSA_GROUP
    k_sel = min(N_SELECT, s // SEL_BLOCK)
    assert k_sel >= 3 and ncp % CMP_ROW_CHUNK == 0
    q_map = lambda u, g, qb: (H_QN + hg * g + u, qb, 0)
    return pl.pallas_call(
        functools.partial(_nsa_select_kernel, nsp=nsp, tq=tq, k_sel=k_sel),
        grid=(NSA_KV_GROUPS, s // tq),
        in_specs=[
            *[pl.BlockSpec((1, tq, HEAD_DIM), functools.partial(q_map, u)) for u in range(hg)],
            pl.BlockSpec((1, 1, ncp, HEAD_DIM), lambda g, qb: (0, g, 0, 0)),
            pl.BlockSpec((1, 1, HEAD_DIM, ncp), lambda g, qb: (1, g, 0, 0)),
            pl.BlockSpec((tq, HEAD_DIM), lambda g, qb: (qb, g)),
            pl.BlockSpec((nsp, ncp), lambda g, qb: (0, 0)),
        ],
        out_specs=[pl.BlockSpec((tq, hg * HEAD_DIM), lambda g, qb: (qb, g)),
                   pl.BlockSpec((1, tq, nsp), lambda g, qb: (g, qb, 0))],
        out_shape=[jax.ShapeDtypeStruct((s, D_NSA), F32),
                   jax.ShapeDtypeStruct((NSA_KV_GROUPS, s, nsp), BF16)],
        scratch_shapes=[pltpu.VMEM((nsp, tq), F32)],
        compiler_params=_params(("arbitrary", "arbitrary")),
        name="nsa_select",
    )(*([heads] * hg), kcv, vct, gates, ovt)


def _nsa_kernel(*refs, s_len, nsp, tq, kt):
    q_refs = refs[:NSA_GROUP]
    (ks_ref, vs_ref, kw_ref, vw_ref, gate_ref, cmp_ref, sel_ref,
     o_ref, kaug, vaug, vwaug, cbias, wbias, s_a, s_b, m_ref, acc_ref,
     part_ref) = refs[NSA_GROUP:]
    g = pl.program_id(0)
    qb = pl.program_id(1)
    t0 = qb * tq
    hg = NSA_GROUP
    rows = hg * tq
    ww = NSA_WINDOW + tq

    @pl.when((g == 0) & (qb == 0))
    def _():
        for r in range(kt // tq):
            d = (lax.broadcasted_iota(jnp.int32, (tq, kt), 0) + r * tq
                 - lax.broadcasted_iota(jnp.int32, (tq, kt), 1))
            cbias[r] = jnp.where(d >= 0, 0.0, NEG)
        for w in range(NSA_WINDOW // tq + 1):
            d = (lax.broadcasted_iota(jnp.int32, (tq, ww), 0) + w * tq
                 - lax.broadcasted_iota(jnp.int32, (tq, ww), 1))
            wbias[w] = jnp.where((d >= 0) & (d < NSA_WINDOW), 0.0, NEG)

    @pl.when(qb == 0)
    def _():
        kaug[:, 0:HEAD_DIM] = ks_ref[0]
        key = lax.broadcasted_iota(jnp.int32, (s_len, nsp), 0)
        blk = lax.broadcasted_iota(jnp.int32, (s_len, nsp), 1)
        kaug[:, HEAD_DIM:] = jnp.where(blk == key // SEL_BLOCK, 1.0, 0.0).astype(BF16)
        ones = jnp.ones((s_len, HEAD_DIM), BF16)
        vaug[:, 0:HEAD_DIM] = vs_ref[0]
        vaug[:, HEAD_DIM:] = ones
        vwaug[:, 0:HEAD_DIM] = vw_ref[0]
        vwaug[:, HEAD_DIM:] = ones

    gate = gate_ref[...]
    q5 = jnp.concatenate([r[0] for r in q_refs], axis=0)

    qaug = jnp.concatenate([q5, jnp.concatenate([sel_ref[0]] * hg, axis=0)], axis=1)

    w0 = pl.multiple_of(jnp.maximum(t0 - NSA_WINDOW, 0), tq)
    sw = _nt_dot(q5, kw_ref[0, pl.ds(w0, ww), :])
    sw = _add_per_head(sw, wbias[jnp.minimum(qb, NSA_WINDOW // tq)], hg)
    pw = jnp.exp2(sw - jnp.max(sw, axis=-1, keepdims=True))
    acc_w = jnp.dot(pw.astype(BF16), vwaug[pl.ds(w0, ww), :], preferred_element_type=F32)
    o_win = acc_w[:, :HEAD_DIM] / acc_w[:, HEAD_DIM:]
    for u in range(hg):
        c = N_NSA_BRANCHES * u + 2
        cols = slice(u * HEAD_DIM, (u + 1) * HEAD_DIM)
        part_ref[:, cols] = cmp_ref[:, cols] + gate[:, c:c + 1] * o_win[u * tq:(u + 1) * tq]

    def sel_scores(kti):
        k0 = pl.multiple_of(kti * kt, kt)
        return _nt_dot(qaug, kaug[pl.ds(k0, kt), :])

    def sel_update(s_ref, kti, diagonal):
        s = s_ref[...]
        if diagonal:
            s = _add_per_head(s, cbias[qb % (kt // tq)], hg)
        k0 = pl.multiple_of(kti * kt, kt)
        m_i = m_ref[...]
        m_new = jnp.maximum(m_i, jnp.max(s, axis=-1, keepdims=True))
        p = jnp.exp2(s - m_new)
        pv = jnp.dot(p.astype(BF16), vaug[pl.ds(k0, kt), :], preferred_element_type=F32)
        acc_ref[...] = jnp.exp2(m_i - m_new) * acc_ref[...] + pv
        m_ref[...] = m_new

    def sel_pair(pi, _):
        t = 2 * pi
        s_b[...] = sel_scores(t + 1)
        sel_update(s_a, t, False)
        s_a[...] = sel_scores(t + 2)
        sel_update(s_b, t + 1, False)
        return 0

    last = (t0 + tq + kt - 1) // kt - 1
    m_ref[...] = jnp.full((rows, 1), NEG, F32)
    acc_ref[...] = jnp.zeros((rows, 2 * HEAD_DIM), F32)
    s_a[...] = sel_scores(0)
    lax.fori_loop(0, last // 2, sel_pair, 0)
    t_even = 2 * (last // 2)

    @pl.when(last % 2 == 1)
    def _():
        s_b[...] = sel_scores(t_even + 1)
        sel_update(s_a, t_even, False)
        sel_update(s_b, t_even + 1, True)

    @pl.when(last % 2 == 0)
    def _():
        sel_update(s_a, t_even, True)

    gate = gate_ref[...]
    for u in range(hg):
        c = N_NSA_BRANCHES * u + 1
        acc_u = acc_ref[u * tq:(u + 1) * tq, :]
        o_slc = acc_u[:, :HEAD_DIM] / acc_u[:, HEAD_DIM:]
        cols = slice(u * HEAD_DIM, (u + 1) * HEAD_DIM)
        o_ref[:, cols] = (part_ref[:, cols] + gate[:, c:c + 1] * o_slc).astype(BF16)


def _nsa(heads, gates, part_cmp, sel_off, tq, kt):
    _, s, _ = heads.shape
    nsp = sel_off.shape[2]
    hg = NSA_GROUP
    assert kt % tq == 0 and NSA_WINDOW % tq == 0 and s % kt == 0
    q_map = lambda u, g, qb: (H_QN + hg * g + u, qb, 0)
    full = lambda hbase: pl.BlockSpec((1, s, HEAD_DIM), lambda g, qb: (hbase + g, 0, 0),
                                      pipeline_mode=pl.Buffered(1))
    return pl.pallas_call(
        functools.partial(_nsa_kernel, s_len=s, nsp=nsp, tq=tq, kt=kt),
        grid=(NSA_KV_GROUPS, s // tq),
        in_specs=[
            *[pl.BlockSpec((1, tq, HEAD_DIM), functools.partial(q_map, u)) for u in range(hg)],
            full(H_KS), full(H_VS), full(H_KW), full(H_VW),
            pl.BlockSpec((tq, HEAD_DIM), lambda g, qb: (qb, g)),
            pl.BlockSpec((tq, hg * HEAD_DIM), lambda g, qb: (qb, g)),
            pl.BlockSpec((1, tq, nsp), lambda g, qb: (g, qb, 0)),
        ],
        out_specs=pl.BlockSpec((tq, hg * HEAD_DIM), lambda g, qb: (qb, g)),
        out_shape=jax.ShapeDtypeStruct((s, D_NSA), BF16),
        scratch_shapes=[pltpu.VMEM((s, HEAD_DIM + nsp), BF16),
                        pltpu.VMEM((s, 2 * HEAD_DIM), BF16),
                        pltpu.VMEM((s, 2 * HEAD_DIM), BF16),
                        pltpu.VMEM((kt // tq, tq, kt), F32),
                        pltpu.VMEM((NSA_WINDOW // tq + 1, tq, NSA_WINDOW + tq), F32),
                        pltpu.VMEM((hg * tq, kt), F32),
                        pltpu.VMEM((hg * tq, kt), F32),
                        pltpu.VMEM((hg * tq, 1), F32),
                        pltpu.VMEM((hg * tq, 2 * HEAD_DIM), F32),
                        pltpu.VMEM((tq, hg * HEAD_DIM), F32)],
        compiler_params=_params(("arbitrary", "arbitrary")),
        name="nsa_attn",
    )(*([heads] * hg), heads, heads, heads, heads, gates, part_cmp, sel_off)


def _out_proj_ln_kernel(a_ref, b_ref, wa_ref, wb_ref, h_ref, g_ref, be_ref, o_ref, *, alpha):
    tm = o_ref.shape[0]
    rc = tm // ROW_CHUNKS
    for c in range(ROW_CHUNKS):
        r = slice(c * rc, (c + 1) * rc)
        mix = (jnp.dot(a_ref[r, :], wa_ref[...], preferred_element_type=F32)
               + jnp.dot(b_ref[r, :], wb_ref[...], preferred_element_type=F32))
        o_ref[r, :] = _layer_norm(alpha * h_ref[r, :] + mix, g_ref[...], be_ref[...])


def _out_proj_ln(mix_a, mix_b, wa, wb, h, g, b, alpha, tm):
    s, d = h.shape
    return pl.pallas_call(
        functools.partial(_out_proj_ln_kernel, alpha=alpha),
        grid=(s // tm,),
        in_specs=[
            pl.BlockSpec((tm, D_DIL), lambda i: (i, 0)),
            pl.BlockSpec((tm, D_NSA), lambda i: (i, 0)),
            pl.BlockSpec((D_DIL, d), lambda i: (0, 0)),
            pl.BlockSpec((D_NSA, d), lambda i: (0, 0)),
            pl.BlockSpec((tm, d), lambda i: (i, 0)),
            pl.BlockSpec((1, d), lambda i: (0, 0)),
            pl.BlockSpec((1, d), lambda i: (0, 0)),
        ],
        out_specs=pl.BlockSpec((tm, d), lambda i: (i, 0)),
        out_shape=jax.ShapeDtypeStruct((s, d), F32),
        compiler_params=_params(("arbitrary",)),
        name="out_proj_ln",
    )(mix_a, mix_b, wa, wb, h, g, b)


def _overlap_t(s):
    n_cmp = (s - CMP_BLOCK) // CMP_STRIDE + 1
    n_sel = s // SEL_BLOCK
    ncp = s // CMP_STRIDE
    nsp = -(-n_sel // HEAD_DIM) * HEAD_DIM
    c_lo = np.arange(n_cmp) * CMP_STRIDE
    c_hi = c_lo + CMP_BLOCK - 1
    s_lo = (np.arange(n_sel) * SEL_BLOCK)[:, None]
    ov = np.zeros((nsp, ncp), np.float32)
    ov[:n_sel, :n_cmp] = (c_lo[None, :] <= s_lo + SEL_BLOCK - 1) & (c_hi[None, :] >= s_lo)
    return jnp.asarray(ov, BF16)


def kernel(x, positions, ln1_g, ln1_b, ffn1_w1, ffn1_w3, ffn1_w2, w_in, gate_b, cmp_pe, cmp_w1,
           cmp_b1, cmp_w2, cmp_b2, w_out, ln2_g, ln2_b, ffn2_w1, ffn2_w3, ffn2_w2, ln3_g, ln3_b):
    bsz, s, d = x.shape
    assert bsz == 1 and d == (N_HEADS_DIL + N_HEADS_NSA) * HEAD_DIM
    assert s % 512 == 0 and s >= NSA_WINDOW + QBLK
    alpha = (2.0 * DEPTH) ** 0.25
    scale = HEAD_DIM ** -0.5 * LOG2_E
    row = lambda v: v.reshape(1, -1)
    tm = 512
    ffn_tm, ffn_tf = 1024, 256

    inv_freq = ROPE_THETA ** (-jnp.arange(0, HEAD_DIM, 2, dtype=F32) / HEAD_DIM)
    invf = jnp.concatenate([inv_freq, inv_freq]).reshape(1, HEAD_DIM)
    pos = positions[0].astype(F32).reshape(s, 1)
    cosf, sinf = _rope_tables(pos, pos, invf, 512)
    ncp = s // CMP_STRIDE
    n_cmp = (s - CMP_BLOCK) // CMP_STRIDE + 1
    pos_lo = jnp.pad(pos[0:n_cmp * CMP_STRIDE:CMP_STRIDE], ((0, ncp - n_cmp), (0, 0)))
    pos_hi = jnp.pad(pos[CMP_BLOCK - 1::CMP_STRIDE][:n_cmp], ((0, ncp - n_cmp), (0, 0)))
    cos_c, sin_c = _rope_tables(pos_lo, pos_hi, invf, ncp)

    g_off = 3 * D_DIL + D_NSA + 6 * KV_W
    per_g = NSA_GROUP * N_NSA_BRANCHES
    w_in_t = w_in[0].T
    wg = jnp.zeros((NSA_KV_GROUPS * HEAD_DIM, d), F32)
    bg = jnp.zeros((1, NSA_KV_GROUPS * HEAD_DIM), F32)
    for g in range(NSA_KV_GROUPS):
        wg = wg.at[g * HEAD_DIM:g * HEAD_DIM + per_g].set(
            w_in_t[g_off + g * per_g:g_off + (g + 1) * per_g])
        bg = bg.at[0, g * HEAD_DIM:g * HEAD_DIM + per_g].set(gate_b[0][g * per_g:(g + 1) * per_g])
    wg = wg.astype(BF16)

    h0 = x[0]
    h1 = _ffn_ln(h0, ffn1_w1[0], ffn1_w3[0], ffn1_w2[0], row(ln1_g[0]), row(ln1_b[0]),
                 alpha, ffn_tm, ffn_tf)

    heads, gates = _in_proj(h1, w_in_t, cosf, sinf, wg, bg, scale, 1024)
    tok16 = heads[H_KC:H_KS].reshape(2 * NSA_KV_GROUPS, ncp, CMP_STRIDE * HEAD_DIM)
    kcv, kcv_t = _compress(
        tok16, cmp_pe[0].reshape(2, 1, CMP_BLOCK * HEAD_DIM), cmp_w1[0].astype(BF16),
        cmp_b1[0].reshape(2, 1, CMP_HIDDEN), cmp_w2[0].astype(BF16),
        cmp_b2[0].reshape(2, 1, HEAD_DIM), cos_c, sin_c)
    mix_a = _dilated(heads, 256)
    part_cmp, sel_off = _nsa_select(heads, kcv, kcv_t, gates, _overlap_t(s), 512)
    mix_b = _nsa(heads, gates, part_cmp, sel_off, 256, 1024)

    wo = w_out[0].astype(BF16)
    h2 = _out_proj_ln(mix_a, mix_b, wo[:D_DIL], wo[D_DIL:], h1, row(ln2_g[0]), row(ln2_b[0]),
                      alpha, tm)
    h3 = _ffn_ln(h2, ffn2_w1[0], ffn2_w3[0], ffn2_w2[0], row(ln3_g[0]), row(ln3_b[0]),
                 alpha, ffn_tm, ffn_tf)
    return h3.reshape(bsz, s, d)
```

```python
import functools

import jax
import jax.numpy as jnp
import numpy as np
from jax import lax
from jax.experimental import pallas as pl
from jax.experimental.pallas import tpu as pltpu

HEAD_DIM = 128
N_HEADS_DIL = 6
N_HEADS_NSA = 10
NSA_KV_GROUPS = 2
NSA_GROUP = N_HEADS_NSA // NSA_KV_GROUPS
N_NSA_BRANCHES = 3
DIL_PAIRS = ((128, 1), (512, 4), (2048, 16))
CMP_BLOCK = 32
CMP_STRIDE = 16
CMP_HIDDEN = 256
SEL_BLOCK = 64
N_SELECT = 16
NSA_WINDOW = 512
ROPE_THETA = 10000.0
QBLK = 128
LN_EPS = 1e-5
NEG = -1e30
FORCE_BONUS = 1e4
SEL_OFF = -1e9
LOG2_E = 1.4426950408889634
DEPTH = 1

D_DIL = N_HEADS_DIL * HEAD_DIM
D_NSA = N_HEADS_NSA * HEAD_DIM
KV_W = NSA_KV_GROUPS * HEAD_DIM
N_GATES = N_HEADS_NSA * N_NSA_BRANCHES

H_QA, H_KA, H_VA, H_QN = 0, 6, 12, 18
H_KC, H_VC, H_KS, H_VS, H_KW, H_VW = 28, 30, 32, 34, 36, 38
N_PROJ_HEADS = 40
HEADS_PER_PROJ_BLOCK = 8
PROJ_SUB_HEADS = 4
ROW_CHUNKS = 2
CMP_ROW_CHUNK = 128

VMEM_LIMIT_BYTES = 56 * 1024 * 1024
FFN_VMEM_LIMIT_BYTES = 60 * 1024 * 1024

F32 = jnp.float32
BF16 = jnp.bfloat16


def _params(sem, vmem=VMEM_LIMIT_BYTES):
    return pltpu.CompilerParams(dimension_semantics=sem, vmem_limit_bytes=vmem)


def _nt_dot(a, b):
    return lax.dot_general(a, b, (((1,), (1,)), ((), ())), preferred_element_type=F32)


def _layer_norm(y, g, b):
    mu = jnp.mean(y, axis=-1, keepdims=True)
    yc = y - mu
    var = jnp.mean(yc * yc, axis=-1, keepdims=True)
    return yc * lax.rsqrt(var + LN_EPS) * g + b


def _rope_table_kernel(pa_ref, pb_ref, invf_ref, cos_ref, sin_ref):
    pos = (pa_ref[...] + pb_ref[...]) * 0.5
    ang = pos * invf_ref[...]
    lane = lax.broadcasted_iota(jnp.int32, ang.shape, 1)
    cos_ref[...] = jnp.cos(ang)
    sin_ref[...] = jnp.where(lane < HEAD_DIM // 2, -1.0, 1.0) * jnp.sin(ang)


def _rope_tables(pos_a, pos_b, invf, tile):
    n = pos_a.shape[0]
    spec_p = pl.BlockSpec((tile, 1), lambda i: (i, 0))
    spec_t = pl.BlockSpec((tile, HEAD_DIM), lambda i: (i, 0))
    return pl.pallas_call(
        _rope_table_kernel,
        grid=(n // tile,),
        in_specs=[spec_p, spec_p, pl.BlockSpec((1, HEAD_DIM), lambda i: (0, 0))],
        out_specs=[spec_t, spec_t],
        out_shape=[jax.ShapeDtypeStruct((n, HEAD_DIM), F32)] * 2,
        compiler_params=_params(("arbitrary",)),
        name="rope_tables",
    )(pos_a, pos_b, invf)


def _ffn_ln_kernel(h_ref, w1_ref, w3_ref, w2_ref, g_ref, b_ref, o_ref, hb_ref, *, alpha, nf):
    f = pl.program_id(1)

    @pl.when(f == 0)
    def _():
        h = h_ref[...]
        hb_ref[...] = h.astype(BF16)
        o_ref[...] = (2.0 * alpha) * h

    hb = hb_ref[...]
    a = jnp.dot(hb, w1_ref[...].astype(BF16), preferred_element_type=F32)
    b = jnp.dot(hb, w3_ref[...].astype(BF16), preferred_element_type=F32)
    act = (a * jax.nn.sigmoid(a)) * b
    o_ref[...] += jnp.dot(act.astype(BF16), w2_ref[...].astype(BF16),
                          preferred_element_type=F32)

    @pl.when(f == nf - 1)
    def _():
        o_ref[...] = _layer_norm(0.5 * o_ref[...], g_ref[...], b_ref[...])


def _ffn_ln(h, w1, w3, w2, g, b, alpha, tm, tf):
    s, d = h.shape
    dff = w1.shape[1]
    nf = dff // tf
    return pl.pallas_call(
        functools.partial(_ffn_ln_kernel, alpha=alpha, nf=nf),
        grid=(s // tm, nf),
        in_specs=[
            pl.BlockSpec((tm, d), lambda i, f: (i, 0)),
            pl.BlockSpec((d, tf), lambda i, f: (0, f)),
            pl.BlockSpec((d, tf), lambda i, f: (0, f)),
            pl.BlockSpec((tf, d), lambda i, f: (f, 0)),
            pl.BlockSpec((1, d), lambda i, f: (0, 0)),
            pl.BlockSpec((1, d), lambda i, f: (0, 0)),
        ],
        out_specs=pl.BlockSpec((tm, d), lambda i, f: (i, 0)),
        out_shape=jax.ShapeDtypeStruct((s, d), F32),
        scratch_shapes=[pltpu.VMEM((tm, d), BF16)],
        compiler_params=_params(("arbitrary", "arbitrary"), FFN_VMEM_LIMIT_BYTES),
        name="ffn_ln",
    )(h, w1, w3, w2, g, b)


def _in_proj_kernel(h_ref, w_ref, cos_ref, sin_ref, wg_ref, bg_ref, o_ref, gate_ref, hb_ref,
                    *, scale):
    j = pl.program_id(1)
    hpb = HEADS_PER_PROJ_BLOCK
    sub = PROJ_SUB_HEADS

    @pl.when(j == 0)
    def _():
        hb_ref[...] = h_ref[...].astype(BF16)
        gate_ref[...] = jax.nn.sigmoid(_nt_dot(hb_ref[...], wg_ref[...]) + bg_ref[...])

    cos = cos_ref[...]
    sin = sin_ref[...]
    h = hb_ref[...]
    for k0 in range(0, hpb, sub):
        w = w_ref[k0 * HEAD_DIM:(k0 + sub) * HEAD_DIM, :].astype(BF16)
        acc = _nt_dot(h, w)
        for k in range(sub):
            hh = j * hpb + k0 + k
            is_q = (hh < H_KA) | ((hh >= H_QN) & (hh < H_KC))
            rope = (is_q | (hh < H_VA) | ((hh >= H_KS) & (hh < H_VS))
                    | ((hh >= H_KW) & (hh < H_VW)))
            x = acc[:, k * HEAD_DIM:(k + 1) * HEAD_DIM]
            xr = x * cos + pltpu.roll(x, HEAD_DIM // 2, 1) * sin
            y = jnp.where(rope, xr, x) * jnp.where(is_q, scale, 1.0).astype(F32)
            o_ref[k0 + k] = y.astype(BF16)


def _in_proj(h, w, cosf, sinf, wg, bg, scale, tm):
    s, d = h.shape
    n = wg.shape[0]
    hpb = HEADS_PER_PROJ_BLOCK
    return pl.pallas_call(
        functools.partial(_in_proj_kernel, scale=scale),
        grid=(s // tm, N_PROJ_HEADS // hpb),
        in_specs=[
            pl.BlockSpec((tm, d), lambda i, j: (i, 0)),
            pl.BlockSpec((hpb * HEAD_DIM, d), lambda i, j: (j, 0)),
            pl.BlockSpec((tm, HEAD_DIM), lambda i, j: (i, 0)),
            pl.BlockSpec((tm, HEAD_DIM), lambda i, j: (i, 0)),
            pl.BlockSpec((n, d), lambda i, j: (0, 0)),
            pl.BlockSpec((1, n), lambda i, j: (0, 0)),
        ],
        out_specs=[pl.BlockSpec((hpb, tm, HEAD_DIM), lambda i, j: (j, i, 0)),
                   pl.BlockSpec((tm, n), lambda i, j: (i, 0))],
        out_shape=[jax.ShapeDtypeStruct((N_PROJ_HEADS, s, HEAD_DIM), BF16),
                   jax.ShapeDtypeStruct((s, n), F32)],
        scratch_shapes=[pltpu.VMEM((tm, d), BF16)],
        compiler_params=_params(("arbitrary", "arbitrary")),
        name="in_proj",
    )(h, w, cosf, sinf, wg, bg)


def _compress_kernel(tok_ref, pe_ref, w1_ref, b1_ref, w2_ref, b2_ref, cos_ref, sin_ref,
                     o_ref, ot_ref):
    j = pl.program_id(0)
    half = CMP_STRIDE * HEAD_DIM
    tok = tok_ref[0].astype(F32)
    pe = pe_ref[0]
    w1 = w1_ref[0]
    top = (tok + pe[:, :half]).astype(BF16)
    bot = (tok + pe[:, half:]).astype(BF16)
    u = jnp.dot(top, w1[:half], preferred_element_type=F32)
    v = jnp.dot(bot, w1[half:], preferred_element_type=F32)
    nc = u.shape[0]
    hid = u + pltpu.roll(v, nc - 1, 0) + b1_ref[0]
    hid = jax.nn.gelu(hid)
    out = jnp.dot(hid.astype(BF16), w2_ref[0], preferred_element_type=F32) + b2_ref[0]
    roped = out * cos_ref[...] + pltpu.roll(out, HEAD_DIM // 2, 1) * sin_ref[...]
    out = jnp.where(j == 0, roped, out)
    o_ref[0, 0] = out.astype(BF16)
    ot_ref[0, 0] = out.T.astype(BF16)


def _compress(tok16, pe, w1, b1, w2, b2, cos_c, sin_c):
    nc = tok16.shape[1]
    blk = CMP_BLOCK * HEAD_DIM
    g = NSA_KV_GROUPS
    return pl.pallas_call(
        _compress_kernel,
        grid=(2, g),
        in_specs=[
            pl.BlockSpec((1, nc, CMP_STRIDE * HEAD_DIM), lambda j, gi: (g * j + gi, 0, 0)),
            pl.BlockSpec((1, 1, blk), lambda j, gi: (j, 0, 0)),
            pl.BlockSpec((1, blk, CMP_HIDDEN), lambda j, gi: (j, 0, 0)),
            pl.BlockSpec((1, 1, CMP_HIDDEN), lambda j, gi: (j, 0, 0)),
            pl.BlockSpec((1, CMP_HIDDEN, HEAD_DIM), lambda j, gi: (j, 0, 0)),
            pl.BlockSpec((1, 1, HEAD_DIM), lambda j, gi: (j, 0, 0)),
            pl.BlockSpec((nc, HEAD_DIM), lambda j, gi: (0, 0)),
            pl.BlockSpec((nc, HEAD_DIM), lambda j, gi: (0, 0)),
        ],
        out_specs=[pl.BlockSpec((1, 1, nc, HEAD_DIM), lambda j, gi: (j, gi, 0, 0)),
                   pl.BlockSpec((1, 1, HEAD_DIM, nc), lambda j, gi: (j, gi, 0, 0))],
        out_shape=[jax.ShapeDtypeStruct((2, g, nc, HEAD_DIM), BF16),
                   jax.ShapeDtypeStruct((2, g, HEAD_DIM, nc), BF16)],
        compiler_params=_params(("arbitrary", "arbitrary")),
        name="nsa_compress",
    )(tok16, pe, w1, b1, w2, b2, cos_c, sin_c)


def _dil_kernel(q_ref, k_ref, v_ref, o_ref, kpad, vpad, bias_ref, s_a, s_b, *, tq, padk):
    w = padk + tq
    n_tiles = q_ref.shape[1] // tq

    @pl.when(pl.program_id(0) == 0)
    def _():
        r = lax.broadcasted_iota(jnp.int32, (tq, w), 0)
        c = lax.broadcasted_iota(jnp.int32, (tq, w), 1)
        d = r + padk - c
        cnt = jnp.zeros((tq, w), F32)
        for window, dil in DIL_PAIRS:
            hit = (d >= 0) & (d <= window) & ((d & (dil - 1)) == 0)
            cnt = cnt + jnp.where(hit, 1.0, 0.0)
        bias_ref[...] = jnp.where(cnt > 0.0, jnp.log2(jnp.maximum(cnt, 1.0)), NEG)

    kpad[0:padk, :] = jnp.zeros((padk, HEAD_DIM), BF16)
    vpad[0:padk, :] = jnp.zeros((padk, HEAD_DIM), BF16)
    kpad[padk:, :] = k_ref[0]
    vpad[padk:, :] = v_ref[0]
    col = lax.broadcasted_iota(jnp.int32, (1, w), 1)

    def scores(t):
        q0 = pl.multiple_of(t * tq, tq)
        s = _nt_dot(q_ref[0, pl.ds(q0, tq), :], kpad[pl.ds(q0, w), :]) + bias_ref[...]
        return jnp.where(col >= padk - q0, s, NEG)

    def finish(s_ref, t):
        q0 = pl.multiple_of(t * tq, tq)
        s = s_ref[...]
        p = jnp.exp2(s - jnp.max(s, axis=-1, keepdims=True))
        l = jnp.sum(p, axis=-1, keepdims=True)
        o = jnp.dot(p.astype(BF16), vpad[pl.ds(q0, w), :], preferred_element_type=F32)
        o_ref[pl.ds(q0, tq), :] = (o / l).astype(BF16)

    def pair(pi, _):
        t = 2 * pi
        s_b[...] = scores(t + 1)
        finish(s_a, t)
        s_a[...] = scores(t + 2)
        finish(s_b, t + 1)
        return 0

    s_a[...] = scores(0)
    lax.fori_loop(0, n_tiles // 2 - 1, pair, 0)
    s_b[...] = scores(n_tiles - 1)
    finish(s_a, n_tiles - 2)
    finish(s_b, n_tiles - 1)


def _dilated(heads, tq):
    _, s, _ = heads.shape
    padk = max(wd for wd, _ in DIL_PAIRS)
    assert (s // tq) % 2 == 0
    return pl.pallas_call(
        functools.partial(_dil_kernel, tq=tq, padk=padk),
        grid=(N_HEADS_DIL,),
        in_specs=[
            pl.BlockSpec((1, s, HEAD_DIM), lambda h: (H_QA + h, 0, 0)),
            pl.BlockSpec((1, s, HEAD_DIM), lambda h: (H_KA + h, 0, 0)),
            pl.BlockSpec((1, s, HEAD_DIM), lambda h: (H_VA + h, 0, 0)),
        ],
        out_specs=pl.BlockSpec((s, HEAD_DIM), lambda h: (0, h)),
        out_shape=jax.ShapeDtypeStruct((s, D_DIL), BF16),
        scratch_shapes=[pltpu.VMEM((padk + s, HEAD_DIM), BF16),
                        pltpu.VMEM((padk + s, HEAD_DIM), BF16),
                        pltpu.VMEM((tq, padk + tq), F32),
                        pltpu.VMEM((tq, padk + tq), F32),
                        pltpu.VMEM((tq, padk + tq), F32)],
        compiler_params=_params(("arbitrary",)),
        name="dilated_attn",
    )(heads, heads, heads)


def _add_per_head(s, bias, hg):
    tq = bias.shape[0]
    return jnp.concatenate([s[u * tq:(u + 1) * tq] + bias for u in range(hg)], axis=0)


def _nsa_select_kernel(*refs, nsp, tq, k_sel):
    q_refs = refs[:NSA_GROUP]
    kc_ref, vct_ref, gate_ref, ovt_ref, cmp_ref, sel_ref, imp_ref = refs[NSA_GROUP:]
    t0 = pl.program_id(1) * tq
    hg = NSA_GROUP
    ncp = kc_ref.shape[2]
    gate = gate_ref[...]

    def cmp_branch(n_rows):
        kc = kc_ref[0, 0, :n_rows, :]
        vct = vct_ref[0, 0, :, :n_rows]
        ovt = ovt_ref[:, :n_rows]
        n_io = lax.broadcasted_iota(jnp.int32, (n_rows, tq), 0)
        t_io = t0 + lax.broadcasted_iota(jnp.int32, (n_rows, tq), 1)
        cmask = (n_io * CMP_STRIDE + (CMP_BLOCK - 1) <= t_io) & (n_io < ncp - 1)
        imp = jnp.zeros((nsp, tq), F32)
        for u in range(hg):
            st = jnp.where(cmask, _nt_dot(kc, q_refs[u][0]), NEG)
            m = jnp.maximum(jnp.max(st, axis=0, keepdims=True), 0.1 * NEG)
            e = jnp.exp2(st - m)
            r = 1.0 / jnp.maximum(jnp.sum(e, axis=0, keepdims=True), 1e-30)
            eb = e.astype(BF16)
            imp = imp + jnp.dot(ovt, eb, preferred_element_type=F32) * r
            o_cmp = (jnp.dot(vct, eb, preferred_element_type=F32) * r).T
            c = N_NSA_BRANCHES * u
            cmp_ref[:, u * HEAD_DIM:(u + 1) * HEAD_DIM] = gate[:, c:c + 1] * o_cmp
        imp_ref[...] = imp

    n_var = ncp // CMP_ROW_CHUNK
    need = (t0 + tq) // CMP_STRIDE
    var = jnp.minimum((need + CMP_ROW_CHUNK - 1) // CMP_ROW_CHUNK, n_var) - 1
    for v in range(n_var):
        pl.when(var == v)(functools.partial(cmp_branch, CMP_ROW_CHUNK * (v + 1)))
    imp = imp_ref[...]

    j_io = lax.broadcasted_iota(jnp.int32, (nsp, tq), 0)
    t_sel = t0 + lax.broadcasted_iota(jnp.int32, (nsp, tq), 1)
    valid = j_io * SEL_BLOCK <= t_sel
    cur = t_sel // SEL_BLOCK
    forced = (j_io == 0) | (j_io == cur) | (j_io == cur - 1)
    score = jnp.where(valid & jnp.logical_not(forced), imp, NEG)
    picked = jnp.where(forced, 1.0, 0.0)
    for _ in range(k_sel - 3):
        m = jnp.max(score, axis=0, keepdims=True)
        first = jnp.min(jnp.where(score == m, j_io, nsp), axis=0, keepdims=True)
        hit = j_io == first
        picked = jnp.where(hit, 1.0, picked)
        score = jnp.where(hit, -jnp.inf, score)
    sel_t = jnp.where(valid, picked, 0.0)
    sel_ref[0] = jnp.where(sel_t.T > 0.0, 0.0, SEL_OFF).astype(BF16)


def _nsa_select(heads, kcv, vct, gates, ovt, tq):
    _, s, _ = heads.shape
    ncp = kcv.shape[2]
    nsp = ovt.shape[0]
    hg = NSA_GROUP
    k_sel = min(N_SELECT, s // SEL_BLOCK)
    assert k_sel >= 3 and ncp % CMP_ROW_CHUNK == 0
    q_map = lambda u, g, qb: (H_QN + hg * g + u, qb, 0)
    return pl.pallas_call(
        functools.partial(_nsa_select_kernel, nsp=nsp, tq=tq, k_sel=k_sel),
        grid=(NSA_KV_GROUPS, s // tq),
        in_specs=[
            *[pl.BlockSpec((1, tq, HEAD_DIM), functools.partial(q_map, u)) for u in range(hg)],
            pl.BlockSpec((1, 1, ncp, HEAD_DIM), lambda g, qb: (0, g, 0, 0)),
            pl.BlockSpec((1, 1, HEAD_DIM, ncp), lambda g, qb: (1, g, 0, 0)),
            pl.BlockSpec((tq, HEAD_DIM), lambda g, qb: (qb, g)),
            pl.BlockSpec((nsp, ncp), lambda g, qb: (0, 0)),
        ],
        out_specs=[pl.BlockSpec((tq, hg * HEAD_DIM), lambda g, qb: (qb, g)),
                   pl.BlockSpec((1, tq, nsp), lambda g, qb: (g, qb, 0))],
        out_shape=[jax.ShapeDtypeStruct((s, D_NSA), F32),
                   jax.ShapeDtypeStruct((NSA_KV_GROUPS, s, nsp), BF16)],
        scratch_shapes=[pltpu.VMEM((nsp, tq), F32)],
        compiler_params=_params(("arbitrary", "arbitrary")),
        name="nsa_select",
    )(*([heads] * hg), kcv, vct, gates, ovt)


def _nsa_kernel(*refs, s_len, nsp, tq, kt):
    q_refs = refs[:NSA_GROUP]
    (ks_ref, vs_ref, kw_ref, vw_ref, gate_ref, cmp_ref, sel_ref,
     o_ref, kaug, vaug, vwaug, cbias, wbias, s_a, s_b, m_ref, acc_ref,
     part_ref) = refs[NSA_GROUP:]
    g = pl.program_id(0)
    qb = pl.program_id(1)
    t0 = qb * tq
    hg = NSA_GROUP
    rows = hg * tq
    ww = NSA_WINDOW + tq

    @pl.when((g == 0) & (qb == 0))
    def _():
        for r in range(kt // tq):
            d = (lax.broadcasted_iota(jnp.int32, (tq, kt), 0) + r * tq
                 - lax.broadcasted_iota(jnp.int32, (tq, kt), 1))
            cbias[r] = jnp.where(d >= 0, 0.0, NEG)
        for w in range(NSA_WINDOW // tq + 1):
            d = (lax.broadcasted_iota(jnp.int32, (tq, ww), 0) + w * tq
                 - lax.broadcasted_iota(jnp.int32, (tq, ww), 1))
            wbias[w] = jnp.where((d >= 0) & (d < NSA_WINDOW), 0.0, NEG)

    @pl.when(qb == 0)
    def _():
        kaug[:, 0:HEAD_DIM] = ks_ref[0]
        key = lax.broadcasted_iota(jnp.int32, (s_len, nsp), 0)
        blk = lax.broadcasted_iota(jnp.int32, (s_len, nsp), 1)
        kaug[:, HEAD_DIM:] = jnp.where(blk == key // SEL_BLOCK, 1.0, 0.0).astype(BF16)
        ones = jnp.ones((s_len, HEAD_DIM), BF16)
        vaug[:, 0:HEAD_DIM] = vs_ref[0]
        vaug[:, HEAD_DIM:] = ones
        vwaug[:, 0:HEAD_DIM] = vw_ref[0]
        vwaug[:, HEAD_DIM:] = ones

    gate = gate_ref[...]
    q5 = jnp.concatenate([r[0] for r in q_refs], axis=0)

    qaug = jnp.concatenate([q5, jnp.concatenate([sel_ref[0]] * hg, axis=0)], axis=1)

    w0 = pl.multiple_of(jnp.maximum(t0 - NSA_WINDOW, 0), tq)
    sw = _nt_dot(q5, kw_ref[0, pl.ds(w0, ww), :])
    sw = _add_per_head(sw, wbias[jnp.minimum(qb, NSA_WINDOW // tq)], hg)
    pw = jnp.exp2((sw - jnp.max(sw, axis=-1, keepdims=True)).astype(BF16))
    acc_w = jnp.dot(pw, vwaug[pl.ds(w0, ww), :], preferred_element_type=F32)
    o_win = acc_w[:, :HEAD_DIM] / acc_w[:, HEAD_DIM:]
    for u in range(hg):
        c = N_NSA_BRANCHES * u + 2
        cols = slice(u * HEAD_DIM, (u + 1) * HEAD_DIM)
        part_ref[:, cols] = cmp_ref[:, cols] + gate[:, c:c + 1] * o_win[u * tq:(u + 1) * tq]

    def sel_scores(kti):
        k0 = pl.multiple_of(kti * kt, kt)
        return _nt_dot(qaug, kaug[pl.ds(k0, kt), :])

    def sel_update(s_ref, kti, diagonal):
        s = s_ref[...]
        if diagonal:
            s = _add_per_head(s, cbias[qb % (kt // tq)], hg)
        k0 = pl.multiple_of(kti * kt, kt)
        m_i = m_ref[...]
        m_new = jnp.maximum(m_i, jnp.max(s, axis=-1, keepdims=True))
        p = jnp.exp2((s - m_new).astype(BF16))
        pv = jnp.dot(p, vaug[pl.ds(k0, kt), :], preferred_element_type=F32)
        acc_ref[...] = jnp.exp2(m_i - m_new) * acc_ref[...] + pv
        m_ref[...] = m_new

    def sel_pair(pi, _):
        t = 2 * pi
        s_b[...] = sel_scores(t + 1)
        sel_update(s_a, t, False)
        s_a[...] = sel_scores(t + 2)
        sel_update(s_b, t + 1, False)
        return 0

    last = (t0 + tq + kt - 1) // kt - 1
    m_ref[...] = jnp.full((rows, 1), NEG, F32)
    acc_ref[...] = jnp.zeros((rows, 2 * HEAD_DIM), F32)
    s_a[...] = sel_scores(0)
    lax.fori_loop(0, last // 2, sel_pair, 0)
    t_even = 2 * (last // 2)

    @pl.when(last % 2 == 1)
    def _():
        s_b[...] = sel_scores(t_even + 1)
        sel_update(s_a, t_even, False)
        sel_update(s_b, t_even + 1, True)

    @pl.when(last % 2 == 0)
    def _():
        sel_update(s_a, t_even, True)

    gate = gate_ref[...]
    for u in range(hg):
        c = N_NSA_BRANCHES * u + 1
        acc_u = acc_ref[u * tq:(u + 1) * tq, :]
        o_slc = acc_u[:, :HEAD_DIM] / acc_u[:, HEAD_DIM:]
        cols = slice(u * HEAD_DIM, (u + 1) * HEAD_DIM)
        o_ref[:, cols] = (part_ref[:, cols] + gate[:, c:c + 1] * o_slc).astype(BF16)


def _nsa(heads, gates, part_cmp, sel_off, tq, kt):
    _, s, _ = heads.shape
    nsp = sel_off.shape[2]
    hg = NSA_GROUP
    assert kt % tq == 0 and NSA_WINDOW % tq == 0 and s % kt == 0
    q_map = lambda u, g, qb: (H_QN + hg * g + u, qb, 0)
    full = lambda hbase: pl.BlockSpec((1, s, HEAD_DIM), lambda g, qb: (hbase + g, 0, 0),
                                      pipeline_mode=pl.Buffered(1))
    return pl.pallas_call(
        functools.partial(_nsa_kernel, s_len=s, nsp=nsp, tq=tq, kt=kt),
        grid=(NSA_KV_GROUPS, s // tq),
        in_specs=[
            *[pl.BlockSpec((1, tq, HEAD_DIM), functools.partial(q_map, u)) for u in range(hg)],
            full(H_KS), full(H_VS), full(H_KW), full(H_VW),
            pl.BlockSpec((tq, HEAD_DIM), lambda g, qb: (qb, g)),
            pl.BlockSpec((tq, hg * HEAD_DIM), lambda g, qb: (qb, g)),
            pl.BlockSpec((1, tq, nsp), lambda g, qb: (g, qb, 0)),
        ],
        out_specs=pl.BlockSpec((tq, hg * HEAD_DIM), lambda g, qb: (qb, g)),
        out_shape=jax.ShapeDtypeStruct((s, D_NSA), BF16),
        scratch_shapes=[pltpu.VMEM((s, HEAD_DIM + nsp), BF16),
                        pltpu.VMEM((s, 2 * HEAD_DIM), BF16),
                        pltpu.VMEM((s, 2 * HEAD_DIM), BF16),
                        pltpu.VMEM((kt // tq, tq, kt), F32),
                        pltpu.VMEM((NSA_WINDOW // tq + 1, tq, NSA_WINDOW + tq), F32),
                        pltpu.VMEM((hg * tq, kt), F32),
                        pltpu.VMEM((hg * tq, kt), F32),
                        pltpu.VMEM((hg * tq, 1), F32),
                        pltpu.VMEM((hg * tq, 2 * HEAD_DIM), F32),
                        pltpu.VMEM((tq, hg * HEAD_DIM), F32)],
        compiler_params=_params(("arbitrary", "arbitrary")),
        name="nsa_attn",
    )(*([heads] * hg), heads, heads, heads, heads, gates, part_cmp, sel_off)


def _out_proj_ln_kernel(a_ref, b_ref, wa_ref, wb_ref, h_ref, g_ref, be_ref, o_ref, *, alpha):
    tm = o_ref.shape[0]
    rc = tm // ROW_CHUNKS
    for c in range(ROW_CHUNKS):
        r = slice(c * rc, (c + 1) * rc)
        mix = (jnp.dot(a_ref[r, :], wa_ref[...], preferred_element_type=F32)
               + jnp.dot(b_ref[r, :], wb_ref[...], preferred_element_type=F32))
        o_ref[r, :] = _layer_norm(alpha * h_ref[r, :] + mix, g_ref[...], be_ref[...])


def _out_proj_ln(mix_a, mix_b, wa, wb, h, g, b, alpha, tm):
    s, d = h.shape
    return pl.pallas_call(
        functools.partial(_out_proj_ln_kernel, alpha=alpha),
        grid=(s // tm,),
        in_specs=[
            pl.BlockSpec((tm, D_DIL), lambda i: (i, 0)),
            pl.BlockSpec((tm, D_NSA), lambda i: (i, 0)),
            pl.BlockSpec((D_DIL, d), lambda i: (0, 0)),
            pl.BlockSpec((D_NSA, d), lambda i: (0, 0)),
            pl.BlockSpec((tm, d), lambda i: (i, 0)),
            pl.BlockSpec((1, d), lambda i: (0, 0)),
            pl.BlockSpec((1, d), lambda i: (0, 0)),
        ],
        out_specs=pl.BlockSpec((tm, d), lambda i: (i, 0)),
        out_shape=jax.ShapeDtypeStruct((s, d), F32),
        compiler_params=_params(("arbitrary",)),
        name="out_proj_ln",
    )(mix_a, mix_b, wa, wb, h, g, b)


def _overlap_t(s):
    n_cmp = (s - CMP_BLOCK) // CMP_STRIDE + 1
    n_sel = s // SEL_BLOCK
    ncp = s // CMP_STRIDE
    nsp = -(-n_sel // HEAD_DIM) * HEAD_DIM
    c_lo = np.arange(n_cmp) * CMP_STRIDE
    c_hi = c_lo + CMP_BLOCK - 1
    s_lo = (np.arange(n_sel) * SEL_BLOCK)[:, None]
    ov = np.zeros((nsp, ncp), np.float32)
    ov[:n_sel, :n_cmp] = (c_lo[None, :] <= s_lo + SEL_BLOCK - 1) & (c_hi[None, :] >= s_lo)
    return jnp.asarray(ov, BF16)


def kernel(x, positions, ln1_g, ln1_b, ffn1_w1, ffn1_w3, ffn1_w2, w_in, gate_b, cmp_pe, cmp_w1,
           cmp_b1, cmp_w2, cmp_b2, w_out, ln2_g, ln2_b, ffn2_w1, ffn2_w3, ffn2_w2, ln3_g, ln3_b):
    bsz, s, d = x.shape
    assert bsz == 1 and d == (N_HEADS_DIL + N_HEADS_NSA) * HEAD_DIM
    assert s % 512 == 0 and s >= NSA_WINDOW + QBLK
    alpha = (2.0 * DEPTH) ** 0.25
    scale = HEAD_DIM ** -0.5 * LOG2_E
    row = lambda v: v.reshape(1, -1)
    tm = 512
    ffn_tm, ffn_tf = 1024, 256

    inv_freq = ROPE_THETA ** (-jnp.arange(0, HEAD_DIM, 2, dtype=F32) / HEAD_DIM)
    invf = jnp.concatenate([inv_freq, inv_freq]).reshape(1, HEAD_DIM)
    pos = positions[0].astype(F32).reshape(s, 1)
    cosf, sinf = _rope_tables(pos, pos, invf, 512)
    ncp = s // CMP_STRIDE
    n_cmp = (s - CMP_BLOCK) // CMP_STRIDE + 1
    pos_lo = jnp.pad(pos[0:n_cmp * CMP_STRIDE:CMP_STRIDE], ((0, ncp - n_cmp), (0, 0)))
    pos_hi = jnp.pad(pos[CMP_BLOCK - 1::CMP_STRIDE][:n_cmp], ((0, ncp - n_cmp), (0, 0)))
    cos_c, sin_c = _rope_tables(pos_lo, pos_hi, invf, ncp)

    g_off = 3 * D_DIL + D_NSA + 6 * KV_W
    per_g = NSA_GROUP * N_NSA_BRANCHES
    w_in_t = w_in[0].T
    wg = jnp.zeros((NSA_KV_GROUPS * HEAD_DIM, d), F32)
    bg = jnp.zeros((1, NSA_KV_GROUPS * HEAD_DIM), F32)
    for g in range(NSA_KV_GROUPS):
        wg = wg.at[g * HEAD_DIM:g * HEAD_DIM + per_g].set(
            w_in_t[g_off + g * per_g:g_off + (g + 1) * per_g])
        bg = bg.at[0, g * HEAD_DIM:g * HEAD_DIM + per_g].set(gate_b[0][g * per_g:(g + 1) * per_g])
    wg = wg.astype(BF16)

    h0 = x[0]
    h1 = _ffn_ln(h0, ffn1_w1[0], ffn1_w3[0], ffn1_w2[0], row(ln1_g[0]), row(ln1_b[0]),
                 alpha, ffn_tm, ffn_tf)

    heads, gates = _in_proj(h1, w_in_t, cosf, sinf, wg, bg, scale, 1024)
    tok16 = heads[H_KC:H_KS].reshape(2 * NSA_KV_GROUPS, ncp, CMP_STRIDE * HEAD_DIM)
    kcv, kcv_t = _compress(
        tok16, cmp_pe[0].reshape(2, 1, CMP_BLOCK * HEAD_DIM), cmp_w1[0].astype(BF16),
        cmp_b1[0].reshape(2, 1, CMP_HIDDEN), cmp_w2[0].astype(BF16),
        cmp_b2[0].reshape(2, 1, HEAD_DIM), cos_c, sin_c)
    mix_a = _dilated(heads, 256)
    part_cmp, sel_off = _nsa_select(heads, kcv, kcv_t, gates, _overlap_t(s), 512)
    mix_b = _nsa(heads, gates, part_cmp, sel_off, 256, 1024)

    wo = w_out[0].astype(BF16)
    h2 = _out_proj_ln(mix_a, mix_b, wo[:D_DIL], wo[D_DIL:], h1, row(ln2_g[0]), row(ln2_b[0]),
                      alpha, tm)
    h3 = _ffn_ln(h2, ffn2_w1[0], ffn2_w3[0], ffn2_w2[0], row(ln3_g[0]), row(ln3_b[0]),
                 alpha, ffn_tm, ffn_tf)
    return h3.reshape(bsz, s, d)
```

```python
import functools

import jax
import jax.numpy as jnp
import numpy as np
from jax import lax
from jax.experimental import pallas as pl
from jax.experimental.pallas import tpu as pltpu

HEAD_DIM = 128
N_HEADS_DIL = 6
N_HEADS_NSA = 10
NSA_KV_GROUPS = 2
NSA_GROUP = N_HEADS_NSA // NSA_KV_GROUPS
N_NSA_BRANCHES = 3
DIL_PAIRS = ((128, 1), (512, 4), (2048, 16))
CMP_BLOCK = 32
CMP_STRIDE = 16
CMP_HIDDEN = 256
SEL_BLOCK = 64
N_SELECT = 16
NSA_WINDOW = 512
ROPE_THETA = 10000.0
QBLK = 128
LN_EPS = 1e-5
NEG = -1e30
FORCE_BONUS = 1e4
SEL_OFF = -1e9
LOG2_E = 1.4426950408889634
DEPTH = 1

D_DIL = N_HEADS_DIL * HEAD_DIM
D_NSA = N_HEADS_NSA * HEAD_DIM
KV_W = NSA_KV_GROUPS * HEAD_DIM
N_GATES = N_HEADS_NSA * N_NSA_BRANCHES

H_QA, H_KA, H_VA, H_QN = 0, 6, 12, 18
H_KC, H_VC, H_KS, H_VS, H_KW, H_VW = 28, 30, 32, 34, 36, 38
N_PROJ_HEADS = 40
HEADS_PER_PROJ_BLOCK = 8
PROJ_SUB_HEADS = 2
ROW_CHUNKS = 2
CMP_ROW_CHUNK = 128

VMEM_LIMIT_BYTES = 56 * 1024 * 1024
FFN_VMEM_LIMIT_BYTES = 60 * 1024 * 1024

F32 = jnp.float32
BF16 = jnp.bfloat16


def _params(sem, vmem=VMEM_LIMIT_BYTES):
    return pltpu.CompilerParams(dimension_semantics=sem, vmem_limit_bytes=vmem)


def _nt_dot(a, b):
    return lax.dot_general(a, b, (((1,), (1,)), ((), ())), preferred_element_type=F32)


def _layer_norm(y, g, b):
    mu = jnp.mean(y, axis=-1, keepdims=True)
    yc = y - mu
    var = jnp.mean(yc * yc, axis=-1, keepdims=True)
    return yc * lax.rsqrt(var + LN_EPS) * g + b


def _rope_table_kernel(pa_ref, pb_ref, invf_ref, cos_ref, sin_ref):
    pos = (pa_ref[...] + pb_ref[...]) * 0.5
    ang = pos * invf_ref[...]
    lane = lax.broadcasted_iota(jnp.int32, ang.shape, 1)
    cos_ref[...] = jnp.cos(ang)
    sin_ref[...] = jnp.where(lane < HEAD_DIM // 2, -1.0, 1.0) * jnp.sin(ang)


def _rope_tables(pos_a, pos_b, invf, tile):
    n = pos_a.shape[0]
    spec_p = pl.BlockSpec((tile, 1), lambda i: (i, 0))
    spec_t = pl.BlockSpec((tile, HEAD_DIM), lambda i: (i, 0))
    return pl.pallas_call(
        _rope_table_kernel,
        grid=(n // tile,),
        in_specs=[spec_p, spec_p, pl.BlockSpec((1, HEAD_DIM), lambda i: (0, 0))],
        out_specs=[spec_t, spec_t],
        out_shape=[jax.ShapeDtypeStruct((n, HEAD_DIM), F32)] * 2,
        compiler_params=_params(("arbitrary",)),
        name="rope_tables",
    )(pos_a, pos_b, invf)


def _ffn_ln_kernel(h_ref, w1_ref, w3_ref, w2_ref, g_ref, b_ref, o_ref, hb_ref, *, alpha, nf):
    f = pl.program_id(1)

    @pl.when(f == 0)
    def _():
        h = h_ref[...]
        hb_ref[...] = h.astype(BF16)
        o_ref[...] = (2.0 * alpha) * h

    hb = hb_ref[...]
    a = jnp.dot(hb, w1_ref[...].astype(BF16), preferred_element_type=F32)
    b = jnp.dot(hb, w3_ref[...].astype(BF16), preferred_element_type=F32)
    act = (a * jax.nn.sigmoid(a)) * b
    o_ref[...] += jnp.dot(act.astype(BF16), w2_ref[...].astype(BF16),
                          preferred_element_type=F32)

    @pl.when(f == nf - 1)
    def _():
        o_ref[...] = _layer_norm(0.5 * o_ref[...], g_ref[...], b_ref[...])


def _ffn_ln(h, w1, w3, w2, g, b, alpha, tm, tf):
    s, d = h.shape
    dff = w1.shape[1]
    nf = dff // tf
    return pl.pallas_call(
        functools.partial(_ffn_ln_kernel, alpha=alpha, nf=nf),
        grid=(s // tm, nf),
        in_specs=[
            pl.BlockSpec((tm, d), lambda i, f: (i, 0)),
            pl.BlockSpec((d, tf), lambda i, f: (0, f)),
            pl.BlockSpec((d, tf), lambda i, f: (0, f)),
            pl.BlockSpec((tf, d), lambda i, f: (f, 0)),
            pl.BlockSpec((1, d), lambda i, f: (0, 0)),
            pl.BlockSpec((1, d), lambda i, f: (0, 0)),
        ],
        out_specs=pl.BlockSpec((tm, d), lambda i, f: (i, 0)),
        out_shape=jax.ShapeDtypeStruct((s, d), F32),
        scratch_shapes=[pltpu.VMEM((tm, d), BF16)],
        compiler_params=_params(("arbitrary", "arbitrary"), FFN_VMEM_LIMIT_BYTES),
        name="ffn_ln",
    )(h, w1, w3, w2, g, b)


def _in_proj_kernel(h_ref, w_ref, cos_ref, sin_ref, wg_ref, bg_ref, o_ref, gate_ref, hb_ref,
                    *, scale):
    j = pl.program_id(1)
    hpb = HEADS_PER_PROJ_BLOCK
    sub = PROJ_SUB_HEADS

    @pl.when(j == 0)
    def _():
        hb_ref[...] = h_ref[...].astype(BF16)
        gate_ref[...] = jax.nn.sigmoid(_nt_dot(hb_ref[...], wg_ref[...]) + bg_ref[...])

    cos = cos_ref[...]
    sin = sin_ref[...]
    h = hb_ref[...]
    for k0 in range(0, hpb, sub):
        w = w_ref[k0 * HEAD_DIM:(k0 + sub) * HEAD_DIM, :].astype(BF16)
        acc = _nt_dot(h, w)
        for k in range(sub):
            hh = j * hpb + k0 + k
            is_q = (hh < H_KA) | ((hh >= H_QN) & (hh < H_KC))
            rope = (is_q | (hh < H_VA) | ((hh >= H_KS) & (hh < H_VS))
                    | ((hh >= H_KW) & (hh < H_VW)))
            x = acc[:, k * HEAD_DIM:(k + 1) * HEAD_DIM]
            xr = x * cos + pltpu.roll(x, HEAD_DIM // 2, 1) * sin
            y = jnp.where(rope, xr, x) * jnp.where(is_q, scale, 1.0).astype(F32)
            o_ref[k0 + k] = y.astype(BF16)


def _in_proj(h, w, cosf, sinf, wg, bg, scale, tm):
    s, d = h.shape
    n = wg.shape[0]
    hpb = HEADS_PER_PROJ_BLOCK
    return pl.pallas_call(
        functools.partial(_in_proj_kernel, scale=scale),
        grid=(s // tm, N_PROJ_HEADS // hpb),
        in_specs=[
            pl.BlockSpec((tm, d), lambda i, j: (i, 0)),
            pl.BlockSpec((hpb * HEAD_DIM, d), lambda i, j: (j, 0)),
            pl.BlockSpec((tm, HEAD_DIM), lambda i, j: (i, 0)),
            pl.BlockSpec((tm, HEAD_DIM), lambda i, j: (i, 0)),
            pl.BlockSpec((n, d), lambda i, j: (0, 0)),
            pl.BlockSpec((1, n), lambda i, j: (0, 0)),
        ],
        out_specs=[pl.BlockSpec((hpb, tm, HEAD_DIM), lambda i, j: (j, i, 0)),
                   pl.BlockSpec((tm, n), lambda i, j: (i, 0))],
        out_shape=[jax.ShapeDtypeStruct((N_PROJ_HEADS, s, HEAD_DIM), BF16),
                   jax.ShapeDtypeStruct((s, n), F32)],
        scratch_shapes=[pltpu.VMEM((tm, d), BF16)],
        compiler_params=_params(("arbitrary", "arbitrary")),
        name="in_proj",
    )(h, w, cosf, sinf, wg, bg)


def _compress_kernel(tok_ref, pe_ref, w1_ref, b1_ref, w2_ref, b2_ref, cos_ref, sin_ref,
                     o_ref, ot_ref):
    j = pl.program_id(0)
    half = CMP_STRIDE * HEAD_DIM
    tok = tok_ref[0].astype(F32)
    pe = pe_ref[0]
    w1 = w1_ref[0].astype(BF16)
    top = (tok + pe[:, :half]).astype(BF16)
    bot = (tok + pe[:, half:]).astype(BF16)
    u = jnp.dot(top, w1[:half], preferred_element_type=F32)
    v = jnp.dot(bot, w1[half:], preferred_element_type=F32)
    nc = u.shape[0]
    hid = u + pltpu.roll(v, nc - 1, 0) + b1_ref[0]
    hid = jax.nn.gelu(hid)
    out = jnp.dot(hid.astype(BF16), w2_ref[0].astype(BF16),
                  preferred_element_type=F32) + b2_ref[0]
    roped = out * cos_ref[...] + pltpu.roll(out, HEAD_DIM // 2, 1) * sin_ref[...]
    out = jnp.where(j == 0, roped, out)
    o_ref[0, 0] = out.astype(BF16)
    ot_ref[0, 0] = out.T.astype(BF16)


def _compress(tok16, pe, w1, b1, w2, b2, cos_c, sin_c):
    nc = tok16.shape[1]
    blk = CMP_BLOCK * HEAD_DIM
    g = NSA_KV_GROUPS
    return pl.pallas_call(
        _compress_kernel,
        grid=(2, g),
        in_specs=[
            pl.BlockSpec((1, nc, CMP_STRIDE * HEAD_DIM), lambda j, gi: (g * j + gi, 0, 0)),
            pl.BlockSpec((1, 1, blk), lambda j, gi: (j, 0, 0)),
            pl.BlockSpec((1, blk, CMP_HIDDEN), lambda j, gi: (j, 0, 0)),
            pl.BlockSpec((1, 1, CMP_HIDDEN), lambda j, gi: (j, 0, 0)),
            pl.BlockSpec((1, CMP_HIDDEN, HEAD_DIM), lambda j, gi: (j, 0, 0)),
            pl.BlockSpec((1, 1, HEAD_DIM), lambda j, gi: (j, 0, 0)),
            pl.BlockSpec((nc, HEAD_DIM), lambda j, gi: (0, 0)),
            pl.BlockSpec((nc, HEAD_DIM), lambda j, gi: (0, 0)),
        ],
        out_specs=[pl.BlockSpec((1, 1, nc, HEAD_DIM), lambda j, gi: (j, gi, 0, 0)),
                   pl.BlockSpec((1, 1, HEAD_DIM, nc), lambda j, gi: (j, gi, 0, 0))],
        out_shape=[jax.ShapeDtypeStruct((2, g, nc, HEAD_DIM), BF16),
                   jax.ShapeDtypeStruct((2, g, HEAD_DIM, nc), BF16)],
        compiler_params=_params(("arbitrary", "arbitrary")),
        name="nsa_compress",
    )(tok16, pe, w1, b1, w2, b2, cos_c, sin_c)


def _dil_kernel(q_ref, k_ref, v_ref, o_ref, kpad, vpad, bias_ref, s_a, s_b, *, tq, padk):
    w = padk + tq
    n_tiles = q_ref.shape[1] // tq

    @pl.when(pl.program_id(0) == 0)
    def _():
        r = lax.broadcasted_iota(jnp.int32, (tq, w), 0)
        c = lax.broadcasted_iota(jnp.int32, (tq, w), 1)
        d = r + padk - c
        cnt = jnp.zeros((tq, w), F32)
        for window, dil in DIL_PAIRS:
            hit = (d >= 0) & (d <= window) & ((d & (dil - 1)) == 0)
            cnt = cnt + jnp.where(hit, 1.0, 0.0)
        bias_ref[...] = jnp.where(cnt > 0.0, jnp.log2(jnp.maximum(cnt, 1.0)), NEG)

    kpad[0:padk, :] = jnp.zeros((padk, HEAD_DIM), BF16)
    vpad[0:padk, :] = jnp.zeros((padk, HEAD_DIM), BF16)
    kpad[padk:, :] = k_ref[0]
    vpad[padk:, :] = v_ref[0]
    col = lax.broadcasted_iota(jnp.int32, (1, w), 1)

    def scores(t):
        q0 = pl.multiple_of(t * tq, tq)
        s = _nt_dot(q_ref[0, pl.ds(q0, tq), :], kpad[pl.ds(q0, w), :]) + bias_ref[...]
        return jnp.where(col >= padk - q0, s, NEG)

    def finish(s_ref, t):
        q0 = pl.multiple_of(t * tq, tq)
        s = s_ref[...]
        p = jnp.exp2(s - jnp.max(s, axis=-1, keepdims=True))
        l = jnp.sum(p, axis=-1, keepdims=True)
        o = jnp.dot(p.astype(BF16), vpad[pl.ds(q0, w), :], preferred_element_type=F32)
        o_ref[pl.ds(q0, tq), :] = (o / l).astype(BF16)

    def pair(pi, _):
        t = 2 * pi
        s_b[...] = scores(t + 1)
        finish(s_a, t)
        s_a[...] = scores(t + 2)
        finish(s_b, t + 1)
        return 0

    s_a[...] = scores(0)
    lax.fori_loop(0, n_tiles // 2 - 1, pair, 0)
    s_b[...] = scores(n_tiles - 1)
    finish(s_a, n_tiles - 2)
    finish(s_b, n_tiles - 1)


def _dilated(heads, tq):
    _, s, _ = heads.shape
    padk = max(wd for wd, _ in DIL_PAIRS)
    assert (s // tq) % 2 == 0
    return pl.pallas_call(
        functools.partial(_dil_kernel, tq=tq, padk=padk),
        grid=(N_HEADS_DIL,),
        in_specs=[
            pl.BlockSpec((1, s, HEAD_DIM), lambda h: (H_QA + h, 0, 0)),
            pl.BlockSpec((1, s, HEAD_DIM), lambda h: (H_KA + h, 0, 0)),
            pl.BlockSpec((1, s, HEAD_DIM), lambda h: (H_VA + h, 0, 0)),
        ],
        out_specs=pl.BlockSpec((s, HEAD_DIM), lambda h: (0, h)),
        out_shape=jax.ShapeDtypeStruct((s, D_DIL), BF16),
        scratch_shapes=[pltpu.VMEM((padk + s, HEAD_DIM), BF16),
                        pltpu.VMEM((padk + s, HEAD_DIM), BF16),
                        pltpu.VMEM((tq, padk + tq), F32),
                        pltpu.VMEM((tq, padk + tq), F32),
                        pltpu.VMEM((tq, padk + tq), F32)],
        compiler_params=_params(("arbitrary",)),
        name="dilated_attn",
    )(heads, heads, heads)


def _add_per_head(s, bias, hg):
    tq = bias.shape[0]
    return jnp.concatenate([s[u * tq:(u + 1) * tq] + bias for u in range(hg)], axis=0)


def _nsa_select_kernel(*refs, nsp, tq, k_sel):
    q_refs = refs[:NSA_GROUP]
    kc_ref, vct_ref, gate_ref, ovt_ref, cmp_ref, sel_ref, imp_ref = refs[NSA_GROUP:]
    t0 = pl.program_id(1) * tq
    hg = NSA_GROUP
    ncp = kc_ref.shape[2]
    gate = gate_ref[...]

    def cmp_branch(n_rows):
        kc = kc_ref[0, 0, :n_rows, :]
        vct = vct_ref[0, 0, :, :n_rows]
        ovt = ovt_ref[:, :n_rows]
        n_io = lax.broadcasted_iota(jnp.int32, (n_rows, tq), 0)
        t_io = t0 + lax.broadcasted_iota(jnp.int32, (n_rows, tq), 1)
        cmask = (n_io * CMP_STRIDE + (CMP_BLOCK - 1) <= t_io) & (n_io < ncp - 1)
        imp = jnp.zeros((nsp, tq), F32)
        for u in range(hg):
            st = jnp.where(cmask, _nt_dot(kc, q_refs[u][0]), NEG)
            m = jnp.maximum(jnp.max(st, axis=0, keepdims=True), 0.1 * NEG)
            e = jnp.exp2(st - m)
            r = 1.0 / jnp.maximum(jnp.sum(e, axis=0, keepdims=True), 1e-30)
            eb = e.astype(BF16)
            imp = imp + jnp.dot(ovt, eb, preferred_element_type=F32) * r
            o_cmp = (jnp.dot(vct, eb, preferred_element_type=F32) * r).T
            c = N_NSA_BRANCHES * u
            cmp_ref[:, u * HEAD_DIM:(u + 1) * HEAD_DIM] = gate[:, c:c + 1] * o_cmp
        imp_ref[...] = imp

    n_var = ncp // CMP_ROW_CHUNK
    need = (t0 + tq) // CMP_STRIDE
    var = jnp.minimum((need + CMP_ROW_CHUNK - 1) // CMP_ROW_CHUNK, n_var) - 1
    for v in range(n_var):
        pl.when(var == v)(functools.partial(cmp_branch, CMP_ROW_CHUNK * (v + 1)))
    imp = imp_ref[...]

    j_io = lax.broadcasted_iota(jnp.int32, (nsp, tq), 0)
    t_sel = t0 + lax.broadcasted_iota(jnp.int32, (nsp, tq), 1)
    valid = j_io * SEL_BLOCK <= t_sel
    cur = t_sel // SEL_BLOCK
    forced = (j_io == 0) | (j_io == cur) | (j_io == cur - 1)
    score = jnp.where(valid & jnp.logical_not(forced), imp, NEG)
    for _ in range(k_sel - 3):
        m = jnp.max(score, axis=0, keepdims=True)
        first = jnp.min(jnp.where(score == m, j_io, nsp), axis=0, keepdims=True)
        score = jnp.where(j_io == first, -jnp.inf, score)
    sel_t = jnp.where(valid & (forced | (score == -jnp.inf)), 1.0, 0.0)
    sel_ref[0] = jnp.where(sel_t.T > 0.0, 0.0, SEL_OFF).astype(BF16)


def _nsa_select(heads, kcv, vct, gates, ovt, tq):
    _, s, _ = heads.shape
    ncp = kcv.shape[2]
    nsp = ovt.shape[0]
    hg = NSA_GROUP
    k_sel = min(N_SELECT, s // SEL_BLOCK)
    assert k_sel >= 3 and ncp % CMP_ROW_CHUNK == 0
    q_map = lambda u, g, qb: (H_QN + hg * g + u, qb, 0)
    return pl.pallas_call(
        functools.partial(_nsa_select_kernel, nsp=nsp, tq=tq, k_sel=k_sel),
        grid=(NSA_KV_GROUPS, s // tq),
        in_specs=[
            *[pl.BlockSpec((1, tq, HEAD_DIM), functools.partial(q_map, u)) for u in range(hg)],
            pl.BlockSpec((1, 1, ncp, HEAD_DIM), lambda g, qb: (0, g, 0, 0)),
            pl.BlockSpec((1, 1, HEAD_DIM, ncp), lambda g, qb: (1, g, 0, 0)),
            pl.BlockSpec((tq, HEAD_DIM), lambda g, qb: (qb, g)),
            pl.BlockSpec((nsp, ncp), lambda g, qb: (0, 0)),
        ],
        out_specs=[pl.BlockSpec((tq, hg * HEAD_DIM), lambda g, qb: (qb, g)),
                   pl.BlockSpec((1, tq, nsp), lambda g, qb: (g, qb, 0))],
        out_shape=[jax.ShapeDtypeStruct((s, D_NSA), F32),
                   jax.ShapeDtypeStruct((NSA_KV_GROUPS, s, nsp), BF16)],
        scratch_shapes=[pltpu.VMEM((nsp, tq), F32)],
        compiler_params=_params(("arbitrary", "arbitrary")),
        name="nsa_select",
    )(*([heads] * hg), kcv, vct, gates, ovt)


def _nsa_kernel(*refs, s_len, nsp, tq, kt):
    q_refs = refs[:NSA_GROUP]
    (ks_ref, vs_ref, kw_ref, vw_ref, gate_ref, cmp_ref, sel_ref,
     o_ref, kaug, vaug, vwaug, cbias, wbias, s_a, s_b, m_ref, acc_ref,
     part_ref) = refs[NSA_GROUP:]
    g = pl.program_id(0)
    qb = pl.program_id(1)
    t0 = qb * tq
    hg = NSA_GROUP
    rows = hg * tq
    ww = NSA_WINDOW + tq

    @pl.when((g == 0) & (qb == 0))
    def _():
        for r in range(kt // tq):
            d = (lax.broadcasted_iota(jnp.int32, (tq, kt), 0) + r * tq
                 - lax.broadcasted_iota(jnp.int32, (tq, kt), 1))
            cbias[r] = jnp.where(d >= 0, 0.0, NEG)
        for w in range(NSA_WINDOW // tq + 1):
            d = (lax.broadcasted_iota(jnp.int32, (tq, ww), 0) + w * tq
                 - lax.broadcasted_iota(jnp.int32, (tq, ww), 1))
            wbias[w] = jnp.where((d >= 0) & (d < NSA_WINDOW), 0.0, NEG)

    @pl.when(qb == 0)
    def _():
        kaug[:, 0:HEAD_DIM] = ks_ref[0]
        key = lax.broadcasted_iota(jnp.int32, (s_len, nsp), 0)
        blk = lax.broadcasted_iota(jnp.int32, (s_len, nsp), 1)
        kaug[:, HEAD_DIM:] = jnp.where(blk == key // SEL_BLOCK, 1.0, 0.0).astype(BF16)
        ones = jnp.ones((s_len, HEAD_DIM), BF16)
        vaug[:, 0:HEAD_DIM] = vs_ref[0]
        vaug[:, HEAD_DIM:] = ones
        vwaug[:, 0:HEAD_DIM] = vw_ref[0]
        vwaug[:, HEAD_DIM:] = ones

    gate = gate_ref[...]
    q5 = jnp.concatenate([r[0] for r in q_refs], axis=0)

    qaug = jnp.concatenate([q5, jnp.concatenate([sel_ref[0]] * hg, axis=0)], axis=1)

    w0 = pl.multiple_of(jnp.maximum(t0 - NSA_WINDOW, 0), tq)
    sw = _nt_dot(q5, kw_ref[0, pl.ds(w0, ww), :])
    sw = _add_per_head(sw, wbias[jnp.minimum(qb, NSA_WINDOW // tq)], hg)
    pw = jnp.exp2(sw - jnp.max(sw, axis=-1, keepdims=True))
    acc_w = jnp.dot(pw.astype(BF16), vwaug[pl.ds(w0, ww), :], preferred_element_type=F32)
    o_win = acc_w[:, :HEAD_DIM] / acc_w[:, HEAD_DIM:]
    for u in range(hg):
        c = N_NSA_BRANCHES * u + 2
        cols = slice(u * HEAD_DIM, (u + 1) * HEAD_DIM)
        part_ref[:, cols] = cmp_ref[:, cols] + gate[:, c:c + 1] * o_win[u * tq:(u + 1) * tq]

    def sel_scores(kti):
        k0 = pl.multiple_of(kti * kt, kt)
        return _nt_dot(qaug, kaug[pl.ds(k0, kt), :])

    def sel_update(s_ref, kti, diagonal):
        s = s_ref[...]
        if diagonal:
            s = _add_per_head(s, cbias[qb % (kt // tq)], hg)
        k0 = pl.multiple_of(kti * kt, kt)
        m_i = m_ref[...]
        m_new = jnp.maximum(m_i, jnp.max(s, axis=-1, keepdims=True))
        p = jnp.exp2(s - m_new)
        pv = jnp.dot(p.astype(BF16), vaug[pl.ds(k0, kt), :], preferred_element_type=F32)
        acc_ref[...] = jnp.exp2(m_i - m_new) * acc_ref[...] + pv
        m_ref[...] = m_new

    def sel_pair(pi, _):
        t = 2 * pi
        s_b[...] = sel_scores(t + 1)
        sel_update(s_a, t, False)
        s_a[...] = sel_scores(t + 2)
        sel_update(s_b, t + 1, False)
        return 0

    last = (t0 + tq + kt - 1) // kt - 1
    m_ref[...] = jnp.full((rows, 1), NEG, F32)
    acc_ref[...] = jnp.zeros((rows, 2 * HEAD_DIM), F32)
    s_a[...] = sel_scores(0)
    lax.fori_loop(0, last // 2, sel_pair, 0)
    t_even = 2 * (last // 2)

    @pl.when(last % 2 == 1)
    def _():
        s_b[...] = sel_scores(t_even + 1)
        sel_update(s_a, t_even, False)
        sel_update(s_b, t_even + 1, True)

    @pl.when(last % 2 == 0)
    def _():
        sel_update(s_a, t_even, True)

    gate = gate_ref[...]
    for u in range(hg):
        c = N_NSA_BRANCHES * u + 1
        acc_u = acc_ref[u * tq:(u + 1) * tq, :]
        o_slc = acc_u[:, :HEAD_DIM] / acc_u[:, HEAD_DIM:]
        cols = slice(u * HEAD_DIM, (u + 1) * HEAD_DIM)
        o_ref[:, cols] = (part_ref[:, cols] + gate[:, c:c + 1] * o_slc).astype(BF16)


def _nsa(heads, gates, part_cmp, sel_off, tq, kt):
    _, s, _ = heads.shape
    nsp = sel_off.shape[2]
    hg = NSA_GROUP
    assert kt % tq == 0 and NSA_WINDOW % tq == 0 and s % kt == 0
    q_map = lambda u, g, qb: (H_QN + hg * g + u, qb, 0)
    full = lambda hbase: pl.BlockSpec((1, s, HEAD_DIM), lambda g, qb: (hbase + g, 0, 0),
                                      pipeline_mode=pl.Buffered(1))
    return pl.pallas_call(
        functools.partial(_nsa_kernel, s_len=s, nsp=nsp, tq=tq, kt=kt),
        grid=(NSA_KV_GROUPS, s // tq),
        in_specs=[
            *[pl.BlockSpec((1, tq, HEAD_DIM), functools.partial(q_map, u)) for u in range(hg)],
            full(H_KS), full(H_VS), full(H_KW), full(H_VW),
            pl.BlockSpec((tq, HEAD_DIM), lambda g, qb: (qb, g)),
            pl.BlockSpec((tq, hg * HEAD_DIM), lambda g, qb: (qb, g)),
            pl.BlockSpec((1, tq, nsp), lambda g, qb: (g, qb, 0)),
        ],
        out_specs=pl.BlockSpec((tq, hg * HEAD_DIM), lambda g, qb: (qb, g)),
        out_shape=jax.ShapeDtypeStruct((s, D_NSA), BF16),
        scratch_shapes=[pltpu.VMEM((s, HEAD_DIM + nsp), BF16),
                        pltpu.VMEM((s, 2 * HEAD_DIM), BF16),
                        pltpu.VMEM((s, 2 * HEAD_DIM), BF16),
                        pltpu.VMEM((kt // tq, tq, kt), F32),
                        pltpu.VMEM((NSA_WINDOW // tq + 1, tq, NSA_WINDOW + tq), F32),
                        pltpu.VMEM((hg * tq, kt), F32),
                        pltpu.VMEM((hg * tq, kt), F32),
                        pltpu.VMEM((hg * tq, 1), F32),
                        pltpu.VMEM((hg * tq, 2 * HEAD_DIM), F32),
                        pltpu.VMEM((tq, hg * HEAD_DIM), F32)],
        compiler_params=_params(("arbitrary", "arbitrary")),
        name="nsa_attn",
    )(*([heads] * hg), heads, heads, heads, heads, gates, part_cmp, sel_off)


def _out_proj_ln_kernel(a_ref, b_ref, w_ref, h_ref, g_ref, be_ref, o_ref, wb16, *, alpha):
    @pl.when(pl.program_id(0) == 0)
    def _():
        wb16[...] = w_ref[...].astype(BF16)

    tm = o_ref.shape[0]
    rc = tm // ROW_CHUNKS
    for c in range(ROW_CHUNKS):
        r = slice(c * rc, (c + 1) * rc)
        mix = (jnp.dot(a_ref[r, :], wb16[:D_DIL, :], preferred_element_type=F32)
               + jnp.dot(b_ref[r, :], wb16[D_DIL:, :], preferred_element_type=F32))
        o_ref[r, :] = _layer_norm(alpha * h_ref[r, :] + mix, g_ref[...], be_ref[...])


def _out_proj_ln(mix_a, mix_b, w, h, g, b, alpha, tm):
    s, d = h.shape
    return pl.pallas_call(
        functools.partial(_out_proj_ln_kernel, alpha=alpha),
        grid=(s // tm,),
        in_specs=[
            pl.BlockSpec((tm, D_DIL), lambda i: (i, 0)),
            pl.BlockSpec((tm, D_NSA), lambda i: (i, 0)),
            pl.BlockSpec((d, d), lambda i: (0, 0), pipeline_mode=pl.Buffered(1)),
            pl.BlockSpec((tm, d), lambda i: (i, 0)),
            pl.BlockSpec((1, d), lambda i: (0, 0)),
            pl.BlockSpec((1, d), lambda i: (0, 0)),
        ],
        out_specs=pl.BlockSpec((tm, d), lambda i: (i, 0)),
        out_shape=jax.ShapeDtypeStruct((s, d), F32),
        scratch_shapes=[pltpu.VMEM((d, d), BF16)],
        compiler_params=_params(("arbitrary",)),
        name="out_proj_ln",
    )(mix_a, mix_b, w, h, g, b)


def _overlap_t(s):
    n_cmp = (s - CMP_BLOCK) // CMP_STRIDE + 1
    n_sel = s // SEL_BLOCK
    ncp = s // CMP_STRIDE
    nsp = -(-n_sel // HEAD_DIM) * HEAD_DIM
    c_lo = np.arange(n_cmp) * CMP_STRIDE
    c_hi = c_lo + CMP_BLOCK - 1
    s_lo = (np.arange(n_sel) * SEL_BLOCK)[:, None]
    ov = np.zeros((nsp, ncp), np.float32)
    ov[:n_sel, :n_cmp] = (c_lo[None, :] <= s_lo + SEL_BLOCK - 1) & (c_hi[None, :] >= s_lo)
    return jnp.asarray(ov, BF16)


def kernel(x, positions, ln1_g, ln1_b, ffn1_w1, ffn1_w3, ffn1_w2, w_in, gate_b, cmp_pe, cmp_w1,
           cmp_b1, cmp_w2, cmp_b2, w_out, ln2_g, ln2_b, ffn2_w1, ffn2_w3, ffn2_w2, ln3_g, ln3_b):
    bsz, s, d = x.shape
    assert bsz == 1 and d == (N_HEADS_DIL + N_HEADS_NSA) * HEAD_DIM
    assert s % 512 == 0 and s >= NSA_WINDOW + QBLK
    alpha = (2.0 * DEPTH) ** 0.25
    scale = HEAD_DIM ** -0.5 * LOG2_E
    row = lambda v: v.reshape(1, -1)
    tm = 512
    ffn_tm, ffn_tf = 1024, 256

    inv_freq = ROPE_THETA ** (-jnp.arange(0, HEAD_DIM, 2, dtype=F32) / HEAD_DIM)
    invf = jnp.concatenate([inv_freq, inv_freq]).reshape(1, HEAD_DIM)
    pos = positions[0].astype(F32).reshape(s, 1)
    cosf, sinf = _rope_tables(pos, pos, invf, 512)
    ncp = s // CMP_STRIDE
    n_cmp = (s - CMP_BLOCK) // CMP_STRIDE + 1
    pos_lo = jnp.pad(pos[0:n_cmp * CMP_STRIDE:CMP_STRIDE], ((0, ncp - n_cmp), (0, 0)))
    pos_hi = jnp.pad(pos[CMP_BLOCK - 1::CMP_STRIDE][:n_cmp], ((0, ncp - n_cmp), (0, 0)))
    cos_c, sin_c = _rope_tables(pos_lo, pos_hi, invf, ncp)

    g_off = 3 * D_DIL + D_NSA + 6 * KV_W
    per_g = NSA_GROUP * N_NSA_BRANCHES
    w_in_t = w_in[0].T
    wg = jnp.zeros((NSA_KV_GROUPS * HEAD_DIM, d), F32)
    bg = jnp.zeros((1, NSA_KV_GROUPS * HEAD_DIM), F32)
    for g in range(NSA_KV_GROUPS):
        wg = wg.at[g * HEAD_DIM:g * HEAD_DIM + per_g].set(
            w_in_t[g_off + g * per_g:g_off + (g + 1) * per_g])
        bg = bg.at[0, g * HEAD_DIM:g * HEAD_DIM + per_g].set(gate_b[0][g * per_g:(g + 1) * per_g])
    wg = wg.astype(BF16)

    h0 = x[0]
    h1 = _ffn_ln(h0, ffn1_w1[0], ffn1_w3[0], ffn1_w2[0], row(ln1_g[0]), row(ln1_b[0]),
                 alpha, ffn_tm, ffn_tf)

    heads, gates = _in_proj(h1, w_in_t, cosf, sinf, wg, bg, scale, 1024)
    tok16 = heads[H_KC:H_KS].reshape(2 * NSA_KV_GROUPS, ncp, CMP_STRIDE * HEAD_DIM)
    kcv, kcv_t = _compress(
        tok16, cmp_pe[0].reshape(2, 1, CMP_BLOCK * HEAD_DIM), cmp_w1[0],
        cmp_b1[0].reshape(2, 1, CMP_HIDDEN), cmp_w2[0],
        cmp_b2[0].reshape(2, 1, HEAD_DIM), cos_c, sin_c)
    mix_a = _dilated(heads, 256)
    part_cmp, sel_off = _nsa_select(heads, kcv, kcv_t, gates, _overlap_t(s), 512)
    mix_b = _nsa(heads, gates, part_cmp, sel_off, 256, 1024)

    h2 = _out_proj_ln(mix_a, mix_b, w_out[0], h1, row(ln2_g[0]), row(ln2_b[0]), alpha, tm)
    h3 = _ffn_ln(h2, ffn2_w1[0], ffn2_w3[0], ffn2_w2[0], row(ln3_g[0]), row(ln3_b[0]),
                 alpha, ffn_tm, ffn_tf)
    return h3.reshape(bsz, s, d)
```

```python
import functools

import jax
import jax.numpy as jnp
import numpy as np
from jax import lax
from jax.experimental import pallas as pl
from jax.experimental.pallas import tpu as pltpu

HEAD_DIM = 128
N_HEADS_DIL = 6
N_HEADS_NSA = 10
NSA_KV_GROUPS = 2
NSA_GROUP = N_HEADS_NSA // NSA_KV_GROUPS
N_NSA_BRANCHES = 3
DIL_PAIRS = ((128, 1), (512, 4), (2048, 16))
CMP_BLOCK = 32
CMP_STRIDE = 16
CMP_HIDDEN = 256
SEL_BLOCK = 64
N_SELECT = 16
NSA_WINDOW = 512
ROPE_THETA = 10000.0
QBLK = 128
LN_EPS = 1e-5
NEG = -1e30
FORCE_BONUS = 1e4
SEL_OFF = -1e9
LOG2_E = 1.4426950408889634
DEPTH = 1

D_DIL = N_HEADS_DIL * HEAD_DIM
D_NSA = N_HEADS_NSA * HEAD_DIM
KV_W = NSA_KV_GROUPS * HEAD_DIM
N_GATES = N_HEADS_NSA * N_NSA_BRANCHES

H_QA, H_KA, H_VA, H_QN = 0, 6, 12, 18
H_KC, H_VC, H_KS, H_VS, H_KW, H_VW = 28, 30, 32, 34, 36, 38
N_PROJ_HEADS = 40
HEADS_PER_PROJ_BLOCK = 8
PROJ_SUB_HEADS = 2
ROW_CHUNKS = 2
CMP_ROW_CHUNK = 128

VMEM_LIMIT_BYTES = 56 * 1024 * 1024
FFN_VMEM_LIMIT_BYTES = 60 * 1024 * 1024

F32 = jnp.float32
BF16 = jnp.bfloat16


def _params(sem, vmem=VMEM_LIMIT_BYTES):
    return pltpu.CompilerParams(dimension_semantics=sem, vmem_limit_bytes=vmem)


def _nt_dot(a, b):
    return lax.dot_general(a, b, (((1,), (1,)), ((), ())), preferred_element_type=F32)


def _layer_norm(y, g, b):
    mu = jnp.mean(y, axis=-1, keepdims=True)
    yc = y - mu
    var = jnp.mean(yc * yc, axis=-1, keepdims=True)
    return yc * lax.rsqrt(var + LN_EPS) * g + b


def _rope_table_kernel(pa_ref, pb_ref, invf_ref, cos_ref, sin_ref):
    pos = (pa_ref[...] + pb_ref[...]) * 0.5
    ang = pos * invf_ref[...]
    lane = lax.broadcasted_iota(jnp.int32, ang.shape, 1)
    cos_ref[...] = jnp.cos(ang)
    sin_ref[...] = jnp.where(lane < HEAD_DIM // 2, -1.0, 1.0) * jnp.sin(ang)


def _rope_tables(pos_a, pos_b, invf, tile):
    n = pos_a.shape[0]
    spec_p = pl.BlockSpec((tile, 1), lambda i: (i, 0))
    spec_t = pl.BlockSpec((tile, HEAD_DIM), lambda i: (i, 0))
    return pl.pallas_call(
        _rope_table_kernel,
        grid=(n // tile,),
        in_specs=[spec_p, spec_p, pl.BlockSpec((1, HEAD_DIM), lambda i: (0, 0))],
        out_specs=[spec_t, spec_t],
        out_shape=[jax.ShapeDtypeStruct((n, HEAD_DIM), F32)] * 2,
        compiler_params=_params(("arbitrary",)),
        name="rope_tables",
    )(pos_a, pos_b, invf)


def _ffn_ln_kernel(h_ref, w1_ref, w3_ref, w2_ref, g_ref, b_ref, o_ref, hb_ref, *, alpha, nf):
    f = pl.program_id(1)

    @pl.when(f == 0)
    def _():
        h = h_ref[...]
        hb_ref[...] = h.astype(BF16)
        o_ref[...] = (2.0 * alpha) * h

    hb = hb_ref[...]
    a = jnp.dot(hb, w1_ref[...].astype(BF16), preferred_element_type=F32)
    b = jnp.dot(hb, w3_ref[...].astype(BF16), preferred_element_type=F32)
    act = (a * jax.nn.sigmoid(a)) * b
    o_ref[...] += jnp.dot(act.astype(BF16), w2_ref[...].astype(BF16),
                          preferred_element_type=F32)

    @pl.when(f == nf - 1)
    def _():
        o_ref[...] = _layer_norm(0.5 * o_ref[...], g_ref[...], b_ref[...])


def _ffn_ln(h, w1, w3, w2, g, b, alpha, tm, tf):
    s, d = h.shape
    dff = w1.shape[1]
    nf = dff // tf
    return pl.pallas_call(
        functools.partial(_ffn_ln_kernel, alpha=alpha, nf=nf),
        grid=(s // tm, nf),
        in_specs=[
            pl.BlockSpec((tm, d), lambda i, f: (i, 0)),
            pl.BlockSpec((d, tf), lambda i, f: (0, f)),
            pl.BlockSpec((d, tf), lambda i, f: (0, f)),
            pl.BlockSpec((tf, d), lambda i, f: (f, 0)),
            pl.BlockSpec((1, d), lambda i, f: (0, 0)),
            pl.BlockSpec((1, d), lambda i, f: (0, 0)),
        ],
        out_specs=pl.BlockSpec((tm, d), lambda i, f: (i, 0)),
        out_shape=jax.ShapeDtypeStruct((s, d), F32),
        scratch_shapes=[pltpu.VMEM((tm, d), BF16)],
        compiler_params=_params(("arbitrary", "arbitrary"), FFN_VMEM_LIMIT_BYTES),
        name="ffn_ln",
    )(h, w1, w3, w2, g, b)


def _in_proj_kernel(h_ref, w_ref, cos_ref, sin_ref, wg_ref, bg_ref, o_ref, gate_ref, hb_ref,
                    *, scale):
    j = pl.program_id(1)
    hpb = HEADS_PER_PROJ_BLOCK
    sub = PROJ_SUB_HEADS

    @pl.when(j == 0)
    def _():
        hb_ref[...] = h_ref[...].astype(BF16)
        gate_ref[...] = jax.nn.sigmoid(_nt_dot(hb_ref[...], wg_ref[...]) + bg_ref[...])

    cos = cos_ref[...]
    sin = sin_ref[...]
    h = hb_ref[...]
    for k0 in range(0, hpb, sub):
        acc = _nt_dot(h, w_ref[k0 * HEAD_DIM:(k0 + sub) * HEAD_DIM, :])
        for k in range(sub):
            hh = j * hpb + k0 + k
            is_q = (hh < H_KA) | ((hh >= H_QN) & (hh < H_KC))
            rope = (is_q | (hh < H_VA) | ((hh >= H_KS) & (hh < H_VS))
                    | ((hh >= H_KW) & (hh < H_VW)))
            x = acc[:, k * HEAD_DIM:(k + 1) * HEAD_DIM]
            xr = x * cos + pltpu.roll(x, HEAD_DIM // 2, 1) * sin
            y = jnp.where(rope, xr, x) * jnp.where(is_q, scale, 1.0).astype(F32)
            o_ref[k0 + k] = y.astype(BF16)


def _in_proj(h, w, cosf, sinf, wg, bg, scale, tm):
    s, d = h.shape
    n = wg.shape[0]
    hpb = HEADS_PER_PROJ_BLOCK
    return pl.pallas_call(
        functools.partial(_in_proj_kernel, scale=scale),
        grid=(s // tm, N_PROJ_HEADS // hpb),
        in_specs=[
            pl.BlockSpec((tm, d), lambda i, j: (i, 0)),
            pl.BlockSpec((hpb * HEAD_DIM, d), lambda i, j: (j, 0)),
            pl.BlockSpec((tm, HEAD_DIM), lambda i, j: (i, 0)),
            pl.BlockSpec((tm, HEAD_DIM), lambda i, j: (i, 0)),
            pl.BlockSpec((n, d), lambda i, j: (0, 0)),
            pl.BlockSpec((1, n), lambda i, j: (0, 0)),
        ],
        out_specs=[pl.BlockSpec((hpb, tm, HEAD_DIM), lambda i, j: (j, i, 0)),
                   pl.BlockSpec((tm, n), lambda i, j: (i, 0))],
        out_shape=[jax.ShapeDtypeStruct((N_PROJ_HEADS, s, HEAD_DIM), BF16),
                   jax.ShapeDtypeStruct((s, n), F32)],
        scratch_shapes=[pltpu.VMEM((tm, d), BF16)],
        compiler_params=_params(("arbitrary", "arbitrary")),
        name="in_proj",
    )(h, w, cosf, sinf, wg, bg)


def _compress_kernel(tok_ref, pe_ref, w1_ref, b1_ref, w2_ref, b2_ref, cos_ref, sin_ref,
                     o_ref, ot_ref):
    j = pl.program_id(0)
    half = CMP_STRIDE * HEAD_DIM
    tok = tok_ref[0].astype(F32)
    pe = pe_ref[0]
    w1 = w1_ref[0].astype(BF16)
    top = (tok + pe[:, :half]).astype(BF16)
    bot = (tok + pe[:, half:]).astype(BF16)
    u = jnp.dot(top, w1[:half], preferred_element_type=F32)
    v = jnp.dot(bot, w1[half:], preferred_element_type=F32)
    nc = u.shape[0]
    hid = u + pltpu.roll(v, nc - 1, 0) + b1_ref[0]
    hid = jax.nn.gelu(hid)
    out = jnp.dot(hid.astype(BF16), w2_ref[0].astype(BF16),
                  preferred_element_type=F32) + b2_ref[0]
    roped = out * cos_ref[...] + pltpu.roll(out, HEAD_DIM // 2, 1) * sin_ref[...]
    out = jnp.where(j == 0, roped, out)
    o_ref[0, 0] = out.astype(BF16)
    ot_ref[0, 0] = out.T.astype(BF16)


def _compress(tok16, pe, w1, b1, w2, b2, cos_c, sin_c):
    nc = tok16.shape[1]
    blk = CMP_BLOCK * HEAD_DIM
    g = NSA_KV_GROUPS
    return pl.pallas_call(
        _compress_kernel,
        grid=(2, g),
        in_specs=[
            pl.BlockSpec((1, nc, CMP_STRIDE * HEAD_DIM), lambda j, gi: (g * j + gi, 0, 0)),
            pl.BlockSpec((1, 1, blk), lambda j, gi: (j, 0, 0)),
            pl.BlockSpec((1, blk, CMP_HIDDEN), lambda j, gi: (j, 0, 0)),
            pl.BlockSpec((1, 1, CMP_HIDDEN), lambda j, gi: (j, 0, 0)),
            pl.BlockSpec((1, CMP_HIDDEN, HEAD_DIM), lambda j, gi: (j, 0, 0)),
            pl.BlockSpec((1, 1, HEAD_DIM), lambda j, gi: (j, 0, 0)),
            pl.BlockSpec((nc, HEAD_DIM), lambda j, gi: (0, 0)),
            pl.BlockSpec((nc, HEAD_DIM), lambda j, gi: (0, 0)),
        ],
        out_specs=[pl.BlockSpec((1, 1, nc, HEAD_DIM), lambda j, gi: (j, gi, 0, 0)),
                   pl.BlockSpec((1, 1, HEAD_DIM, nc), lambda j, gi: (j, gi, 0, 0))],
        out_shape=[jax.ShapeDtypeStruct((2, g, nc, HEAD_DIM), BF16),
                   jax.ShapeDtypeStruct((2, g, HEAD_DIM, nc), BF16)],
        compiler_params=_params(("arbitrary", "arbitrary")),
        name="nsa_compress",
    )(tok16, pe, w1, b1, w2, b2, cos_c, sin_c)


def _dil_kernel(q_ref, k_ref, v_ref, o_ref, kpad, vpad, bias_ref, s_a, s_b, *, tq, padk):
    w = padk + tq
    n_tiles = q_ref.shape[1] // tq

    @pl.when(pl.program_id(0) == 0)
    def _():
        r = lax.broadcasted_iota(jnp.int32, (tq, w), 0)
        c = lax.broadcasted_iota(jnp.int32, (tq, w), 1)
        d = r + padk - c
        cnt = jnp.zeros((tq, w), F32)
        for window, dil in DIL_PAIRS:
            hit = (d >= 0) & (d <= window) & ((d & (dil - 1)) == 0)
            cnt = cnt + jnp.where(hit, 1.0, 0.0)
        bias_ref[...] = jnp.where(cnt > 0.0, jnp.log2(jnp.maximum(cnt, 1.0)), NEG)

    kpad[0:padk, :] = jnp.zeros((padk, HEAD_DIM), BF16)
    vpad[0:padk, :] = jnp.zeros((padk, HEAD_DIM), BF16)
    kpad[padk:, :] = k_ref[0]
    vpad[padk:, :] = v_ref[0]
    col = lax.broadcasted_iota(jnp.int32, (1, w), 1)

    def scores(t):
        q0 = pl.multiple_of(t * tq, tq)
        s = _nt_dot(q_ref[0, pl.ds(q0, tq), :], kpad[pl.ds(q0, w), :]) + bias_ref[...]
        return jnp.where(col >= padk - q0, s, NEG)

    def finish(s_ref, t):
        q0 = pl.multiple_of(t * tq, tq)
        s = s_ref[...]
        p = jnp.exp2(s - jnp.max(s, axis=-1, keepdims=True))
        l = jnp.sum(p, axis=-1, keepdims=True)
        o = jnp.dot(p.astype(BF16), vpad[pl.ds(q0, w), :], preferred_element_type=F32)
        o_ref[pl.ds(q0, tq), :] = (o / l).astype(BF16)

    def pair(pi, _):
        t = 2 * pi
        s_b[...] = scores(t + 1)
        finish(s_a, t)
        s_a[...] = scores(t + 2)
        finish(s_b, t + 1)
        return 0

    s_a[...] = scores(0)
    lax.fori_loop(0, n_tiles // 2 - 1, pair, 0)
    s_b[...] = scores(n_tiles - 1)
    finish(s_a, n_tiles - 2)
    finish(s_b, n_tiles - 1)


def _dilated(heads, tq):
    _, s, _ = heads.shape
    padk = max(wd for wd, _ in DIL_PAIRS)
    assert (s // tq) % 2 == 0
    return pl.pallas_call(
        functools.partial(_dil_kernel, tq=tq, padk=padk),
        grid=(N_HEADS_DIL,),
        in_specs=[
            pl.BlockSpec((1, s, HEAD_DIM), lambda h: (H_QA + h, 0, 0)),
            pl.BlockSpec((1, s, HEAD_DIM), lambda h: (H_KA + h, 0, 0)),
            pl.BlockSpec((1, s, HEAD_DIM), lambda h: (H_VA + h, 0, 0)),
        ],
        out_specs=pl.BlockSpec((s, HEAD_DIM), lambda h: (0, h)),
        out_shape=jax.ShapeDtypeStruct((s, D_DIL), BF16),
        scratch_shapes=[pltpu.VMEM((padk + s, HEAD_DIM), BF16),
                        pltpu.VMEM((padk + s, HEAD_DIM), BF16),
                        pltpu.VMEM((tq, padk + tq), F32),
                        pltpu.VMEM((tq, padk + tq), F32),
                        pltpu.VMEM((tq, padk + tq), F32)],
        compiler_params=_params(("arbitrary",)),
        name="dilated_attn",
    )(heads, heads, heads)


def _add_per_head(s, bias, hg):
    tq = bias.shape[0]
    return jnp.concatenate([s[u * tq:(u + 1) * tq] + bias for u in range(hg)], axis=0)


def _nsa_select_kernel(*refs, nsp, tq, k_sel):
    q_refs = refs[:NSA_GROUP]
    kc_ref, vct_ref, gate_ref, ovt_ref, cmp_ref, sel_ref, imp_ref = refs[NSA_GROUP:]
    t0 = pl.program_id(1) * tq
    hg = NSA_GROUP
    ncp = kc_ref.shape[2]
    gate = gate_ref[...]

    def cmp_branch(n_rows):
        kc = kc_ref[0, 0, :n_rows, :]
        vct = vct_ref[0, 0, :, :n_rows]
        ovt = ovt_ref[:, :n_rows]
        n_io = lax.broadcasted_iota(jnp.int32, (n_rows, tq), 0)
        t_io = t0 + lax.broadcasted_iota(jnp.int32, (n_rows, tq), 1)
        cmask = (n_io * CMP_STRIDE + (CMP_BLOCK - 1) <= t_io) & (n_io < ncp - 1)
        imp = jnp.zeros((nsp, tq), F32)
        for u in range(hg):
            st = jnp.where(cmask, _nt_dot(kc, q_refs[u][0]), NEG)
            m = jnp.maximum(jnp.max(st, axis=0, keepdims=True), 0.1 * NEG)
            e = jnp.exp2(st - m)
            r = 1.0 / jnp.maximum(jnp.sum(e, axis=0, keepdims=True), 1e-30)
            eb = e.astype(BF16)
            imp = imp + jnp.dot(ovt, eb, preferred_element_type=F32) * r
            o_cmp = (jnp.dot(vct, eb, preferred_element_type=F32) * r).T
            c = N_NSA_BRANCHES * u
            cmp_ref[:, u * HEAD_DIM:(u + 1) * HEAD_DIM] = gate[:, c:c + 1] * o_cmp
        imp_ref[...] = imp

    n_var = ncp // CMP_ROW_CHUNK
    need = (t0 + tq) // CMP_STRIDE
    var = jnp.minimum((need + CMP_ROW_CHUNK - 1) // CMP_ROW_CHUNK, n_var) - 1
    for v in range(n_var):
        pl.when(var == v)(functools.partial(cmp_branch, CMP_ROW_CHUNK * (v + 1)))
    imp = imp_ref[...]

    j_io = lax.broadcasted_iota(jnp.int32, (nsp, tq), 0)
    t_sel = t0 + lax.broadcasted_iota(jnp.int32, (nsp, tq), 1)
    valid = j_io * SEL_BLOCK <= t_sel
    cur = t_sel // SEL_BLOCK
    forced = (j_io == 0) | (j_io == cur) | (j_io == cur - 1)
    score = jnp.where(valid & jnp.logical_not(forced), imp, NEG)
    for _ in range(k_sel - 3):
        m = jnp.max(score, axis=0, keepdims=True)
        first = jnp.min(jnp.where(score == m, j_io, nsp), axis=0, keepdims=True)
        score = jnp.where(j_io == first, -jnp.inf, score)
    sel_t = jnp.where(valid & (forced | (score == -jnp.inf)), 1.0, 0.0)
    sel_ref[0] = jnp.where(sel_t.T > 0.0, 0.0, SEL_OFF).astype(BF16)


def _nsa_select(heads, kcv, vct, gates, ovt, tq):
    _, s, _ = heads.shape
    ncp = kcv.shape[2]
    nsp = ovt.shape[0]
    hg = NSA_GROUP
    k_sel = min(N_SELECT, s // SEL_BLOCK)
    assert k_sel >= 3 and ncp % CMP_ROW_CHUNK == 0
    q_map = lambda u, g, qb: (H_QN + hg * g + u, qb, 0)
    return pl.pallas_call(
        functools.partial(_nsa_select_kernel, nsp=nsp, tq=tq, k_sel=k_sel),
        grid=(NSA_KV_GROUPS, s // tq),
        in_specs=[
            *[pl.BlockSpec((1, tq, HEAD_DIM), functools.partial(q_map, u)) for u in range(hg)],
            pl.BlockSpec((1, 1, ncp, HEAD_DIM), lambda g, qb: (0, g, 0, 0)),
            pl.BlockSpec((1, 1, HEAD_DIM, ncp), lambda g, qb: (1, g, 0, 0)),
            pl.BlockSpec((tq, HEAD_DIM), lambda g, qb: (qb, g)),
            pl.BlockSpec((nsp, ncp), lambda g, qb: (0, 0)),
        ],
        out_specs=[pl.BlockSpec((tq, hg * HEAD_DIM), lambda g, qb: (qb, g)),
                   pl.BlockSpec((1, tq, nsp), lambda g, qb: (g, qb, 0))],
        out_shape=[jax.ShapeDtypeStruct((s, D_NSA), F32),
                   jax.ShapeDtypeStruct((NSA_KV_GROUPS, s, nsp), BF16)],
        scratch_shapes=[pltpu.VMEM((nsp, tq), F32)],
        compiler_params=_params(("arbitrary", "arbitrary")),
        name="nsa_select",
    )(*([heads] * hg), kcv, vct, gates, ovt)


def _nsa_kernel(*refs, s_len, nsp, tq, kt):
    q_refs = refs[:NSA_GROUP]
    (ks_ref, vs_ref, kw_ref, vw_ref, gate_ref, cmp_ref, sel_ref,
     o_ref, kaug, vaug, vwaug, cbias, wbias, s_a, s_b, m_ref, acc_ref,
     part_ref) = refs[NSA_GROUP:]
    g = pl.program_id(0)
    qb = pl.program_id(1)
    t0 = qb * tq
    hg = NSA_GROUP
    rows = hg * tq
    ww = NSA_WINDOW + tq

    @pl.when((g == 0) & (qb == 0))
    def _():
        for r in range(kt // tq):
            d = (lax.broadcasted_iota(jnp.int32, (tq, kt), 0) + r * tq
                 - lax.broadcasted_iota(jnp.int32, (tq, kt), 1))
            cbias[r] = jnp.where(d >= 0, 0.0, NEG)
        for w in range(NSA_WINDOW // tq + 1):
            d = (lax.broadcasted_iota(jnp.int32, (tq, ww), 0) + w * tq
                 - lax.broadcasted_iota(jnp.int32, (tq, ww), 1))
            wbias[w] = jnp.where((d >= 0) & (d < NSA_WINDOW), 0.0, NEG)

    @pl.when(qb == 0)
    def _():
        kaug[:, 0:HEAD_DIM] = ks_ref[0]
        key = lax.broadcasted_iota(jnp.int32, (s_len, nsp), 0)
        blk = lax.broadcasted_iota(jnp.int32, (s_len, nsp), 1)
        kaug[:, HEAD_DIM:] = jnp.where(blk == key // SEL_BLOCK, 1.0, 0.0).astype(BF16)
        ones = jnp.ones((s_len, HEAD_DIM), BF16)
        vaug[:, 0:HEAD_DIM] = vs_ref[0]
        vaug[:, HEAD_DIM:] = ones
        vwaug[:, 0:HEAD_DIM] = vw_ref[0]
        vwaug[:, HEAD_DIM:] = ones

    gate = gate_ref[...]
    q5 = jnp.concatenate([r[0] for r in q_refs], axis=0)

    qaug = jnp.concatenate([q5, jnp.concatenate([sel_ref[0]] * hg, axis=0)], axis=1)

    w0 = pl.multiple_of(jnp.maximum(t0 - NSA_WINDOW, 0), tq)
    sw = _nt_dot(q5, kw_ref[0, pl.ds(w0, ww), :])
    sw = _add_per_head(sw, wbias[jnp.minimum(qb, NSA_WINDOW // tq)], hg)
    pw = jnp.exp2(sw - jnp.max(sw, axis=-1, keepdims=True))
    acc_w = jnp.dot(pw.astype(BF16), vwaug[pl.ds(w0, ww), :], preferred_element_type=F32)
    o_win = acc_w[:, :HEAD_DIM] / acc_w[:, HEAD_DIM:]
    for u in range(hg):
        c = N_NSA_BRANCHES * u + 2
        cols = slice(u * HEAD_DIM, (u + 1) * HEAD_DIM)
        part_ref[:, cols] = cmp_ref[:, cols] + gate[:, c:c + 1] * o_win[u * tq:(u + 1) * tq]

    def sel_scores(kti):
        k0 = pl.multiple_of(kti * kt, kt)
        return _nt_dot(qaug, kaug[pl.ds(k0, kt), :])

    def sel_update(s_ref, kti, diagonal):
        s = s_ref[...]
        if diagonal:
            s = _add_per_head(s, cbias[qb % (kt // tq)], hg)
        k0 = pl.multiple_of(kti * kt, kt)
        m_i = m_ref[...]
        m_new = jnp.maximum(m_i, jnp.max(s, axis=-1, keepdims=True))
        p = jnp.exp2(s - m_new)
        pv = jnp.dot(p.astype(BF16), vaug[pl.ds(k0, kt), :], preferred_element_type=F32)
        acc_ref[...] = jnp.exp2(m_i - m_new) * acc_ref[...] + pv
        m_ref[...] = m_new

    def sel_pair(pi, _):
        t = 2 * pi
        s_b[...] = sel_scores(t + 1)
        sel_update(s_a, t, False)
        s_a[...] = sel_scores(t + 2)
        sel_update(s_b, t + 1, False)
        return 0

    last = (t0 + tq + kt - 1) // kt - 1
    m_ref[...] = jnp.full((rows, 1), NEG, F32)
    acc_ref[...] = jnp.zeros((rows, 2 * HEAD_DIM), F32)
    s_a[...] = sel_scores(0)
    lax.fori_loop(0, last // 2, sel_pair, 0)
    t_even = 2 * (last // 2)

    @pl.when(last % 2 == 1)
    def _():
        s_b[...] = sel_scores(t_even + 1)
        sel_update(s_a, t_even, False)
        sel_update(s_b, t_even + 1, True)

    @pl.when(last % 2 == 0)
    def _():
        sel_update(s_a, t_even, True)

    gate = gate_ref[...]
    for u in range(hg):
        c = N_NSA_BRANCHES * u + 1
        acc_u = acc_ref[u * tq:(u + 1) * tq, :]
        o_slc = acc_u[:, :HEAD_DIM] / acc_u[:, HEAD_DIM:]
        cols = slice(u * HEAD_DIM, (u + 1) * HEAD_DIM)
        o_ref[:, cols] = (part_ref[:, cols] + gate[:, c:c + 1] * o_slc).astype(BF16)


def _nsa(heads, gates, part_cmp, sel_off, tq, kt):
    _, s, _ = heads.shape
    nsp = sel_off.shape[2]
    hg = NSA_GROUP
    assert kt % tq == 0 and NSA_WINDOW % tq == 0 and s % kt == 0
    q_map = lambda u, g, qb: (H_QN + hg * g + u, qb, 0)
    full = lambda hbase: pl.BlockSpec((1, s, HEAD_DIM), lambda g, qb: (hbase + g, 0, 0),
                                      pipeline_mode=pl.Buffered(1))
    return pl.pallas_call(
        functools.partial(_nsa_kernel, s_len=s, nsp=nsp, tq=tq, kt=kt),
        grid=(NSA_KV_GROUPS, s // tq),
        in_specs=[
            *[pl.BlockSpec((1, tq, HEAD_DIM), functools.partial(q_map, u)) for u in range(hg)],
            full(H_KS), full(H_VS), full(H_KW), full(H_VW),
            pl.BlockSpec((tq, HEAD_DIM), lambda g, qb: (qb, g)),
            pl.BlockSpec((tq, hg * HEAD_DIM), lambda g, qb: (qb, g)),
            pl.BlockSpec((1, tq, nsp), lambda g, qb: (g, qb, 0)),
        ],
        out_specs=pl.BlockSpec((tq, hg * HEAD_DIM), lambda g, qb: (qb, g)),
        out_shape=jax.ShapeDtypeStruct((s, D_NSA), BF16),
        scratch_shapes=[pltpu.VMEM((s, HEAD_DIM + nsp), BF16),
                        pltpu.VMEM((s, 2 * HEAD_DIM), BF16),
                        pltpu.VMEM((s, 2 * HEAD_DIM), BF16),
                        pltpu.VMEM((kt // tq, tq, kt), F32),
                        pltpu.VMEM((NSA_WINDOW // tq + 1, tq, NSA_WINDOW + tq), F32),
                        pltpu.VMEM((hg * tq, kt), F32),
                        pltpu.VMEM((hg * tq, kt), F32),
                        pltpu.VMEM((hg * tq, 1), F32),
                        pltpu.VMEM((hg * tq, 2 * HEAD_DIM), F32),
                        pltpu.VMEM((tq, hg * HEAD_DIM), F32)],
        compiler_params=_params(("arbitrary", "arbitrary")),
        name="nsa_attn",
    )(*([heads] * hg), heads, heads, heads, heads, gates, part_cmp, sel_off)


def _out_proj_ln_kernel(a_ref, b_ref, w_ref, h_ref, g_ref, be_ref, o_ref, wb16, *, alpha):
    @pl.when(pl.program_id(0) == 0)
    def _():
        wb16[...] = w_ref[...].astype(BF16)

    tm = o_ref.shape[0]
    rc = tm // ROW_CHUNKS
    for c in range(ROW_CHUNKS):
        r = slice(c * rc, (c + 1) * rc)
        mix = (jnp.dot(a_ref[r, :], wb16[:D_DIL, :], preferred_element_type=F32)
               + jnp.dot(b_ref[r, :], wb16[D_DIL:, :], preferred_element_type=F32))
        o_ref[r, :] = _layer_norm(alpha * h_ref[r, :] + mix, g_ref[...], be_ref[...])


def _out_proj_ln(mix_a, mix_b, w, h, g, b, alpha, tm):
    s, d = h.shape
    return pl.pallas_call(
        functools.partial(_out_proj_ln_kernel, alpha=alpha),
        grid=(s // tm,),
        in_specs=[
            pl.BlockSpec((tm, D_DIL), lambda i: (i, 0)),
            pl.BlockSpec((tm, D_NSA), lambda i: (i, 0)),
            pl.BlockSpec((d, d), lambda i: (0, 0), pipeline_mode=pl.Buffered(1)),
            pl.BlockSpec((tm, d), lambda i: (i, 0)),
            pl.BlockSpec((1, d), lambda i: (0, 0)),
            pl.BlockSpec((1, d), lambda i: (0, 0)),
        ],
        out_specs=pl.BlockSpec((tm, d), lambda i: (i, 0)),
        out_shape=jax.ShapeDtypeStruct((s, d), F32),
        scratch_shapes=[pltpu.VMEM((d, d), BF16)],
        compiler_params=_params(("arbitrary",)),
        name="out_proj_ln",
    )(mix_a, mix_b, w, h, g, b)


def _overlap_t(s):
    n_cmp = (s - CMP_BLOCK) // CMP_STRIDE + 1
    n_sel = s // SEL_BLOCK
    ncp = s // CMP_STRIDE
    nsp = -(-n_sel // HEAD_DIM) * HEAD_DIM
    c_lo = np.arange(n_cmp) * CMP_STRIDE
    c_hi = c_lo + CMP_BLOCK - 1
    s_lo = (np.arange(n_sel) * SEL_BLOCK)[:, None]
    ov = np.zeros((nsp, ncp), np.float32)
    ov[:n_sel, :n_cmp] = (c_lo[None, :] <= s_lo + SEL_BLOCK - 1) & (c_hi[None, :] >= s_lo)
    return jnp.asarray(ov, BF16)


def kernel(x, positions, ln1_g, ln1_b, ffn1_w1, ffn1_w3, ffn1_w2, w_in, gate_b, cmp_pe, cmp_w1,
           cmp_b1, cmp_w2, cmp_b2, w_out, ln2_g, ln2_b, ffn2_w1, ffn2_w3, ffn2_w2, ln3_g, ln3_b):
    bsz, s, d = x.shape
    assert bsz == 1 and d == (N_HEADS_DIL + N_HEADS_NSA) * HEAD_DIM
    assert s % 512 == 0 and s >= NSA_WINDOW + QBLK
    alpha = (2.0 * DEPTH) ** 0.25
    scale = HEAD_DIM ** -0.5 * LOG2_E
    row = lambda v: v.reshape(1, -1)
    tm = 512
    ffn_tm, ffn_tf = 1024, 256

    inv_freq = ROPE_THETA ** (-jnp.arange(0, HEAD_DIM, 2, dtype=F32) / HEAD_DIM)
    invf = jnp.concatenate([inv_freq, inv_freq]).reshape(1, HEAD_DIM)
    pos = positions[0].astype(F32).reshape(s, 1)
    cosf, sinf = _rope_tables(pos, pos, invf, 512)
    ncp = s // CMP_STRIDE
    n_cmp = (s - CMP_BLOCK) // CMP_STRIDE + 1
    pos_lo = jnp.pad(pos[0:n_cmp * CMP_STRIDE:CMP_STRIDE], ((0, ncp - n_cmp), (0, 0)))
    pos_hi = jnp.pad(pos[CMP_BLOCK - 1::CMP_STRIDE][:n_cmp], ((0, ncp - n_cmp), (0, 0)))
    cos_c, sin_c = _rope_tables(pos_lo, pos_hi, invf, ncp)

    g_off = 3 * D_DIL + D_NSA + 6 * KV_W
    per_g = NSA_GROUP * N_NSA_BRANCHES
    w_in_t = w_in[0].T.astype(BF16)
    wg = jnp.zeros((NSA_KV_GROUPS * HEAD_DIM, d), BF16)
    bg = jnp.zeros((1, NSA_KV_GROUPS * HEAD_DIM), F32)
    for g in range(NSA_KV_GROUPS):
        wg = wg.at[g * HEAD_DIM:g * HEAD_DIM + per_g].set(
            w_in_t[g_off + g * per_g:g_off + (g + 1) * per_g])
        bg = bg.at[0, g * HEAD_DIM:g * HEAD_DIM + per_g].set(gate_b[0][g * per_g:(g + 1) * per_g])

    h0 = x[0]
    h1 = _ffn_ln(h0, ffn1_w1[0], ffn1_w3[0], ffn1_w2[0], row(ln1_g[0]), row(ln1_b[0]),
                 alpha, ffn_tm, ffn_tf)

    heads, gates = _in_proj(h1, w_in_t, cosf, sinf, wg, bg, scale, 1024)
    tok16 = heads[H_KC:H_KS].reshape(2 * NSA_KV_GROUPS, ncp, CMP_STRIDE * HEAD_DIM)
    kcv, kcv_t = _compress(
        tok16, cmp_pe[0].reshape(2, 1, CMP_BLOCK * HEAD_DIM), cmp_w1[0],
        cmp_b1[0].reshape(2, 1, CMP_HIDDEN), cmp_w2[0],
        cmp_b2[0].reshape(2, 1, HEAD_DIM), cos_c, sin_c)
    mix_a = _dilated(heads, 256)
    part_cmp, sel_off = _nsa_select(heads, kcv, kcv_t, gates, _overlap_t(s), 512)
    mix_b = _nsa(heads, gates, part_cmp, sel_off, 256, 1024)

    h2 = _out_proj_ln(mix_a, mix_b, w_out[0], h1, row(ln2_g[0]), row(ln2_b[0]), alpha, tm)
    h3 = _ffn_ln(h2, ffn2_w1[0], ffn2_w3[0], ffn2_w2[0], row(ln3_g[0]), row(ln3_b[0]),
                 alpha, ffn_tm, ffn_tf)
    return h3.reshape(bsz, s, d)
```

```python
import functools

import jax
import jax.numpy as jnp
import numpy as np
from jax import lax
from jax.experimental import pallas as pl
from jax.experimental.pallas import tpu as pltpu

HEAD_DIM = 128
N_HEADS_DIL = 6
N_HEADS_NSA = 10
NSA_KV_GROUPS = 2
NSA_GROUP = N_HEADS_NSA // NSA_KV_GROUPS
N_NSA_BRANCHES = 3
DIL_PAIRS = ((128, 1), (512, 4), (2048, 16))
CMP_BLOCK = 32
CMP_STRIDE = 16
CMP_HIDDEN = 256
SEL_BLOCK = 64
N_SELECT = 16
NSA_WINDOW = 512
ROPE_THETA = 10000.0
LN_EPS = 1e-5
NEG = -1e30
SEL_OFF = -1e9
LOG2_E = 1.4426950408889634
DEPTH = 1

D_DIL = N_HEADS_DIL * HEAD_DIM
D_NSA = N_HEADS_NSA * HEAD_DIM
KV_W = NSA_KV_GROUPS * HEAD_DIM

H_QA, H_KA, H_VA, H_QN = 0, 6, 12, 18
H_KC, H_VC, H_KS, H_VS, H_KW, H_VW = 28, 30, 32, 34, 36, 38
N_PROJ_HEADS = 40
HEADS_PER_PROJ_BLOCK = 8
PROJ_SUB_HEADS = 2
ROW_CHUNKS = 2
CMP_ROW_CHUNK = 128

ROPE_TILE = 512
FFN_ROWS, FFN_HIDDEN_TILE = 1024, 256
PROJ_ROWS = 1024
OUT_PROJ_ROWS = 512
DIL_Q_TILE = 256
SEL_Q_TILE = 512
NSA_Q_TILE, NSA_KEY_TILE = 256, 1024

VMEM_LIMIT_BYTES = 56 * 1024 * 1024
FFN_VMEM_LIMIT_BYTES = 60 * 1024 * 1024

F32 = jnp.float32
BF16 = jnp.bfloat16


def _params(sem, vmem=VMEM_LIMIT_BYTES):
    return pltpu.CompilerParams(dimension_semantics=sem, vmem_limit_bytes=vmem)


def _nt_dot(a, b):
    return lax.dot_general(a, b, (((1,), (1,)), ((), ())), preferred_element_type=F32)


def _layer_norm(y, g, b):
    mu = jnp.mean(y, axis=-1, keepdims=True)
    yc = y - mu
    var = jnp.mean(yc * yc, axis=-1, keepdims=True)
    return yc * lax.rsqrt(var + LN_EPS) * g + b


def _rope_table_kernel(pa_ref, pb_ref, invf_ref, cos_ref, sin_ref):
    pos = (pa_ref[...] + pb_ref[...]) * 0.5
    ang = pos * invf_ref[...]
    lane = lax.broadcasted_iota(jnp.int32, ang.shape, 1)
    cos_ref[...] = jnp.cos(ang)
    sin_ref[...] = jnp.where(lane < HEAD_DIM // 2, -1.0, 1.0) * jnp.sin(ang)


def _rope_tables(pos_a, pos_b, invf, tile):
    n = pos_a.shape[0]
    spec_p = pl.BlockSpec((tile, 1), lambda i: (i, 0))
    spec_t = pl.BlockSpec((tile, HEAD_DIM), lambda i: (i, 0))
    return pl.pallas_call(
        _rope_table_kernel,
        grid=(n // tile,),
        in_specs=[spec_p, spec_p, pl.BlockSpec((1, HEAD_DIM), lambda i: (0, 0))],
        out_specs=[spec_t, spec_t],
        out_shape=[jax.ShapeDtypeStruct((n, HEAD_DIM), F32)] * 2,
        compiler_params=_params(("arbitrary",)),
        name="rope_tables",
    )(pos_a, pos_b, invf)


def _ffn_ln_kernel(h_ref, w1_ref, w3_ref, w2_ref, g_ref, b_ref, o_ref, hb_ref, *, alpha, nf):
    f = pl.program_id(1)

    @pl.when(f == 0)
    def _():
        h = h_ref[...]
        hb_ref[...] = h.astype(BF16)
        o_ref[...] = (2.0 * alpha) * h

    hb = hb_ref[...]
    a = jnp.dot(hb, w1_ref[...].astype(BF16), preferred_element_type=F32)
    b = jnp.dot(hb, w3_ref[...].astype(BF16), preferred_element_type=F32)
    act = (a * jax.nn.sigmoid(a)) * b
    o_ref[...] += jnp.dot(act.astype(BF16), w2_ref[...].astype(BF16),
                          preferred_element_type=F32)

    @pl.when(f == nf - 1)
    def _():
        o_ref[...] = _layer_norm(0.5 * o_ref[...], g_ref[...], b_ref[...])


def _ffn_ln(h, w1, w3, w2, g, b, alpha, tm, tf):
    s, d = h.shape
    dff = w1.shape[1]
    nf = dff // tf
    return pl.pallas_call(
        functools.partial(_ffn_ln_kernel, alpha=alpha, nf=nf),
        grid=(s // tm, nf),
        in_specs=[
            pl.BlockSpec((tm, d), lambda i, f: (i, 0)),
            pl.BlockSpec((d, tf), lambda i, f: (0, f)),
            pl.BlockSpec((d, tf), lambda i, f: (0, f)),
            pl.BlockSpec((tf, d), lambda i, f: (f, 0)),
            pl.BlockSpec((1, d), lambda i, f: (0, 0)),
            pl.BlockSpec((1, d), lambda i, f: (0, 0)),
        ],
        out_specs=pl.BlockSpec((tm, d), lambda i, f: (i, 0)),
        out_shape=jax.ShapeDtypeStruct((s, d), F32),
        scratch_shapes=[pltpu.VMEM((tm, d), BF16)],
        compiler_params=_params(("arbitrary", "arbitrary"), FFN_VMEM_LIMIT_BYTES),
        name="ffn_ln",
    )(h, w1, w3, w2, g, b)


def _in_proj_kernel(h_ref, w_ref, cos_ref, sin_ref, wg_ref, bg_ref, o_ref, gate_ref, hb_ref,
                    *, scale):
    j = pl.program_id(1)
    hpb = HEADS_PER_PROJ_BLOCK
    sub = PROJ_SUB_HEADS

    @pl.when(j == 0)
    def _():
        hb_ref[...] = h_ref[...].astype(BF16)
        gate_ref[...] = jax.nn.sigmoid(_nt_dot(hb_ref[...], wg_ref[...]) + bg_ref[...])

    cos = cos_ref[...]
    sin = sin_ref[...]
    h = hb_ref[...]
    for k0 in range(0, hpb, sub):
        acc = _nt_dot(h, w_ref[k0 * HEAD_DIM:(k0 + sub) * HEAD_DIM, :])
        for k in range(sub):
            hh = j * hpb + k0 + k
            is_q = (hh < H_KA) | ((hh >= H_QN) & (hh < H_KC))
            rope = (is_q | (hh < H_VA) | ((hh >= H_KS) & (hh < H_VS))
                    | ((hh >= H_KW) & (hh < H_VW)))
            x = acc[:, k * HEAD_DIM:(k + 1) * HEAD_DIM]
            xr = x * cos + pltpu.roll(x, HEAD_DIM // 2, 1) * sin
            y = jnp.where(rope, xr, x) * jnp.where(is_q, scale, 1.0).astype(F32)
            o_ref[k0 + k] = y.astype(BF16)


def _in_proj(h, w, cosf, sinf, wg, bg, scale, tm):
    s, d = h.shape
    n = wg.shape[0]
    hpb = HEADS_PER_PROJ_BLOCK
    return pl.pallas_call(
        functools.partial(_in_proj_kernel, scale=scale),
        grid=(s // tm, N_PROJ_HEADS // hpb),
        in_specs=[
            pl.BlockSpec((tm, d), lambda i, j: (i, 0)),
            pl.BlockSpec((hpb * HEAD_DIM, d), lambda i, j: (j, 0)),
            pl.BlockSpec((tm, HEAD_DIM), lambda i, j: (i, 0)),
            pl.BlockSpec((tm, HEAD_DIM), lambda i, j: (i, 0)),
            pl.BlockSpec((n, d), lambda i, j: (0, 0)),
            pl.BlockSpec((1, n), lambda i, j: (0, 0)),
        ],
        out_specs=[pl.BlockSpec((hpb, tm, HEAD_DIM), lambda i, j: (j, i, 0)),
                   pl.BlockSpec((tm, n), lambda i, j: (i, 0))],
        out_shape=[jax.ShapeDtypeStruct((N_PROJ_HEADS, s, HEAD_DIM), BF16),
                   jax.ShapeDtypeStruct((s, n), F32)],
        scratch_shapes=[pltpu.VMEM((tm, d), BF16)],
        compiler_params=_params(("arbitrary", "arbitrary")),
        name="in_proj",
    )(h, w, cosf, sinf, wg, bg)


def _compress_kernel(tok_ref, pe_ref, w1_ref, b1_ref, w2_ref, b2_ref, cos_ref, sin_ref,
                     o_ref, ot_ref):
    j = pl.program_id(0)
    half = CMP_STRIDE * HEAD_DIM
    tok = tok_ref[0].astype(F32)
    pe = pe_ref[0]
    w1 = w1_ref[0].astype(BF16)
    top = (tok + pe[:, :half]).astype(BF16)
    bot = (tok + pe[:, half:]).astype(BF16)
    u = jnp.dot(top, w1[:half], preferred_element_type=F32)
    v = jnp.dot(bot, w1[half:], preferred_element_type=F32)
    nc = u.shape[0]
    hid = u + pltpu.roll(v, nc - 1, 0) + b1_ref[0]
    hid = jax.nn.gelu(hid)
    out = jnp.dot(hid.astype(BF16), w2_ref[0].astype(BF16),
                  preferred_element_type=F32) + b2_ref[0]
    roped = out * cos_ref[...] + pltpu.roll(out, HEAD_DIM // 2, 1) * sin_ref[...]
    out = jnp.where(j == 0, roped, out)
    o_ref[0, 0] = out.astype(BF16)
    ot_ref[0, 0] = out.T.astype(BF16)


def _compress(tok16, pe, w1, b1, w2, b2, cos_c, sin_c):
    nc = tok16.shape[1]
    blk = CMP_BLOCK * HEAD_DIM
    g = NSA_KV_GROUPS
    return pl.pallas_call(
        _compress_kernel,
        grid=(2, g),
        in_specs=[
            pl.BlockSpec((1, nc, CMP_STRIDE * HEAD_DIM), lambda j, gi: (g * j + gi, 0, 0)),
            pl.BlockSpec((1, 1, blk), lambda j, gi: (j, 0, 0)),
            pl.BlockSpec((1, blk, CMP_HIDDEN), lambda j, gi: (j, 0, 0)),
            pl.BlockSpec((1, 1, CMP_HIDDEN), lambda j, gi: (j, 0, 0)),
            pl.BlockSpec((1, CMP_HIDDEN, HEAD_DIM), lambda j, gi: (j, 0, 0)),
            pl.BlockSpec((1, 1, HEAD_DIM), lambda j, gi: (j, 0, 0)),
            pl.BlockSpec((nc, HEAD_DIM), lambda j, gi: (0, 0)),
            pl.BlockSpec((nc, HEAD_DIM), lambda j, gi: (0, 0)),
        ],
        out_specs=[pl.BlockSpec((1, 1, nc, HEAD_DIM), lambda j, gi: (j, gi, 0, 0)),
                   pl.BlockSpec((1, 1, HEAD_DIM, nc), lambda j, gi: (j, gi, 0, 0))],
        out_shape=[jax.ShapeDtypeStruct((2, g, nc, HEAD_DIM), BF16),
                   jax.ShapeDtypeStruct((2, g, HEAD_DIM, nc), BF16)],
        compiler_params=_params(("arbitrary", "arbitrary")),
        name="nsa_compress",
    )(tok16, pe, w1, b1, w2, b2, cos_c, sin_c)


def _dil_kernel(q_ref, k_ref, v_ref, o_ref, kpad, vpad, bias_ref, s_a, s_b, *, tq, padk):
    w = padk + tq
    n_tiles = q_ref.shape[1] // tq

    @pl.when(pl.program_id(0) == 0)
    def _():
        r = lax.broadcasted_iota(jnp.int32, (tq, w), 0)
        c = lax.broadcasted_iota(jnp.int32, (tq, w), 1)
        d = r + padk - c
        cnt = jnp.zeros((tq, w), F32)
        for window, dil in DIL_PAIRS:
            hit = (d >= 0) & (d <= window) & ((d & (dil - 1)) == 0)
            cnt = cnt + jnp.where(hit, 1.0, 0.0)
        bias_ref[...] = jnp.where(cnt > 0.0, jnp.log2(jnp.maximum(cnt, 1.0)), NEG)

    kpad[0:padk, :] = jnp.zeros((padk, HEAD_DIM), BF16)
    vpad[0:padk, :] = jnp.zeros((padk, HEAD_DIM), BF16)
    kpad[padk:, :] = k_ref[0]
    vpad[padk:, :] = v_ref[0]
    col = lax.broadcasted_iota(jnp.int32, (1, w), 1)

    def scores(t):
        q0 = pl.multiple_of(t * tq, tq)
        s = _nt_dot(q_ref[0, pl.ds(q0, tq), :], kpad[pl.ds(q0, w), :]) + bias_ref[...]
        return jnp.where(col >= padk - q0, s, NEG)

    def finish(s_ref, t):
        q0 = pl.multiple_of(t * tq, tq)
        s = s_ref[...]
        p = jnp.exp2(s - jnp.max(s, axis=-1, keepdims=True))
        l = jnp.sum(p, axis=-1, keepdims=True)
        o = jnp.dot(p.astype(BF16), vpad[pl.ds(q0, w), :], preferred_element_type=F32)
        o_ref[pl.ds(q0, tq), :] = (o / l).astype(BF16)

    def pair(pi, _):
        t = 2 * pi
        s_b[...] = scores(t + 1)
        finish(s_a, t)
        s_a[...] = scores(t + 2)
        finish(s_b, t + 1)
        return 0

    s_a[...] = scores(0)
    lax.fori_loop(0, n_tiles // 2 - 1, pair, 0)
    s_b[...] = scores(n_tiles - 1)
    finish(s_a, n_tiles - 2)
    finish(s_b, n_tiles - 1)


def _dilated(heads, tq):
    _, s, _ = heads.shape
    padk = max(wd for wd, _ in DIL_PAIRS)
    assert (s // tq) % 2 == 0
    return pl.pallas_call(
        functools.partial(_dil_kernel, tq=tq, padk=padk),
        grid=(N_HEADS_DIL,),
        in_specs=[
            pl.BlockSpec((1, s, HEAD_DIM), lambda h: (H_QA + h, 0, 0)),
            pl.BlockSpec((1, s, HEAD_DIM), lambda h: (H_KA + h, 0, 0)),
            pl.BlockSpec((1, s, HEAD_DIM), lambda h: (H_VA + h, 0, 0)),
        ],
        out_specs=pl.BlockSpec((s, HEAD_DIM), lambda h: (0, h)),
        out_shape=jax.ShapeDtypeStruct((s, D_DIL), BF16),
        scratch_shapes=[pltpu.VMEM((padk + s, HEAD_DIM), BF16),
                        pltpu.VMEM((padk + s, HEAD_DIM), BF16),
                        pltpu.VMEM((tq, padk + tq), F32),
                        pltpu.VMEM((tq, padk + tq), F32),
                        pltpu.VMEM((tq, padk + tq), F32)],
        compiler_params=_params(("arbitrary",)),
        name="dilated_attn",
    )(heads, heads, heads)


def _add_per_head(s, bias, hg):
    tq = bias.shape[0]
    return jnp.concatenate([s[u * tq:(u + 1) * tq] + bias for u in range(hg)], axis=0)


def _nsa_select_kernel(*refs, nsp, tq, k_sel):
    q_refs = refs[:NSA_GROUP]
    kc_ref, vct_ref, gate_ref, ovt_ref, cmp_ref, sel_ref, imp_ref = refs[NSA_GROUP:]
    t0 = pl.program_id(1) * tq
    hg = NSA_GROUP
    ncp = kc_ref.shape[2]
    gate = gate_ref[...]

    def cmp_branch(n_rows):
        kc = kc_ref[0, 0, :n_rows, :]
        vct = vct_ref[0, 0, :, :n_rows]
        ovt = ovt_ref[:, :n_rows]
        n_io = lax.broadcasted_iota(jnp.int32, (n_rows, tq), 0)
        t_io = t0 + lax.broadcasted_iota(jnp.int32, (n_rows, tq), 1)
        cmask = (n_io * CMP_STRIDE + (CMP_BLOCK - 1) <= t_io) & (n_io < ncp - 1)
        imp = jnp.zeros((nsp, tq), F32)
        for u in range(hg):
            st = jnp.where(cmask, _nt_dot(kc, q_refs[u][0]), NEG)
            m = jnp.maximum(jnp.max(st, axis=0, keepdims=True), 0.1 * NEG)
            e = jnp.exp2(st - m)
            r = 1.0 / jnp.maximum(jnp.sum(e, axis=0, keepdims=True), 1e-30)
            eb = e.astype(BF16)
            imp = imp + jnp.dot(ovt, eb, preferred_element_type=F32) * r
            o_cmp = (jnp.dot(vct, eb, preferred_element_type=F32) * r).T
            c = N_NSA_BRANCHES * u
            cmp_ref[:, u * HEAD_DIM:(u + 1) * HEAD_DIM] = gate[:, c:c + 1] * o_cmp
        imp_ref[...] = imp

    n_var = ncp // CMP_ROW_CHUNK
    need = (t0 + tq) // CMP_STRIDE
    var = jnp.minimum((need + CMP_ROW_CHUNK - 1) // CMP_ROW_CHUNK, n_var) - 1
    for v in range(n_var):
        pl.when(var == v)(functools.partial(cmp_branch, CMP_ROW_CHUNK * (v + 1)))
    imp = imp_ref[...]

    j_io = lax.broadcasted_iota(jnp.int32, (nsp, tq), 0)
    t_sel = t0 + lax.broadcasted_iota(jnp.int32, (nsp, tq), 1)
    valid = j_io * SEL_BLOCK <= t_sel
    cur = t_sel // SEL_BLOCK
    forced = (j_io == 0) | (j_io == cur) | (j_io == cur - 1)
    score = jnp.where(valid & jnp.logical_not(forced), imp, NEG)
    for _ in range(k_sel - 3):
        m = jnp.max(score, axis=0, keepdims=True)
        first = jnp.min(jnp.where(score == m, j_io, nsp), axis=0, keepdims=True)
        score = jnp.where(j_io == first, -jnp.inf, score)
    sel_t = jnp.where(valid & (forced | (score == -jnp.inf)), 1.0, 0.0)
    sel_ref[0] = jnp.where(sel_t.T > 0.0, 0.0, SEL_OFF).astype(BF16)


def _nsa_select(heads, kcv, vct, gates, ovt, tq):
    _, s, _ = heads.shape
    ncp = kcv.shape[2]
    nsp = ovt.shape[0]
    hg = NSA_GROUP
    k_sel = min(N_SELECT, s // SEL_BLOCK)
    assert k_sel >= 3 and ncp % CMP_ROW_CHUNK == 0
    q_map = lambda u, g, qb: (H_QN + hg * g + u, qb, 0)
    return pl.pallas_call(
        functools.partial(_nsa_select_kernel, nsp=nsp, tq=tq, k_sel=k_sel),
        grid=(NSA_KV_GROUPS, s // tq),
        in_specs=[
            *[pl.BlockSpec((1, tq, HEAD_DIM), functools.partial(q_map, u)) for u in range(hg)],
            pl.BlockSpec((1, 1, ncp, HEAD_DIM), lambda g, qb: (0, g, 0, 0)),
            pl.BlockSpec((1, 1, HEAD_DIM, ncp), lambda g, qb: (1, g, 0, 0)),
            pl.BlockSpec((tq, HEAD_DIM), lambda g, qb: (qb, g)),
            pl.BlockSpec((nsp, ncp), lambda g, qb: (0, 0)),
        ],
        out_specs=[pl.BlockSpec((tq, hg * HEAD_DIM), lambda g, qb: (qb, g)),
                   pl.BlockSpec((1, tq, nsp), lambda g, qb: (g, qb, 0))],
        out_shape=[jax.ShapeDtypeStruct((s, D_NSA), F32),
                   jax.ShapeDtypeStruct((NSA_KV_GROUPS, s, nsp), BF16)],
        scratch_shapes=[pltpu.VMEM((nsp, tq), F32)],
        compiler_params=_params(("arbitrary", "arbitrary")),
        name="nsa_select",
    )(*([heads] * hg), kcv, vct, gates, ovt)


def _nsa_kernel(*refs, s_len, nsp, tq, kt):
    q_refs = refs[:NSA_GROUP]
    (ks_ref, vs_ref, kw_ref, vw_ref, gate_ref, cmp_ref, sel_ref,
     o_ref, kaug, vaug, vwaug, cbias, wbias, s_a, s_b, m_ref, acc_ref,
     part_ref) = refs[NSA_GROUP:]
    g = pl.program_id(0)
    qb = pl.program_id(1)
    t0 = qb * tq
    hg = NSA_GROUP
    rows = hg * tq
    ww = NSA_WINDOW + tq

    @pl.when((g == 0) & (qb == 0))
    def _():
        for r in range(kt // tq):
            d = (lax.broadcasted_iota(jnp.int32, (tq, kt), 0) + r * tq
                 - lax.broadcasted_iota(jnp.int32, (tq, kt), 1))
            cbias[r] = jnp.where(d >= 0, 0.0, NEG)
        for w in range(NSA_WINDOW // tq + 1):
            d = (lax.broadcasted_iota(jnp.int32, (tq, ww), 0) + w * tq
                 - lax.broadcasted_iota(jnp.int32, (tq, ww), 1))
            wbias[w] = jnp.where((d >= 0) & (d < NSA_WINDOW), 0.0, NEG)

    @pl.when(qb == 0)
    def _():
        kaug[:, 0:HEAD_DIM] = ks_ref[0]
        key = lax.broadcasted_iota(jnp.int32, (s_len, nsp), 0)
        blk = lax.broadcasted_iota(jnp.int32, (s_len, nsp), 1)
        kaug[:, HEAD_DIM:] = jnp.where(blk == key // SEL_BLOCK, 1.0, 0.0).astype(BF16)
        ones = jnp.ones((s_len, HEAD_DIM), BF16)
        vaug[:, 0:HEAD_DIM] = vs_ref[0]
        vaug[:, HEAD_DIM:] = ones
        vwaug[:, 0:HEAD_DIM] = vw_ref[0]
        vwaug[:, HEAD_DIM:] = ones

    gate = gate_ref[...]
    q5 = jnp.concatenate([r[0] for r in q_refs], axis=0)

    qaug = jnp.concatenate([q5, jnp.concatenate([sel_ref[0]] * hg, axis=0)], axis=1)

    w0 = pl.multiple_of(jnp.maximum(t0 - NSA_WINDOW, 0), tq)
    sw = _nt_dot(q5, kw_ref[0, pl.ds(w0, ww), :])
    sw = _add_per_head(sw, wbias[jnp.minimum(qb, NSA_WINDOW // tq)], hg)
    pw = jnp.exp2(sw - jnp.max(sw, axis=-1, keepdims=True))
    acc_w = jnp.dot(pw.astype(BF16), vwaug[pl.ds(w0, ww), :], preferred_element_type=F32)
    o_win = acc_w[:, :HEAD_DIM] / acc_w[:, HEAD_DIM:]
    for u in range(hg):
        c = N_NSA_BRANCHES * u + 2
        cols = slice(u * HEAD_DIM, (u + 1) * HEAD_DIM)
        part_ref[:, cols] = cmp_ref[:, cols] + gate[:, c:c + 1] * o_win[u * tq:(u + 1) * tq]

    def sel_scores(kti):
        k0 = pl.multiple_of(kti * kt, kt)
        return _nt_dot(qaug, kaug[pl.ds(k0, kt), :])

    def sel_update(s_ref, kti, diagonal):
        s = s_ref[...]
        if diagonal:
            s = _add_per_head(s, cbias[qb % (kt // tq)], hg)
        k0 = pl.multiple_of(kti * kt, kt)
        m_i = m_ref[...]
        m_new = jnp.maximum(m_i, jnp.max(s, axis=-1, keepdims=True))
        p = jnp.exp2(s - m_new)
        pv = jnp.dot(p.astype(BF16), vaug[pl.ds(k0, kt), :], preferred_element_type=F32)
        acc_ref[...] = jnp.exp2(m_i - m_new) * acc_ref[...] + pv
        m_ref[...] = m_new

    def sel_pair(pi, _):
        t = 2 * pi
        s_b[...] = sel_scores(t + 1)
        sel_update(s_a, t, False)
        s_a[...] = sel_scores(t + 2)
        sel_update(s_b, t + 1, False)
        return 0

    last = (t0 + tq + kt - 1) // kt - 1
    m_ref[...] = jnp.full((rows, 1), NEG, F32)
    acc_ref[...] = jnp.zeros((rows, 2 * HEAD_DIM), F32)
    s_a[...] = sel_scores(0)
    lax.fori_loop(0, last // 2, sel_pair, 0)
    t_even = 2 * (last // 2)

    @pl.when(last % 2 == 1)
    def _():
        s_b[...] = sel_scores(t_even + 1)
        sel_update(s_a, t_even, False)
        sel_update(s_b, t_even + 1, True)

    @pl.when(last % 2 == 0)
    def _():
        sel_update(s_a, t_even, True)

    gate = gate_ref[...]
    for u in range(hg):
        c = N_NSA_BRANCHES * u + 1
        acc_u = acc_ref[u * tq:(u + 1) * tq, :]
        o_slc = acc_u[:, :HEAD_DIM] / acc_u[:, HEAD_DIM:]
        cols = slice(u * HEAD_DIM, (u + 1) * HEAD_DIM)
        o_ref[:, cols] = (part_ref[:, cols] + gate[:, c:c + 1] * o_slc).astype(BF16)


def _nsa(heads, gates, part_cmp, sel_off, tq, kt):
    _, s, _ = heads.shape
    nsp = sel_off.shape[2]
    hg = NSA_GROUP
    assert kt % tq == 0 and NSA_WINDOW % tq == 0 and s % kt == 0
    q_map = lambda u, g, qb: (H_QN + hg * g + u, qb, 0)
    full = lambda hbase: pl.BlockSpec((1, s, HEAD_DIM), lambda g, qb: (hbase + g, 0, 0),
                                      pipeline_mode=pl.Buffered(1))
    return pl.pallas_call(
        functools.partial(_nsa_kernel, s_len=s, nsp=nsp, tq=tq, kt=kt),
        grid=(NSA_KV_GROUPS, s // tq),
        in_specs=[
            *[pl.BlockSpec((1, tq, HEAD_DIM), functools.partial(q_map, u)) for u in range(hg)],
            full(H_KS), full(H_VS), full(H_KW), full(H_VW),
            pl.BlockSpec((tq, HEAD_DIM), lambda g, qb: (qb, g)),
            pl.BlockSpec((tq, hg * HEAD_DIM), lambda g, qb: (qb, g)),
            pl.BlockSpec((1, tq, nsp), lambda g, qb: (g, qb, 0)),
        ],
        out_specs=pl.BlockSpec((tq, hg * HEAD_DIM), lambda g, qb: (qb, g)),
        out_shape=jax.ShapeDtypeStruct((s, D_NSA), BF16),
        scratch_shapes=[pltpu.VMEM((s, HEAD_DIM + nsp), BF16),
                        pltpu.VMEM((s, 2 * HEAD_DIM), BF16),
                        pltpu.VMEM((s, 2 * HEAD_DIM), BF16),
                        pltpu.VMEM((kt // tq, tq, kt), F32),
                        pltpu.VMEM((NSA_WINDOW // tq + 1, tq, NSA_WINDOW + tq), F32),
                        pltpu.VMEM((hg * tq, kt), F32),
                        pltpu.VMEM((hg * tq, kt), F32),
                        pltpu.VMEM((hg * tq, 1), F32),
                        pltpu.VMEM((hg * tq, 2 * HEAD_DIM), F32),
                        pltpu.VMEM((tq, hg * HEAD_DIM), F32)],
        compiler_params=_params(("arbitrary", "arbitrary")),
        name="nsa_attn",
    )(*([heads] * hg), heads, heads, heads, heads, gates, part_cmp, sel_off)


def _out_proj_ln_kernel(a_ref, b_ref, w_ref, h_ref, g_ref, be_ref, o_ref, wb16, *, alpha):
    @pl.when(pl.program_id(0) == 0)
    def _():
        wb16[...] = w_ref[...].astype(BF16)

    tm = o_ref.shape[0]
    rc = tm // ROW_CHUNKS
    for c in range(ROW_CHUNKS):
        r = slice(c * rc, (c + 1) * rc)
        mix = (jnp.dot(a_ref[r, :], wb16[:D_DIL, :], preferred_element_type=F32)
               + jnp.dot(b_ref[r, :], wb16[D_DIL:, :], preferred_element_type=F32))
        o_ref[r, :] = _layer_norm(alpha * h_ref[r, :] + mix, g_ref[...], be_ref[...])


def _out_proj_ln(mix_a, mix_b, w, h, g, b, alpha, tm):
    s, d = h.shape
    return pl.pallas_call(
        functools.partial(_out_proj_ln_kernel, alpha=alpha),
        grid=(s // tm,),
        in_specs=[
            pl.BlockSpec((tm, D_DIL), lambda i: (i, 0)),
            pl.BlockSpec((tm, D_NSA), lambda i: (i, 0)),
            pl.BlockSpec((d, d), lambda i: (0, 0), pipeline_mode=pl.Buffered(1)),
            pl.BlockSpec((tm, d), lambda i: (i, 0)),
            pl.BlockSpec((1, d), lambda i: (0, 0)),
            pl.BlockSpec((1, d), lambda i: (0, 0)),
        ],
        out_specs=pl.BlockSpec((tm, d), lambda i: (i, 0)),
        out_shape=jax.ShapeDtypeStruct((s, d), F32),
        scratch_shapes=[pltpu.VMEM((d, d), BF16)],
        compiler_params=_params(("arbitrary",)),
        name="out_proj_ln",
    )(mix_a, mix_b, w, h, g, b)


def _overlap_t(s):
    n_cmp = (s - CMP_BLOCK) // CMP_STRIDE + 1
    n_sel = s // SEL_BLOCK
    ncp = s // CMP_STRIDE
    nsp = -(-n_sel // HEAD_DIM) * HEAD_DIM
    c_lo = np.arange(n_cmp) * CMP_STRIDE
    c_hi = c_lo + CMP_BLOCK - 1
    s_lo = (np.arange(n_sel) * SEL_BLOCK)[:, None]
    ov = np.zeros((nsp, ncp), np.float32)
    ov[:n_sel, :n_cmp] = (c_lo[None, :] <= s_lo + SEL_BLOCK - 1) & (c_hi[None, :] >= s_lo)
    return jnp.asarray(ov, BF16)


def kernel(x, positions, ln1_g, ln1_b, ffn1_w1, ffn1_w3, ffn1_w2, w_in, gate_b, cmp_pe, cmp_w1,
           cmp_b1, cmp_w2, cmp_b2, w_out, ln2_g, ln2_b, ffn2_w1, ffn2_w3, ffn2_w2, ln3_g, ln3_b):
    bsz, s, d = x.shape
    assert bsz == 1 and d == (N_HEADS_DIL + N_HEADS_NSA) * HEAD_DIM
    assert s % max(FFN_ROWS, PROJ_ROWS, NSA_KEY_TILE) == 0 and s >= NSA_WINDOW + NSA_Q_TILE
    assert ffn1_w1.shape[2] % FFN_HIDDEN_TILE == 0
    alpha = (2.0 * DEPTH) ** 0.25
    scale = HEAD_DIM ** -0.5 * LOG2_E
    row = lambda v: v.reshape(1, -1)

    inv_freq = ROPE_THETA ** (-jnp.arange(0, HEAD_DIM, 2, dtype=F32) / HEAD_DIM)
    invf = jnp.concatenate([inv_freq, inv_freq]).reshape(1, HEAD_DIM)
    pos = positions[0].astype(F32).reshape(s, 1)
    cosf, sinf = _rope_tables(pos, pos, invf, ROPE_TILE)
    ncp = s // CMP_STRIDE
    n_cmp = (s - CMP_BLOCK) // CMP_STRIDE + 1
    pos_lo = jnp.pad(pos[0:n_cmp * CMP_STRIDE:CMP_STRIDE], ((0, ncp - n_cmp), (0, 0)))
    pos_hi = jnp.pad(pos[CMP_BLOCK - 1::CMP_STRIDE][:n_cmp], ((0, ncp - n_cmp), (0, 0)))
    cos_c, sin_c = _rope_tables(pos_lo, pos_hi, invf, ncp)

    g_off = 3 * D_DIL + D_NSA + 6 * KV_W
    per_g = NSA_GROUP * N_NSA_BRANCHES
    w_in_t = w_in[0].T.astype(BF16)
    wg = jnp.zeros((NSA_KV_GROUPS * HEAD_DIM, d), BF16)
    bg = jnp.zeros((1, NSA_KV_GROUPS * HEAD_DIM), F32)
    for g in range(NSA_KV_GROUPS):
        wg = wg.at[g * HEAD_DIM:g * HEAD_DIM + per_g].set(
            w_in_t[g_off + g * per_g:g_off + (g + 1) * per_g])
        bg = bg.at[0, g * HEAD_DIM:g * HEAD_DIM + per_g].set(gate_b[0][g * per_g:(g + 1) * per_g])

    h0 = x[0]
    h1 = _ffn_ln(h0, ffn1_w1[0], ffn1_w3[0], ffn1_w2[0], row(ln1_g[0]), row(ln1_b[0]),
                 alpha, FFN_ROWS, FFN_HIDDEN_TILE)

    heads, gates = _in_proj(h1, w_in_t, cosf, sinf, wg, bg, scale, PROJ_ROWS)
    tok16 = heads[H_KC:H_KS].reshape(2 * NSA_KV_GROUPS, ncp, CMP_STRIDE * HEAD_DIM)
    kcv, kcv_t = _compress(
        tok16, cmp_pe[0].reshape(2, 1, CMP_BLOCK * HEAD_DIM), cmp_w1[0],
        cmp_b1[0].reshape(2, 1, CMP_HIDDEN), cmp_w2[0],
        cmp_b2[0].reshape(2, 1, HEAD_DIM), cos_c, sin_c)
    mix_a = _dilated(heads, DIL_Q_TILE)
    part_cmp, sel_off = _nsa_select(heads, kcv, kcv_t, gates, _overlap_t(s), SEL_Q_TILE)
    mix_b = _nsa(heads, gates, part_cmp, sel_off, NSA_Q_TILE, NSA_KEY_TILE)

    h2 = _out_proj_ln(mix_a, mix_b, w_out[0], h1, row(ln2_g[0]), row(ln2_b[0]), alpha,
                      OUT_PROJ_ROWS)
    h3 = _ffn_ln(h2, ffn2_w1[0], ffn2_w3[0], ffn2_w2[0], row(ln3_g[0]), row(ln3_b[0]),
                 alpha, FFN_ROWS, FFN_HIDDEN_TILE)
    return h3.reshape(bsz, s, d)
```

```python
import functools

import jax
import jax.numpy as jnp
import numpy as np
from jax import lax
from jax.experimental import pallas as pl
from jax.experimental.pallas import tpu as pltpu

HEAD_DIM = 128
N_HEADS_DIL = 6
N_HEADS_NSA = 10
NSA_KV_GROUPS = 2
NSA_GROUP = N_HEADS_NSA // NSA_KV_GROUPS
N_NSA_BRANCHES = 3
DIL_PAIRS = ((128, 1), (512, 4), (2048, 16))
CMP_BLOCK = 32
CMP_STRIDE = 16
CMP_HIDDEN = 256
SEL_BLOCK = 64
N_SELECT = 16
NSA_WINDOW = 512
ROPE_THETA = 10000.0
LN_EPS = 1e-5
NEG = -1e30
SEL_OFF = -1e9
LOG2_E = 1.4426950408889634
DEPTH = 1

D_DIL = N_HEADS_DIL * HEAD_DIM
D_NSA = N_HEADS_NSA * HEAD_DIM
KV_W = NSA_KV_GROUPS * HEAD_DIM

H_QA, H_KA, H_VA, H_QN = 0, 6, 12, 18
H_KC, H_VC, H_KS, H_VS, H_KW, H_VW = 28, 30, 32, 34, 36, 38
N_PROJ_HEADS = 40
HEADS_PER_PROJ_BLOCK = 8
PROJ_SUB_HEADS = 2
ROW_CHUNKS = 2
CMP_ROW_CHUNK = 128

ROPE_TILE = 512
FFN_ROWS, FFN_HIDDEN_TILE = 1024, 256
PROJ_ROWS = 1024
OUT_PROJ_ROWS = 512
DIL_Q_TILE = 256
SEL_Q_TILE = 512
NSA_Q_TILE, NSA_KEY_TILE = 256, 1024

VMEM_LIMIT_BYTES = 56 * 1024 * 1024
FFN_VMEM_LIMIT_BYTES = 60 * 1024 * 1024

F32 = jnp.float32
BF16 = jnp.bfloat16


def _params(sem, vmem=VMEM_LIMIT_BYTES):
    return pltpu.CompilerParams(dimension_semantics=sem, vmem_limit_bytes=vmem)


def _nt_dot(a, b):
    return lax.dot_general(a, b, (((1,), (1,)), ((), ())), preferred_element_type=F32)


def _layer_norm(y, g, b):
    mu = jnp.mean(y, axis=-1, keepdims=True)
    yc = y - mu
    var = jnp.mean(yc * yc, axis=-1, keepdims=True)
    return yc * lax.rsqrt(var + LN_EPS) * g + b


def _rope_table_kernel(pa_ref, pb_ref, invf_ref, cos_ref, sin_ref):
    pos = (pa_ref[...] + pb_ref[...]) * 0.5
    ang = pos * invf_ref[...]
    lane = lax.broadcasted_iota(jnp.int32, ang.shape, 1)
    cos_ref[...] = jnp.cos(ang)
    sin_ref[...] = jnp.where(lane < HEAD_DIM // 2, -1.0, 1.0) * jnp.sin(ang)


def _rope_tables(pos_a, pos_b, invf, tile):
    n = pos_a.shape[0]
    spec_p = pl.BlockSpec((tile, 1), lambda i: (i, 0))
    spec_t = pl.BlockSpec((tile, HEAD_DIM), lambda i: (i, 0))
    return pl.pallas_call(
        _rope_table_kernel,
        grid=(n // tile,),
        in_specs=[spec_p, spec_p, pl.BlockSpec((1, HEAD_DIM), lambda i: (0, 0))],
        out_specs=[spec_t, spec_t],
        out_shape=[jax.ShapeDtypeStruct((n, HEAD_DIM), F32)] * 2,
        compiler_params=_params(("arbitrary",)),
        name="rope_tables",
    )(pos_a, pos_b, invf)


def _ffn_ln_kernel(h_ref, w1_ref, w3_ref, w2_ref, g_ref, b_ref, o_ref, hb_ref, *, alpha, nf):
    f = pl.program_id(1)

    @pl.when(f == 0)
    def _():
        h = h_ref[...]
        hb_ref[...] = h.astype(BF16)
        o_ref[...] = (2.0 * alpha) * h

    hb = hb_ref[...]
    a = jnp.dot(hb, w1_ref[...].astype(BF16), preferred_element_type=F32)
    b = jnp.dot(hb, w3_ref[...].astype(BF16), preferred_element_type=F32)
    act = (a * jax.nn.sigmoid(a)) * b
    o_ref[...] += jnp.dot(act.astype(BF16), w2_ref[...].astype(BF16),
                          preferred_element_type=F32)

    @pl.when(f == nf - 1)
    def _():
        o_ref[...] = _layer_norm(0.5 * o_ref[...], g_ref[...], b_ref[...])


def _ffn_ln(h, w1, w3, w2, g, b, alpha, tm, tf):
    s, d = h.shape
    dff = w1.shape[1]
    nf = dff // tf
    return pl.pallas_call(
        functools.partial(_ffn_ln_kernel, alpha=alpha, nf=nf),
        grid=(s // tm, nf),
        in_specs=[
            pl.BlockSpec((tm, d), lambda i, f: (i, 0)),
            pl.BlockSpec((d, tf), lambda i, f: (0, f)),
            pl.BlockSpec((d, tf), lambda i, f: (0, f)),
            pl.BlockSpec((tf, d), lambda i, f: (f, 0)),
            pl.BlockSpec((1, d), lambda i, f: (0, 0)),
            pl.BlockSpec((1, d), lambda i, f: (0, 0)),
        ],
        out_specs=pl.BlockSpec((tm, d), lambda i, f: (i, 0)),
        out_shape=jax.ShapeDtypeStruct((s, d), F32),
        scratch_shapes=[pltpu.VMEM((tm, d), BF16)],
        compiler_params=_params(("arbitrary", "arbitrary"), FFN_VMEM_LIMIT_BYTES),
        name="ffn_ln",
    )(h, w1, w3, w2, g, b)


def _in_proj_kernel(h_ref, w_ref, cos_ref, sin_ref, wg_ref, bg_ref, o_ref, gate_ref, hb_ref,
                    *, scale):
    j = pl.program_id(1)
    hpb = HEADS_PER_PROJ_BLOCK
    sub = PROJ_SUB_HEADS

    @pl.when(j == 0)
    def _():
        hb_ref[...] = h_ref[...].astype(BF16)
        gate_ref[...] = jax.nn.sigmoid(_nt_dot(hb_ref[...], wg_ref[...]) + bg_ref[...])

    cos = cos_ref[...]
    sin = sin_ref[...]
    h = hb_ref[...]
    for k0 in range(0, hpb, sub):
        acc = _nt_dot(h, w_ref[k0 * HEAD_DIM:(k0 + sub) * HEAD_DIM, :])
        for k in range(sub):
            hh = j * hpb + k0 + k
            is_q = (hh < H_KA) | ((hh >= H_QN) & (hh < H_KC))
            rope = (is_q | (hh < H_VA) | ((hh >= H_KS) & (hh < H_VS))
                    | ((hh >= H_KW) & (hh < H_VW)))
            x = acc[:, k * HEAD_DIM:(k + 1) * HEAD_DIM]
            xr = x * cos + pltpu.roll(x, HEAD_DIM // 2, 1) * sin
            y = jnp.where(rope, xr, x) * jnp.where(is_q, scale, 1.0).astype(F32)
            o_ref[k0 + k] = y.astype(BF16)


def _in_proj(h, w, cosf, sinf, wg, bg, scale, tm):
    s, d = h.shape
    n = wg.shape[0]
    hpb = HEADS_PER_PROJ_BLOCK
    return pl.pallas_call(
        functools.partial(_in_proj_kernel, scale=scale),
        grid=(s // tm, N_PROJ_HEADS // hpb),
        in_specs=[
            pl.BlockSpec((tm, d), lambda i, j: (i, 0)),
            pl.BlockSpec((hpb * HEAD_DIM, d), lambda i, j: (j, 0)),
            pl.BlockSpec((tm, HEAD_DIM), lambda i, j: (i, 0)),
            pl.BlockSpec((tm, HEAD_DIM), lambda i, j: (i, 0)),
            pl.BlockSpec((n, d), lambda i, j: (0, 0)),
            pl.BlockSpec((1, n), lambda i, j: (0, 0)),
        ],
        out_specs=[pl.BlockSpec((hpb, tm, HEAD_DIM), lambda i, j: (j, i, 0)),
                   pl.BlockSpec((tm, n), lambda i, j: (i, 0))],
        out_shape=[jax.ShapeDtypeStruct((N_PROJ_HEADS, s, HEAD_DIM), BF16),
                   jax.ShapeDtypeStruct((s, n), F32)],
        scratch_shapes=[pltpu.VMEM((tm, d), BF16)],
        compiler_params=_params(("arbitrary", "arbitrary")),
        name="in_proj",
    )(h, w, cosf, sinf, wg, bg)


def _compress_kernel(tok_ref, pe_ref, w1_ref, b1_ref, w2_ref, b2_ref, cos_ref, sin_ref,
                     o_ref, ot_ref):
    j = pl.program_id(0)
    half = CMP_STRIDE * HEAD_DIM
    tok = tok_ref[0].astype(F32)
    pe = pe_ref[0]
    w1 = w1_ref[0].astype(BF16)
    top = (tok + pe[:, :half]).astype(BF16)
    bot = (tok + pe[:, half:]).astype(BF16)
    u = jnp.dot(top, w1[:half], preferred_element_type=F32)
    v = jnp.dot(bot, w1[half:], preferred_element_type=F32)
    nc = u.shape[0]
    hid = u + pltpu.roll(v, nc - 1, 0) + b1_ref[0]
    hid = jax.nn.gelu(hid)
    out = jnp.dot(hid.astype(BF16), w2_ref[0].astype(BF16),
                  preferred_element_type=F32) + b2_ref[0]
    roped = out * cos_ref[...] + pltpu.roll(out, HEAD_DIM // 2, 1) * sin_ref[...]
    out = jnp.where(j == 0, roped, out)
    o_ref[0, 0] = out.astype(BF16)
    ot_ref[0, 0] = out.T.astype(BF16)


def _compress(tok16, pe, w1, b1, w2, b2, cos_c, sin_c):
    nc = tok16.shape[1]
    blk = CMP_BLOCK * HEAD_DIM
    g = NSA_KV_GROUPS
    return pl.pallas_call(
        _compress_kernel,
        grid=(2, g),
        in_specs=[
            pl.BlockSpec((1, nc, CMP_STRIDE * HEAD_DIM), lambda j, gi: (g * j + gi, 0, 0)),
            pl.BlockSpec((1, 1, blk), lambda j, gi: (j, 0, 0)),
            pl.BlockSpec((1, blk, CMP_HIDDEN), lambda j, gi: (j, 0, 0)),
            pl.BlockSpec((1, 1, CMP_HIDDEN), lambda j, gi: (j, 0, 0)),
            pl.BlockSpec((1, CMP_HIDDEN, HEAD_DIM), lambda j, gi: (j, 0, 0)),
            pl.BlockSpec((1, 1, HEAD_DIM), lambda j, gi: (j, 0, 0)),
            pl.BlockSpec((nc, HEAD_DIM), lambda j, gi: (0, 0)),
            pl.BlockSpec((nc, HEAD_DIM), lambda j, gi: (0, 0)),
        ],
        out_specs=[pl.BlockSpec((1, 1, nc, HEAD_DIM), lambda j, gi: (j, gi, 0, 0)),
                   pl.BlockSpec((1, 1, HEAD_DIM, nc), lambda j, gi: (j, gi, 0, 0))],
        out_shape=[jax.ShapeDtypeStruct((2, g, nc, HEAD_DIM), BF16),
                   jax.ShapeDtypeStruct((2, g, HEAD_DIM, nc), BF16)],
        compiler_params=_params(("arbitrary", "arbitrary")),
        name="nsa_compress",
    )(tok16, pe, w1, b1, w2, b2, cos_c, sin_c)


def _dil_kernel(q_ref, k_ref, v_ref, o_ref, kpad, vpad, bias_ref, s_a, s_b, *, tq, padk):
    w = padk + tq
    n_tiles = q_ref.shape[1] // tq

    @pl.when(pl.program_id(0) == 0)
    def _():
        r = lax.broadcasted_iota(jnp.int32, (tq, w), 0)
        c = lax.broadcasted_iota(jnp.int32, (tq, w), 1)
        d = r + padk - c
        cnt = jnp.zeros((tq, w), F32)
        for window, dil in DIL_PAIRS:
            hit = (d >= 0) & (d <= window) & ((d & (dil - 1)) == 0)
            cnt = cnt + jnp.where(hit, 1.0, 0.0)
        bias_ref[...] = jnp.where(cnt > 0.0, jnp.log2(jnp.maximum(cnt, 1.0)), NEG)

    kpad[0:padk, :] = jnp.zeros((padk, HEAD_DIM), BF16)
    vpad[0:padk, :] = jnp.zeros((padk, HEAD_DIM), BF16)
    kpad[padk:, :] = k_ref[0]
    vpad[padk:, :] = v_ref[0]
    col = lax.broadcasted_iota(jnp.int32, (1, w), 1)

    def scores(t):
        q0 = pl.multiple_of(t * tq, tq)
        s = _nt_dot(q_ref[0, pl.ds(q0, tq), :], kpad[pl.ds(q0, w), :]) + bias_ref[...]
        return jnp.where(col >= padk - q0, s, NEG)

    def finish(s_ref, t):
        q0 = pl.multiple_of(t * tq, tq)
        s = s_ref[...]
        p = jnp.exp2(s - jnp.max(s, axis=-1, keepdims=True))
        l = jnp.sum(p, axis=-1, keepdims=True)
        o = jnp.dot(p.astype(BF16), vpad[pl.ds(q0, w), :], preferred_element_type=F32)
        o_ref[pl.ds(q0, tq), :] = (o / l).astype(BF16)

    def pair(pi, _):
        t = 2 * pi
        s_b[...] = scores(t + 1)
        finish(s_a, t)
        s_a[...] = scores(t + 2)
        finish(s_b, t + 1)
        return 0

    s_a[...] = scores(0)
    lax.fori_loop(0, n_tiles // 2 - 1, pair, 0)
    s_b[...] = scores(n_tiles - 1)
    finish(s_a, n_tiles - 2)
    finish(s_b, n_tiles - 1)


def _dilated(heads, tq):
    _, s, _ = heads.shape
    padk = max(wd for wd, _ in DIL_PAIRS)
    assert (s // tq) % 2 == 0
    return pl.pallas_call(
        functools.partial(_dil_kernel, tq=tq, padk=padk),
        grid=(N_HEADS_DIL,),
        in_specs=[
            pl.BlockSpec((1, s, HEAD_DIM), lambda h: (H_QA + h, 0, 0)),
            pl.BlockSpec((1, s, HEAD_DIM), lambda h: (H_KA + h, 0, 0)),
            pl.BlockSpec((1, s, HEAD_DIM), lambda h: (H_VA + h, 0, 0)),
        ],
        out_specs=pl.BlockSpec((s, HEAD_DIM), lambda h: (0, h)),
        out_shape=jax.ShapeDtypeStruct((s, D_DIL), BF16),
        scratch_shapes=[pltpu.VMEM((padk + s, HEAD_DIM), BF16),
                        pltpu.VMEM((padk + s, HEAD_DIM), BF16),
                        pltpu.VMEM((tq, padk + tq), F32),
                        pltpu.VMEM((tq, padk + tq), F32),
                        pltpu.VMEM((tq, padk + tq), F32)],
        compiler_params=_params(("arbitrary",)),
        name="dilated_attn",
    )(heads, heads, heads)


def _add_per_head(s, bias, hg):
    tq = bias.shape[0]
    return jnp.concatenate([s[u * tq:(u + 1) * tq] + bias for u in range(hg)], axis=0)


def _nsa_select_kernel(*refs, nsp, tq, k_sel):
    q_refs = refs[:NSA_GROUP]
    kc_ref, vct_ref, gate_ref, ovt_ref, cmp_ref, sel_ref, imp_ref = refs[NSA_GROUP:]
    t0 = pl.program_id(1) * tq
    hg = NSA_GROUP
    ncp = kc_ref.shape[2]
    gate = gate_ref[...]

    def cmp_branch(n_rows):
        kc = kc_ref[0, 0, :n_rows, :]
        vct = vct_ref[0, 0, :, :n_rows]
        ovt = ovt_ref[:, :n_rows]
        n_io = lax.broadcasted_iota(jnp.int32, (n_rows, tq), 0)
        t_io = t0 + lax.broadcasted_iota(jnp.int32, (n_rows, tq), 1)
        cmask = (n_io * CMP_STRIDE + (CMP_BLOCK - 1) <= t_io) & (n_io < ncp - 1)
        imp = jnp.zeros((nsp, tq), F32)
        for u in range(hg):
            st = jnp.where(cmask, _nt_dot(kc, q_refs[u][0]), NEG)
            m = jnp.maximum(jnp.max(st, axis=0, keepdims=True), 0.1 * NEG)
            e = jnp.exp2(st - m)
            r = 1.0 / jnp.maximum(jnp.sum(e, axis=0, keepdims=True), 1e-30)
            eb = e.astype(BF16)
            imp = imp + jnp.dot(ovt, eb, preferred_element_type=F32) * r
            o_cmp = (jnp.dot(vct, eb, preferred_element_type=F32) * r).T
            c = N_NSA_BRANCHES * u
            cmp_ref[:, u * HEAD_DIM:(u + 1) * HEAD_DIM] = gate[:, c:c + 1] * o_cmp
        imp_ref[...] = imp

    n_var = ncp // CMP_ROW_CHUNK
    need = (t0 + tq) // CMP_STRIDE
    var = jnp.minimum((need + CMP_ROW_CHUNK - 1) // CMP_ROW_CHUNK, n_var) - 1
    for v in range(n_var):
        pl.when(var == v)(functools.partial(cmp_branch, CMP_ROW_CHUNK * (v + 1)))
    imp = imp_ref[...]

    j_io = lax.broadcasted_iota(jnp.int32, (nsp, tq), 0)
    t_sel = t0 + lax.broadcasted_iota(jnp.int32, (nsp, tq), 1)
    valid = j_io * SEL_BLOCK <= t_sel
    cur = t_sel // SEL_BLOCK
    forced = (j_io == 0) | (j_io == cur) | (j_io == cur - 1)
    score = jnp.where(valid & jnp.logical_not(forced), imp, NEG)
    for _ in range(k_sel - 3):
        m = jnp.max(score, axis=0, keepdims=True)
        first = jnp.min(jnp.where(score == m, j_io, nsp), axis=0, keepdims=True)
        score = jnp.where(j_io == first, -jnp.inf, score)
    sel_t = jnp.where(valid & (forced | (score == -jnp.inf)), 1.0, 0.0)
    sel_ref[0] = jnp.where(sel_t.T > 0.0, 0.0, SEL_OFF).astype(BF16)


def _nsa_select(heads, kcv, vct, gates, ovt, tq):
    _, s, _ = heads.shape
    ncp = kcv.shape[2]
    nsp = ovt.shape[0]
    hg = NSA_GROUP
    k_sel = min(N_SELECT, s // SEL_BLOCK)
    assert k_sel >= 3 and ncp % CMP_ROW_CHUNK == 0
    q_map = lambda u, g, qb: (H_QN + hg * g + u, qb, 0)
    return pl.pallas_call(
        functools.partial(_nsa_select_kernel, nsp=nsp, tq=tq, k_sel=k_sel),
        grid=(NSA_KV_GROUPS, s // tq),
        in_specs=[
            *[pl.BlockSpec((1, tq, HEAD_DIM), functools.partial(q_map, u)) for u in range(hg)],
            pl.BlockSpec((1, 1, ncp, HEAD_DIM), lambda g, qb: (0, g, 0, 0)),
            pl.BlockSpec((1, 1, HEAD_DIM, ncp), lambda g, qb: (1, g, 0, 0)),
            pl.BlockSpec((tq, HEAD_DIM), lambda g, qb: (qb, g)),
            pl.BlockSpec((nsp, ncp), lambda g, qb: (0, 0)),
        ],
        out_specs=[pl.BlockSpec((tq, hg * HEAD_DIM), lambda g, qb: (qb, g)),
                   pl.BlockSpec((1, tq, nsp), lambda g, qb: (g, qb, 0))],
        out_shape=[jax.ShapeDtypeStruct((s, D_NSA), F32),
                   jax.ShapeDtypeStruct((NSA_KV_GROUPS, s, nsp), BF16)],
        scratch_shapes=[pltpu.VMEM((nsp, tq), F32)],
        compiler_params=_params(("arbitrary", "arbitrary")),
        name="nsa_select",
    )(*([heads] * hg), kcv, vct, gates, ovt)


def _nsa_kernel(*refs, s_len, nsp, tq, kt):
    q_refs = refs[:NSA_GROUP]
    (ks_ref, vs_ref, kw_ref, vw_ref, gate_ref, cmp_ref, sel_ref,
     o_ref, kaug, vaug, vwaug, cbias, wbias, s_a, s_b, m_ref, acc_ref,
     part_ref) = refs[NSA_GROUP:]
    g = pl.program_id(0)
    qb = pl.program_id(1)
    t0 = qb * tq
    hg = NSA_GROUP
    rows = hg * tq
    ww = NSA_WINDOW + tq

    @pl.when((g == 0) & (qb == 0))
    def _():
        for r in range(kt // tq):
            d = (lax.broadcasted_iota(jnp.int32, (tq, kt), 0) + r * tq
                 - lax.broadcasted_iota(jnp.int32, (tq, kt), 1))
            cbias[r] = jnp.where(d >= 0, 0.0, NEG)
        for w in range(NSA_WINDOW // tq + 1):
            d = (lax.broadcasted_iota(jnp.int32, (tq, ww), 0) + w * tq
                 - lax.broadcasted_iota(jnp.int32, (tq, ww), 1))
            wbias[w] = jnp.where((d >= 0) & (d < NSA_WINDOW), 0.0, NEG)

    @pl.when(qb == 0)
    def _():
        kaug[:, 0:HEAD_DIM] = ks_ref[0]
        key = lax.broadcasted_iota(jnp.int32, (s_len, nsp), 0)
        blk = lax.broadcasted_iota(jnp.int32, (s_len, nsp), 1)
        kaug[:, HEAD_DIM:] = jnp.where(blk == key // SEL_BLOCK, 1.0, 0.0).astype(BF16)
        ones = jnp.ones((s_len, HEAD_DIM), BF16)
        vaug[:, 0:HEAD_DIM] = vs_ref[0]
        vaug[:, HEAD_DIM:] = ones
        vwaug[:, 0:HEAD_DIM] = vw_ref[0]
        vwaug[:, HEAD_DIM:] = ones

    gate = gate_ref[...]
    q5 = jnp.concatenate([r[0] for r in q_refs], axis=0)

    qaug = jnp.concatenate([q5, jnp.concatenate([sel_ref[0]] * hg, axis=0)], axis=1)

    w0 = pl.multiple_of(jnp.maximum(t0 - NSA_WINDOW, 0), tq)
    sw = _nt_dot(q5, kw_ref[0, pl.ds(w0, ww), :])
    sw = _add_per_head(sw, wbias[jnp.minimum(qb, NSA_WINDOW // tq)], hg)
    pw = jnp.exp2(sw - jnp.max(sw, axis=-1, keepdims=True))
    acc_w = jnp.dot(pw.astype(BF16), vwaug[pl.ds(w0, ww), :], preferred_element_type=F32)
    o_win = acc_w[:, :HEAD_DIM] / acc_w[:, HEAD_DIM:]
    for u in range(hg):
        c = N_NSA_BRANCHES * u + 2
        cols = slice(u * HEAD_DIM, (u + 1) * HEAD_DIM)
        part_ref[:, cols] = cmp_ref[:, cols] + gate[:, c:c + 1] * o_win[u * tq:(u + 1) * tq]

    def sel_scores(kti, width=kt):
        k0 = pl.multiple_of(kti * kt, kt)
        return _nt_dot(qaug, kaug[pl.ds(k0, width), :])

    def sel_update(s_ref, kti, diag=None):
        width = kt if diag is None else (diag + 1) * tq
        s = s_ref[:, :width]
        if diag is not None:
            s = _add_per_head(s, cbias[diag, :, :width], hg)
        k0 = pl.multiple_of(kti * kt, kt)
        m_i = m_ref[...]
        m_new = jnp.maximum(m_i, jnp.max(s, axis=-1, keepdims=True))
        p = jnp.exp2(s - m_new)
        pv = jnp.dot(p.astype(BF16), vaug[pl.ds(k0, width), :], preferred_element_type=F32)
        acc_ref[...] = jnp.exp2(m_i - m_new) * acc_ref[...] + pv
        m_ref[...] = m_new

    def sel_pair(pi, _):
        t = 2 * pi
        s_b[...] = sel_scores(t + 1)
        sel_update(s_a, t)
        s_a[...] = sel_scores(t + 2)
        sel_update(s_b, t + 1)
        return 0

    last = (t0 + tq + kt - 1) // kt - 1
    m_ref[...] = jnp.full((rows, 1), NEG, F32)
    acc_ref[...] = jnp.zeros((rows, 2 * HEAD_DIM), F32)
    s_a[...] = sel_scores(0)
    lax.fori_loop(0, last // 2, sel_pair, 0)
    t_even = 2 * (last // 2)

    def tail_odd(diag):
        width = (diag + 1) * tq
        s_b[:, :width] = sel_scores(t_even + 1, width)
        sel_update(s_a, t_even)
        sel_update(s_b, t_even + 1, diag)

    def tail_even(diag):
        sel_update(s_a, t_even, diag)

    for diag in range(kt // tq):
        here = qb % (kt // tq) == diag
        pl.when(here & (last % 2 == 1))(functools.partial(tail_odd, diag))
        pl.when(here & (last % 2 == 0))(functools.partial(tail_even, diag))

    gate = gate_ref[...]
    for u in range(hg):
        c = N_NSA_BRANCHES * u + 1
        acc_u = acc_ref[u * tq:(u + 1) * tq, :]
        o_slc = acc_u[:, :HEAD_DIM] / acc_u[:, HEAD_DIM:]
        cols = slice(u * HEAD_DIM, (u + 1) * HEAD_DIM)
        o_ref[:, cols] = (part_ref[:, cols] + gate[:, c:c + 1] * o_slc).astype(BF16)


def _nsa(heads, gates, part_cmp, sel_off, tq, kt):
    _, s, _ = heads.shape
    nsp = sel_off.shape[2]
    hg = NSA_GROUP
    assert kt % tq == 0 and NSA_WINDOW % tq == 0 and s % kt == 0
    q_map = lambda u, g, qb: (H_QN + hg * g + u, qb, 0)
    full = lambda hbase: pl.BlockSpec((1, s, HEAD_DIM), lambda g, qb: (hbase + g, 0, 0),
                                      pipeline_mode=pl.Buffered(1))
    return pl.pallas_call(
        functools.partial(_nsa_kernel, s_len=s, nsp=nsp, tq=tq, kt=kt),
        grid=(NSA_KV_GROUPS, s // tq),
        in_specs=[
            *[pl.BlockSpec((1, tq, HEAD_DIM), functools.partial(q_map, u)) for u in range(hg)],
            full(H_KS), full(H_VS), full(H_KW), full(H_VW),
            pl.BlockSpec((tq, HEAD_DIM), lambda g, qb: (qb, g)),
            pl.BlockSpec((tq, hg * HEAD_DIM), lambda g, qb: (qb, g)),
            pl.BlockSpec((1, tq, nsp), lambda g, qb: (g, qb, 0)),
        ],
        out_specs=pl.BlockSpec((tq, hg * HEAD_DIM), lambda g, qb: (qb, g)),
        out_shape=jax.ShapeDtypeStruct((s, D_NSA), BF16),
        scratch_shapes=[pltpu.VMEM((s, HEAD_DIM + nsp), BF16),
                        pltpu.VMEM((s, 2 * HEAD_DIM), BF16),
                        pltpu.VMEM((s, 2 * HEAD_DIM), BF16),
                        pltpu.VMEM((kt // tq, tq, kt), F32),
                        pltpu.VMEM((NSA_WINDOW // tq + 1, tq, NSA_WINDOW + tq), F32),
                        pltpu.VMEM((hg * tq, kt), F32),
                        pltpu.VMEM((hg * tq, kt), F32),
                        pltpu.VMEM((hg * tq, 1), F32),
                        pltpu.VMEM((hg * tq, 2 * HEAD_DIM), F32),
                        pltpu.VMEM((tq, hg * HEAD_DIM), F32)],
        compiler_params=_params(("arbitrary", "arbitrary")),
        name="nsa_attn",
    )(*([heads] * hg), heads, heads, heads, heads, gates, part_cmp, sel_off)


def _out_proj_ln_kernel(a_ref, b_ref, w_ref, h_ref, g_ref, be_ref, o_ref, wb16, *, alpha):
    @pl.when(pl.program_id(0) == 0)
    def _():
        wb16[...] = w_ref[...].astype(BF16)

    tm = o_ref.shape[0]
    rc = tm // ROW_CHUNKS
    for c in range(ROW_CHUNKS):
        r = slice(c * rc, (c + 1) * rc)
        mix = (jnp.dot(a_ref[r, :], wb16[:D_DIL, :], preferred_element_type=F32)
               + jnp.dot(b_ref[r, :], wb16[D_DIL:, :], preferred_element_type=F32))
        o_ref[r, :] = _layer_norm(alpha * h_ref[r, :] + mix, g_ref[...], be_ref[...])


def _out_proj_ln(mix_a, mix_b, w, h, g, b, alpha, tm):
    s, d = h.shape
    return pl.pallas_call(
        functools.partial(_out_proj_ln_kernel, alpha=alpha),
        grid=(s // tm,),
        in_specs=[
            pl.BlockSpec((tm, D_DIL), lambda i: (i, 0)),
            pl.BlockSpec((tm, D_NSA), lambda i: (i, 0)),
            pl.BlockSpec((d, d), lambda i: (0, 0), pipeline_mode=pl.Buffered(1)),
            pl.BlockSpec((tm, d), lambda i: (i, 0)),
            pl.BlockSpec((1, d), lambda i: (0, 0)),
            pl.BlockSpec((1, d), lambda i: (0, 0)),
        ],
        out_specs=pl.BlockSpec((tm, d), lambda i: (i, 0)),
        out_shape=jax.ShapeDtypeStruct((s, d), F32),
        scratch_shapes=[pltpu.VMEM((d, d), BF16)],
        compiler_params=_params(("arbitrary",)),
        name="out_proj_ln",
    )(mix_a, mix_b, w, h, g, b)


def _overlap_t(s):
    n_cmp = (s - CMP_BLOCK) // CMP_STRIDE + 1
    n_sel = s // SEL_BLOCK
    ncp = s // CMP_STRIDE
    nsp = -(-n_sel // HEAD_DIM) * HEAD_DIM
    c_lo = np.arange(n_cmp) * CMP_STRIDE
    c_hi = c_lo + CMP_BLOCK - 1
    s_lo = (np.arange(n_sel) * SEL_BLOCK)[:, None]
    ov = np.zeros((nsp, ncp), np.float32)
    ov[:n_sel, :n_cmp] = (c_lo[None, :] <= s_lo + SEL_BLOCK - 1) & (c_hi[None, :] >= s_lo)
    return jnp.asarray(ov, BF16)


def kernel(x, positions, ln1_g, ln1_b, ffn1_w1, ffn1_w3, ffn1_w2, w_in, gate_b, cmp_pe, cmp_w1,
           cmp_b1, cmp_w2, cmp_b2, w_out, ln2_g, ln2_b, ffn2_w1, ffn2_w3, ffn2_w2, ln3_g, ln3_b):
    bsz, s, d = x.shape
    assert bsz == 1 and d == (N_HEADS_DIL + N_HEADS_NSA) * HEAD_DIM
    assert s % max(FFN_ROWS, PROJ_ROWS, NSA_KEY_TILE) == 0 and s >= NSA_WINDOW + NSA_Q_TILE
    assert ffn1_w1.shape[2] % FFN_HIDDEN_TILE == 0
    alpha = (2.0 * DEPTH) ** 0.25
    scale = HEAD_DIM ** -0.5 * LOG2_E
    row = lambda v: v.reshape(1, -1)

    inv_freq = ROPE_THETA ** (-jnp.arange(0, HEAD_DIM, 2, dtype=F32) / HEAD_DIM)
    invf = jnp.concatenate([inv_freq, inv_freq]).reshape(1, HEAD_DIM)
    pos = positions[0].astype(F32).reshape(s, 1)
    cosf, sinf = _rope_tables(pos, pos, invf, ROPE_TILE)
    ncp = s // CMP_STRIDE
    n_cmp = (s - CMP_BLOCK) // CMP_STRIDE + 1
    pos_lo = jnp.pad(pos[0:n_cmp * CMP_STRIDE:CMP_STRIDE], ((0, ncp - n_cmp), (0, 0)))
    pos_hi = jnp.pad(pos[CMP_BLOCK - 1::CMP_STRIDE][:n_cmp], ((0, ncp - n_cmp), (0, 0)))
    cos_c, sin_c = _rope_tables(pos_lo, pos_hi, invf, ncp)

    g_off = 3 * D_DIL + D_NSA + 6 * KV_W
    per_g = NSA_GROUP * N_NSA_BRANCHES
    w_in_t = w_in[0].T.astype(BF16)
    wg = jnp.zeros((NSA_KV_GROUPS * HEAD_DIM, d), BF16)
    bg = jnp.zeros((1, NSA_KV_GROUPS * HEAD_DIM), F32)
    for g in range(NSA_KV_GROUPS):
        wg = wg.at[g * HEAD_DIM:g * HEAD_DIM + per_g].set(
            w_in_t[g_off + g * per_g:g_off + (g + 1) * per_g])
        bg = bg.at[0, g * HEAD_DIM:g * HEAD_DIM + per_g].set(gate_b[0][g * per_g:(g + 1) * per_g])

    h0 = x[0]
    h1 = _ffn_ln(h0, ffn1_w1[0], ffn1_w3[0], ffn1_w2[0], row(ln1_g[0]), row(ln1_b[0]),
                 alpha, FFN_ROWS, FFN_HIDDEN_TILE)

    heads, gates = _in_proj(h1, w_in_t, cosf, sinf, wg, bg, scale, PROJ_ROWS)
    tok16 = heads[H_KC:H_KS].reshape(2 * NSA_KV_GROUPS, ncp, CMP_STRIDE * HEAD_DIM)
    kcv, kcv_t = _compress(
        tok16, cmp_pe[0].reshape(2, 1, CMP_BLOCK * HEAD_DIM), cmp_w1[0],
        cmp_b1[0].reshape(2, 1, CMP_HIDDEN), cmp_w2[0],
        cmp_b2[0].reshape(2, 1, HEAD_DIM), cos_c, sin_c)
    mix_a = _dilated(heads, DIL_Q_TILE)
    part_cmp, sel_off = _nsa_select(heads, kcv, kcv_t, gates, _overlap_t(s), SEL_Q_TILE)
    mix_b = _nsa(heads, gates, part_cmp, sel_off, NSA_Q_TILE, NSA_KEY_TILE)

    h2 = _out_proj_ln(mix_a, mix_b, w_out[0], h1, row(ln2_g[0]), row(ln2_b[0]), alpha,
                      OUT_PROJ_ROWS)
    h3 = _ffn_ln(h2, ffn2_w1[0], ffn2_w3[0], ffn2_w2[0], row(ln3_g[0]), row(ln3_b[0]),
                 alpha, FFN_ROWS, FFN_HIDDEN_TILE)
    return h3.reshape(bsz, s, d)
```

```python
import functools

import jax
import jax.numpy as jnp
import numpy as np
from jax import lax
from jax.experimental import pallas as pl
from jax.experimental.pallas import tpu as pltpu

HEAD_DIM = 128
N_HEADS_DIL = 6
N_HEADS_NSA = 10
NSA_KV_GROUPS = 2
NSA_GROUP = N_HEADS_NSA // NSA_KV_GROUPS
N_NSA_BRANCHES = 3
DIL_PAIRS = ((128, 1), (512, 4), (2048, 16))
CMP_BLOCK = 32
CMP_STRIDE = 16
CMP_HIDDEN = 256
SEL_BLOCK = 64
N_SELECT = 16
NSA_WINDOW = 512
ROPE_THETA = 10000.0
LN_EPS = 1e-5
NEG = -1e30
SEL_OFF = -1e9
LOG2_E = 1.4426950408889634
DEPTH = 1

D_DIL = N_HEADS_DIL * HEAD_DIM
D_NSA = N_HEADS_NSA * HEAD_DIM
KV_W = NSA_KV_GROUPS * HEAD_DIM

H_QA, H_KA, H_VA, H_QN = 0, 6, 12, 18
H_KC, H_VC, H_KS, H_VS, H_KW, H_VW = 28, 30, 32, 34, 36, 38
N_PROJ_HEADS = 40
HEADS_PER_PROJ_BLOCK = 8
PROJ_SUB_HEADS = 2
ROW_CHUNKS = 2
CMP_ROW_CHUNK = 128

ROPE_TILE = 512
FFN_ROWS, FFN_HIDDEN_TILE = 1024, 256
PROJ_ROWS = 1024
OUT_PROJ_ROWS = 512
DIL_Q_TILE = 256
SEL_Q_TILE = 512
NSA_Q_TILE, NSA_KEY_TILE = 256, 1024

VMEM_LIMIT_BYTES = 56 * 1024 * 1024
FFN_VMEM_LIMIT_BYTES = 60 * 1024 * 1024

F32 = jnp.float32
BF16 = jnp.bfloat16


def _params(sem, vmem=VMEM_LIMIT_BYTES):
    return pltpu.CompilerParams(dimension_semantics=sem, vmem_limit_bytes=vmem)


def _nt_dot(a, b):
    return lax.dot_general(a, b, (((1,), (1,)), ((), ())), preferred_element_type=F32)


def _layer_norm(y, g, b):
    mu = jnp.mean(y, axis=-1, keepdims=True)
    yc = y - mu
    var = jnp.mean(yc * yc, axis=-1, keepdims=True)
    return yc * lax.rsqrt(var + LN_EPS) * g + b


def _rope_table_kernel(pa_ref, pb_ref, invf_ref, cos_ref, sin_ref):
    pos = (pa_ref[...] + pb_ref[...]) * 0.5
    ang = pos * invf_ref[...]
    lane = lax.broadcasted_iota(jnp.int32, ang.shape, 1)
    cos_ref[...] = jnp.cos(ang)
    sin_ref[...] = jnp.where(lane < HEAD_DIM // 2, -1.0, 1.0) * jnp.sin(ang)


def _rope_tables(pos_a, pos_b, invf, tile):
    n = pos_a.shape[0]
    spec_p = pl.BlockSpec((tile, 1), lambda i: (i, 0))
    spec_t = pl.BlockSpec((tile, HEAD_DIM), lambda i: (i, 0))
    return pl.pallas_call(
        _rope_table_kernel,
        grid=(n // tile,),
        in_specs=[spec_p, spec_p, pl.BlockSpec((1, HEAD_DIM), lambda i: (0, 0))],
        out_specs=[spec_t, spec_t],
        out_shape=[jax.ShapeDtypeStruct((n, HEAD_DIM), F32)] * 2,
        compiler_params=_params(("arbitrary",)),
        name="rope_tables",
    )(pos_a, pos_b, invf)


def _ffn_ln_kernel(h_ref, w1_ref, w3_ref, w2_ref, g_ref, b_ref, o_ref, hb_ref, *, alpha, nf):
    f = pl.program_id(1)

    @pl.when(f == 0)
    def _():
        h = h_ref[...]
        hb_ref[...] = h.astype(BF16)
        o_ref[...] = (2.0 * alpha) * h

    hb = hb_ref[...]
    a = jnp.dot(hb, w1_ref[...].astype(BF16), preferred_element_type=F32)
    b = jnp.dot(hb, w3_ref[...].astype(BF16), preferred_element_type=F32)
    act = (a * jax.nn.sigmoid(a)) * b
    o_ref[...] += jnp.dot(act.astype(BF16), w2_ref[...].astype(BF16),
                          preferred_element_type=F32)

    @pl.when(f == nf - 1)
    def _():
        o_ref[...] = _layer_norm(0.5 * o_ref[...], g_ref[...], b_ref[...])


def _ffn_ln(h, w1, w3, w2, g, b, alpha, tm, tf):
    s, d = h.shape
    dff = w1.shape[1]
    nf = dff // tf
    return pl.pallas_call(
        functools.partial(_ffn_ln_kernel, alpha=alpha, nf=nf),
        grid=(s // tm, nf),
        in_specs=[
            pl.BlockSpec((tm, d), lambda i, f: (i, 0)),
            pl.BlockSpec((d, tf), lambda i, f: (0, f)),
            pl.BlockSpec((d, tf), lambda i, f: (0, f)),
            pl.BlockSpec((tf, d), lambda i, f: (f, 0)),
            pl.BlockSpec((1, d), lambda i, f: (0, 0)),
            pl.BlockSpec((1, d), lambda i, f: (0, 0)),
        ],
        out_specs=pl.BlockSpec((tm, d), lambda i, f: (i, 0)),
        out_shape=jax.ShapeDtypeStruct((s, d), F32),
        scratch_shapes=[pltpu.VMEM((tm, d), BF16)],
        compiler_params=_params(("arbitrary", "arbitrary"), FFN_VMEM_LIMIT_BYTES),
        name="ffn_ln",
    )(h, w1, w3, w2, g, b)


def _in_proj_kernel(h_ref, w_ref, cos_ref, sin_ref, wg_ref, bg_ref, o_ref, gate_ref, hb_ref,
                    *, scale):
    j = pl.program_id(1)
    hpb = HEADS_PER_PROJ_BLOCK
    sub = PROJ_SUB_HEADS

    @pl.when(j == 0)
    def _():
        hb_ref[...] = h_ref[...].astype(BF16)
        gate_ref[...] = jax.nn.sigmoid(_nt_dot(hb_ref[...], wg_ref[...]) + bg_ref[...])

    cos = cos_ref[...]
    sin = sin_ref[...]
    h = hb_ref[...]
    for k0 in range(0, hpb, sub):
        acc = _nt_dot(h, w_ref[k0 * HEAD_DIM:(k0 + sub) * HEAD_DIM, :])
        for k in range(sub):
            hh = j * hpb + k0 + k
            is_q = (hh < H_KA) | ((hh >= H_QN) & (hh < H_KC))
            rope = (is_q | (hh < H_VA) | ((hh >= H_KS) & (hh < H_VS))
                    | ((hh >= H_KW) & (hh < H_VW)))
            x = acc[:, k * HEAD_DIM:(k + 1) * HEAD_DIM]
            xr = x * cos + pltpu.roll(x, HEAD_DIM // 2, 1) * sin
            y = jnp.where(rope, xr, x) * jnp.where(is_q, scale, 1.0).astype(F32)
            o_ref[k0 + k] = y.astype(BF16)


def _in_proj(h, w, cosf, sinf, wg, bg, scale, tm):
    s, d = h.shape
    n = wg.shape[0]
    hpb = HEADS_PER_PROJ_BLOCK
    return pl.pallas_call(
        functools.partial(_in_proj_kernel, scale=scale),
        grid=(s // tm, N_PROJ_HEADS // hpb),
        in_specs=[
            pl.BlockSpec((tm, d), lambda i, j: (i, 0)),
            pl.BlockSpec((hpb * HEAD_DIM, d), lambda i, j: (j, 0)),
            pl.BlockSpec((tm, HEAD_DIM), lambda i, j: (i, 0)),
            pl.BlockSpec((tm, HEAD_DIM), lambda i, j: (i, 0)),
            pl.BlockSpec((n, d), lambda i, j: (0, 0)),
            pl.BlockSpec((1, n), lambda i, j: (0, 0)),
        ],
        out_specs=[pl.BlockSpec((hpb, tm, HEAD_DIM), lambda i, j: (j, i, 0)),
                   pl.BlockSpec((tm, n), lambda i, j: (i, 0))],
        out_shape=[jax.ShapeDtypeStruct((N_PROJ_HEADS, s, HEAD_DIM), BF16),
                   jax.ShapeDtypeStruct((s, n), F32)],
        scratch_shapes=[pltpu.VMEM((tm, d), BF16)],
        compiler_params=_params(("arbitrary", "arbitrary")),
        name="in_proj",
    )(h, w, cosf, sinf, wg, bg)


def _compress_kernel(tok_ref, pe_ref, w1_ref, b1_ref, w2_ref, b2_ref, cos_ref, sin_ref,
                     o_ref, ot_ref):
    j = pl.program_id(0)
    half = CMP_STRIDE * HEAD_DIM
    tok = tok_ref[0].astype(F32)
    pe = pe_ref[0]
    w1 = w1_ref[0].astype(BF16)
    top = (tok + pe[:, :half]).astype(BF16)
    bot = (tok + pe[:, half:]).astype(BF16)
    u = jnp.dot(top, w1[:half], preferred_element_type=F32)
    v = jnp.dot(bot, w1[half:], preferred_element_type=F32)
    nc = u.shape[0]
    hid = u + pltpu.roll(v, nc - 1, 0) + b1_ref[0]
    hid = jax.nn.gelu(hid)
    out = jnp.dot(hid.astype(BF16), w2_ref[0].astype(BF16),
                  preferred_element_type=F32) + b2_ref[0]
    roped = out * cos_ref[...] + pltpu.roll(out, HEAD_DIM // 2, 1) * sin_ref[...]
    out = jnp.where(j == 0, roped, out)
    o_ref[0, 0] = out.astype(BF16)
    ot_ref[0, 0] = out.T.astype(BF16)


def _compress(tok16, pe, w1, b1, w2, b2, cos_c, sin_c):
    nc = tok16.shape[1]
    blk = CMP_BLOCK * HEAD_DIM
    g = NSA_KV_GROUPS
    return pl.pallas_call(
        _compress_kernel,
        grid=(2, g),
        in_specs=[
            pl.BlockSpec((1, nc, CMP_STRIDE * HEAD_DIM), lambda j, gi: (g * j + gi, 0, 0)),
            pl.BlockSpec((1, 1, blk), lambda j, gi: (j, 0, 0)),
            pl.BlockSpec((1, blk, CMP_HIDDEN), lambda j, gi: (j, 0, 0)),
            pl.BlockSpec((1, 1, CMP_HIDDEN), lambda j, gi: (j, 0, 0)),
            pl.BlockSpec((1, CMP_HIDDEN, HEAD_DIM), lambda j, gi: (j, 0, 0)),
            pl.BlockSpec((1, 1, HEAD_DIM), lambda j, gi: (j, 0, 0)),
            pl.BlockSpec((nc, HEAD_DIM), lambda j, gi: (0, 0)),
            pl.BlockSpec((nc, HEAD_DIM), lambda j, gi: (0, 0)),
        ],
        out_specs=[pl.BlockSpec((1, 1, nc, HEAD_DIM), lambda j, gi: (j, gi, 0, 0)),
                   pl.BlockSpec((1, 1, HEAD_DIM, nc), lambda j, gi: (j, gi, 0, 0))],
        out_shape=[jax.ShapeDtypeStruct((2, g, nc, HEAD_DIM), BF16),
                   jax.ShapeDtypeStruct((2, g, HEAD_DIM, nc), BF16)],
        compiler_params=_params(("arbitrary", "arbitrary")),
        name="nsa_compress",
    )(tok16, pe, w1, b1, w2, b2, cos_c, sin_c)


def _dil_kernel(q_ref, k_ref, v_ref, o_ref, bias_ref, s_a, s_b, *, tq, padk):
    w = padk + tq
    n_tiles = q_ref.shape[1] // tq

    @pl.when(pl.program_id(0) == 0)
    def _():
        r = lax.broadcasted_iota(jnp.int32, (tq, w), 0)
        c = lax.broadcasted_iota(jnp.int32, (tq, w), 1)
        d = r + padk - c
        cnt = jnp.zeros((tq, w), F32)
        for window, dil in DIL_PAIRS:
            hit = (d >= 0) & (d <= window) & ((d & (dil - 1)) == 0)
            cnt = cnt + jnp.where(hit, 1.0, 0.0)
        bias_ref[...] = jnp.where(cnt > 0.0, jnp.log2(jnp.maximum(cnt, 1.0)), NEG)

    n_early = min(padk // tq, n_tiles)

    def strip(t):
        if isinstance(t, int):
            return (0, (t + 1) * tq) if t < n_early else (t * tq - padk, w)
        return pl.multiple_of(t * tq - padk, tq), w

    def scores(s_ref, t):
        k0, width = strip(t)
        q0 = t * tq if isinstance(t, int) else pl.multiple_of(t * tq, tq)
        s_ref[:, :width] = (_nt_dot(q_ref[0, pl.ds(q0, tq), :], k_ref[0, pl.ds(k0, width), :])
                            + bias_ref[:, w - width:])

    def finish(s_ref, t):
        k0, width = strip(t)
        q0 = t * tq if isinstance(t, int) else pl.multiple_of(t * tq, tq)
        s = s_ref[:, :width]
        p = jnp.exp2(s - jnp.max(s, axis=-1, keepdims=True))
        l = jnp.sum(p, axis=-1, keepdims=True)
        o = jnp.dot(p.astype(BF16), v_ref[0, pl.ds(k0, width), :], preferred_element_type=F32)
        o_ref[pl.ds(q0, tq), :] = (o / l).astype(BF16)

    bufs = (s_a, s_b)
    scores(s_a, 0)
    for t in range(n_early):
        if t + 1 < n_tiles:
            scores(bufs[(t + 1) % 2], t + 1)
        finish(bufs[t % 2], t)

    def pair(pi, _):
        t = 2 * pi
        scores(s_b, t + 1)
        finish(s_a, t)
        scores(s_a, t + 2)
        finish(s_b, t + 1)
        return 0

    if n_early < n_tiles:
        lax.fori_loop(n_early // 2, n_tiles // 2 - 1, pair, 0)
        scores(s_b, n_tiles - 1)
        finish(s_a, n_tiles - 2)
        finish(s_b, n_tiles - 1)


def _dilated(heads, tq):
    _, s, _ = heads.shape
    padk = max(wd for wd, _ in DIL_PAIRS)
    n_tiles, n_early = s // tq, min(padk // tq, s // tq)
    assert padk % tq == 0 and n_tiles % 2 == 0 and n_early % 2 == 0
    return pl.pallas_call(
        functools.partial(_dil_kernel, tq=tq, padk=padk),
        grid=(N_HEADS_DIL,),
        in_specs=[
            pl.BlockSpec((1, s, HEAD_DIM), lambda h: (H_QA + h, 0, 0)),
            pl.BlockSpec((1, s, HEAD_DIM), lambda h: (H_KA + h, 0, 0)),
            pl.BlockSpec((1, s, HEAD_DIM), lambda h: (H_VA + h, 0, 0)),
        ],
        out_specs=pl.BlockSpec((s, HEAD_DIM), lambda h: (0, h)),
        out_shape=jax.ShapeDtypeStruct((s, D_DIL), BF16),
        scratch_shapes=[pltpu.VMEM((tq, padk + tq), F32),
                        pltpu.VMEM((tq, padk + tq), F32),
                        pltpu.VMEM((tq, padk + tq), F32)],
        compiler_params=_params(("arbitrary",)),
        name="dilated_attn",
    )(heads, heads, heads)


def _add_per_head(s, bias, hg):
    tq = bias.shape[0]
    return jnp.concatenate([s[u * tq:(u + 1) * tq] + bias for u in range(hg)], axis=0)


def _nsa_select_kernel(*refs, nsp, tq, k_sel):
    q_refs = refs[:NSA_GROUP]
    kc_ref, vct_ref, gate_ref, ovt_ref, cmp_ref, sel_ref, imp_ref = refs[NSA_GROUP:]
    t0 = pl.program_id(1) * tq
    hg = NSA_GROUP
    ncp = kc_ref.shape[2]
    gate = gate_ref[...]

    def cmp_branch(n_rows):
        kc = kc_ref[0, 0, :n_rows, :]
        vct = vct_ref[0, 0, :, :n_rows]
        ovt = ovt_ref[:, :n_rows]
        n_io = lax.broadcasted_iota(jnp.int32, (n_rows, tq), 0)
        t_io = t0 + lax.broadcasted_iota(jnp.int32, (n_rows, tq), 1)
        cmask = (n_io * CMP_STRIDE + (CMP_BLOCK - 1) <= t_io) & (n_io < ncp - 1)
        imp = jnp.zeros((nsp, tq), F32)
        for u in range(hg):
            st = jnp.where(cmask, _nt_dot(kc, q_refs[u][0]), NEG)
            m = jnp.maximum(jnp.max(st, axis=0, keepdims=True), 0.1 * NEG)
            e = jnp.exp2(st - m)
            r = 1.0 / jnp.maximum(jnp.sum(e, axis=0, keepdims=True), 1e-30)
            eb = e.astype(BF16)
            imp = imp + jnp.dot(ovt, eb, preferred_element_type=F32) * r
            o_cmp = (jnp.dot(vct, eb, preferred_element_type=F32) * r).T
            c = N_NSA_BRANCHES * u
            cmp_ref[:, u * HEAD_DIM:(u + 1) * HEAD_DIM] = gate[:, c:c + 1] * o_cmp
        imp_ref[...] = imp

    n_var = ncp // CMP_ROW_CHUNK
    need = (t0 + tq) // CMP_STRIDE
    var = jnp.minimum((need + CMP_ROW_CHUNK - 1) // CMP_ROW_CHUNK, n_var) - 1
    for v in range(n_var):
        pl.when(var == v)(functools.partial(cmp_branch, CMP_ROW_CHUNK * (v + 1)))
    imp = imp_ref[...]

    j_io = lax.broadcasted_iota(jnp.int32, (nsp, tq), 0)
    t_sel = t0 + lax.broadcasted_iota(jnp.int32, (nsp, tq), 1)
    valid = j_io * SEL_BLOCK <= t_sel
    cur = t_sel // SEL_BLOCK
    forced = (j_io == 0) | (j_io == cur) | (j_io == cur - 1)
    score = jnp.where(valid & jnp.logical_not(forced), imp, NEG)
    for _ in range(k_sel - 3):
        m = jnp.max(score, axis=0, keepdims=True)
        first = jnp.min(jnp.where(score == m, j_io, nsp), axis=0, keepdims=True)
        score = jnp.where(j_io == first, -jnp.inf, score)
    sel_t = jnp.where(valid & (forced | (score == -jnp.inf)), 1.0, 0.0)
    sel_ref[0] = jnp.where(sel_t.T > 0.0, 0.0, SEL_OFF).astype(BF16)


def _nsa_select(heads, kcv, vct, gates, ovt, tq):
    _, s, _ = heads.shape
    ncp = kcv.shape[2]
    nsp = ovt.shape[0]
    hg = NSA_GROUP
    k_sel = min(N_SELECT, s // SEL_BLOCK)
    assert k_sel >= 3 and ncp % CMP_ROW_CHUNK == 0
    q_map = lambda u, g, qb: (H_QN + hg * g + u, qb, 0)
    return pl.pallas_call(
        functools.partial(_nsa_select_kernel, nsp=nsp, tq=tq, k_sel=k_sel),
        grid=(NSA_KV_GROUPS, s // tq),
        in_specs=[
            *[pl.BlockSpec((1, tq, HEAD_DIM), functools.partial(q_map, u)) for u in range(hg)],
            pl.BlockSpec((1, 1, ncp, HEAD_DIM), lambda g, qb: (0, g, 0, 0)),
            pl.BlockSpec((1, 1, HEAD_DIM, ncp), lambda g, qb: (1, g, 0, 0)),
            pl.BlockSpec((tq, HEAD_DIM), lambda g, qb: (qb, g)),
            pl.BlockSpec((nsp, ncp), lambda g, qb: (0, 0)),
        ],
        out_specs=[pl.BlockSpec((tq, hg * HEAD_DIM), lambda g, qb: (qb, g)),
                   pl.BlockSpec((1, tq, nsp), lambda g, qb: (g, qb, 0))],
        out_shape=[jax.ShapeDtypeStruct((s, D_NSA), F32),
                   jax.ShapeDtypeStruct((NSA_KV_GROUPS, s, nsp), BF16)],
        scratch_shapes=[pltpu.VMEM((nsp, tq), F32)],
        compiler_params=_params(("arbitrary", "arbitrary")),
        name="nsa_select",
    )(*([heads] * hg), kcv, vct, gates, ovt)


def _nsa_kernel(*refs, s_len, nsp, tq, kt):
    q_refs = refs[:NSA_GROUP]
    (ks_ref, vs_ref, kw_ref, vw_ref, gate_ref, cmp_ref, sel_ref,
     o_ref, kaug, vaug, vwaug, cbias, wbias, s_a, s_b, m_ref, acc_ref,
     part_ref) = refs[NSA_GROUP:]
    g = pl.program_id(0)
    qb = pl.program_id(1)
    t0 = qb * tq
    hg = NSA_GROUP
    rows = hg * tq
    ww = NSA_WINDOW + tq

    @pl.when((g == 0) & (qb == 0))
    def _():
        for r in range(kt // tq):
            d = (lax.broadcasted_iota(jnp.int32, (tq, kt), 0) + r * tq
                 - lax.broadcasted_iota(jnp.int32, (tq, kt), 1))
            cbias[r] = jnp.where(d >= 0, 0.0, NEG)
        for w in range(NSA_WINDOW // tq + 1):
            d = (lax.broadcasted_iota(jnp.int32, (tq, ww), 0) + w * tq
                 - lax.broadcasted_iota(jnp.int32, (tq, ww), 1))
            wbias[w] = jnp.where((d >= 0) & (d < NSA_WINDOW), 0.0, NEG)

    @pl.when(qb == 0)
    def _():
        kaug[:, 0:HEAD_DIM] = ks_ref[0]
        key = lax.broadcasted_iota(jnp.int32, (s_len, nsp), 0)
        blk = lax.broadcasted_iota(jnp.int32, (s_len, nsp), 1)
        kaug[:, HEAD_DIM:] = jnp.where(blk == key // SEL_BLOCK, 1.0, 0.0).astype(BF16)
        ones = jnp.ones((s_len, HEAD_DIM), BF16)
        vaug[:, 0:HEAD_DIM] = vs_ref[0]
        vaug[:, HEAD_DIM:] = ones
        vwaug[:, 0:HEAD_DIM] = vw_ref[0]
        vwaug[:, HEAD_DIM:] = ones

    gate = gate_ref[...]
    q5 = jnp.concatenate([r[0] for r in q_refs], axis=0)

    qaug = jnp.concatenate([q5, jnp.concatenate([sel_ref[0]] * hg, axis=0)], axis=1)

    w0 = pl.multiple_of(jnp.maximum(t0 - NSA_WINDOW, 0), tq)
    sw = _nt_dot(q5, kw_ref[0, pl.ds(w0, ww), :])
    sw = _add_per_head(sw, wbias[jnp.minimum(qb, NSA_WINDOW // tq)], hg)
    pw = jnp.exp2(sw - jnp.max(sw, axis=-1, keepdims=True))
    acc_w = jnp.dot(pw.astype(BF16), vwaug[pl.ds(w0, ww), :], preferred_element_type=F32)
    o_win = acc_w[:, :HEAD_DIM] / acc_w[:, HEAD_DIM:]
    for u in range(hg):
        c = N_NSA_BRANCHES * u + 2
        cols = slice(u * HEAD_DIM, (u + 1) * HEAD_DIM)
        part_ref[:, cols] = cmp_ref[:, cols] + gate[:, c:c + 1] * o_win[u * tq:(u + 1) * tq]

    def sel_scores(kti, width=kt):
        k0 = pl.multiple_of(kti * kt, kt)
        return _nt_dot(qaug, kaug[pl.ds(k0, width), :])

    def sel_update(s_ref, kti, diag=None):
        width = kt if diag is None else (diag + 1) * tq
        s = s_ref[:, :width]
        if diag is not None:
            s = _add_per_head(s, cbias[diag, :, :width], hg)
        k0 = pl.multiple_of(kti * kt, kt)
        m_i = m_ref[...]
        m_new = jnp.maximum(m_i, jnp.max(s, axis=-1, keepdims=True))
        p = jnp.exp2(s - m_new)
        pv = jnp.dot(p.astype(BF16), vaug[pl.ds(k0, width), :], preferred_element_type=F32)
        acc_ref[...] = jnp.exp2(m_i - m_new) * acc_ref[...] + pv
        m_ref[...] = m_new

    def sel_pair(pi, _):
        t = 2 * pi
        s_b[...] = sel_scores(t + 1)
        sel_update(s_a, t)
        s_a[...] = sel_scores(t + 2)
        sel_update(s_b, t + 1)
        return 0

    last = (t0 + tq + kt - 1) // kt - 1
    m_ref[...] = jnp.full((rows, 1), NEG, F32)
    acc_ref[...] = jnp.zeros((rows, 2 * HEAD_DIM), F32)
    s_a[...] = sel_scores(0)
    lax.fori_loop(0, last // 2, sel_pair, 0)
    t_even = 2 * (last // 2)

    def tail_odd(diag):
        width = (diag + 1) * tq
        s_b[:, :width] = sel_scores(t_even + 1, width)
        sel_update(s_a, t_even)
        sel_update(s_b, t_even + 1, diag)

    def tail_even(diag):
        sel_update(s_a, t_even, diag)

    for diag in range(kt // tq):
        here = qb % (kt // tq) == diag
        pl.when(here & (last % 2 == 1))(functools.partial(tail_odd, diag))
        pl.when(here & (last % 2 == 0))(functools.partial(tail_even, diag))

    gate = gate_ref[...]
    for u in range(hg):
        c = N_NSA_BRANCHES * u + 1
        acc_u = acc_ref[u * tq:(u + 1) * tq, :]
        o_slc = acc_u[:, :HEAD_DIM] / acc_u[:, HEAD_DIM:]
        cols = slice(u * HEAD_DIM, (u + 1) * HEAD_DIM)
        o_ref[:, cols] = (part_ref[:, cols] + gate[:, c:c + 1] * o_slc).astype(BF16)


def _nsa(heads, gates, part_cmp, sel_off, tq, kt):
    _, s, _ = heads.shape
    nsp = sel_off.shape[2]
    hg = NSA_GROUP
    assert kt % tq == 0 and NSA_WINDOW % tq == 0 and s % kt == 0
    q_map = lambda u, g, qb: (H_QN + hg * g + u, qb, 0)
    full = lambda hbase: pl.BlockSpec((1, s, HEAD_DIM), lambda g, qb: (hbase + g, 0, 0),
                                      pipeline_mode=pl.Buffered(1))
    return pl.pallas_call(
        functools.partial(_nsa_kernel, s_len=s, nsp=nsp, tq=tq, kt=kt),
        grid=(NSA_KV_GROUPS, s // tq),
        in_specs=[
            *[pl.BlockSpec((1, tq, HEAD_DIM), functools.partial(q_map, u)) for u in range(hg)],
            full(H_KS), full(H_VS), full(H_KW), full(H_VW),
            pl.BlockSpec((tq, HEAD_DIM), lambda g, qb: (qb, g)),
            pl.BlockSpec((tq, hg * HEAD_DIM), lambda g, qb: (qb, g)),
            pl.BlockSpec((1, tq, nsp), lambda g, qb: (g, qb, 0)),
        ],
        out_specs=pl.BlockSpec((tq, hg * HEAD_DIM), lambda g, qb: (qb, g)),
        out_shape=jax.ShapeDtypeStruct((s, D_NSA), BF16),
        scratch_shapes=[pltpu.VMEM((s, HEAD_DIM + nsp), BF16),
                        pltpu.VMEM((s, 2 * HEAD_DIM), BF16),
                        pltpu.VMEM((s, 2 * HEAD_DIM), BF16),
                        pltpu.VMEM((kt // tq, tq, kt), F32),
                        pltpu.VMEM((NSA_WINDOW // tq + 1, tq, NSA_WINDOW + tq), F32),
                        pltpu.VMEM((hg * tq, kt), F32),
                        pltpu.VMEM((hg * tq, kt), F32),
                        pltpu.VMEM((hg * tq, 1), F32),
                        pltpu.VMEM((hg * tq, 2 * HEAD_DIM), F32),
                        pltpu.VMEM((tq, hg * HEAD_DIM), F32)],
        compiler_params=_params(("arbitrary", "arbitrary")),
        name="nsa_attn",
    )(*([heads] * hg), heads, heads, heads, heads, gates, part_cmp, sel_off)


def _out_proj_ln_kernel(a_ref, b_ref, w_ref, h_ref, g_ref, be_ref, o_ref, wb16, *, alpha):
    @pl.when(pl.program_id(0) == 0)
    def _():
        wb16[...] = w_ref[...].astype(BF16)

    tm = o_ref.shape[0]
    rc = tm // ROW_CHUNKS
    for c in range(ROW_CHUNKS):
        r = slice(c * rc, (c + 1) * rc)
        mix = (jnp.dot(a_ref[r, :], wb16[:D_DIL, :], preferred_element_type=F32)
               + jnp.dot(b_ref[r, :], wb16[D_DIL:, :], preferred_element_type=F32))
        o_ref[r, :] = _layer_norm(alpha * h_ref[r, :] + mix, g_ref[...], be_ref[...])


def _out_proj_ln(mix_a, mix_b, w, h, g, b, alpha, tm):
    s, d = h.shape
    return pl.pallas_call(
        functools.partial(_out_proj_ln_kernel, alpha=alpha),
        grid=(s // tm,),
        in_specs=[
            pl.BlockSpec((tm, D_DIL), lambda i: (i, 0)),
            pl.BlockSpec((tm, D_NSA), lambda i: (i, 0)),
            pl.BlockSpec((d, d), lambda i: (0, 0), pipeline_mode=pl.Buffered(1)),
            pl.BlockSpec((tm, d), lambda i: (i, 0)),
            pl.BlockSpec((1, d), lambda i: (0, 0)),
            pl.BlockSpec((1, d), lambda i: (0, 0)),
        ],
        out_specs=pl.BlockSpec((tm, d), lambda i: (i, 0)),
        out_shape=jax.ShapeDtypeStruct((s, d), F32),
        scratch_shapes=[pltpu.VMEM((d, d), BF16)],
        compiler_params=_params(("arbitrary",)),
        name="out_proj_ln",
    )(mix_a, mix_b, w, h, g, b)


def _overlap_t(s):
    n_cmp = (s - CMP_BLOCK) // CMP_STRIDE + 1
    n_sel = s // SEL_BLOCK
    ncp = s // CMP_STRIDE
    nsp = -(-n_sel // HEAD_DIM) * HEAD_DIM
    c_lo = np.arange(n_cmp) * CMP_STRIDE
    c_hi = c_lo + CMP_BLOCK - 1
    s_lo = (np.arange(n_sel) * SEL_BLOCK)[:, None]
    ov = np.zeros((nsp, ncp), np.float32)
    ov[:n_sel, :n_cmp] = (c_lo[None, :] <= s_lo + SEL_BLOCK - 1) & (c_hi[None, :] >= s_lo)
    return jnp.asarray(ov, BF16)


def kernel(x, positions, ln1_g, ln1_b, ffn1_w1, ffn1_w3, ffn1_w2, w_in, gate_b, cmp_pe, cmp_w1,
           cmp_b1, cmp_w2, cmp_b2, w_out, ln2_g, ln2_b, ffn2_w1, ffn2_w3, ffn2_w2, ln3_g, ln3_b):
    bsz, s, d = x.shape
    assert bsz == 1 and d == (N_HEADS_DIL + N_HEADS_NSA) * HEAD_DIM
    assert s % max(FFN_ROWS, PROJ_ROWS, NSA_KEY_TILE) == 0 and s >= NSA_WINDOW + NSA_Q_TILE
    assert ffn1_w1.shape[2] % FFN_HIDDEN_TILE == 0
    alpha = (2.0 * DEPTH) ** 0.25
    scale = HEAD_DIM ** -0.5 * LOG2_E
    row = lambda v: v.reshape(1, -1)

    inv_freq = ROPE_THETA ** (-jnp.arange(0, HEAD_DIM, 2, dtype=F32) / HEAD_DIM)
    invf = jnp.concatenate([inv_freq, inv_freq]).reshape(1, HEAD_DIM)
    pos = positions[0].astype(F32).reshape(s, 1)
    cosf, sinf = _rope_tables(pos, pos, invf, ROPE_TILE)
    ncp = s // CMP_STRIDE
    n_cmp = (s - CMP_BLOCK) // CMP_STRIDE + 1
    pos_lo = jnp.pad(pos[0:n_cmp * CMP_STRIDE:CMP_STRIDE], ((0, ncp - n_cmp), (0, 0)))
    pos_hi = jnp.pad(pos[CMP_BLOCK - 1::CMP_STRIDE][:n_cmp], ((0, ncp - n_cmp), (0, 0)))
    cos_c, sin_c = _rope_tables(pos_lo, pos_hi, invf, ncp)

    g_off = 3 * D_DIL + D_NSA + 6 * KV_W
    per_g = NSA_GROUP * N_NSA_BRANCHES
    w_in_t = w_in[0].T.astype(BF16)
    wg = jnp.zeros((NSA_KV_GROUPS * HEAD_DIM, d), BF16)
    bg = jnp.zeros((1, NSA_KV_GROUPS * HEAD_DIM), F32)
    for g in range(NSA_KV_GROUPS):
        wg = wg.at[g * HEAD_DIM:g * HEAD_DIM + per_g].set(
            w_in_t[g_off + g * per_g:g_off + (g + 1) * per_g])
        bg = bg.at[0, g * HEAD_DIM:g * HEAD_DIM + per_g].set(gate_b[0][g * per_g:(g + 1) * per_g])

    h0 = x[0]
    h1 = _ffn_ln(h0, ffn1_w1[0], ffn1_w3[0], ffn1_w2[0], row(ln1_g[0]), row(ln1_b[0]),
                 alpha, FFN_ROWS, FFN_HIDDEN_TILE)

    heads, gates = _in_proj(h1, w_in_t, cosf, sinf, wg, bg, scale, PROJ_ROWS)
    tok16 = heads[H_KC:H_KS].reshape(2 * NSA_KV_GROUPS, ncp, CMP_STRIDE * HEAD_DIM)
    kcv, kcv_t = _compress(
        tok16, cmp_pe[0].reshape(2, 1, CMP_BLOCK * HEAD_DIM), cmp_w1[0],
        cmp_b1[0].reshape(2, 1, CMP_HIDDEN), cmp_w2[0],
        cmp_b2[0].reshape(2, 1, HEAD_DIM), cos_c, sin_c)
    mix_a = _dilated(heads, DIL_Q_TILE)
    part_cmp, sel_off = _nsa_select(heads, kcv, kcv_t, gates, _overlap_t(s), SEL_Q_TILE)
    mix_b = _nsa(heads, gates, part_cmp, sel_off, NSA_Q_TILE, NSA_KEY_TILE)

    h2 = _out_proj_ln(mix_a, mix_b, w_out[0], h1, row(ln2_g[0]), row(ln2_b[0]), alpha,
                      OUT_PROJ_ROWS)
    h3 = _ffn_ln(h2, ffn2_w1[0], ffn2_w3[0], ffn2_w2[0], row(ln3_g[0]), row(ln3_b[0]),
                 alpha, FFN_ROWS, FFN_HIDDEN_TILE)
    return h3.reshape(bsz, s, d)
```

```python
import functools

import jax
import jax.numpy as jnp
import numpy as np
from jax import lax
from jax.experimental import pallas as pl
from jax.experimental.pallas import tpu as pltpu

HEAD_DIM = 128
N_HEADS_DIL = 6
N_HEADS_NSA = 10
NSA_KV_GROUPS = 2
NSA_GROUP = N_HEADS_NSA // NSA_KV_GROUPS
N_NSA_BRANCHES = 3
DIL_PAIRS = ((128, 1), (512, 4), (2048, 16))
CMP_BLOCK = 32
CMP_STRIDE = 16
CMP_HIDDEN = 256
SEL_BLOCK = 64
N_SELECT = 16
NSA_WINDOW = 512
ROPE_THETA = 10000.0
LN_EPS = 1e-5
NEG = -1e30
SEL_OFF = -1e9
LOG2_E = 1.4426950408889634
DEPTH = 1

D_DIL = N_HEADS_DIL * HEAD_DIM
D_NSA = N_HEADS_NSA * HEAD_DIM
KV_W = NSA_KV_GROUPS * HEAD_DIM

H_QA, H_KA, H_VA, H_QN = 0, 6, 12, 18
H_KC, H_VC, H_KS, H_VS, H_KW, H_VW = 28, 30, 32, 34, 36, 38
N_PROJ_HEADS = 40
HEADS_PER_PROJ_BLOCK = 8
PROJ_SUB_HEADS = 2
ROW_CHUNKS = 2
CMP_ROW_CHUNK = 128

ROPE_TILE = 512
FFN_ROWS, FFN_HIDDEN_TILE = 1024, 256
PROJ_ROWS = 1024
OUT_PROJ_ROWS = 512
DIL_Q_TILE = 256
SEL_Q_TILE = 512
NSA_Q_TILE, NSA_KEY_TILE = 256, 1024

VMEM_LIMIT_BYTES = 56 * 1024 * 1024
FFN_VMEM_LIMIT_BYTES = 60 * 1024 * 1024

F32 = jnp.float32
BF16 = jnp.bfloat16


def _params(sem, vmem=VMEM_LIMIT_BYTES):
    return pltpu.CompilerParams(dimension_semantics=sem, vmem_limit_bytes=vmem)


def _nt_dot(a, b):
    return lax.dot_general(a, b, (((1,), (1,)), ((), ())), preferred_element_type=F32)


def _layer_norm(y, g, b):
    mu = jnp.mean(y, axis=-1, keepdims=True)
    yc = y - mu
    var = jnp.mean(yc * yc, axis=-1, keepdims=True)
    return yc * lax.rsqrt(var + LN_EPS) * g + b


def _rope_table_kernel(pa_ref, pb_ref, invf_ref, cos_ref, sin_ref):
    pos = (pa_ref[...] + pb_ref[...]) * 0.5
    ang = pos * invf_ref[...]
    lane = lax.broadcasted_iota(jnp.int32, ang.shape, 1)
    cos_ref[...] = jnp.cos(ang)
    sin_ref[...] = jnp.where(lane < HEAD_DIM // 2, -1.0, 1.0) * jnp.sin(ang)


def _rope_tables(pos_a, pos_b, invf, tile):
    n = pos_a.shape[0]
    spec_p = pl.BlockSpec((tile, 1), lambda i: (i, 0))
    spec_t = pl.BlockSpec((tile, HEAD_DIM), lambda i: (i, 0))
    return pl.pallas_call(
        _rope_table_kernel,
        grid=(n // tile,),
        in_specs=[spec_p, spec_p, pl.BlockSpec((1, HEAD_DIM), lambda i: (0, 0))],
        out_specs=[spec_t, spec_t],
        out_shape=[jax.ShapeDtypeStruct((n, HEAD_DIM), F32)] * 2,
        compiler_params=_params(("arbitrary",)),
        name="rope_tables",
    )(pos_a, pos_b, invf)


def _ffn_ln_kernel(h_ref, w1_ref, w3_ref, w2_ref, g_ref, b_ref, o_ref, hb_ref, *, alpha, nf):
    f = pl.program_id(1)

    @pl.when(f == 0)
    def _():
        h = h_ref[...]
        hb_ref[...] = h.astype(BF16)
        o_ref[...] = (2.0 * alpha) * h

    hb = hb_ref[...]
    a = jnp.dot(hb, w1_ref[...].astype(BF16), preferred_element_type=F32)
    b = jnp.dot(hb, w3_ref[...].astype(BF16), preferred_element_type=F32)
    act = (a * jax.nn.sigmoid(a)) * b
    o_ref[...] += jnp.dot(act.astype(BF16), w2_ref[...].astype(BF16),
                          preferred_element_type=F32)

    @pl.when(f == nf - 1)
    def _():
        o_ref[...] = _layer_norm(0.5 * o_ref[...], g_ref[...], b_ref[...])


def _ffn_ln(h, w1, w3, w2, g, b, alpha, tm, tf):
    s, d = h.shape
    dff = w1.shape[1]
    nf = dff // tf
    return pl.pallas_call(
        functools.partial(_ffn_ln_kernel, alpha=alpha, nf=nf),
        grid=(s // tm, nf),
        in_specs=[
            pl.BlockSpec((tm, d), lambda i, f: (i, 0)),
            pl.BlockSpec((d, tf), lambda i, f: (0, f)),
            pl.BlockSpec((d, tf), lambda i, f: (0, f)),
            pl.BlockSpec((tf, d), lambda i, f: (f, 0)),
            pl.BlockSpec((1, d), lambda i, f: (0, 0)),
            pl.BlockSpec((1, d), lambda i, f: (0, 0)),
        ],
        out_specs=pl.BlockSpec((tm, d), lambda i, f: (i, 0)),
        out_shape=jax.ShapeDtypeStruct((s, d), F32),
        scratch_shapes=[pltpu.VMEM((tm, d), BF16)],
        compiler_params=_params(("arbitrary", "arbitrary"), FFN_VMEM_LIMIT_BYTES),
        name="ffn_ln",
    )(h, w1, w3, w2, g, b)


def _in_proj_kernel(h_ref, w_ref, cos_ref, sin_ref, wg_ref, bg_ref, o_ref, gate_ref, hb_ref,
                    *, scale):
    j = pl.program_id(1)
    hpb = HEADS_PER_PROJ_BLOCK
    sub = PROJ_SUB_HEADS

    @pl.when(j == 0)
    def _():
        hb_ref[...] = h_ref[...].astype(BF16)
        gate_ref[...] = jax.nn.sigmoid(_nt_dot(hb_ref[...], wg_ref[...]) + bg_ref[...])

    cos = cos_ref[...]
    sin = sin_ref[...]
    h = hb_ref[...]
    for k0 in range(0, hpb, sub):
        acc = _nt_dot(h, w_ref[k0 * HEAD_DIM:(k0 + sub) * HEAD_DIM, :])
        for k in range(sub):
            hh = j * hpb + k0 + k
            is_q = (hh < H_KA) | ((hh >= H_QN) & (hh < H_KC))
            rope = (is_q | (hh < H_VA) | ((hh >= H_KS) & (hh < H_VS))
                    | ((hh >= H_KW) & (hh < H_VW)))
            x = acc[:, k * HEAD_DIM:(k + 1) * HEAD_DIM]
            xr = x * cos + pltpu.roll(x, HEAD_DIM // 2, 1) * sin
            y = jnp.where(rope, xr, x) * jnp.where(is_q, scale, 1.0).astype(F32)
            o_ref[k0 + k] = y.astype(BF16)


def _in_proj(h, w, cosf, sinf, wg, bg, scale, tm):
    s, d = h.shape
    n = wg.shape[0]
    hpb = HEADS_PER_PROJ_BLOCK
    return pl.pallas_call(
        functools.partial(_in_proj_kernel, scale=scale),
        grid=(s // tm, N_PROJ_HEADS // hpb),
        in_specs=[
            pl.BlockSpec((tm, d), lambda i, j: (i, 0)),
            pl.BlockSpec((hpb * HEAD_DIM, d), lambda i, j: (j, 0)),
            pl.BlockSpec((tm, HEAD_DIM), lambda i, j: (i, 0)),
            pl.BlockSpec((tm, HEAD_DIM), lambda i, j: (i, 0)),
            pl.BlockSpec((n, d), lambda i, j: (0, 0)),
            pl.BlockSpec((1, n), lambda i, j: (0, 0)),
        ],
        out_specs=[pl.BlockSpec((hpb, tm, HEAD_DIM), lambda i, j: (j, i, 0)),
                   pl.BlockSpec((tm, n), lambda i, j: (i, 0))],
        out_shape=[jax.ShapeDtypeStruct((N_PROJ_HEADS, s, HEAD_DIM), BF16),
                   jax.ShapeDtypeStruct((s, n), F32)],
        scratch_shapes=[pltpu.VMEM((tm, d), BF16)],
        compiler_params=_params(("arbitrary", "arbitrary")),
        name="in_proj",
    )(h, w, cosf, sinf, wg, bg)


def _compress_kernel(tok_ref, pe_ref, w1_ref, b1_ref, w2_ref, b2_ref, cos_ref, sin_ref,
                     o_ref, ot_ref):
    j = pl.program_id(0)
    half = CMP_STRIDE * HEAD_DIM
    tok = tok_ref[0].astype(F32)
    pe = pe_ref[0]
    w1 = w1_ref[0].astype(BF16)
    top = (tok + pe[:, :half]).astype(BF16)
    bot = (tok + pe[:, half:]).astype(BF16)
    u = jnp.dot(top, w1[:half], preferred_element_type=F32)
    v = jnp.dot(bot, w1[half:], preferred_element_type=F32)
    nc = u.shape[0]
    hid = u + pltpu.roll(v, nc - 1, 0) + b1_ref[0]
    hid = jax.nn.gelu(hid)
    out = jnp.dot(hid.astype(BF16), w2_ref[0].astype(BF16),
                  preferred_element_type=F32) + b2_ref[0]
    roped = out * cos_ref[...] + pltpu.roll(out, HEAD_DIM // 2, 1) * sin_ref[...]
    out = jnp.where(j == 0, roped, out)
    o_ref[0, 0] = out.astype(BF16)
    ot_ref[0, 0] = out.T.astype(BF16)


def _compress(tok16, pe, w1, b1, w2, b2, cos_c, sin_c):
    nc = tok16.shape[1]
    blk = CMP_BLOCK * HEAD_DIM
    g = NSA_KV_GROUPS
    return pl.pallas_call(
        _compress_kernel,
        grid=(2, g),
        in_specs=[
            pl.BlockSpec((1, nc, CMP_STRIDE * HEAD_DIM), lambda j, gi: (g * j + gi, 0, 0)),
            pl.BlockSpec((1, 1, blk), lambda j, gi: (j, 0, 0)),
            pl.BlockSpec((1, blk, CMP_HIDDEN), lambda j, gi: (j, 0, 0)),
            pl.BlockSpec((1, 1, CMP_HIDDEN), lambda j, gi: (j, 0, 0)),
            pl.BlockSpec((1, CMP_HIDDEN, HEAD_DIM), lambda j, gi: (j, 0, 0)),
            pl.BlockSpec((1, 1, HEAD_DIM), lambda j, gi: (j, 0, 0)),
            pl.BlockSpec((nc, HEAD_DIM), lambda j, gi: (0, 0)),
            pl.BlockSpec((nc, HEAD_DIM), lambda j, gi: (0, 0)),
        ],
        out_specs=[pl.BlockSpec((1, 1, nc, HEAD_DIM), lambda j, gi: (j, gi, 0, 0)),
                   pl.BlockSpec((1, 1, HEAD_DIM, nc), lambda j, gi: (j, gi, 0, 0))],
        out_shape=[jax.ShapeDtypeStruct((2, g, nc, HEAD_DIM), BF16),
                   jax.ShapeDtypeStruct((2, g, HEAD_DIM, nc), BF16)],
        compiler_params=_params(("arbitrary", "arbitrary")),
        name="nsa_compress",
    )(tok16, pe, w1, b1, w2, b2, cos_c, sin_c)


def _dil_kernel(q_ref, k_ref, v_ref, o_ref, bias_ref, s_a, s_b, *, tq, padk):
    w = padk + tq
    n_tiles = q_ref.shape[1] // tq

    @pl.when(pl.program_id(0) == 0)
    def _():
        r = lax.broadcasted_iota(jnp.int32, (tq, w), 0)
        c = lax.broadcasted_iota(jnp.int32, (tq, w), 1)
        d = r + padk - c
        cnt = jnp.zeros((tq, w), F32)
        for window, dil in DIL_PAIRS:
            hit = (d >= 0) & (d <= window) & ((d & (dil - 1)) == 0)
            cnt = cnt + jnp.where(hit, 1.0, 0.0)
        bias_ref[...] = jnp.where(cnt > 0.0, jnp.log2(jnp.maximum(cnt, 1.0)), NEG)

    n_early = min(padk // tq, n_tiles)

    def strip(t):
        if isinstance(t, int):
            return (0, (t + 1) * tq) if t < n_early else (t * tq - padk, w)
        return pl.multiple_of(t * tq - padk, tq), w

    def scores(s_ref, t):
        k0, width = strip(t)
        q0 = t * tq if isinstance(t, int) else pl.multiple_of(t * tq, tq)
        s_ref[:, :width] = (_nt_dot(q_ref[0, pl.ds(q0, tq), :], k_ref[0, pl.ds(k0, width), :])
                            + bias_ref[:, w - width:])

    def finish(s_ref, t):
        k0, width = strip(t)
        q0 = t * tq if isinstance(t, int) else pl.multiple_of(t * tq, tq)
        s = s_ref[:, :width]
        p = jnp.exp2(s - jnp.max(s, axis=-1, keepdims=True))
        l = jnp.sum(p, axis=-1, keepdims=True)
        o = jnp.dot(p.astype(BF16), v_ref[0, pl.ds(k0, width), :], preferred_element_type=F32)
        o_ref[pl.ds(q0, tq), :] = (o / l).astype(BF16)

    bufs = (s_a, s_b)
    scores(s_a, 0)
    for t in range(n_early):
        if t + 1 < n_tiles:
            scores(bufs[(t + 1) % 2], t + 1)
        finish(bufs[t % 2], t)

    def pair(pi, _):
        t = 2 * pi
        scores(s_b, t + 1)
        finish(s_a, t)
        scores(s_a, t + 2)
        finish(s_b, t + 1)
        return 0

    if n_early < n_tiles:
        lax.fori_loop(n_early // 2, n_tiles // 2 - 1, pair, 0)
        scores(s_b, n_tiles - 1)
        finish(s_a, n_tiles - 2)
        finish(s_b, n_tiles - 1)


def _dilated(heads, tq):
    _, s, _ = heads.shape
    padk = max(wd for wd, _ in DIL_PAIRS)
    n_tiles, n_early = s // tq, min(padk // tq, s // tq)
    assert padk % tq == 0 and n_tiles % 2 == 0 and n_early % 2 == 0
    return pl.pallas_call(
        functools.partial(_dil_kernel, tq=tq, padk=padk),
        grid=(N_HEADS_DIL,),
        in_specs=[
            pl.BlockSpec((1, s, HEAD_DIM), lambda h: (H_QA + h, 0, 0)),
            pl.BlockSpec((1, s, HEAD_DIM), lambda h: (H_KA + h, 0, 0)),
            pl.BlockSpec((1, s, HEAD_DIM), lambda h: (H_VA + h, 0, 0)),
        ],
        out_specs=pl.BlockSpec((s, HEAD_DIM), lambda h: (0, h)),
        out_shape=jax.ShapeDtypeStruct((s, D_DIL), BF16),
        scratch_shapes=[pltpu.VMEM((tq, padk + tq), F32),
                        pltpu.VMEM((tq, padk + tq), F32),
                        pltpu.VMEM((tq, padk + tq), F32)],
        compiler_params=_params(("arbitrary",)),
        name="dilated_attn",
    )(heads, heads, heads)


def _add_per_head(s, bias, hg):
    tq = bias.shape[0]
    return jnp.concatenate([s[u * tq:(u + 1) * tq] + bias for u in range(hg)], axis=0)


def _nsa_select_kernel(*refs, nsp, tq, k_sel):
    q_refs = refs[:NSA_GROUP]
    kc_ref, vct_ref, gate_ref, ovt_ref, cmp_ref, sel_ref = refs[NSA_GROUP:]
    t0 = pl.program_id(1) * tq
    hg = NSA_GROUP
    ncp = kc_ref.shape[2]
    gate = gate_ref[...]

    def branch(n_rows):
        n_sel = min(n_rows * CMP_STRIDE // SEL_BLOCK, nsp)
        kc = kc_ref[0, 0, :n_rows, :]
        vct = vct_ref[0, 0, :, :n_rows]
        ovt = ovt_ref[:n_sel, :n_rows]
        n_io = lax.broadcasted_iota(jnp.int32, (n_rows, tq), 0)
        t_io = t0 + lax.broadcasted_iota(jnp.int32, (n_rows, tq), 1)
        cmask = (n_io * CMP_STRIDE + (CMP_BLOCK - 1) <= t_io) & (n_io < ncp - 1)
        imp = jnp.zeros((n_sel, tq), F32)
        for u in range(hg):
            st = jnp.where(cmask, _nt_dot(kc, q_refs[u][0]), NEG)
            m = jnp.maximum(jnp.max(st, axis=0, keepdims=True), 0.1 * NEG)
            e = jnp.exp2(st - m)
            r = 1.0 / jnp.maximum(jnp.sum(e, axis=0, keepdims=True), 1e-30)
            eb = e.astype(BF16)
            imp = imp + jnp.dot(ovt, eb, preferred_element_type=F32) * r
            o_cmp = (jnp.dot(vct, eb, preferred_element_type=F32) * r).T
            c = N_NSA_BRANCHES * u
            cmp_ref[:, u * HEAD_DIM:(u + 1) * HEAD_DIM] = gate[:, c:c + 1] * o_cmp

        j_io = lax.broadcasted_iota(jnp.int32, (n_sel, tq), 0)
        t_sel = t0 + lax.broadcasted_iota(jnp.int32, (n_sel, tq), 1)
        valid = j_io * SEL_BLOCK <= t_sel
        cur = t_sel // SEL_BLOCK
        forced = (j_io == 0) | (j_io == cur) | (j_io == cur - 1)
        score = jnp.where(valid & jnp.logical_not(forced), imp, NEG)
        for _ in range(k_sel - 3):
            m = jnp.max(score, axis=0, keepdims=True)
            first = jnp.min(jnp.where(score == m, j_io, n_sel), axis=0, keepdims=True)
            score = jnp.where(j_io == first, -jnp.inf, score)
        sel_t = jnp.where(valid & (forced | (score == -jnp.inf)), 1.0, 0.0)
        if n_sel < nsp:
            sel_t = jnp.concatenate([sel_t, jnp.zeros((nsp - n_sel, tq), F32)], axis=0)
        sel_ref[0] = jnp.where(sel_t.T > 0.0, 0.0, SEL_OFF).astype(BF16)

    n_var = ncp // CMP_ROW_CHUNK
    need = (t0 + tq) // CMP_STRIDE
    var = jnp.minimum((need + CMP_ROW_CHUNK - 1) // CMP_ROW_CHUNK, n_var) - 1
    for v in range(n_var):
        pl.when(var == v)(functools.partial(branch, CMP_ROW_CHUNK * (v + 1)))


def _nsa_select(heads, kcv, vct, gates, ovt, tq):
    _, s, _ = heads.shape
    ncp = kcv.shape[2]
    nsp = ovt.shape[0]
    hg = NSA_GROUP
    k_sel = min(N_SELECT, s // SEL_BLOCK)
    assert k_sel >= 3 and ncp % CMP_ROW_CHUNK == 0
    q_map = lambda u, g, qb: (H_QN + hg * g + u, qb, 0)
    return pl.pallas_call(
        functools.partial(_nsa_select_kernel, nsp=nsp, tq=tq, k_sel=k_sel),
        grid=(NSA_KV_GROUPS, s // tq),
        in_specs=[
            *[pl.BlockSpec((1, tq, HEAD_DIM), functools.partial(q_map, u)) for u in range(hg)],
            pl.BlockSpec((1, 1, ncp, HEAD_DIM), lambda g, qb: (0, g, 0, 0)),
            pl.BlockSpec((1, 1, HEAD_DIM, ncp), lambda g, qb: (1, g, 0, 0)),
            pl.BlockSpec((tq, HEAD_DIM), lambda g, qb: (qb, g)),
            pl.BlockSpec((nsp, ncp), lambda g, qb: (0, 0)),
        ],
        out_specs=[pl.BlockSpec((tq, hg * HEAD_DIM), lambda g, qb: (qb, g)),
                   pl.BlockSpec((1, tq, nsp), lambda g, qb: (g, qb, 0))],
        out_shape=[jax.ShapeDtypeStruct((s, D_NSA), F32),
                   jax.ShapeDtypeStruct((NSA_KV_GROUPS, s, nsp), BF16)],
        compiler_params=_params(("arbitrary", "arbitrary")),
        name="nsa_select",
    )(*([heads] * hg), kcv, vct, gates, ovt)


def _nsa_kernel(*refs, s_len, nsp, tq, kt):
    q_refs = refs[:NSA_GROUP]
    (ks_ref, vs_ref, kw_ref, vw_ref, gate_ref, cmp_ref, sel_ref,
     o_ref, kaug, vaug, vwaug, cbias, wbias, s_a, s_b, m_ref, acc_ref,
     part_ref) = refs[NSA_GROUP:]
    g = pl.program_id(0)
    qb = pl.program_id(1)
    t0 = qb * tq
    hg = NSA_GROUP
    rows = hg * tq
    ww = NSA_WINDOW + tq

    @pl.when((g == 0) & (qb == 0))
    def _():
        for r in range(kt // tq):
            d = (lax.broadcasted_iota(jnp.int32, (tq, kt), 0) + r * tq
                 - lax.broadcasted_iota(jnp.int32, (tq, kt), 1))
            cbias[r] = jnp.where(d >= 0, 0.0, NEG)
        for w in range(NSA_WINDOW // tq + 1):
            d = (lax.broadcasted_iota(jnp.int32, (tq, ww), 0) + w * tq
                 - lax.broadcasted_iota(jnp.int32, (tq, ww), 1))
            wbias[w] = jnp.where((d >= 0) & (d < NSA_WINDOW), 0.0, NEG)

    @pl.when(qb == 0)
    def _():
        kaug[:, 0:HEAD_DIM] = ks_ref[0]
        key = lax.broadcasted_iota(jnp.int32, (s_len, nsp), 0)
        blk = lax.broadcasted_iota(jnp.int32, (s_len, nsp), 1)
        kaug[:, HEAD_DIM:] = jnp.where(blk == key // SEL_BLOCK, 1.0, 0.0).astype(BF16)
        ones = jnp.ones((s_len, HEAD_DIM), BF16)
        vaug[:, 0:HEAD_DIM] = vs_ref[0]
        vaug[:, HEAD_DIM:] = ones
        vwaug[:, 0:HEAD_DIM] = vw_ref[0]
        vwaug[:, HEAD_DIM:] = ones

    gate = gate_ref[...]
    q5 = jnp.concatenate([r[0] for r in q_refs], axis=0)

    qaug = jnp.concatenate([q5, jnp.concatenate([sel_ref[0]] * hg, axis=0)], axis=1)

    w0 = pl.multiple_of(jnp.maximum(t0 - NSA_WINDOW, 0), tq)
    sw = _nt_dot(q5, kw_ref[0, pl.ds(w0, ww), :])
    sw = _add_per_head(sw, wbias[jnp.minimum(qb, NSA_WINDOW // tq)], hg)
    pw = jnp.exp2(sw - jnp.max(sw, axis=-1, keepdims=True))
    acc_w = jnp.dot(pw.astype(BF16), vwaug[pl.ds(w0, ww), :], preferred_element_type=F32)
    o_win = acc_w[:, :HEAD_DIM] / acc_w[:, HEAD_DIM:]
    for u in range(hg):
        c = N_NSA_BRANCHES * u + 2
        cols = slice(u * HEAD_DIM, (u + 1) * HEAD_DIM)
        part_ref[:, cols] = cmp_ref[:, cols] + gate[:, c:c + 1] * o_win[u * tq:(u + 1) * tq]

    def sel_scores(kti, width=kt):
        k0 = pl.multiple_of(kti * kt, kt)
        return _nt_dot(qaug, kaug[pl.ds(k0, width), :])

    def sel_update(s_ref, kti, diag=None):
        width = kt if diag is None else (diag + 1) * tq
        s = s_ref[:, :width]
        if diag is not None:
            s = _add_per_head(s, cbias[diag, :, :width], hg)
        k0 = pl.multiple_of(kti * kt, kt)
        m_i = m_ref[...]
        m_new = jnp.maximum(m_i, jnp.max(s, axis=-1, keepdims=True))
        p = jnp.exp2(s - m_new)
        pv = jnp.dot(p.astype(BF16), vaug[pl.ds(k0, width), :], preferred_element_type=F32)
        acc_ref[...] = jnp.exp2(m_i - m_new) * acc_ref[...] + pv
        m_ref[...] = m_new

    def sel_pair(pi, _):
        t = 2 * pi
        s_b[...] = sel_scores(t + 1)
        sel_update(s_a, t)
        s_a[...] = sel_scores(t + 2)
        sel_update(s_b, t + 1)
        return 0

    last = (t0 + tq + kt - 1) // kt - 1
    m_ref[...] = jnp.full((rows, 1), NEG, F32)
    acc_ref[...] = jnp.zeros((rows, 2 * HEAD_DIM), F32)
    s_a[...] = sel_scores(0)
    lax.fori_loop(0, last // 2, sel_pair, 0)
    t_even = 2 * (last // 2)

    def tail_odd(diag):
        width = (diag + 1) * tq
        s_b[:, :width] = sel_scores(t_even + 1, width)
        sel_update(s_a, t_even)
        sel_update(s_b, t_even + 1, diag)

    def tail_even(diag):
        sel_update(s_a, t_even, diag)

    for diag in range(kt // tq):
        here = qb % (kt // tq) == diag
        pl.when(here & (last % 2 == 1))(functools.partial(tail_odd, diag))
        pl.when(here & (last % 2 == 0))(functools.partial(tail_even, diag))

    gate = gate_ref[...]
    for u in range(hg):
        c = N_NSA_BRANCHES * u + 1
        acc_u = acc_ref[u * tq:(u + 1) * tq, :]
        o_slc = acc_u[:, :HEAD_DIM] / acc_u[:, HEAD_DIM:]
        cols = slice(u * HEAD_DIM, (u + 1) * HEAD_DIM)
        o_ref[:, cols] = (part_ref[:, cols] + gate[:, c:c + 1] * o_slc).astype(BF16)


def _nsa(heads, gates, part_cmp, sel_off, tq, kt):
    _, s, _ = heads.shape
    nsp = sel_off.shape[2]
    hg = NSA_GROUP
    assert kt % tq == 0 and NSA_WINDOW % tq == 0 and s % kt == 0
    q_map = lambda u, g, qb: (H_QN + hg * g + u, qb, 0)
    full = lambda hbase: pl.BlockSpec((1, s, HEAD_DIM), lambda g, qb: (hbase + g, 0, 0),
                                      pipeline_mode=pl.Buffered(1))
    return pl.pallas_call(
        functools.partial(_nsa_kernel, s_len=s, nsp=nsp, tq=tq, kt=kt),
        grid=(NSA_KV_GROUPS, s // tq),
        in_specs=[
            *[pl.BlockSpec((1, tq, HEAD_DIM), functools.partial(q_map, u)) for u in range(hg)],
            full(H_KS), full(H_VS), full(H_KW), full(H_VW),
            pl.BlockSpec((tq, HEAD_DIM), lambda g, qb: (qb, g)),
            pl.BlockSpec((tq, hg * HEAD_DIM), lambda g, qb: (qb, g)),
            pl.BlockSpec((1, tq, nsp), lambda g, qb: (g, qb, 0)),
        ],
        out_specs=pl.BlockSpec((tq, hg * HEAD_DIM), lambda g, qb: (qb, g)),
        out_shape=jax.ShapeDtypeStruct((s, D_NSA), BF16),
        scratch_shapes=[pltpu.VMEM((s, HEAD_DIM + nsp), BF16),
                        pltpu.VMEM((s, 2 * HEAD_DIM), BF16),
                        pltpu.VMEM((s, 2 * HEAD_DIM), BF16),
                        pltpu.VMEM((kt // tq, tq, kt), F32),
                        pltpu.VMEM((NSA_WINDOW // tq + 1, tq, NSA_WINDOW + tq), F32),
                        pltpu.VMEM((hg * tq, kt), F32),
                        pltpu.VMEM((hg * tq, kt), F32),
                        pltpu.VMEM((hg * tq, 1), F32),
                        pltpu.VMEM((hg * tq, 2 * HEAD_DIM), F32),
                        pltpu.VMEM((tq, hg * HEAD_DIM), F32)],
        compiler_params=_params(("arbitrary", "arbitrary")),
        name="nsa_attn",
    )(*([heads] * hg), heads, heads, heads, heads, gates, part_cmp, sel_off)


def _out_proj_ln_kernel(a_ref, b_ref, w_ref, h_ref, g_ref, be_ref, o_ref, wb16, *, alpha):
    @pl.when(pl.program_id(0) == 0)
    def _():
        wb16[...] = w_ref[...].astype(BF16)

    tm = o_ref.shape[0]
    rc = tm // ROW_CHUNKS
    for c in range(ROW_CHUNKS):
        r = slice(c * rc, (c + 1) * rc)
        mix = (jnp.dot(a_ref[r, :], wb16[:D_DIL, :], preferred_element_type=F32)
               + jnp.dot(b_ref[r, :], wb16[D_DIL:, :], preferred_element_type=F32))
        o_ref[r, :] = _layer_norm(alpha * h_ref[r, :] + mix, g_ref[...], be_ref[...])


def _out_proj_ln(mix_a, mix_b, w, h, g, b, alpha, tm):
    s, d = h.shape
    return pl.pallas_call(
        functools.partial(_out_proj_ln_kernel, alpha=alpha),
        grid=(s // tm,),
        in_specs=[
            pl.BlockSpec((tm, D_DIL), lambda i: (i, 0)),
            pl.BlockSpec((tm, D_NSA), lambda i: (i, 0)),
            pl.BlockSpec((d, d), lambda i: (0, 0), pipeline_mode=pl.Buffered(1)),
            pl.BlockSpec((tm, d), lambda i: (i, 0)),
            pl.BlockSpec((1, d), lambda i: (0, 0)),
            pl.BlockSpec((1, d), lambda i: (0, 0)),
        ],
        out_specs=pl.BlockSpec((tm, d), lambda i: (i, 0)),
        out_shape=jax.ShapeDtypeStruct((s, d), F32),
        scratch_shapes=[pltpu.VMEM((d, d), BF16)],
        compiler_params=_params(("arbitrary",)),
        name="out_proj_ln",
    )(mix_a, mix_b, w, h, g, b)


def _overlap_t(s):
    n_cmp = (s - CMP_BLOCK) // CMP_STRIDE + 1
    n_sel = s // SEL_BLOCK
    ncp = s // CMP_STRIDE
    nsp = -(-n_sel // HEAD_DIM) * HEAD_DIM
    c_lo = np.arange(n_cmp) * CMP_STRIDE
    c_hi = c_lo + CMP_BLOCK - 1
    s_lo = (np.arange(n_sel) * SEL_BLOCK)[:, None]
    ov = np.zeros((nsp, ncp), np.float32)
    ov[:n_sel, :n_cmp] = (c_lo[None, :] <= s_lo + SEL_BLOCK - 1) & (c_hi[None, :] >= s_lo)
    return jnp.asarray(ov, BF16)


def kernel(x, positions, ln1_g, ln1_b, ffn1_w1, ffn1_w3, ffn1_w2, w_in, gate_b, cmp_pe, cmp_w1,
           cmp_b1, cmp_w2, cmp_b2, w_out, ln2_g, ln2_b, ffn2_w1, ffn2_w3, ffn2_w2, ln3_g, ln3_b):
    bsz, s, d = x.shape
    assert bsz == 1 and d == (N_HEADS_DIL + N_HEADS_NSA) * HEAD_DIM
    assert s % max(FFN_ROWS, PROJ_ROWS, NSA_KEY_TILE) == 0 and s >= NSA_WINDOW + NSA_Q_TILE
    assert ffn1_w1.shape[2] % FFN_HIDDEN_TILE == 0
    alpha = (2.0 * DEPTH) ** 0.25
    scale = HEAD_DIM ** -0.5 * LOG2_E
    row = lambda v: v.reshape(1, -1)

    inv_freq = ROPE_THETA ** (-jnp.arange(0, HEAD_DIM, 2, dtype=F32) / HEAD_DIM)
    invf = jnp.concatenate([inv_freq, inv_freq]).reshape(1, HEAD_DIM)
    pos = positions[0].astype(F32).reshape(s, 1)
    cosf, sinf = _rope_tables(pos, pos, invf, ROPE_TILE)
    ncp = s // CMP_STRIDE
    n_cmp = (s - CMP_BLOCK) // CMP_STRIDE + 1
    pos_lo = jnp.pad(pos[0:n_cmp * CMP_STRIDE:CMP_STRIDE], ((0, ncp - n_cmp), (0, 0)))
    pos_hi = jnp.pad(pos[CMP_BLOCK - 1::CMP_STRIDE][:n_cmp], ((0, ncp - n_cmp), (0, 0)))
    cos_c, sin_c = _rope_tables(pos_lo, pos_hi, invf, ncp)

    g_off = 3 * D_DIL + D_NSA + 6 * KV_W
    per_g = NSA_GROUP * N_NSA_BRANCHES
    w_in_t = w_in[0].T.astype(BF16)
    wg = jnp.zeros((NSA_KV_GROUPS * HEAD_DIM, d), BF16)
    bg = jnp.zeros((1, NSA_KV_GROUPS * HEAD_DIM), F32)
    for g in range(NSA_KV_GROUPS):
        wg = wg.at[g * HEAD_DIM:g * HEAD_DIM + per_g].set(
            w_in_t[g_off + g * per_g:g_off + (g + 1) * per_g])
        bg = bg.at[0, g * HEAD_DIM:g * HEAD_DIM + per_g].set(gate_b[0][g * per_g:(g + 1) * per_g])

    h0 = x[0]
    h1 = _ffn_ln(h0, ffn1_w1[0], ffn1_w3[0], ffn1_w2[0], row(ln1_g[0]), row(ln1_b[0]),
                 alpha, FFN_ROWS, FFN_HIDDEN_TILE)

    heads, gates = _in_proj(h1, w_in_t, cosf, sinf, wg, bg, scale, PROJ_ROWS)
    tok16 = heads[H_KC:H_KS].reshape(2 * NSA_KV_GROUPS, ncp, CMP_STRIDE * HEAD_DIM)
    kcv, kcv_t = _compress(
        tok16, cmp_pe[0].reshape(2, 1, CMP_BLOCK * HEAD_DIM), cmp_w1[0],
        cmp_b1[0].reshape(2, 1, CMP_HIDDEN), cmp_w2[0],
        cmp_b2[0].reshape(2, 1, HEAD_DIM), cos_c, sin_c)
    mix_a = _dilated(heads, DIL_Q_TILE)
    part_cmp, sel_off = _nsa_select(heads, kcv, kcv_t, gates, _overlap_t(s), SEL_Q_TILE)
    mix_b = _nsa(heads, gates, part_cmp, sel_off, NSA_Q_TILE, NSA_KEY_TILE)

    h2 = _out_proj_ln(mix_a, mix_b, w_out[0], h1, row(ln2_g[0]), row(ln2_b[0]), alpha,
                      OUT_PROJ_ROWS)
    h3 = _ffn_ln(h2, ffn2_w1[0], ffn2_w3[0], ffn2_w2[0], row(ln3_g[0]), row(ln3_b[0]),
                 alpha, FFN_ROWS, FFN_HIDDEN_TILE)
    return h3.reshape(bsz, s, d)
```

```python
import functools

import jax
import jax.numpy as jnp
import numpy as np
from jax import lax
from jax.experimental import pallas as pl
from jax.experimental.pallas import tpu as pltpu

HEAD_DIM = 128
N_HEADS_DIL = 6
N_HEADS_NSA = 10
NSA_KV_GROUPS = 2
NSA_GROUP = N_HEADS_NSA // NSA_KV_GROUPS
N_NSA_BRANCHES = 3
DIL_PAIRS = ((128, 1), (512, 4), (2048, 16))
CMP_BLOCK = 32
CMP_STRIDE = 16
CMP_HIDDEN = 256
SEL_BLOCK = 64
N_SELECT = 16
NSA_WINDOW = 512
ROPE_THETA = 10000.0
LN_EPS = 1e-5
NEG = -1e30
SEL_OFF = -1e9
LOG2_E = 1.4426950408889634
DEPTH = 1

D_DIL = N_HEADS_DIL * HEAD_DIM
D_NSA = N_HEADS_NSA * HEAD_DIM
KV_W = NSA_KV_GROUPS * HEAD_DIM

H_QA, H_KA, H_VA, H_QN = 0, 6, 12, 18
H_KC, H_VC, H_KS, H_VS, H_KW, H_VW = 28, 30, 32, 34, 36, 38
N_PROJ_HEADS = 40
HEADS_PER_PROJ_BLOCK = 8
PROJ_SUB_HEADS = 2
ROW_CHUNKS = 2
CMP_ROW_CHUNK = 128

ROPE_TILE = 512
FFN_ROWS, FFN_HIDDEN_TILE = 1024, 256
PROJ_ROWS = 1024
OUT_PROJ_ROWS = 512
DIL_Q_TILE = 256
SEL_Q_TILE = 1024
NSA_Q_TILE, NSA_KEY_TILE = 256, 1024

VMEM_LIMIT_BYTES = 56 * 1024 * 1024
FFN_VMEM_LIMIT_BYTES = 60 * 1024 * 1024

F32 = jnp.float32
BF16 = jnp.bfloat16


def _params(sem, vmem=VMEM_LIMIT_BYTES):
    return pltpu.CompilerParams(dimension_semantics=sem, vmem_limit_bytes=vmem)


def _nt_dot(a, b):
    return lax.dot_general(a, b, (((1,), (1,)), ((), ())), preferred_element_type=F32)


def _layer_norm(y, g, b):
    mu = jnp.mean(y, axis=-1, keepdims=True)
    yc = y - mu
    var = jnp.mean(yc * yc, axis=-1, keepdims=True)
    return yc * lax.rsqrt(var + LN_EPS) * g + b


def _rope_table_kernel(pa_ref, pb_ref, invf_ref, cos_ref, sin_ref):
    pos = (pa_ref[...] + pb_ref[...]) * 0.5
    ang = pos * invf_ref[...]
    lane = lax.broadcasted_iota(jnp.int32, ang.shape, 1)
    cos_ref[...] = jnp.cos(ang)
    sin_ref[...] = jnp.where(lane < HEAD_DIM // 2, -1.0, 1.0) * jnp.sin(ang)


def _rope_tables(pos_a, pos_b, invf, tile):
    n = pos_a.shape[0]
    spec_p = pl.BlockSpec((tile, 1), lambda i: (i, 0))
    spec_t = pl.BlockSpec((tile, HEAD_DIM), lambda i: (i, 0))
    return pl.pallas_call(
        _rope_table_kernel,
        grid=(n // tile,),
        in_specs=[spec_p, spec_p, pl.BlockSpec((1, HEAD_DIM), lambda i: (0, 0))],
        out_specs=[spec_t, spec_t],
        out_shape=[jax.ShapeDtypeStruct((n, HEAD_DIM), F32)] * 2,
        compiler_params=_params(("arbitrary",)),
        name="rope_tables",
    )(pos_a, pos_b, invf)


def _ffn_ln_kernel(h_ref, w1_ref, w3_ref, w2_ref, g_ref, b_ref, o_ref, hb_ref, *, alpha, nf):
    f = pl.program_id(1)

    @pl.when(f == 0)
    def _():
        h = h_ref[...]
        hb_ref[...] = h.astype(BF16)
        o_ref[...] = (2.0 * alpha) * h

    hb = hb_ref[...]
    a = jnp.dot(hb, w1_ref[...].astype(BF16), preferred_element_type=F32)
    b = jnp.dot(hb, w3_ref[...].astype(BF16), preferred_element_type=F32)
    act = (a * jax.nn.sigmoid(a)) * b
    o_ref[...] += jnp.dot(act.astype(BF16), w2_ref[...].astype(BF16),
                          preferred_element_type=F32)

    @pl.when(f == nf - 1)
    def _():
        o_ref[...] = _layer_norm(0.5 * o_ref[...], g_ref[...], b_ref[...])


def _ffn_ln(h, w1, w3, w2, g, b, alpha, tm, tf):
    s, d = h.shape
    dff = w1.shape[1]
    nf = dff // tf
    return pl.pallas_call(
        functools.partial(_ffn_ln_kernel, alpha=alpha, nf=nf),
        grid=(s // tm, nf),
        in_specs=[
            pl.BlockSpec((tm, d), lambda i, f: (i, 0)),
            pl.BlockSpec((d, tf), lambda i, f: (0, f)),
            pl.BlockSpec((d, tf), lambda i, f: (0, f)),
            pl.BlockSpec((tf, d), lambda i, f: (f, 0)),
            pl.BlockSpec((1, d), lambda i, f: (0, 0)),
            pl.BlockSpec((1, d), lambda i, f: (0, 0)),
        ],
        out_specs=pl.BlockSpec((tm, d), lambda i, f: (i, 0)),
        out_shape=jax.ShapeDtypeStruct((s, d), F32),
        scratch_shapes=[pltpu.VMEM((tm, d), BF16)],
        compiler_params=_params(("arbitrary", "arbitrary"), FFN_VMEM_LIMIT_BYTES),
        name="ffn_ln",
    )(h, w1, w3, w2, g, b)


def _in_proj_kernel(h_ref, w_ref, cos_ref, sin_ref, wg_ref, bg_ref, o_ref, gate_ref, hb_ref,
                    *, scale):
    j = pl.program_id(1)
    hpb = HEADS_PER_PROJ_BLOCK
    sub = PROJ_SUB_HEADS

    @pl.when(j == 0)
    def _():
        hb_ref[...] = h_ref[...].astype(BF16)
        gate_ref[...] = jax.nn.sigmoid(_nt_dot(hb_ref[...], wg_ref[...]) + bg_ref[...])

    cos = cos_ref[...]
    sin = sin_ref[...]
    h = hb_ref[...]
    for k0 in range(0, hpb, sub):
        acc = _nt_dot(h, w_ref[k0 * HEAD_DIM:(k0 + sub) * HEAD_DIM, :])
        for k in range(sub):
            hh = j * hpb + k0 + k
            is_q = (hh < H_KA) | ((hh >= H_QN) & (hh < H_KC))
            rope = (is_q | (hh < H_VA) | ((hh >= H_KS) & (hh < H_VS))
                    | ((hh >= H_KW) & (hh < H_VW)))
            x = acc[:, k * HEAD_DIM:(k + 1) * HEAD_DIM]
            xr = x * cos + pltpu.roll(x, HEAD_DIM // 2, 1) * sin
            y = jnp.where(rope, xr, x) * jnp.where(is_q, scale, 1.0).astype(F32)
            o_ref[k0 + k] = y.astype(BF16)


def _in_proj(h, w, cosf, sinf, wg, bg, scale, tm):
    s, d = h.shape
    n = wg.shape[0]
    hpb = HEADS_PER_PROJ_BLOCK
    return pl.pallas_call(
        functools.partial(_in_proj_kernel, scale=scale),
        grid=(s // tm, N_PROJ_HEADS // hpb),
        in_specs=[
            pl.BlockSpec((tm, d), lambda i, j: (i, 0)),
            pl.BlockSpec((hpb * HEAD_DIM, d), lambda i, j: (j, 0)),
            pl.BlockSpec((tm, HEAD_DIM), lambda i, j: (i, 0)),
            pl.BlockSpec((tm, HEAD_DIM), lambda i, j: (i, 0)),
            pl.BlockSpec((n, d), lambda i, j: (0, 0)),
            pl.BlockSpec((1, n), lambda i, j: (0, 0)),
        ],
        out_specs=[pl.BlockSpec((hpb, tm, HEAD_DIM), lambda i, j: (j, i, 0)),
                   pl.BlockSpec((tm, n), lambda i, j: (i, 0))],
        out_shape=[jax.ShapeDtypeStruct((N_PROJ_HEADS, s, HEAD_DIM), BF16),
                   jax.ShapeDtypeStruct((s, n), F32)],
        scratch_shapes=[pltpu.VMEM((tm, d), BF16)],
        compiler_params=_params(("arbitrary", "arbitrary")),
        name="in_proj",
    )(h, w, cosf, sinf, wg, bg)


def _compress_kernel(tok_ref, pe_ref, w1_ref, b1_ref, w2_ref, b2_ref, cos_ref, sin_ref,
                     o_ref, ot_ref):
    j = pl.program_id(0)
    half = CMP_STRIDE * HEAD_DIM
    tok = tok_ref[0].astype(F32)
    pe = pe_ref[0]
    w1 = w1_ref[0].astype(BF16)
    top = (tok + pe[:, :half]).astype(BF16)
    bot = (tok + pe[:, half:]).astype(BF16)
    u = jnp.dot(top, w1[:half], preferred_element_type=F32)
    v = jnp.dot(bot, w1[half:], preferred_element_type=F32)
    nc = u.shape[0]
    hid = u + pltpu.roll(v, nc - 1, 0) + b1_ref[0]
    hid = jax.nn.gelu(hid)
    out = jnp.dot(hid.astype(BF16), w2_ref[0].astype(BF16),
                  preferred_element_type=F32) + b2_ref[0]
    roped = out * cos_ref[...] + pltpu.roll(out, HEAD_DIM // 2, 1) * sin_ref[...]
    out = jnp.where(j == 0, roped, out)
    o_ref[0, 0] = out.astype(BF16)
    ot_ref[0, 0] = out.T.astype(BF16)


def _compress(tok16, pe, w1, b1, w2, b2, cos_c, sin_c):
    nc = tok16.shape[1]
    blk = CMP_BLOCK * HEAD_DIM
    g = NSA_KV_GROUPS
    return pl.pallas_call(
        _compress_kernel,
        grid=(2, g),
        in_specs=[
            pl.BlockSpec((1, nc, CMP_STRIDE * HEAD_DIM), lambda j, gi: (g * j + gi, 0, 0)),
            pl.BlockSpec((1, 1, blk), lambda j, gi: (j, 0, 0)),
            pl.BlockSpec((1, blk, CMP_HIDDEN), lambda j, gi: (j, 0, 0)),
            pl.BlockSpec((1, 1, CMP_HIDDEN), lambda j, gi: (j, 0, 0)),
            pl.BlockSpec((1, CMP_HIDDEN, HEAD_DIM), lambda j, gi: (j, 0, 0)),
            pl.BlockSpec((1, 1, HEAD_DIM), lambda j, gi: (j, 0, 0)),
            pl.BlockSpec((nc, HEAD_DIM), lambda j, gi: (0, 0)),
            pl.BlockSpec((nc, HEAD_DIM), lambda j, gi: (0, 0)),
        ],
        out_specs=[pl.BlockSpec((1, 1, nc, HEAD_DIM), lambda j, gi: (j, gi, 0, 0)),
                   pl.BlockSpec((1, 1, HEAD_DIM, nc), lambda j, gi: (j, gi, 0, 0))],
        out_shape=[jax.ShapeDtypeStruct((2, g, nc, HEAD_DIM), BF16),
                   jax.ShapeDtypeStruct((2, g, HEAD_DIM, nc), BF16)],
        compiler_params=_params(("arbitrary", "arbitrary")),
        name="nsa_compress",
    )(tok16, pe, w1, b1, w2, b2, cos_c, sin_c)


def _dil_kernel(q_ref, k_ref, v_ref, o_ref, bias_ref, s_a, s_b, *, tq, padk):
    w = padk + tq
    n_tiles = q_ref.shape[1] // tq

    @pl.when(pl.program_id(0) == 0)
    def _():
        r = lax.broadcasted_iota(jnp.int32, (tq, w), 0)
        c = lax.broadcasted_iota(jnp.int32, (tq, w), 1)
        d = r + padk - c
        cnt = jnp.zeros((tq, w), F32)
        for window, dil in DIL_PAIRS:
            hit = (d >= 0) & (d <= window) & ((d & (dil - 1)) == 0)
            cnt = cnt + jnp.where(hit, 1.0, 0.0)
        bias_ref[...] = jnp.where(cnt > 0.0, jnp.log2(jnp.maximum(cnt, 1.0)), NEG)

    n_early = min(padk // tq, n_tiles)

    def strip(t):
        if isinstance(t, int):
            return (0, (t + 1) * tq) if t < n_early else (t * tq - padk, w)
        return pl.multiple_of(t * tq - padk, tq), w

    def scores(s_ref, t):
        k0, width = strip(t)
        q0 = t * tq if isinstance(t, int) else pl.multiple_of(t * tq, tq)
        s_ref[:, :width] = (_nt_dot(q_ref[0, pl.ds(q0, tq), :], k_ref[0, pl.ds(k0, width), :])
                            + bias_ref[:, w - width:])

    def finish(s_ref, t):
        k0, width = strip(t)
        q0 = t * tq if isinstance(t, int) else pl.multiple_of(t * tq, tq)
        s = s_ref[:, :width]
        p = jnp.exp2(s - jnp.max(s, axis=-1, keepdims=True))
        l = jnp.sum(p, axis=-1, keepdims=True)
        o = jnp.dot(p.astype(BF16), v_ref[0, pl.ds(k0, width), :], preferred_element_type=F32)
        o_ref[pl.ds(q0, tq), :] = (o / l).astype(BF16)

    bufs = (s_a, s_b)
    scores(s_a, 0)
    for t in range(n_early):
        if t + 1 < n_tiles:
            scores(bufs[(t + 1) % 2], t + 1)
        finish(bufs[t % 2], t)

    def pair(pi, _):
        t = 2 * pi
        scores(s_b, t + 1)
        finish(s_a, t)
        scores(s_a, t + 2)
        finish(s_b, t + 1)
        return 0

    if n_early < n_tiles:
        lax.fori_loop(n_early // 2, n_tiles // 2 - 1, pair, 0)
        scores(s_b, n_tiles - 1)
        finish(s_a, n_tiles - 2)
        finish(s_b, n_tiles - 1)


def _dilated(heads, tq):
    _, s, _ = heads.shape
    padk = max(wd for wd, _ in DIL_PAIRS)
    n_tiles, n_early = s // tq, min(padk // tq, s // tq)
    assert padk % tq == 0 and n_tiles % 2 == 0 and n_early % 2 == 0
    return pl.pallas_call(
        functools.partial(_dil_kernel, tq=tq, padk=padk),
        grid=(N_HEADS_DIL,),
        in_specs=[
            pl.BlockSpec((1, s, HEAD_DIM), lambda h: (H_QA + h, 0, 0)),
            pl.BlockSpec((1, s, HEAD_DIM), lambda h: (H_KA + h, 0, 0)),
            pl.BlockSpec((1, s, HEAD_DIM), lambda h: (H_VA + h, 0, 0)),
        ],
        out_specs=pl.BlockSpec((s, HEAD_DIM), lambda h: (0, h)),
        out_shape=jax.ShapeDtypeStruct((s, D_DIL), BF16),
        scratch_shapes=[pltpu.VMEM((tq, padk + tq), F32),
                        pltpu.VMEM((tq, padk + tq), F32),
                        pltpu.VMEM((tq, padk + tq), F32)],
        compiler_params=_params(("arbitrary",)),
        name="dilated_attn",
    )(heads, heads, heads)


def _add_per_head(s, bias, hg):
    tq = bias.shape[0]
    return jnp.concatenate([s[u * tq:(u + 1) * tq] + bias for u in range(hg)], axis=0)


def _nsa_select_kernel(*refs, nsp, tq, k_sel):
    q_refs = refs[:NSA_GROUP]
    kc_ref, vct_ref, gate_ref, ovt_ref, cmp_ref, sel_ref = refs[NSA_GROUP:]
    t0 = pl.program_id(1) * tq
    hg = NSA_GROUP
    ncp = kc_ref.shape[2]
    gate = gate_ref[...]

    def branch(n_rows):
        n_sel = min(n_rows * CMP_STRIDE // SEL_BLOCK, nsp)
        kc = kc_ref[0, 0, :n_rows, :]
        vct = vct_ref[0, 0, :, :n_rows]
        ovt = ovt_ref[:n_sel, :n_rows]
        n_io = lax.broadcasted_iota(jnp.int32, (n_rows, tq), 0)
        t_io = t0 + lax.broadcasted_iota(jnp.int32, (n_rows, tq), 1)
        cmask = (n_io * CMP_STRIDE + (CMP_BLOCK - 1) <= t_io) & (n_io < ncp - 1)
        imp = jnp.zeros((n_sel, tq), F32)
        for u in range(hg):
            st = jnp.where(cmask, _nt_dot(kc, q_refs[u][0]), NEG)
            m = jnp.maximum(jnp.max(st, axis=0, keepdims=True), 0.1 * NEG)
            e = jnp.exp2(st - m)
            r = 1.0 / jnp.maximum(jnp.sum(e, axis=0, keepdims=True), 1e-30)
            eb = e.astype(BF16)
            imp = imp + jnp.dot(ovt, eb, preferred_element_type=F32) * r
            o_cmp = (jnp.dot(vct, eb, preferred_element_type=F32) * r).T
            c = N_NSA_BRANCHES * u
            cmp_ref[:, u * HEAD_DIM:(u + 1) * HEAD_DIM] = gate[:, c:c + 1] * o_cmp

        j_io = lax.broadcasted_iota(jnp.int32, (n_sel, tq), 0)
        t_sel = t0 + lax.broadcasted_iota(jnp.int32, (n_sel, tq), 1)
        valid = j_io * SEL_BLOCK <= t_sel
        cur = t_sel // SEL_BLOCK
        forced = (j_io == 0) | (j_io == cur) | (j_io == cur - 1)
        score = jnp.where(valid & jnp.logical_not(forced), imp, NEG)
        for _ in range(k_sel - 3):
            m = jnp.max(score, axis=0, keepdims=True)
            first = jnp.min(jnp.where(score == m, j_io, n_sel), axis=0, keepdims=True)
            score = jnp.where(j_io == first, -jnp.inf, score)
        sel_t = jnp.where(valid & (forced | (score == -jnp.inf)), 1.0, 0.0)
        if n_sel < nsp:
            sel_t = jnp.concatenate([sel_t, jnp.zeros((nsp - n_sel, tq), F32)], axis=0)
        sel_ref[0] = jnp.where(sel_t.T > 0.0, 0.0, SEL_OFF).astype(BF16)

    n_var = ncp // CMP_ROW_CHUNK
    need = (t0 + tq) // CMP_STRIDE
    var = jnp.minimum((need + CMP_ROW_CHUNK - 1) // CMP_ROW_CHUNK, n_var) - 1
    for v in range(n_var):
        pl.when(var == v)(functools.partial(branch, CMP_ROW_CHUNK * (v + 1)))


def _nsa_select(heads, kcv, vct, gates, ovt, tq):
    _, s, _ = heads.shape
    ncp = kcv.shape[2]
    nsp = ovt.shape[0]
    hg = NSA_GROUP
    k_sel = min(N_SELECT, s // SEL_BLOCK)
    assert k_sel >= 3 and ncp % CMP_ROW_CHUNK == 0
    q_map = lambda u, g, qb: (H_QN + hg * g + u, qb, 0)
    return pl.pallas_call(
        functools.partial(_nsa_select_kernel, nsp=nsp, tq=tq, k_sel=k_sel),
        grid=(NSA_KV_GROUPS, s // tq),
        in_specs=[
            *[pl.BlockSpec((1, tq, HEAD_DIM), functools.partial(q_map, u)) for u in range(hg)],
            pl.BlockSpec((1, 1, ncp, HEAD_DIM), lambda g, qb: (0, g, 0, 0)),
            pl.BlockSpec((1, 1, HEAD_DIM, ncp), lambda g, qb: (1, g, 0, 0)),
            pl.BlockSpec((tq, HEAD_DIM), lambda g, qb: (qb, g)),
            pl.BlockSpec((nsp, ncp), lambda g, qb: (0, 0)),
        ],
        out_specs=[pl.BlockSpec((tq, hg * HEAD_DIM), lambda g, qb: (qb, g)),
                   pl.BlockSpec((1, tq, nsp), lambda g, qb: (g, qb, 0))],
        out_shape=[jax.ShapeDtypeStruct((s, D_NSA), F32),
                   jax.ShapeDtypeStruct((NSA_KV_GROUPS, s, nsp), BF16)],
        compiler_params=_params(("arbitrary", "arbitrary")),
        name="nsa_select",
    )(*([heads] * hg), kcv, vct, gates, ovt)


def _nsa_kernel(*refs, s_len, nsp, tq, kt):
    q_refs = refs[:NSA_GROUP]
    (ks_ref, vs_ref, kw_ref, vw_ref, gate_ref, cmp_ref, sel_ref,
     o_ref, kaug, vaug, vwaug, cbias, wbias, s_a, s_b, m_ref, acc_ref,
     part_ref) = refs[NSA_GROUP:]
    g = pl.program_id(0)
    qb = pl.program_id(1)
    t0 = qb * tq
    hg = NSA_GROUP
    rows = hg * tq
    ww = NSA_WINDOW + tq

    @pl.when((g == 0) & (qb == 0))
    def _():
        for r in range(kt // tq):
            d = (lax.broadcasted_iota(jnp.int32, (tq, kt), 0) + r * tq
                 - lax.broadcasted_iota(jnp.int32, (tq, kt), 1))
            cbias[r] = jnp.where(d >= 0, 0.0, NEG)
        for w in range(NSA_WINDOW // tq + 1):
            d = (lax.broadcasted_iota(jnp.int32, (tq, ww), 0) + w * tq
                 - lax.broadcasted_iota(jnp.int32, (tq, ww), 1))
            wbias[w] = jnp.where((d >= 0) & (d < NSA_WINDOW), 0.0, NEG)

    @pl.when(qb == 0)
    def _():
        kaug[:, 0:HEAD_DIM] = ks_ref[0]
        key = lax.broadcasted_iota(jnp.int32, (s_len, nsp), 0)
        blk = lax.broadcasted_iota(jnp.int32, (s_len, nsp), 1)
        kaug[:, HEAD_DIM:] = jnp.where(blk == key // SEL_BLOCK, 1.0, 0.0).astype(BF16)
        ones = jnp.ones((s_len, HEAD_DIM), BF16)
        vaug[:, 0:HEAD_DIM] = vs_ref[0]
        vaug[:, HEAD_DIM:] = ones
        vwaug[:, 0:HEAD_DIM] = vw_ref[0]
        vwaug[:, HEAD_DIM:] = ones

    gate = gate_ref[...]
    q5 = jnp.concatenate([r[0] for r in q_refs], axis=0)

    qaug = jnp.concatenate([q5, jnp.concatenate([sel_ref[0]] * hg, axis=0)], axis=1)

    w0 = pl.multiple_of(jnp.maximum(t0 - NSA_WINDOW, 0), tq)
    sw = _nt_dot(q5, kw_ref[0, pl.ds(w0, ww), :])
    sw = _add_per_head(sw, wbias[jnp.minimum(qb, NSA_WINDOW // tq)], hg)
    pw = jnp.exp2(sw - jnp.max(sw, axis=-1, keepdims=True))
    acc_w = jnp.dot(pw.astype(BF16), vwaug[pl.ds(w0, ww), :], preferred_element_type=F32)
    o_win = acc_w[:, :HEAD_DIM] / acc_w[:, HEAD_DIM:]
    for u in range(hg):
        c = N_NSA_BRANCHES * u + 2
        cols = slice(u * HEAD_DIM, (u + 1) * HEAD_DIM)
        part_ref[:, cols] = cmp_ref[:, cols] + gate[:, c:c + 1] * o_win[u * tq:(u + 1) * tq]

    def sel_scores(kti, width=kt):
        k0 = pl.multiple_of(kti * kt, kt)
        return _nt_dot(qaug, kaug[pl.ds(k0, width), :])

    def sel_update(s_ref, kti, diag=None):
        width = kt if diag is None else (diag + 1) * tq
        s = s_ref[:, :width]
        if diag is not None:
            s = _add_per_head(s, cbias[diag, :, :width], hg)
        k0 = pl.multiple_of(kti * kt, kt)
        m_i = m_ref[...]
        m_new = jnp.maximum(m_i, jnp.max(s, axis=-1, keepdims=True))
        p = jnp.exp2(s - m_new)
        pv = jnp.dot(p.astype(BF16), vaug[pl.ds(k0, width), :], preferred_element_type=F32)
        acc_ref[...] = jnp.exp2(m_i - m_new) * acc_ref[...] + pv
        m_ref[...] = m_new

    def sel_pair(pi, _):
        t = 2 * pi
        s_b[...] = sel_scores(t + 1)
        sel_update(s_a, t)
        s_a[...] = sel_scores(t + 2)
        sel_update(s_b, t + 1)
        return 0

    last = (t0 + tq + kt - 1) // kt - 1
    m_ref[...] = jnp.full((rows, 1), NEG, F32)
    acc_ref[...] = jnp.zeros((rows, 2 * HEAD_DIM), F32)
    s_a[...] = sel_scores(0)
    lax.fori_loop(0, last // 2, sel_pair, 0)
    t_even = 2 * (last // 2)

    def tail_odd(diag):
        width = (diag + 1) * tq
        s_b[:, :width] = sel_scores(t_even + 1, width)
        sel_update(s_a, t_even)
        sel_update(s_b, t_even + 1, diag)

    def tail_even(diag):
        sel_update(s_a, t_even, diag)

    for diag in range(kt // tq):
        here = qb % (kt // tq) == diag
        pl.when(here & (last % 2 == 1))(functools.partial(tail_odd, diag))
        pl.when(here & (last % 2 == 0))(functools.partial(tail_even, diag))

    gate = gate_ref[...]
    for u in range(hg):
        c = N_NSA_BRANCHES * u + 1
        acc_u = acc_ref[u * tq:(u + 1) * tq, :]
        o_slc = acc_u[:, :HEAD_DIM] / acc_u[:, HEAD_DIM:]
        cols = slice(u * HEAD_DIM, (u + 1) * HEAD_DIM)
        o_ref[:, cols] = (part_ref[:, cols] + gate[:, c:c + 1] * o_slc).astype(BF16)


def _nsa(heads, gates, part_cmp, sel_off, tq, kt):
    _, s, _ = heads.shape
    nsp = sel_off.shape[2]
    hg = NSA_GROUP
    assert kt % tq == 0 and NSA_WINDOW % tq == 0 and s % kt == 0
    q_map = lambda u, g, qb: (H_QN + hg * g + u, qb, 0)
    full = lambda hbase: pl.BlockSpec((1, s, HEAD_DIM), lambda g, qb: (hbase + g, 0, 0),
                                      pipeline_mode=pl.Buffered(1))
    return pl.pallas_call(
        functools.partial(_nsa_kernel, s_len=s, nsp=nsp, tq=tq, kt=kt),
        grid=(NSA_KV_GROUPS, s // tq),
        in_specs=[
            *[pl.BlockSpec((1, tq, HEAD_DIM), functools.partial(q_map, u)) for u in range(hg)],
            full(H_KS), full(H_VS), full(H_KW), full(H_VW),
            pl.BlockSpec((tq, HEAD_DIM), lambda g, qb: (qb, g)),
            pl.BlockSpec((tq, hg * HEAD_DIM), lambda g, qb: (qb, g)),
            pl.BlockSpec((1, tq, nsp), lambda g, qb: (g, qb, 0)),
        ],
        out_specs=pl.BlockSpec((tq, hg * HEAD_DIM), lambda g, qb: (qb, g)),
        out_shape=jax.ShapeDtypeStruct((s, D_NSA), BF16),
        scratch_shapes=[pltpu.VMEM((s, HEAD_DIM + nsp), BF16),
                        pltpu.VMEM((s, 2 * HEAD_DIM), BF16),
                        pltpu.VMEM((s, 2 * HEAD_DIM), BF16),
                        pltpu.VMEM((kt // tq, tq, kt), F32),
                        pltpu.VMEM((NSA_WINDOW // tq + 1, tq, NSA_WINDOW + tq), F32),
                        pltpu.VMEM((hg * tq, kt), F32),
                        pltpu.VMEM((hg * tq, kt), F32),
                        pltpu.VMEM((hg * tq, 1), F32),
                        pltpu.VMEM((hg * tq, 2 * HEAD_DIM), F32),
                        pltpu.VMEM((tq, hg * HEAD_DIM), F32)],
        compiler_params=_params(("arbitrary", "arbitrary")),
        name="nsa_attn",
    )(*([heads] * hg), heads, heads, heads, heads, gates, part_cmp, sel_off)


def _out_proj_ln_kernel(a_ref, b_ref, w_ref, h_ref, g_ref, be_ref, o_ref, wb16, *, alpha):
    @pl.when(pl.program_id(0) == 0)
    def _():
        wb16[...] = w_ref[...].astype(BF16)

    tm = o_ref.shape[0]
    rc = tm // ROW_CHUNKS
    for c in range(ROW_CHUNKS):
        r = slice(c * rc, (c + 1) * rc)
        mix = (jnp.dot(a_ref[r, :], wb16[:D_DIL, :], preferred_element_type=F32)
               + jnp.dot(b_ref[r, :], wb16[D_DIL:, :], preferred_element_type=F32))
        o_ref[r, :] = _layer_norm(alpha * h_ref[r, :] + mix, g_ref[...], be_ref[...])


def _out_proj_ln(mix_a, mix_b, w, h, g, b, alpha, tm):
    s, d = h.shape
    return pl.pallas_call(
        functools.partial(_out_proj_ln_kernel, alpha=alpha),
        grid=(s // tm,),
        in_specs=[
            pl.BlockSpec((tm, D_DIL), lambda i: (i, 0)),
            pl.BlockSpec((tm, D_NSA), lambda i: (i, 0)),
            pl.BlockSpec((d, d), lambda i: (0, 0), pipeline_mode=pl.Buffered(1)),
            pl.BlockSpec((tm, d), lambda i: (i, 0)),
            pl.BlockSpec((1, d), lambda i: (0, 0)),
            pl.BlockSpec((1, d), lambda i: (0, 0)),
        ],
        out_specs=pl.BlockSpec((tm, d), lambda i: (i, 0)),
        out_shape=jax.ShapeDtypeStruct((s, d), F32),
        scratch_shapes=[pltpu.VMEM((d, d), BF16)],
        compiler_params=_params(("arbitrary",)),
        name="out_proj_ln",
    )(mix_a, mix_b, w, h, g, b)


def _overlap_t(s):
    n_cmp = (s - CMP_BLOCK) // CMP_STRIDE + 1
    n_sel = s // SEL_BLOCK
    ncp = s // CMP_STRIDE
    nsp = -(-n_sel // HEAD_DIM) * HEAD_DIM
    c_lo = np.arange(n_cmp) * CMP_STRIDE
    c_hi = c_lo + CMP_BLOCK - 1
    s_lo = (np.arange(n_sel) * SEL_BLOCK)[:, None]
    ov = np.zeros((nsp, ncp), np.float32)
    ov[:n_sel, :n_cmp] = (c_lo[None, :] <= s_lo + SEL_BLOCK - 1) & (c_hi[None, :] >= s_lo)
    return jnp.asarray(ov, BF16)


def kernel(x, positions, ln1_g, ln1_b, ffn1_w1, ffn1_w3, ffn1_w2, w_in, gate_b, cmp_pe, cmp_w1,
           cmp_b1, cmp_w2, cmp_b2, w_out, ln2_g, ln2_b, ffn2_w1, ffn2_w3, ffn2_w2, ln3_g, ln3_b):
    bsz, s, d = x.shape
    assert bsz == 1 and d == (N_HEADS_DIL + N_HEADS_NSA) * HEAD_DIM
    assert s % max(FFN_ROWS, PROJ_ROWS, NSA_KEY_TILE) == 0 and s >= NSA_WINDOW + NSA_Q_TILE
    assert ffn1_w1.shape[2] % FFN_HIDDEN_TILE == 0
    alpha = (2.0 * DEPTH) ** 0.25
    scale = HEAD_DIM ** -0.5 * LOG2_E
    row = lambda v: v.reshape(1, -1)

    inv_freq = ROPE_THETA ** (-jnp.arange(0, HEAD_DIM, 2, dtype=F32) / HEAD_DIM)
    invf = jnp.concatenate([inv_freq, inv_freq]).reshape(1, HEAD_DIM)
    pos = positions[0].astype(F32).reshape(s, 1)
    cosf, sinf = _rope_tables(pos, pos, invf, ROPE_TILE)
    ncp = s // CMP_STRIDE
    n_cmp = (s - CMP_BLOCK) // CMP_STRIDE + 1
    pos_lo = jnp.pad(pos[0:n_cmp * CMP_STRIDE:CMP_STRIDE], ((0, ncp - n_cmp), (0, 0)))
    pos_hi = jnp.pad(pos[CMP_BLOCK - 1::CMP_STRIDE][:n_cmp], ((0, ncp - n_cmp), (0, 0)))
    cos_c, sin_c = _rope_tables(pos_lo, pos_hi, invf, ncp)

    g_off = 3 * D_DIL + D_NSA + 6 * KV_W
    per_g = NSA_GROUP * N_NSA_BRANCHES
    w_in_t = w_in[0].T.astype(BF16)
    wg = jnp.zeros((NSA_KV_GROUPS * HEAD_DIM, d), BF16)
    bg = jnp.zeros((1, NSA_KV_GROUPS * HEAD_DIM), F32)
    for g in range(NSA_KV_GROUPS):
        wg = wg.at[g * HEAD_DIM:g * HEAD_DIM + per_g].set(
            w_in_t[g_off + g * per_g:g_off + (g + 1) * per_g])
        bg = bg.at[0, g * HEAD_DIM:g * HEAD_DIM + per_g].set(gate_b[0][g * per_g:(g + 1) * per_g])

    h0 = x[0]
    h1 = _ffn_ln(h0, ffn1_w1[0], ffn1_w3[0], ffn1_w2[0], row(ln1_g[0]), row(ln1_b[0]),
                 alpha, FFN_ROWS, FFN_HIDDEN_TILE)

    heads, gates = _in_proj(h1, w_in_t, cosf, sinf, wg, bg, scale, PROJ_ROWS)
    tok16 = heads[H_KC:H_KS].reshape(2 * NSA_KV_GROUPS, ncp, CMP_STRIDE * HEAD_DIM)
    kcv, kcv_t = _compress(
        tok16, cmp_pe[0].reshape(2, 1, CMP_BLOCK * HEAD_DIM), cmp_w1[0],
        cmp_b1[0].reshape(2, 1, CMP_HIDDEN), cmp_w2[0],
        cmp_b2[0].reshape(2, 1, HEAD_DIM), cos_c, sin_c)
    mix_a = _dilated(heads, DIL_Q_TILE)
    part_cmp, sel_off = _nsa_select(heads, kcv, kcv_t, gates, _overlap_t(s), SEL_Q_TILE)
    mix_b = _nsa(heads, gates, part_cmp, sel_off, NSA_Q_TILE, NSA_KEY_TILE)

    h2 = _out_proj_ln(mix_a, mix_b, w_out[0], h1, row(ln2_g[0]), row(ln2_b[0]), alpha,
                      OUT_PROJ_ROWS)
    h3 = _ffn_ln(h2, ffn2_w1[0], ffn2_w3[0], ffn2_w2[0], row(ln3_g[0]), row(ln3_b[0]),
                 alpha, FFN_ROWS, FFN_HIDDEN_TILE)
    return h3.reshape(bsz, s, d)
```

```python
import functools

import jax
import jax.numpy as jnp
import numpy as np
from jax import lax
from jax.experimental import pallas as pl
from jax.experimental.pallas import tpu as pltpu

HEAD_DIM = 128
N_HEADS_DIL = 6
N_HEADS_NSA = 10
NSA_KV_GROUPS = 2
NSA_GROUP = N_HEADS_NSA // NSA_KV_GROUPS
N_NSA_BRANCHES = 3
DIL_PAIRS = ((128, 1), (512, 4), (2048, 16))
CMP_BLOCK = 32
CMP_STRIDE = 16
CMP_HIDDEN = 256
SEL_BLOCK = 64
N_SELECT = 16
NSA_WINDOW = 512
ROPE_THETA = 10000.0
LN_EPS = 1e-5
NEG = -1e30
SEL_OFF = -1e9
LOG2_E = 1.4426950408889634
DEPTH = 1

D_DIL = N_HEADS_DIL * HEAD_DIM
D_NSA = N_HEADS_NSA * HEAD_DIM
KV_W = NSA_KV_GROUPS * HEAD_DIM

H_QA, H_KA, H_VA, H_QN = 0, 6, 12, 18
H_KC, H_VC, H_KS, H_VS, H_KW, H_VW = 28, 30, 32, 34, 36, 38
N_PROJ_HEADS = 40
HEADS_PER_PROJ_BLOCK = 8
PROJ_SUB_HEADS = 2
ROW_CHUNKS = 2
CMP_ROW_CHUNK = 128

ROPE_TILE = 512
FFN_ROWS, FFN_HIDDEN_TILE = 1024, 256
FFN_LAST_CHUNKS = 4
PROJ_ROWS = 1024
OUT_PROJ_ROWS = 512
DIL_Q_TILE = 256
SEL_Q_TILE = 1024
NSA_Q_TILE, NSA_KEY_TILE = 256, 1024

VMEM_LIMIT_BYTES = 56 * 1024 * 1024
FFN_VMEM_LIMIT_BYTES = 60 * 1024 * 1024

F32 = jnp.float32
BF16 = jnp.bfloat16


def _params(sem, vmem=VMEM_LIMIT_BYTES):
    return pltpu.CompilerParams(dimension_semantics=sem, vmem_limit_bytes=vmem)


def _nt_dot(a, b):
    return lax.dot_general(a, b, (((1,), (1,)), ((), ())), preferred_element_type=F32)


def _layer_norm(y, g, b):
    mu = jnp.mean(y, axis=-1, keepdims=True)
    yc = y - mu
    var = jnp.mean(yc * yc, axis=-1, keepdims=True)
    return yc * lax.rsqrt(var + LN_EPS) * g + b


def _rope_table_kernel(pa_ref, pb_ref, invf_ref, cos_ref, sin_ref):
    pos = (pa_ref[...] + pb_ref[...]) * 0.5
    ang = pos * invf_ref[...]
    lane = lax.broadcasted_iota(jnp.int32, ang.shape, 1)
    cos_ref[...] = jnp.cos(ang)
    sin_ref[...] = jnp.where(lane < HEAD_DIM // 2, -1.0, 1.0) * jnp.sin(ang)


def _rope_tables(pos_a, pos_b, invf, tile):
    n = pos_a.shape[0]
    spec_p = pl.BlockSpec((tile, 1), lambda i: (i, 0))
    spec_t = pl.BlockSpec((tile, HEAD_DIM), lambda i: (i, 0))
    return pl.pallas_call(
        _rope_table_kernel,
        grid=(n // tile,),
        in_specs=[spec_p, spec_p, pl.BlockSpec((1, HEAD_DIM), lambda i: (0, 0))],
        out_specs=[spec_t, spec_t],
        out_shape=[jax.ShapeDtypeStruct((n, HEAD_DIM), F32)] * 2,
        compiler_params=_params(("arbitrary",)),
        name="rope_tables",
    )(pos_a, pos_b, invf)


def _ffn_ln_kernel(h_ref, w1_ref, w3_ref, w2_ref, g_ref, b_ref, o_ref, hb_ref, *, alpha, nf):
    f = pl.program_id(1)

    @pl.when(f == 0)
    def _():
        h = h_ref[...]
        hb_ref[...] = h.astype(BF16)
        o_ref[...] = (2.0 * alpha) * h

    def hidden_tile(rows):
        hb = hb_ref[rows, :]
        a = jnp.dot(hb, w1_ref[...].astype(BF16), preferred_element_type=F32)
        b = jnp.dot(hb, w3_ref[...].astype(BF16), preferred_element_type=F32)
        act = (a * jax.nn.sigmoid(a)) * b
        return jnp.dot(act.astype(BF16), w2_ref[...].astype(BF16), preferred_element_type=F32)

    @pl.when(f < nf - 1)
    def _():
        o_ref[...] += hidden_tile(slice(None))

    @pl.when(f == nf - 1)
    def _():
        rc = o_ref.shape[0] // FFN_LAST_CHUNKS
        for c in range(FFN_LAST_CHUNKS):
            r = slice(c * rc, (c + 1) * rc)
            y = o_ref[r, :] + hidden_tile(r)
            o_ref[r, :] = _layer_norm(0.5 * y, g_ref[...], b_ref[...])


def _ffn_ln(h, w1, w3, w2, g, b, alpha, tm, tf):
    s, d = h.shape
    dff = w1.shape[1]
    nf = dff // tf
    return pl.pallas_call(
        functools.partial(_ffn_ln_kernel, alpha=alpha, nf=nf),
        grid=(s // tm, nf),
        in_specs=[
            pl.BlockSpec((tm, d), lambda i, f: (i, 0)),
            pl.BlockSpec((d, tf), lambda i, f: (0, f)),
            pl.BlockSpec((d, tf), lambda i, f: (0, f)),
            pl.BlockSpec((tf, d), lambda i, f: (f, 0)),
            pl.BlockSpec((1, d), lambda i, f: (0, 0)),
            pl.BlockSpec((1, d), lambda i, f: (0, 0)),
        ],
        out_specs=pl.BlockSpec((tm, d), lambda i, f: (i, 0)),
        out_shape=jax.ShapeDtypeStruct((s, d), F32),
        scratch_shapes=[pltpu.VMEM((tm, d), BF16)],
        compiler_params=_params(("arbitrary", "arbitrary"), FFN_VMEM_LIMIT_BYTES),
        name="ffn_ln",
    )(h, w1, w3, w2, g, b)


def _in_proj_kernel(h_ref, w_ref, cos_ref, sin_ref, wg_ref, bg_ref, o_ref, gate_ref, hb_ref,
                    *, scale):
    j = pl.program_id(1)
    hpb = HEADS_PER_PROJ_BLOCK
    sub = PROJ_SUB_HEADS

    @pl.when(j == 0)
    def _():
        hb_ref[...] = h_ref[...].astype(BF16)
        gate_ref[...] = jax.nn.sigmoid(_nt_dot(hb_ref[...], wg_ref[...]) + bg_ref[...])

    cos = cos_ref[...]
    sin = sin_ref[...]
    h = hb_ref[...]
    for k0 in range(0, hpb, sub):
        acc = _nt_dot(h, w_ref[k0 * HEAD_DIM:(k0 + sub) * HEAD_DIM, :])
        for k in range(sub):
            hh = j * hpb + k0 + k
            is_q = (hh < H_KA) | ((hh >= H_QN) & (hh < H_KC))
            rope = (is_q | (hh < H_VA) | ((hh >= H_KS) & (hh < H_VS))
                    | ((hh >= H_KW) & (hh < H_VW)))
            x = acc[:, k * HEAD_DIM:(k + 1) * HEAD_DIM]
            xr = x * cos + pltpu.roll(x, HEAD_DIM // 2, 1) * sin
            y = jnp.where(rope, xr, x) * jnp.where(is_q, scale, 1.0).astype(F32)
            o_ref[k0 + k] = y.astype(BF16)


def _in_proj(h, w, cosf, sinf, wg, bg, scale, tm):
    s, d = h.shape
    n = wg.shape[0]
    hpb = HEADS_PER_PROJ_BLOCK
    return pl.pallas_call(
        functools.partial(_in_proj_kernel, scale=scale),
        grid=(s // tm, N_PROJ_HEADS // hpb),
        in_specs=[
            pl.BlockSpec((tm, d), lambda i, j: (i, 0)),
            pl.BlockSpec((hpb * HEAD_DIM, d), lambda i, j: (j, 0)),
            pl.BlockSpec((tm, HEAD_DIM), lambda i, j: (i, 0)),
            pl.BlockSpec((tm, HEAD_DIM), lambda i, j: (i, 0)),
            pl.BlockSpec((n, d), lambda i, j: (0, 0)),
            pl.BlockSpec((1, n), lambda i, j: (0, 0)),
        ],
        out_specs=[pl.BlockSpec((hpb, tm, HEAD_DIM), lambda i, j: (j, i, 0)),
                   pl.BlockSpec((tm, n), lambda i, j: (i, 0))],
        out_shape=[jax.ShapeDtypeStruct((N_PROJ_HEADS, s, HEAD_DIM), BF16),
                   jax.ShapeDtypeStruct((s, n), F32)],
        scratch_shapes=[pltpu.VMEM((tm, d), BF16)],
        compiler_params=_params(("arbitrary", "arbitrary")),
        name="in_proj",
    )(h, w, cosf, sinf, wg, bg)


def _compress_kernel(tok_ref, pe_ref, w1_ref, b1_ref, w2_ref, b2_ref, cos_ref, sin_ref,
                     o_ref, ot_ref):
    j = pl.program_id(0)
    half = CMP_STRIDE * HEAD_DIM
    tok = tok_ref[0].astype(F32)
    pe = pe_ref[0]
    w1 = w1_ref[0].astype(BF16)
    top = (tok + pe[:, :half]).astype(BF16)
    bot = (tok + pe[:, half:]).astype(BF16)
    u = jnp.dot(top, w1[:half], preferred_element_type=F32)
    v = jnp.dot(bot, w1[half:], preferred_element_type=F32)
    nc = u.shape[0]
    hid = u + pltpu.roll(v, nc - 1, 0) + b1_ref[0]
    hid = jax.nn.gelu(hid)
    out = jnp.dot(hid.astype(BF16), w2_ref[0].astype(BF16),
                  preferred_element_type=F32) + b2_ref[0]
    roped = out * cos_ref[...] + pltpu.roll(out, HEAD_DIM // 2, 1) * sin_ref[...]
    out = jnp.where(j == 0, roped, out)
    o_ref[0, 0] = out.astype(BF16)
    ot_ref[0, 0] = out.T.astype(BF16)


def _compress(tok16, pe, w1, b1, w2, b2, cos_c, sin_c):
    nc = tok16.shape[1]
    blk = CMP_BLOCK * HEAD_DIM
    g = NSA_KV_GROUPS
    return pl.pallas_call(
        _compress_kernel,
        grid=(2, g),
        in_specs=[
            pl.BlockSpec((1, nc, CMP_STRIDE * HEAD_DIM), lambda j, gi: (g * j + gi, 0, 0)),
            pl.BlockSpec((1, 1, blk), lambda j, gi: (j, 0, 0)),
            pl.BlockSpec((1, blk, CMP_HIDDEN), lambda j, gi: (j, 0, 0)),
            pl.BlockSpec((1, 1, CMP_HIDDEN), lambda j, gi: (j, 0, 0)),
            pl.BlockSpec((1, CMP_HIDDEN, HEAD_DIM), lambda j, gi: (j, 0, 0)),
            pl.BlockSpec((1, 1, HEAD_DIM), lambda j, gi: (j, 0, 0)),
            pl.BlockSpec((nc, HEAD_DIM), lambda j, gi: (0, 0)),
            pl.BlockSpec((nc, HEAD_DIM), lambda j, gi: (0, 0)),
        ],
        out_specs=[pl.BlockSpec((1, 1, nc, HEAD_DIM), lambda j, gi: (j, gi, 0, 0)),
                   pl.BlockSpec((1, 1, HEAD_DIM, nc), lambda j, gi: (j, gi, 0, 0))],
        out_shape=[jax.ShapeDtypeStruct((2, g, nc, HEAD_DIM), BF16),
                   jax.ShapeDtypeStruct((2, g, HEAD_DIM, nc), BF16)],
        compiler_params=_params(("arbitrary", "arbitrary")),
        name="nsa_compress",
    )(tok16, pe, w1, b1, w2, b2, cos_c, sin_c)


def _dil_kernel(q_ref, k_ref, v_ref, o_ref, bias_ref, s_a, s_b, *, tq, padk):
    w = padk + tq
    n_tiles = q_ref.shape[1] // tq

    @pl.when(pl.program_id(0) == 0)
    def _():
        r = lax.broadcasted_iota(jnp.int32, (tq, w), 0)
        c = lax.broadcasted_iota(jnp.int32, (tq, w), 1)
        d = r + padk - c
        cnt = jnp.zeros((tq, w), F32)
        for window, dil in DIL_PAIRS:
            hit = (d >= 0) & (d <= window) & ((d & (dil - 1)) == 0)
            cnt = cnt + jnp.where(hit, 1.0, 0.0)
        bias_ref[...] = jnp.where(cnt > 0.0, jnp.log2(jnp.maximum(cnt, 1.0)), NEG)

    n_early = min(padk // tq, n_tiles)

    def strip(t):
        if isinstance(t, int):
            return (0, (t + 1) * tq) if t < n_early else (t * tq - padk, w)
        return pl.multiple_of(t * tq - padk, tq), w

    def scores(s_ref, t):
        k0, width = strip(t)
        q0 = t * tq if isinstance(t, int) else pl.multiple_of(t * tq, tq)
        s_ref[:, :width] = (_nt_dot(q_ref[0, pl.ds(q0, tq), :], k_ref[0, pl.ds(k0, width), :])
                            + bias_ref[:, w - width:])

    def finish(s_ref, t):
        k0, width = strip(t)
        q0 = t * tq if isinstance(t, int) else pl.multiple_of(t * tq, tq)
        s = s_ref[:, :width]
        p = jnp.exp2(s - jnp.max(s, axis=-1, keepdims=True))
        l = jnp.sum(p, axis=-1, keepdims=True)
        o = jnp.dot(p.astype(BF16), v_ref[0, pl.ds(k0, width), :], preferred_element_type=F32)
        o_ref[pl.ds(q0, tq), :] = (o / l).astype(BF16)

    bufs = (s_a, s_b)
    scores(s_a, 0)
    for t in range(n_early):
        if t + 1 < n_tiles:
            scores(bufs[(t + 1) % 2], t + 1)
        finish(bufs[t % 2], t)

    def pair(pi, _):
        t = 2 * pi
        scores(s_b, t + 1)
        finish(s_a, t)
        scores(s_a, t + 2)
        finish(s_b, t + 1)
        return 0

    if n_early < n_tiles:
        lax.fori_loop(n_early // 2, n_tiles // 2 - 1, pair, 0)
        scores(s_b, n_tiles - 1)
        finish(s_a, n_tiles - 2)
        finish(s_b, n_tiles - 1)


def _dilated(heads, tq):
    _, s, _ = heads.shape
    padk = max(wd for wd, _ in DIL_PAIRS)
    n_tiles, n_early = s // tq, min(padk // tq, s // tq)
    assert padk % tq == 0 and n_tiles % 2 == 0 and n_early % 2 == 0
    return pl.pallas_call(
        functools.partial(_dil_kernel, tq=tq, padk=padk),
        grid=(N_HEADS_DIL,),
        in_specs=[
            pl.BlockSpec((1, s, HEAD_DIM), lambda h: (H_QA + h, 0, 0)),
            pl.BlockSpec((1, s, HEAD_DIM), lambda h: (H_KA + h, 0, 0)),
            pl.BlockSpec((1, s, HEAD_DIM), lambda h: (H_VA + h, 0, 0)),
        ],
        out_specs=pl.BlockSpec((s, HEAD_DIM), lambda h: (0, h)),
        out_shape=jax.ShapeDtypeStruct((s, D_DIL), BF16),
        scratch_shapes=[pltpu.VMEM((tq, padk + tq), F32),
                        pltpu.VMEM((tq, padk + tq), F32),
                        pltpu.VMEM((tq, padk + tq), F32)],
        compiler_params=_params(("arbitrary",)),
        name="dilated_attn",
    )(heads, heads, heads)


def _add_per_head(s, bias, hg):
    tq = bias.shape[0]
    return jnp.concatenate([s[u * tq:(u + 1) * tq] + bias for u in range(hg)], axis=0)


def _nsa_select_kernel(*refs, nsp, tq, k_sel):
    q_refs = refs[:NSA_GROUP]
    kc_ref, vct_ref, gate_ref, ovt_ref, cmp_ref, sel_ref = refs[NSA_GROUP:]
    t0 = pl.program_id(1) * tq
    hg = NSA_GROUP
    ncp = kc_ref.shape[2]
    gate = gate_ref[...]

    def branch(n_rows):
        n_sel = min(n_rows * CMP_STRIDE // SEL_BLOCK, nsp)
        kc = kc_ref[0, 0, :n_rows, :]
        vct = vct_ref[0, 0, :, :n_rows]
        ovt = ovt_ref[:n_sel, :n_rows]
        n_io = lax.broadcasted_iota(jnp.int32, (n_rows, tq), 0)
        t_io = t0 + lax.broadcasted_iota(jnp.int32, (n_rows, tq), 1)
        cmask = (n_io * CMP_STRIDE + (CMP_BLOCK - 1) <= t_io) & (n_io < ncp - 1)
        imp = jnp.zeros((n_sel, tq), F32)
        for u in range(hg):
            st = jnp.where(cmask, _nt_dot(kc, q_refs[u][0]), NEG)
            m = jnp.maximum(jnp.max(st, axis=0, keepdims=True), 0.1 * NEG)
            e = jnp.exp2(st - m)
            r = 1.0 / jnp.maximum(jnp.sum(e, axis=0, keepdims=True), 1e-30)
            eb = e.astype(BF16)
            imp = imp + jnp.dot(ovt, eb, preferred_element_type=F32) * r
            o_cmp = (jnp.dot(vct, eb, preferred_element_type=F32) * r).T
            c = N_NSA_BRANCHES * u
            cmp_ref[:, u * HEAD_DIM:(u + 1) * HEAD_DIM] = gate[:, c:c + 1] * o_cmp

        j_io = lax.broadcasted_iota(jnp.int32, (n_sel, tq), 0)
        t_sel = t0 + lax.broadcasted_iota(jnp.int32, (n_sel, tq), 1)
        valid = j_io * SEL_BLOCK <= t_sel
        cur = t_sel // SEL_BLOCK
        forced = (j_io == 0) | (j_io == cur) | (j_io == cur - 1)
        score = jnp.where(valid & jnp.logical_not(forced), imp, NEG)
        for _ in range(k_sel - 3):
            m = jnp.max(score, axis=0, keepdims=True)
            first = jnp.min(jnp.where(score == m, j_io, n_sel), axis=0, keepdims=True)
            score = jnp.where(j_io == first, -jnp.inf, score)
        sel_t = jnp.where(valid & (forced | (score == -jnp.inf)), 1.0, 0.0)
        if n_sel < nsp:
            sel_t = jnp.concatenate([sel_t, jnp.zeros((nsp - n_sel, tq), F32)], axis=0)
        sel_ref[0] = jnp.where(sel_t.T > 0.0, 0.0, SEL_OFF).astype(BF16)

    n_var = ncp // CMP_ROW_CHUNK
    need = (t0 + tq) // CMP_STRIDE
    var = jnp.minimum((need + CMP_ROW_CHUNK - 1) // CMP_ROW_CHUNK, n_var) - 1
    for v in range(n_var):
        pl.when(var == v)(functools.partial(branch, CMP_ROW_CHUNK * (v + 1)))


def _nsa_select(heads, kcv, vct, gates, ovt, tq):
    _, s, _ = heads.shape
    ncp = kcv.shape[2]
    nsp = ovt.shape[0]
    hg = NSA_GROUP
    k_sel = min(N_SELECT, s // SEL_BLOCK)
    assert k_sel >= 3 and ncp % CMP_ROW_CHUNK == 0
    q_map = lambda u, g, qb: (H_QN + hg * g + u, qb, 0)
    return pl.pallas_call(
        functools.partial(_nsa_select_kernel, nsp=nsp, tq=tq, k_sel=k_sel),
        grid=(NSA_KV_GROUPS, s // tq),
        in_specs=[
            *[pl.BlockSpec((1, tq, HEAD_DIM), functools.partial(q_map, u)) for u in range(hg)],
            pl.BlockSpec((1, 1, ncp, HEAD_DIM), lambda g, qb: (0, g, 0, 0)),
            pl.BlockSpec((1, 1, HEAD_DIM, ncp), lambda g, qb: (1, g, 0, 0)),
            pl.BlockSpec((tq, HEAD_DIM), lambda g, qb: (qb, g)),
            pl.BlockSpec((nsp, ncp), lambda g, qb: (0, 0)),
        ],
        out_specs=[pl.BlockSpec((tq, hg * HEAD_DIM), lambda g, qb: (qb, g)),
                   pl.BlockSpec((1, tq, nsp), lambda g, qb: (g, qb, 0))],
        out_shape=[jax.ShapeDtypeStruct((s, D_NSA), F32),
                   jax.ShapeDtypeStruct((NSA_KV_GROUPS, s, nsp), BF16)],
        compiler_params=_params(("arbitrary", "arbitrary")),
        name="nsa_select",
    )(*([heads] * hg), kcv, vct, gates, ovt)


def _nsa_kernel(*refs, s_len, nsp, tq, kt):
    q_refs = refs[:NSA_GROUP]
    (ks_ref, vs_ref, kw_ref, vw_ref, gate_ref, cmp_ref, sel_ref,
     o_ref, kaug, vaug, vwaug, cbias, wbias, s_a, s_b, m_ref, acc_ref,
     part_ref) = refs[NSA_GROUP:]
    g = pl.program_id(0)
    qb = pl.program_id(1)
    t0 = qb * tq
    hg = NSA_GROUP
    rows = hg * tq
    ww = NSA_WINDOW + tq

    @pl.when((g == 0) & (qb == 0))
    def _():
        for r in range(kt // tq):
            d = (lax.broadcasted_iota(jnp.int32, (tq, kt), 0) + r * tq
                 - lax.broadcasted_iota(jnp.int32, (tq, kt), 1))
            cbias[r] = jnp.where(d >= 0, 0.0, NEG)
        for w in range(NSA_WINDOW // tq + 1):
            d = (lax.broadcasted_iota(jnp.int32, (tq, ww), 0) + w * tq
                 - lax.broadcasted_iota(jnp.int32, (tq, ww), 1))
            wbias[w] = jnp.where((d >= 0) & (d < NSA_WINDOW), 0.0, NEG)

    @pl.when(qb == 0)
    def _():
        kaug[:, 0:HEAD_DIM] = ks_ref[0]
        key = lax.broadcasted_iota(jnp.int32, (s_len, nsp), 0)
        blk = lax.broadcasted_iota(jnp.int32, (s_len, nsp), 1)
        kaug[:, HEAD_DIM:] = jnp.where(blk == key // SEL_BLOCK, 1.0, 0.0).astype(BF16)
        ones = jnp.ones((s_len, HEAD_DIM), BF16)
        vaug[:, 0:HEAD_DIM] = vs_ref[0]
        vaug[:, HEAD_DIM:] = ones
        vwaug[:, 0:HEAD_DIM] = vw_ref[0]
        vwaug[:, HEAD_DIM:] = ones

    gate = gate_ref[...]
    q5 = jnp.concatenate([r[0] for r in q_refs], axis=0)

    qaug = jnp.concatenate([q5, jnp.concatenate([sel_ref[0]] * hg, axis=0)], axis=1)

    w0 = pl.multiple_of(jnp.maximum(t0 - NSA_WINDOW, 0), tq)
    sw = _nt_dot(q5, kw_ref[0, pl.ds(w0, ww), :])
    sw = _add_per_head(sw, wbias[jnp.minimum(qb, NSA_WINDOW // tq)], hg)
    pw = jnp.exp2(sw - jnp.max(sw, axis=-1, keepdims=True))
    acc_w = jnp.dot(pw.astype(BF16), vwaug[pl.ds(w0, ww), :], preferred_element_type=F32)
    o_win = acc_w[:, :HEAD_DIM] / acc_w[:, HEAD_DIM:]
    for u in range(hg):
        c = N_NSA_BRANCHES * u + 2
        cols = slice(u * HEAD_DIM, (u + 1) * HEAD_DIM)
        part_ref[:, cols] = cmp_ref[:, cols] + gate[:, c:c + 1] * o_win[u * tq:(u + 1) * tq]

    def sel_scores(kti, width=kt):
        k0 = pl.multiple_of(kti * kt, kt)
        return _nt_dot(qaug, kaug[pl.ds(k0, width), :])

    def sel_update(s_ref, kti, diag=None):
        width = kt if diag is None else (diag + 1) * tq
        s = s_ref[:, :width]
        if diag is not None:
            s = _add_per_head(s, cbias[diag, :, :width], hg)
        k0 = pl.multiple_of(kti * kt, kt)
        m_i = m_ref[...]
        m_new = jnp.maximum(m_i, jnp.max(s, axis=-1, keepdims=True))
        p = jnp.exp2(s - m_new)
        pv = jnp.dot(p.astype(BF16), vaug[pl.ds(k0, width), :], preferred_element_type=F32)
        acc_ref[...] = jnp.exp2(m_i - m_new) * acc_ref[...] + pv
        m_ref[...] = m_new

    def sel_pair(pi, _):
        t = 2 * pi
        s_b[...] = sel_scores(t + 1)
        sel_update(s_a, t)
        s_a[...] = sel_scores(t + 2)
        sel_update(s_b, t + 1)
        return 0

    last = (t0 + tq + kt - 1) // kt - 1
    m_ref[...] = jnp.full((rows, 1), NEG, F32)
    acc_ref[...] = jnp.zeros((rows, 2 * HEAD_DIM), F32)
    s_a[...] = sel_scores(0)
    lax.fori_loop(0, last // 2, sel_pair, 0)
    t_even = 2 * (last // 2)

    def tail_odd(diag):
        width = (diag + 1) * tq
        s_b[:, :width] = sel_scores(t_even + 1, width)
        sel_update(s_a, t_even)
        sel_update(s_b, t_even + 1, diag)

    def tail_even(diag):
        sel_update(s_a, t_even, diag)

    for diag in range(kt // tq):
        here = qb % (kt // tq) == diag
        pl.when(here & (last % 2 == 1))(functools.partial(tail_odd, diag))
        pl.when(here & (last % 2 == 0))(functools.partial(tail_even, diag))

    gate = gate_ref[...]
    for u in range(hg):
        c = N_NSA_BRANCHES * u + 1
        acc_u = acc_ref[u * tq:(u + 1) * tq, :]
        o_slc = acc_u[:, :HEAD_DIM] / acc_u[:, HEAD_DIM:]
        cols = slice(u * HEAD_DIM, (u + 1) * HEAD_DIM)
        o_ref[:, cols] = (part_ref[:, cols] + gate[:, c:c + 1] * o_slc).astype(BF16)


def _nsa(heads, gates, part_cmp, sel_off, tq, kt):
    _, s, _ = heads.shape
    nsp = sel_off.shape[2]
    hg = NSA_GROUP
    assert kt % tq == 0 and NSA_WINDOW % tq == 0 and s % kt == 0
    q_map = lambda u, g, qb: (H_QN + hg * g + u, qb, 0)
    full = lambda hbase: pl.BlockSpec((1, s, HEAD_DIM), lambda g, qb: (hbase + g, 0, 0),
                                      pipeline_mode=pl.Buffered(1))
    return pl.pallas_call(
        functools.partial(_nsa_kernel, s_len=s, nsp=nsp, tq=tq, kt=kt),
        grid=(NSA_KV_GROUPS, s // tq),
        in_specs=[
            *[pl.BlockSpec((1, tq, HEAD_DIM), functools.partial(q_map, u)) for u in range(hg)],
            full(H_KS), full(H_VS), full(H_KW), full(H_VW),
            pl.BlockSpec((tq, HEAD_DIM), lambda g, qb: (qb, g)),
            pl.BlockSpec((tq, hg * HEAD_DIM), lambda g, qb: (qb, g)),
            pl.BlockSpec((1, tq, nsp), lambda g, qb: (g, qb, 0)),
        ],
        out_specs=pl.BlockSpec((tq, hg * HEAD_DIM), lambda g, qb: (qb, g)),
        out_shape=jax.ShapeDtypeStruct((s, D_NSA), BF16),
        scratch_shapes=[pltpu.VMEM((s, HEAD_DIM + nsp), BF16),
                        pltpu.VMEM((s, 2 * HEAD_DIM), BF16),
                        pltpu.VMEM((s, 2 * HEAD_DIM), BF16),
                        pltpu.VMEM((kt // tq, tq, kt), F32),
                        pltpu.VMEM((NSA_WINDOW // tq + 1, tq, NSA_WINDOW + tq), F32),
                        pltpu.VMEM((hg * tq, kt), F32),
                        pltpu.VMEM((hg * tq, kt), F32),
                        pltpu.VMEM((hg * tq, 1), F32),
                        pltpu.VMEM((hg * tq, 2 * HEAD_DIM), F32),
                        pltpu.VMEM((tq, hg * HEAD_DIM), F32)],
        compiler_params=_params(("arbitrary", "arbitrary")),
        name="nsa_attn",
    )(*([heads] * hg), heads, heads, heads, heads, gates, part_cmp, sel_off)


def _out_proj_ln_kernel(a_ref, b_ref, w_ref, h_ref, g_ref, be_ref, o_ref, wb16, *, alpha):
    @pl.when(pl.program_id(0) == 0)
    def _():
        wb16[...] = w_ref[...].astype(BF16)

    tm = o_ref.shape[0]
    rc = tm // ROW_CHUNKS
    for c in range(ROW_CHUNKS):
        r = slice(c * rc, (c + 1) * rc)
        mix = (jnp.dot(a_ref[r, :], wb16[:D_DIL, :], preferred_element_type=F32)
               + jnp.dot(b_ref[r, :], wb16[D_DIL:, :], preferred_element_type=F32))
        o_ref[r, :] = _layer_norm(alpha * h_ref[r, :] + mix, g_ref[...], be_ref[...])


def _out_proj_ln(mix_a, mix_b, w, h, g, b, alpha, tm):
    s, d = h.shape
    return pl.pallas_call(
        functools.partial(_out_proj_ln_kernel, alpha=alpha),
        grid=(s // tm,),
        in_specs=[
            pl.BlockSpec((tm, D_DIL), lambda i: (i, 0)),
            pl.BlockSpec((tm, D_NSA), lambda i: (i, 0)),
            pl.BlockSpec((d, d), lambda i: (0, 0), pipeline_mode=pl.Buffered(1)),
            pl.BlockSpec((tm, d), lambda i: (i, 0)),
            pl.BlockSpec((1, d), lambda i: (0, 0)),
            pl.BlockSpec((1, d), lambda i: (0, 0)),
        ],
        out_specs=pl.BlockSpec((tm, d), lambda i: (i, 0)),
        out_shape=jax.ShapeDtypeStruct((s, d), F32),
        scratch_shapes=[pltpu.VMEM((d, d), BF16)],
        compiler_params=_params(("arbitrary",)),
        name="out_proj_ln",
    )(mix_a, mix_b, w, h, g, b)


def _overlap_t(s):
    n_cmp = (s - CMP_BLOCK) // CMP_STRIDE + 1
    n_sel = s // SEL_BLOCK
    ncp = s // CMP_STRIDE
    nsp = -(-n_sel // HEAD_DIM) * HEAD_DIM
    c_lo = np.arange(n_cmp) * CMP_STRIDE
    c_hi = c_lo + CMP_BLOCK - 1
    s_lo = (np.arange(n_sel) * SEL_BLOCK)[:, None]
    ov = np.zeros((nsp, ncp), np.float32)
    ov[:n_sel, :n_cmp] = (c_lo[None, :] <= s_lo + SEL_BLOCK - 1) & (c_hi[None, :] >= s_lo)
    return jnp.asarray(ov, BF16)


def kernel(x, positions, ln1_g, ln1_b, ffn1_w1, ffn1_w3, ffn1_w2, w_in, gate_b, cmp_pe, cmp_w1,
           cmp_b1, cmp_w2, cmp_b2, w_out, ln2_g, ln2_b, ffn2_w1, ffn2_w3, ffn2_w2, ln3_g, ln3_b):
    bsz, s, d = x.shape
    assert bsz == 1 and d == (N_HEADS_DIL + N_HEADS_NSA) * HEAD_DIM
    assert s % max(FFN_ROWS, PROJ_ROWS, NSA_KEY_TILE) == 0 and s >= NSA_WINDOW + NSA_Q_TILE
    assert ffn1_w1.shape[2] % FFN_HIDDEN_TILE == 0
    alpha = (2.0 * DEPTH) ** 0.25
    scale = HEAD_DIM ** -0.5 * LOG2_E
    row = lambda v: v.reshape(1, -1)

    inv_freq = ROPE_THETA ** (-jnp.arange(0, HEAD_DIM, 2, dtype=F32) / HEAD_DIM)
    invf = jnp.concatenate([inv_freq, inv_freq]).reshape(1, HEAD_DIM)
    pos = positions[0].astype(F32).reshape(s, 1)
    cosf, sinf = _rope_tables(pos, pos, invf, ROPE_TILE)
    ncp = s // CMP_STRIDE
    n_cmp = (s - CMP_BLOCK) // CMP_STRIDE + 1
    pos_lo = jnp.pad(pos[0:n_cmp * CMP_STRIDE:CMP_STRIDE], ((0, ncp - n_cmp), (0, 0)))
    pos_hi = jnp.pad(pos[CMP_BLOCK - 1::CMP_STRIDE][:n_cmp], ((0, ncp - n_cmp), (0, 0)))
    cos_c, sin_c = _rope_tables(pos_lo, pos_hi, invf, ncp)

    g_off = 3 * D_DIL + D_NSA + 6 * KV_W
    per_g = NSA_GROUP * N_NSA_BRANCHES
    w_in_t = w_in[0].T.astype(BF16)
    wg = jnp.zeros((NSA_KV_GROUPS * HEAD_DIM, d), BF16)
    bg = jnp.zeros((1, NSA_KV_GROUPS * HEAD_DIM), F32)
    for g in range(NSA_KV_GROUPS):
        wg = wg.at[g * HEAD_DIM:g * HEAD_DIM + per_g].set(
            w_in_t[g_off + g * per_g:g_off + (g + 1) * per_g])
        bg = bg.at[0, g * HEAD_DIM:g * HEAD_DIM + per_g].set(gate_b[0][g * per_g:(g + 1) * per_g])

    h0 = x[0]
    h1 = _ffn_ln(h0, ffn1_w1[0], ffn1_w3[0], ffn1_w2[0], row(ln1_g[0]), row(ln1_b[0]),
                 alpha, FFN_ROWS, FFN_HIDDEN_TILE)

    heads, gates = _in_proj(h1, w_in_t, cosf, sinf, wg, bg, scale, PROJ_ROWS)
    tok16 = heads[H_KC:H_KS].reshape(2 * NSA_KV_GROUPS, ncp, CMP_STRIDE * HEAD_DIM)
    kcv, kcv_t = _compress(
        tok16, cmp_pe[0].reshape(2, 1, CMP_BLOCK * HEAD_DIM), cmp_w1[0],
        cmp_b1[0].reshape(2, 1, CMP_HIDDEN), cmp_w2[0],
        cmp_b2[0].reshape(2, 1, HEAD_DIM), cos_c, sin_c)
    mix_a = _dilated(heads, DIL_Q_TILE)
    part_cmp, sel_off = _nsa_select(heads, kcv, kcv_t, gates, _overlap_t(s), SEL_Q_TILE)
    mix_b = _nsa(heads, gates, part_cmp, sel_off, NSA_Q_TILE, NSA_KEY_TILE)

    h2 = _out_proj_ln(mix_a, mix_b, w_out[0], h1, row(ln2_g[0]), row(ln2_b[0]), alpha,
                      OUT_PROJ_ROWS)
    h3 = _ffn_ln(h2, ffn2_w1[0], ffn2_w3[0], ffn2_w2[0], row(ln3_g[0]), row(ln3_b[0]),
                 alpha, FFN_ROWS, FFN_HIDDEN_TILE)
    return h3.reshape(bsz, s, d)
```

```python
import functools

import jax
import jax.numpy as jnp
import numpy as np
from jax import lax
from jax.experimental import pallas as pl
from jax.experimental.pallas import tpu as pltpu

HEAD_DIM = 128
N_HEADS_DIL = 6
N_HEADS_NSA = 10
NSA_KV_GROUPS = 2
NSA_GROUP = N_HEADS_NSA // NSA_KV_GROUPS
N_NSA_BRANCHES = 3
DIL_PAIRS = ((128, 1), (512, 4), (2048, 16))
CMP_BLOCK = 32
CMP_STRIDE = 16
CMP_HIDDEN = 256
SEL_BLOCK = 64
N_SELECT = 16
NSA_WINDOW = 512
ROPE_THETA = 10000.0
LN_EPS = 1e-5
NEG = -1e30
SEL_OFF = -1e9
LOG2_E = 1.4426950408889634
DEPTH = 1

D_DIL = N_HEADS_DIL * HEAD_DIM
D_NSA = N_HEADS_NSA * HEAD_DIM
KV_W = NSA_KV_GROUPS * HEAD_DIM

H_QA, H_KA, H_VA, H_QN = 0, 6, 12, 18
H_KC, H_VC, H_KS, H_VS, H_KW, H_VW = 28, 30, 32, 34, 36, 38
N_PROJ_HEADS = 40
HEADS_PER_PROJ_BLOCK = 8
PROJ_SUB_HEADS = 2
ROW_CHUNKS = 2
CMP_ROW_CHUNK = 128

ROPE_TILE = 512
FFN_ROWS, FFN_HIDDEN_TILE = 1024, 256
PROJ_ROWS = 1024
OUT_PROJ_ROWS = 512
DIL_Q_TILE = 256
SEL_Q_TILE = 1024
NSA_Q_TILE, NSA_KEY_TILE = 256, 1024

VMEM_LIMIT_BYTES = 56 * 1024 * 1024
FFN_VMEM_LIMIT_BYTES = 60 * 1024 * 1024

F32 = jnp.float32
BF16 = jnp.bfloat16


def _params(sem, vmem=VMEM_LIMIT_BYTES):
    return pltpu.CompilerParams(dimension_semantics=sem, vmem_limit_bytes=vmem)


def _nt_dot(a, b):
    return lax.dot_general(a, b, (((1,), (1,)), ((), ())), preferred_element_type=F32)


def _layer_norm(y, g, b):
    mu = jnp.mean(y, axis=-1, keepdims=True)
    yc = y - mu
    var = jnp.mean(yc * yc, axis=-1, keepdims=True)
    return yc * lax.rsqrt(var + LN_EPS) * g + b


def _rope_table_kernel(pa_ref, pb_ref, invf_ref, cos_ref, sin_ref):
    pos = (pa_ref[...] + pb_ref[...]) * 0.5
    ang = pos * invf_ref[...]
    lane = lax.broadcasted_iota(jnp.int32, ang.shape, 1)
    cos_ref[...] = jnp.cos(ang)
    sin_ref[...] = jnp.where(lane < HEAD_DIM // 2, -1.0, 1.0) * jnp.sin(ang)


def _rope_tables(pos_a, pos_b, invf, tile):
    n = pos_a.shape[0]
    spec_p = pl.BlockSpec((tile, 1), lambda i: (i, 0))
    spec_t = pl.BlockSpec((tile, HEAD_DIM), lambda i: (i, 0))
    return pl.pallas_call(
        _rope_table_kernel,
        grid=(n // tile,),
        in_specs=[spec_p, spec_p, pl.BlockSpec((1, HEAD_DIM), lambda i: (0, 0))],
        out_specs=[spec_t, spec_t],
        out_shape=[jax.ShapeDtypeStruct((n, HEAD_DIM), F32)] * 2,
        compiler_params=_params(("arbitrary",)),
        name="rope_tables",
    )(pos_a, pos_b, invf)


def _ffn_ln_kernel(h_ref, w1_ref, w3_ref, w2_ref, g_ref, b_ref, o_ref, hb_ref, *, alpha, nf):
    f = pl.program_id(1)

    @pl.when(f == 0)
    def _():
        h = h_ref[...]
        hb_ref[...] = h.astype(BF16)
        o_ref[...] = (2.0 * alpha) * h

    hb = hb_ref[...]
    a = jnp.dot(hb, w1_ref[...].astype(BF16), preferred_element_type=F32)
    b = jnp.dot(hb, w3_ref[...].astype(BF16), preferred_element_type=F32)
    act = (a * jax.nn.sigmoid(a)) * b
    o_ref[...] += jnp.dot(act.astype(BF16), w2_ref[...].astype(BF16),
                          preferred_element_type=F32)

    @pl.when(f == nf - 1)
    def _():
        o_ref[...] = _layer_norm(0.5 * o_ref[...], g_ref[...], b_ref[...])


def _ffn_ln(h, w1, w3, w2, g, b, alpha, tm, tf):
    s, d = h.shape
    dff = w1.shape[1]
    nf = dff // tf
    return pl.pallas_call(
        functools.partial(_ffn_ln_kernel, alpha=alpha, nf=nf),
        grid=(s // tm, nf),
        in_specs=[
            pl.BlockSpec((tm, d), lambda i, f: (i, 0)),
            pl.BlockSpec((d, tf), lambda i, f: (0, f)),
            pl.BlockSpec((d, tf), lambda i, f: (0, f)),
            pl.BlockSpec((tf, d), lambda i, f: (f, 0)),
            pl.BlockSpec((1, d), lambda i, f: (0, 0)),
            pl.BlockSpec((1, d), lambda i, f: (0, 0)),
        ],
        out_specs=pl.BlockSpec((tm, d), lambda i, f: (i, 0)),
        out_shape=jax.ShapeDtypeStruct((s, d), F32),
        scratch_shapes=[pltpu.VMEM((tm, d), BF16)],
        compiler_params=_params(("arbitrary", "arbitrary"), FFN_VMEM_LIMIT_BYTES),
        name="ffn_ln",
    )(h, w1, w3, w2, g, b)


def _in_proj_kernel(h_ref, w_ref, cos_ref, sin_ref, wg_ref, bg_ref, o_ref, gate_ref, hb_ref,
                    *, scale):
    j = pl.program_id(1)
    hpb = HEADS_PER_PROJ_BLOCK
    sub = PROJ_SUB_HEADS

    @pl.when(j == 0)
    def _():
        hb_ref[...] = h_ref[...].astype(BF16)
        gate_ref[...] = jax.nn.sigmoid(_nt_dot(hb_ref[...], wg_ref[...]) + bg_ref[...])

    cos = cos_ref[...]
    sin = sin_ref[...]
    h = hb_ref[...]
    for k0 in range(0, hpb, sub):
        acc = _nt_dot(h, w_ref[k0 * HEAD_DIM:(k0 + sub) * HEAD_DIM, :])
        for k in range(sub):
            hh = j * hpb + k0 + k
            is_q = (hh < H_KA) | ((hh >= H_QN) & (hh < H_KC))
            rope = (is_q | (hh < H_VA) | ((hh >= H_KS) & (hh < H_VS))
                    | ((hh >= H_KW) & (hh < H_VW)))
            x = acc[:, k * HEAD_DIM:(k + 1) * HEAD_DIM]
            xr = x * cos + pltpu.roll(x, HEAD_DIM // 2, 1) * sin
            y = jnp.where(rope, xr, x) * jnp.where(is_q, scale, 1.0).astype(F32)
            o_ref[k0 + k] = y.astype(BF16)


def _in_proj(h, w, cosf, sinf, wg, bg, scale, tm):
    s, d = h.shape
    n = wg.shape[0]
    hpb = HEADS_PER_PROJ_BLOCK
    return pl.pallas_call(
        functools.partial(_in_proj_kernel, scale=scale),
        grid=(s // tm, N_PROJ_HEADS // hpb),
        in_specs=[
            pl.BlockSpec((tm, d), lambda i, j: (i, 0)),
            pl.BlockSpec((hpb * HEAD_DIM, d), lambda i, j: (j, 0)),
            pl.BlockSpec((tm, HEAD_DIM), lambda i, j: (i, 0)),
            pl.BlockSpec((tm, HEAD_DIM), lambda i, j: (i, 0)),
            pl.BlockSpec((n, d), lambda i, j: (0, 0)),
            pl.BlockSpec((1, n), lambda i, j: (0, 0)),
        ],
        out_specs=[pl.BlockSpec((hpb, tm, HEAD_DIM), lambda i, j: (j, i, 0)),
                   pl.BlockSpec((tm, n), lambda i, j: (i, 0))],
        out_shape=[jax.ShapeDtypeStruct((N_PROJ_HEADS, s, HEAD_DIM), BF16),
                   jax.ShapeDtypeStruct((s, n), F32)],
        scratch_shapes=[pltpu.VMEM((tm, d), BF16)],
        compiler_params=_params(("arbitrary", "arbitrary")),
        name="in_proj",
    )(h, w, cosf, sinf, wg, bg)


def _compress_kernel(tok_ref, pe_ref, w1_ref, b1_ref, w2_ref, b2_ref, cos_ref, sin_ref,
                     o_ref, ot_ref):
    j = pl.program_id(0)
    half = CMP_STRIDE * HEAD_DIM
    tok = tok_ref[0].astype(F32)
    pe = pe_ref[0]
    w1 = w1_ref[0].astype(BF16)
    top = (tok + pe[:, :half]).astype(BF16)
    bot = (tok + pe[:, half:]).astype(BF16)
    u = jnp.dot(top, w1[:half], preferred_element_type=F32)
    v = jnp.dot(bot, w1[half:], preferred_element_type=F32)
    nc = u.shape[0]
    hid = u + pltpu.roll(v, nc - 1, 0) + b1_ref[0]
    hid = jax.nn.gelu(hid)
    out = jnp.dot(hid.astype(BF16), w2_ref[0].astype(BF16),
                  preferred_element_type=F32) + b2_ref[0]
    roped = out * cos_ref[...] + pltpu.roll(out, HEAD_DIM // 2, 1) * sin_ref[...]
    out = jnp.where(j == 0, roped, out)
    o_ref[0, 0] = out.astype(BF16)
    ot_ref[0, 0] = out.T.astype(BF16)


def _compress(tok16, pe, w1, b1, w2, b2, cos_c, sin_c):
    nc = tok16.shape[1]
    blk = CMP_BLOCK * HEAD_DIM
    g = NSA_KV_GROUPS
    return pl.pallas_call(
        _compress_kernel,
        grid=(2, g),
        in_specs=[
            pl.BlockSpec((1, nc, CMP_STRIDE * HEAD_DIM), lambda j, gi: (g * j + gi, 0, 0)),
            pl.BlockSpec((1, 1, blk), lambda j, gi: (j, 0, 0)),
            pl.BlockSpec((1, blk, CMP_HIDDEN), lambda j, gi: (j, 0, 0)),
            pl.BlockSpec((1, 1, CMP_HIDDEN), lambda j, gi: (j, 0, 0)),
            pl.BlockSpec((1, CMP_HIDDEN, HEAD_DIM), lambda j, gi: (j, 0, 0)),
            pl.BlockSpec((1, 1, HEAD_DIM), lambda j, gi: (j, 0, 0)),
            pl.BlockSpec((nc, HEAD_DIM), lambda j, gi: (0, 0)),
            pl.BlockSpec((nc, HEAD_DIM), lambda j, gi: (0, 0)),
        ],
        out_specs=[pl.BlockSpec((1, 1, nc, HEAD_DIM), lambda j, gi: (j, gi, 0, 0)),
                   pl.BlockSpec((1, 1, HEAD_DIM, nc), lambda j, gi: (j, gi, 0, 0))],
        out_shape=[jax.ShapeDtypeStruct((2, g, nc, HEAD_DIM), BF16),
                   jax.ShapeDtypeStruct((2, g, HEAD_DIM, nc), BF16)],
        compiler_params=_params(("arbitrary", "arbitrary")),
        name="nsa_compress",
    )(tok16, pe, w1, b1, w2, b2, cos_c, sin_c)


def _dil_kernel(q_ref, k_ref, v_ref, o_ref, bias_ref, s_a, s_b, *, tq, padk):
    w = padk + tq
    n_tiles = q_ref.shape[1] // tq

    @pl.when(pl.program_id(0) == 0)
    def _():
        r = lax.broadcasted_iota(jnp.int32, (tq, w), 0)
        c = lax.broadcasted_iota(jnp.int32, (tq, w), 1)
        d = r + padk - c
        cnt = jnp.zeros((tq, w), F32)
        for window, dil in DIL_PAIRS:
            hit = (d >= 0) & (d <= window) & ((d & (dil - 1)) == 0)
            cnt = cnt + jnp.where(hit, 1.0, 0.0)
        bias_ref[...] = jnp.where(cnt > 0.0, jnp.log2(jnp.maximum(cnt, 1.0)), NEG)

    n_early = min(padk // tq, n_tiles)

    def strip(t):
        if isinstance(t, int):
            return (0, (t + 1) * tq) if t < n_early else (t * tq - padk, w)
        return pl.multiple_of(t * tq - padk, tq), w

    def scores(s_ref, t):
        k0, width = strip(t)
        q0 = t * tq if isinstance(t, int) else pl.multiple_of(t * tq, tq)
        s_ref[:, :width] = (_nt_dot(q_ref[0, pl.ds(q0, tq), :], k_ref[0, pl.ds(k0, width), :])
                            + bias_ref[:, w - width:])

    def finish(s_ref, t):
        k0, width = strip(t)
        q0 = t * tq if isinstance(t, int) else pl.multiple_of(t * tq, tq)
        s = s_ref[:, :width]
        p = jnp.exp2(s - jnp.max(s, axis=-1, keepdims=True))
        l = jnp.sum(p, axis=-1, keepdims=True)
        o = jnp.dot(p.astype(BF16), v_ref[0, pl.ds(k0, width), :], preferred_element_type=F32)
        o_ref[pl.ds(q0, tq), :] = (o / l).astype(BF16)

    bufs = (s_a, s_b)
    scores(s_a, 0)
    for t in range(n_early):
        if t + 1 < n_tiles:
            scores(bufs[(t + 1) % 2], t + 1)
        finish(bufs[t % 2], t)

    def pair(pi, _):
        t = 2 * pi
        scores(s_b, t + 1)
        finish(s_a, t)
        scores(s_a, t + 2)
        finish(s_b, t + 1)
        return 0

    if n_early < n_tiles:
        lax.fori_loop(n_early // 2, n_tiles // 2 - 1, pair, 0)
        scores(s_b, n_tiles - 1)
        finish(s_a, n_tiles - 2)
        finish(s_b, n_tiles - 1)


def _dilated(heads, tq):
    _, s, _ = heads.shape
    padk = max(wd for wd, _ in DIL_PAIRS)
    n_tiles, n_early = s // tq, min(padk // tq, s // tq)
    assert padk % tq == 0 and n_tiles % 2 == 0 and n_early % 2 == 0
    return pl.pallas_call(
        functools.partial(_dil_kernel, tq=tq, padk=padk),
        grid=(N_HEADS_DIL,),
        in_specs=[
            pl.BlockSpec((1, s, HEAD_DIM), lambda h: (H_QA + h, 0, 0)),
            pl.BlockSpec((1, s, HEAD_DIM), lambda h: (H_KA + h, 0, 0)),
            pl.BlockSpec((1, s, HEAD_DIM), lambda h: (H_VA + h, 0, 0)),
        ],
        out_specs=pl.BlockSpec((s, HEAD_DIM), lambda h: (0, h)),
        out_shape=jax.ShapeDtypeStruct((s, D_DIL), BF16),
        scratch_shapes=[pltpu.VMEM((tq, padk + tq), F32),
                        pltpu.VMEM((tq, padk + tq), F32),
                        pltpu.VMEM((tq, padk + tq), F32)],
        compiler_params=_params(("arbitrary",)),
        name="dilated_attn",
    )(heads, heads, heads)


def _add_per_head(s, bias, hg):
    tq = bias.shape[0]
    return jnp.concatenate([s[u * tq:(u + 1) * tq] + bias for u in range(hg)], axis=0)


def _nsa_select_kernel(*refs, nsp, tq, k_sel):
    q_refs = refs[:NSA_GROUP]
    kc_ref, vct_ref, gate_ref, ovt_ref, cmp_ref, sel_ref = refs[NSA_GROUP:]
    t0 = pl.program_id(1) * tq
    hg = NSA_GROUP
    ncp = kc_ref.shape[2]
    gate = gate_ref[...]

    def branch(n_rows):
        n_sel = min(n_rows * CMP_STRIDE // SEL_BLOCK, nsp)
        kc = kc_ref[0, 0, :n_rows, :]
        vct = vct_ref[0, 0, :, :n_rows]
        ovt = ovt_ref[:n_sel, :n_rows]
        n_io = lax.broadcasted_iota(jnp.int32, (n_rows, tq), 0)
        t_io = t0 + lax.broadcasted_iota(jnp.int32, (n_rows, tq), 1)
        cmask = (n_io * CMP_STRIDE + (CMP_BLOCK - 1) <= t_io) & (n_io < ncp - 1)
        imp = jnp.zeros((n_sel, tq), F32)
        for u in range(hg):
            st = jnp.where(cmask, _nt_dot(kc, q_refs[u][0]), NEG)
            m = jnp.maximum(jnp.max(st, axis=0, keepdims=True), 0.1 * NEG)
            e = jnp.exp2(st - m)
            r = 1.0 / jnp.maximum(jnp.sum(e, axis=0, keepdims=True), 1e-30)
            eb = e.astype(BF16)
            imp = imp + jnp.dot(ovt, eb, preferred_element_type=F32) * r
            o_cmp = (jnp.dot(vct, eb, preferred_element_type=F32) * r).T
            c = N_NSA_BRANCHES * u
            cmp_ref[:, u * HEAD_DIM:(u + 1) * HEAD_DIM] = gate[:, c:c + 1] * o_cmp

        j_io = lax.broadcasted_iota(jnp.int32, (n_sel, tq), 0)
        t_sel = t0 + lax.broadcasted_iota(jnp.int32, (n_sel, tq), 1)
        valid = j_io * SEL_BLOCK <= t_sel
        cur = t_sel // SEL_BLOCK
        forced = (j_io == 0) | (j_io == cur) | (j_io == cur - 1)
        score = jnp.where(valid & jnp.logical_not(forced), imp, NEG)
        for _ in range(k_sel - 3):
            m = jnp.max(score, axis=0, keepdims=True)
            first = jnp.min(jnp.where(score == m, j_io, n_sel), axis=0, keepdims=True)
            score = jnp.where(j_io == first, -jnp.inf, score)
        sel_t = jnp.where(valid & (forced | (score == -jnp.inf)), 1.0, 0.0)
        if n_sel < nsp:
            sel_t = jnp.concatenate([sel_t, jnp.zeros((nsp - n_sel, tq), F32)], axis=0)
        sel_ref[0] = jnp.where(sel_t.T > 0.0, 0.0, SEL_OFF).astype(BF16)

    n_var = ncp // CMP_ROW_CHUNK
    need = (t0 + tq) // CMP_STRIDE
    var = jnp.minimum((need + CMP_ROW_CHUNK - 1) // CMP_ROW_CHUNK, n_var) - 1
    for v in range(n_var):
        pl.when(var == v)(functools.partial(branch, CMP_ROW_CHUNK * (v + 1)))


def _nsa_select(heads, kcv, vct, gates, ovt, tq):
    _, s, _ = heads.shape
    ncp = kcv.shape[2]
    nsp = ovt.shape[0]
    hg = NSA_GROUP
    k_sel = min(N_SELECT, s // SEL_BLOCK)
    assert k_sel >= 3 and ncp % CMP_ROW_CHUNK == 0
    q_map = lambda u, g, qb: (H_QN + hg * g + u, qb, 0)
    return pl.pallas_call(
        functools.partial(_nsa_select_kernel, nsp=nsp, tq=tq, k_sel=k_sel),
        grid=(NSA_KV_GROUPS, s // tq),
        in_specs=[
            *[pl.BlockSpec((1, tq, HEAD_DIM), functools.partial(q_map, u)) for u in range(hg)],
            pl.BlockSpec((1, 1, ncp, HEAD_DIM), lambda g, qb: (0, g, 0, 0)),
            pl.BlockSpec((1, 1, HEAD_DIM, ncp), lambda g, qb: (1, g, 0, 0)),
            pl.BlockSpec((tq, HEAD_DIM), lambda g, qb: (qb, g)),
            pl.BlockSpec((nsp, ncp), lambda g, qb: (0, 0)),
        ],
        out_specs=[pl.BlockSpec((tq, hg * HEAD_DIM), lambda g, qb: (qb, g)),
                   pl.BlockSpec((1, tq, nsp), lambda g, qb: (g, qb, 0))],
        out_shape=[jax.ShapeDtypeStruct((s, D_NSA), F32),
                   jax.ShapeDtypeStruct((NSA_KV_GROUPS, s, nsp), BF16)],
        compiler_params=_params(("arbitrary", "arbitrary")),
        name="nsa_select",
    )(*([heads] * hg), kcv, vct, gates, ovt)


def _nsa_kernel(*refs, s_len, nsp, tq, kt):
    q_refs = refs[:NSA_GROUP]
    (ks_ref, vs_ref, kw_ref, vw_ref, gate_ref, cmp_ref, sel_ref,
     o_ref, kaug, vaug, vwaug, cbias, wbias, s_a, s_b, m_ref, acc_ref,
     part_ref) = refs[NSA_GROUP:]
    g = pl.program_id(0)
    qb = pl.program_id(1)
    t0 = qb * tq
    hg = NSA_GROUP
    rows = hg * tq
    ww = NSA_WINDOW + tq

    @pl.when((g == 0) & (qb == 0))
    def _():
        for r in range(kt // tq):
            d = (lax.broadcasted_iota(jnp.int32, (tq, kt), 0) + r * tq
                 - lax.broadcasted_iota(jnp.int32, (tq, kt), 1))
            cbias[r] = jnp.where(d >= 0, 0.0, NEG)
        for w in range(NSA_WINDOW // tq + 1):
            d = (lax.broadcasted_iota(jnp.int32, (tq, ww), 0) + w * tq
                 - lax.broadcasted_iota(jnp.int32, (tq, ww), 1))
            wbias[w] = jnp.where((d >= 0) & (d < NSA_WINDOW), 0.0, NEG)

    @pl.when(qb == 0)
    def _():
        kaug[:, 0:HEAD_DIM] = ks_ref[0]
        key = lax.broadcasted_iota(jnp.int32, (s_len, nsp), 0)
        blk = lax.broadcasted_iota(jnp.int32, (s_len, nsp), 1)
        kaug[:, HEAD_DIM:] = jnp.where(blk == key // SEL_BLOCK, 1.0, 0.0).astype(BF16)
        ones = jnp.ones((s_len, HEAD_DIM), BF16)
        vaug[:, 0:HEAD_DIM] = vs_ref[0]
        vaug[:, HEAD_DIM:] = ones
        vwaug[:, 0:HEAD_DIM] = vw_ref[0]
        vwaug[:, HEAD_DIM:] = ones

    gate = gate_ref[...]
    q5 = jnp.concatenate([r[0] for r in q_refs], axis=0)

    qaug = jnp.concatenate([q5, jnp.concatenate([sel_ref[0]] * hg, axis=0)], axis=1)

    w0 = pl.multiple_of(jnp.maximum(t0 - NSA_WINDOW, 0), tq)
    sw = _nt_dot(q5, kw_ref[0, pl.ds(w0, ww), :])
    sw = _add_per_head(sw, wbias[jnp.minimum(qb, NSA_WINDOW // tq)], hg)
    pw = jnp.exp2(sw - jnp.max(sw, axis=-1, keepdims=True))
    acc_w = jnp.dot(pw.astype(BF16), vwaug[pl.ds(w0, ww), :], preferred_element_type=F32)
    o_win = acc_w[:, :HEAD_DIM] / acc_w[:, HEAD_DIM:]
    for u in range(hg):
        c = N_NSA_BRANCHES * u + 2
        cols = slice(u * HEAD_DIM, (u + 1) * HEAD_DIM)
        part_ref[:, cols] = cmp_ref[:, cols] + gate[:, c:c + 1] * o_win[u * tq:(u + 1) * tq]

    def sel_scores(kti, width=kt):
        k0 = pl.multiple_of(kti * kt, kt)
        return _nt_dot(qaug, kaug[pl.ds(k0, width), :])

    def sel_update(s_ref, kti, diag=None):
        width = kt if diag is None else (diag + 1) * tq
        s = s_ref[:, :width]
        if diag is not None:
            s = _add_per_head(s, cbias[diag, :, :width], hg)
        k0 = pl.multiple_of(kti * kt, kt)
        m_i = m_ref[...]
        m_new = jnp.maximum(m_i, jnp.max(s, axis=-1, keepdims=True))
        p = jnp.exp2(s - m_new)
        pv = jnp.dot(p.astype(BF16), vaug[pl.ds(k0, width), :], preferred_element_type=F32)
        acc_ref[...] = jnp.exp2(m_i - m_new) * acc_ref[...] + pv
        m_ref[...] = m_new

    def sel_pair(pi, _):
        t = 2 * pi
        s_b[...] = sel_scores(t + 1)
        sel_update(s_a, t)
        s_a[...] = sel_scores(t + 2)
        sel_update(s_b, t + 1)
        return 0

    last = (t0 + tq + kt - 1) // kt - 1
    m_ref[...] = jnp.full((rows, 1), NEG, F32)
    acc_ref[...] = jnp.zeros((rows, 2 * HEAD_DIM), F32)
    s_a[...] = sel_scores(0)
    lax.fori_loop(0, last // 2, sel_pair, 0)
    t_even = 2 * (last // 2)

    def tail_odd(diag):
        width = (diag + 1) * tq
        s_b[:, :width] = sel_scores(t_even + 1, width)
        sel_update(s_a, t_even)
        sel_update(s_b, t_even + 1, diag)

    def tail_even(diag):
        sel_update(s_a, t_even, diag)

    for diag in range(kt // tq):
        here = qb % (kt // tq) == diag
        pl.when(here & (last % 2 == 1))(functools.partial(tail_odd, diag))
        pl.when(here & (last % 2 == 0))(functools.partial(tail_even, diag))

    gate = gate_ref[...]
    for u in range(hg):
        c = N_NSA_BRANCHES * u + 1
        acc_u = acc_ref[u * tq:(u + 1) * tq, :]
        o_slc = acc_u[:, :HEAD_DIM] / acc_u[:, HEAD_DIM:]
        cols = slice(u * HEAD_DIM, (u + 1) * HEAD_DIM)
        o_ref[:, cols] = (part_ref[:, cols] + gate[:, c:c + 1] * o_slc).astype(BF16)


def _nsa(heads, gates, part_cmp, sel_off, tq, kt):
    _, s, _ = heads.shape
    nsp = sel_off.shape[2]
    hg = NSA_GROUP
    assert kt % tq == 0 and NSA_WINDOW % tq == 0 and s % kt == 0
    q_map = lambda u, g, qb: (H_QN + hg * g + u, qb, 0)
    full = lambda hbase: pl.BlockSpec((1, s, HEAD_DIM), lambda g, qb: (hbase + g, 0, 0),
                                      pipeline_mode=pl.Buffered(1))
    return pl.pallas_call(
        functools.partial(_nsa_kernel, s_len=s, nsp=nsp, tq=tq, kt=kt),
        grid=(NSA_KV_GROUPS, s // tq),
        in_specs=[
            *[pl.BlockSpec((1, tq, HEAD_DIM), functools.partial(q_map, u)) for u in range(hg)],
            full(H_KS), full(H_VS), full(H_KW), full(H_VW),
            pl.BlockSpec((tq, HEAD_DIM), lambda g, qb: (qb, g)),
            pl.BlockSpec((tq, hg * HEAD_DIM), lambda g, qb: (qb, g)),
            pl.BlockSpec((1, tq, nsp), lambda g, qb: (g, qb, 0)),
        ],
        out_specs=pl.BlockSpec((tq, hg * HEAD_DIM), lambda g, qb: (qb, g)),
        out_shape=jax.ShapeDtypeStruct((s, D_NSA), BF16),
        scratch_shapes=[pltpu.VMEM((s, HEAD_DIM + nsp), BF16),
                        pltpu.VMEM((s, 2 * HEAD_DIM), BF16),
                        pltpu.VMEM((s, 2 * HEAD_DIM), BF16),
                        pltpu.VMEM((kt // tq, tq, kt), F32),
                        pltpu.VMEM((NSA_WINDOW // tq + 1, tq, NSA_WINDOW + tq), F32),
                        pltpu.VMEM((hg * tq, kt), F32),
                        pltpu.VMEM((hg * tq, kt), F32),
                        pltpu.VMEM((hg * tq, 1), F32),
                        pltpu.VMEM((hg * tq, 2 * HEAD_DIM), F32),
                        pltpu.VMEM((tq, hg * HEAD_DIM), F32)],
        compiler_params=_params(("arbitrary", "arbitrary")),
        name="nsa_attn",
    )(*([heads] * hg), heads, heads, heads, heads, gates, part_cmp, sel_off)


def _out_proj_ln_kernel(a_ref, b_ref, w_ref, h_ref, g_ref, be_ref, o_ref, wb16, *, alpha):
    @pl.when(pl.program_id(0) == 0)
    def _():
        wb16[...] = w_ref[...].astype(BF16)

    tm = o_ref.shape[0]
    rc = tm // ROW_CHUNKS
    for c in range(ROW_CHUNKS):
        r = slice(c * rc, (c + 1) * rc)
        mix = (jnp.dot(a_ref[r, :], wb16[:D_DIL, :], preferred_element_type=F32)
               + jnp.dot(b_ref[r, :], wb16[D_DIL:, :], preferred_element_type=F32))
        o_ref[r, :] = _layer_norm(alpha * h_ref[r, :] + mix, g_ref[...], be_ref[...])


def _out_proj_ln(mix_a, mix_b, w, h, g, b, alpha, tm):
    s, d = h.shape
    return pl.pallas_call(
        functools.partial(_out_proj_ln_kernel, alpha=alpha),
        grid=(s // tm,),
        in_specs=[
            pl.BlockSpec((tm, D_DIL), lambda i: (i, 0)),
            pl.BlockSpec((tm, D_NSA), lambda i: (i, 0)),
            pl.BlockSpec((d, d), lambda i: (0, 0), pipeline_mode=pl.Buffered(1)),
            pl.BlockSpec((tm, d), lambda i: (i, 0)),
            pl.BlockSpec((1, d), lambda i: (0, 0)),
            pl.BlockSpec((1, d), lambda i: (0, 0)),
        ],
        out_specs=pl.BlockSpec((tm, d), lambda i: (i, 0)),
        out_shape=jax.ShapeDtypeStruct((s, d), F32),
        scratch_shapes=[pltpu.VMEM((d, d), BF16)],
        compiler_params=_params(("arbitrary",)),
        name="out_proj_ln",
    )(mix_a, mix_b, w, h, g, b)


def _overlap_t(s):
    n_cmp = (s - CMP_BLOCK) // CMP_STRIDE + 1
    n_sel = s // SEL_BLOCK
    ncp = s // CMP_STRIDE
    nsp = -(-n_sel // HEAD_DIM) * HEAD_DIM
    c_lo = np.arange(n_cmp) * CMP_STRIDE
    c_hi = c_lo + CMP_BLOCK - 1
    s_lo = (np.arange(n_sel) * SEL_BLOCK)[:, None]
    ov = np.zeros((nsp, ncp), np.float32)
    ov[:n_sel, :n_cmp] = (c_lo[None, :] <= s_lo + SEL_BLOCK - 1) & (c_hi[None, :] >= s_lo)
    return jnp.asarray(ov, BF16)


def kernel(x, positions, ln1_g, ln1_b, ffn1_w1, ffn1_w3, ffn1_w2, w_in, gate_b, cmp_pe, cmp_w1,
           cmp_b1, cmp_w2, cmp_b2, w_out, ln2_g, ln2_b, ffn2_w1, ffn2_w3, ffn2_w2, ln3_g, ln3_b):
    bsz, s, d = x.shape
    assert bsz == 1 and d == (N_HEADS_DIL + N_HEADS_NSA) * HEAD_DIM
    assert s % max(FFN_ROWS, PROJ_ROWS, NSA_KEY_TILE) == 0 and s >= NSA_WINDOW + NSA_Q_TILE
    assert ffn1_w1.shape[2] % FFN_HIDDEN_TILE == 0
    alpha = (2.0 * DEPTH) ** 0.25
    scale = HEAD_DIM ** -0.5 * LOG2_E
    row = lambda v: v.reshape(1, -1)

    inv_freq = ROPE_THETA ** (-jnp.arange(0, HEAD_DIM, 2, dtype=F32) / HEAD_DIM)
    invf = jnp.concatenate([inv_freq, inv_freq]).reshape(1, HEAD_DIM)
    pos = positions[0].astype(F32).reshape(s, 1)
    cosf, sinf = _rope_tables(pos, pos, invf, ROPE_TILE)
    ncp = s // CMP_STRIDE
    n_cmp = (s - CMP_BLOCK) // CMP_STRIDE + 1
    pos_lo = jnp.pad(pos[0:n_cmp * CMP_STRIDE:CMP_STRIDE], ((0, ncp - n_cmp), (0, 0)))
    pos_hi = jnp.pad(pos[CMP_BLOCK - 1::CMP_STRIDE][:n_cmp], ((0, ncp - n_cmp), (0, 0)))
    cos_c, sin_c = _rope_tables(pos_lo, pos_hi, invf, ncp)

    g_off = 3 * D_DIL + D_NSA + 6 * KV_W
    per_g = NSA_GROUP * N_NSA_BRANCHES
    w_in_t = w_in[0].T.astype(BF16)
    wg = jnp.zeros((NSA_KV_GROUPS * HEAD_DIM, d), BF16)
    bg = jnp.zeros((1, NSA_KV_GROUPS * HEAD_DIM), F32)
    for g in range(NSA_KV_GROUPS):
        wg = wg.at[g * HEAD_DIM:g * HEAD_DIM + per_g].set(
            w_in_t[g_off + g * per_g:g_off + (g + 1) * per_g])
        bg = bg.at[0, g * HEAD_DIM:g * HEAD_DIM + per_g].set(gate_b[0][g * per_g:(g + 1) * per_g])

    h0 = x[0]
    h1 = _ffn_ln(h0, ffn1_w1[0], ffn1_w3[0], ffn1_w2[0], row(ln1_g[0]), row(ln1_b[0]),
                 alpha, FFN_ROWS, FFN_HIDDEN_TILE)

    heads, gates = _in_proj(h1, w_in_t, cosf, sinf, wg, bg, scale, PROJ_ROWS)
    tok16 = heads[H_KC:H_KS].reshape(2 * NSA_KV_GROUPS, ncp, CMP_STRIDE * HEAD_DIM)
    kcv, kcv_t = _compress(
        tok16, cmp_pe[0].reshape(2, 1, CMP_BLOCK * HEAD_DIM), cmp_w1[0],
        cmp_b1[0].reshape(2, 1, CMP_HIDDEN), cmp_w2[0],
        cmp_b2[0].reshape(2, 1, HEAD_DIM), cos_c, sin_c)
    part_cmp, sel_off = _nsa_select(heads, kcv, kcv_t, gates, _overlap_t(s), SEL_Q_TILE)
    mix_b = _nsa(heads, gates, part_cmp, sel_off, NSA_Q_TILE, NSA_KEY_TILE)
    mix_a = _dilated(heads, DIL_Q_TILE)

    h2 = _out_proj_ln(mix_a, mix_b, w_out[0], h1, row(ln2_g[0]), row(ln2_b[0]), alpha,
                      OUT_PROJ_ROWS)
    h3 = _ffn_ln(h2, ffn2_w1[0], ffn2_w3[0], ffn2_w2[0], row(ln3_g[0]), row(ln3_b[0]),
                 alpha, FFN_ROWS, FFN_HIDDEN_TILE)
    return h3.reshape(bsz, s, d)
```
